```python
import math
import jax, jax.numpy as jnp
from jax import lax
import numpy as np

D_MODEL = 1024
BATCH = 8
SEQ = 16384
DEPTH = 1

N_MEM = 256
D_FF = 2816
D_CONV = D_MODEL
CONV_WIDTH = 31
D_SGU = D_MODEL
SGU_GROUPS = 4
CHUNK = 128
X_HEADS = 4
X_HEAD_DIM = D_MODEL // X_HEADS
D_IN = 2 * D_CONV + 2 * D_SGU + 2 * D_MODEL
EPS_RMS = 1e-6
EPS_LN = 1e-5

kernel_name = "hybrid_conformer_gmlp_memxattn_block"


def rms_norm(x, g):
    xf = x.astype(jnp.float32)
    y = xf * lax.rsqrt(jnp.mean(xf * xf, axis=-1, keepdims=True) + EPS_RMS)
    return (y * g.astype(jnp.float32)).astype(x.dtype)


def layer_norm(x, g, b):
    xf = x.astype(jnp.float32)
    mu = jnp.mean(xf, axis=-1, keepdims=True)
    xc = xf - mu
    var = jnp.mean(xc * xc, axis=-1, keepdims=True)
    y = xc * lax.rsqrt(var + EPS_LN)
    return (y * g.astype(jnp.float32) + b.astype(jnp.float32)).astype(x.dtype)


def swiglu(x, w_gu, w_down):
    gu = x @ w_gu
    g, u = jnp.split(gu, 2, axis=-1)
    return (jax.nn.silu(g) * u) @ w_down


def causal_depthwise_conv(a, w, b):
    c = a.shape[-1]
    y = lax.conv_general_dilated(
        a, w.astype(a.dtype)[:, None, :],
        window_strides=(1,), padding=[(CONV_WIDTH - 1, 0)],
        dimension_numbers=("NWC", "WIO", "NWC"),
        feature_group_count=c)
    return y + b.astype(a.dtype)


def conformer_conv_branch(a_val, a_gate, conv_w, conv_b, ln_g, ln_b, w_a_out):
    a = a_val * jax.nn.sigmoid(a_gate)
    a = causal_depthwise_conv(a, conv_w, conv_b)
    a = jax.nn.silu(layer_norm(a, ln_g, ln_b))
    return a @ w_a_out


def spatial_gating_branch(u, v, ln_g, ln_b, sgu_w, sgu_b, w_b_out):
    bsz, s, _ = u.shape
    u = jax.nn.gelu(u)
    v = layer_norm(jax.nn.gelu(v), ln_g, ln_b)
    n_chunks = s // CHUNK
    gd = D_SGU // SGU_GROUPS
    vc = v.reshape(bsz, n_chunks, CHUNK, SGU_GROUPS, gd)
    mask = jnp.tril(jnp.ones((CHUNK, CHUNK), dtype=bool))
    w_s = jnp.where(mask[None], sgu_w, 0.0).astype(v.dtype)
    mixed = jnp.einsum("gts,bcsgd->bctgd", w_s, vc)
    mixed = mixed + jnp.transpose(sgu_b)[None, None, :, :, None].astype(v.dtype)
    out = u * mixed.reshape(bsz, s, D_SGU)
    return out @ w_b_out


def memory_cross_attention(xn, memn, w_q, w_kv, w_o):
    bsz, s, _ = xn.shape
    q = (xn @ w_q).reshape(bsz, s, X_HEADS, X_HEAD_DIM)
    kv = memn @ w_kv
    k, v = jnp.split(kv, 2, axis=-1)
    k = k.reshape(bsz, N_MEM, X_HEADS, X_HEAD_DIM)
    v = v.reshape(bsz, N_MEM, X_HEADS, X_HEAD_DIM)
    scores = jnp.einsum("bshd,bmhd->bhsm", q.astype(jnp.float32), k.astype(jnp.float32))
    p = jax.nn.softmax(scores * (1.0 / math.sqrt(X_HEAD_DIM)), axis=-1).astype(v.dtype)
    o = jnp.einsum("bhsm,bmhd->bshd", p, v).reshape(bsz, s, D_MODEL)
    return o @ w_o


def _fwd_setup_inputs(seed: int = 0) -> dict:
    key = jax.random.key(seed)
    ks = jax.random.split(key, 32)

    def dense(k, shape, fan_in):
        return jax.random.normal(k, shape, jnp.float32) * (fan_in ** -0.5)

    def gain(k, shape):
        return 1.0 + 0.02 * jax.random.normal(k, shape, jnp.float32)

    def small(k, shape):
        return 0.02 * jax.random.normal(k, shape, jnp.float32)

    L = DEPTH
    return {
        "x": jax.random.normal(ks[0], (BATCH, SEQ, D_MODEL), jnp.float32),
        "mem": jax.random.normal(ks[1], (BATCH, N_MEM, D_MODEL), jnp.float32),
        "ffn1_norm": gain(ks[2], (L, D_MODEL)),
        "ffn1_w_gu": dense(ks[3], (L, D_MODEL, 2 * D_FF), D_MODEL),
        "ffn1_w_down": dense(ks[4], (L, D_FF, D_MODEL), D_FF),
        "mix_norm": gain(ks[5], (L, D_MODEL)),
        "w_in": dense(ks[6], (L, D_MODEL, D_IN), D_MODEL),
        "b_in": small(ks[7], (L, D_IN)),
        "conv_w": dense(ks[8], (L, CONV_WIDTH, D_CONV), CONV_WIDTH),
        "conv_b": small(ks[9], (L, D_CONV)),
        "conv_ln_g": gain(ks[10], (L, D_CONV)),
        "conv_ln_b": small(ks[11], (L, D_CONV)),
        "w_a_out": dense(ks[12], (L, D_CONV, D_MODEL), D_CONV),
        "sgu_ln_g": gain(ks[13], (L, D_SGU)),
        "sgu_ln_b": small(ks[14], (L, D_SGU)),
        "sgu_w": dense(ks[15], (L, SGU_GROUPS, CHUNK, CHUNK), CHUNK),
        "sgu_b": gain(ks[16], (L, SGU_GROUPS, CHUNK)),
        "w_b_out": dense(ks[17], (L, D_SGU, D_MODEL), D_SGU),
        "w_out": dense(ks[18], (L, D_MODEL, D_MODEL), D_MODEL),
        "xattn_norm": gain(ks[19], (L, D_MODEL)),
        "mem_norm": gain(ks[20], (L, D_MODEL)),
        "w_q": dense(ks[21], (L, D_MODEL, D_MODEL), D_MODEL),
        "w_kv": dense(ks[22], (L, D_MODEL, 2 * D_MODEL), D_MODEL),
        "w_o": dense(ks[23], (L, D_MODEL, D_MODEL), D_MODEL),
        "ffn2_norm": gain(ks[24], (L, D_MODEL)),
        "ffn2_w_gu": dense(ks[25], (L, D_MODEL, 2 * D_FF), D_MODEL),
        "ffn2_w_down": dense(ks[26], (L, D_FF, D_MODEL), D_FF),
        "final_norm": gain(ks[27], (D_MODEL,)),
    }


def _fwd_reference(x, mem, ffn1_norm, ffn1_w_gu, ffn1_w_down, mix_norm, w_in, b_in,
              conv_w, conv_b, conv_ln_g, conv_ln_b, w_a_out,
              sgu_ln_g, sgu_ln_b, sgu_w, sgu_b, w_b_out, w_out,
              xattn_norm, mem_norm, w_q, w_kv, w_o,
              ffn2_norm, ffn2_w_gu, ffn2_w_down, final_norm):
    split_at = [D_CONV, 2 * D_CONV, 2 * D_CONV + D_SGU, 2 * D_CONV + 2 * D_SGU,
                2 * D_CONV + 2 * D_SGU + D_MODEL]
    h = x
    for l in range(DEPTH):
        h = h + 0.5 * swiglu(rms_norm(h, ffn1_norm[l]), ffn1_w_gu[l], ffn1_w_down[l])

        n = rms_norm(h, mix_norm[l])
        p = n @ w_in[l] + b_in[l]
        a_val, a_gate, b_u, b_v, g_a, g_b = jnp.split(p, split_at, axis=-1)
        y_a = conformer_conv_branch(a_val, a_gate, conv_w[l], conv_b[l],
                                    conv_ln_g[l], conv_ln_b[l], w_a_out[l])
        y_b = spatial_gating_branch(b_u, b_v, sgu_ln_g[l], sgu_ln_b[l],
                                    sgu_w[l], sgu_b[l], w_b_out[l])
        merged = jax.nn.sigmoid(g_a) * y_a + jax.nn.sigmoid(g_b) * y_b
        h = h + merged @ w_out[l]

        h = h + memory_cross_attention(rms_norm(h, xattn_norm[l]), rms_norm(mem, mem_norm[l]),
                                       w_q[l], w_kv[l], w_o[l])

        h = h + 0.5 * swiglu(rms_norm(h, ffn2_norm[l]), ffn2_w_gu[l], ffn2_w_down[l])
    return rms_norm(h, final_norm)


import jax as _jax
import jax.numpy as _jnp

TWIN_FORMAT = 'train_step'
FWD_PARAMS = ['x', 'mem', 'ffn1_norm', 'ffn1_w_gu', 'ffn1_w_down', 'mix_norm', 'w_in', 'b_in', 'conv_w', 'conv_b', 'conv_ln_g', 'conv_ln_b', 'w_a_out', 'sgu_ln_g', 'sgu_ln_b', 'sgu_w', 'sgu_b', 'w_b_out', 'w_out', 'xattn_norm', 'mem_norm', 'w_q', 'w_kv', 'w_o', 'ffn2_norm', 'ffn2_w_gu', 'ffn2_w_down', 'final_norm']
TWIN_WEIGHTS = ['ffn1_norm', 'ffn1_w_gu', 'ffn1_w_down', 'mix_norm', 'w_in', 'b_in', 'conv_w', 'conv_b', 'conv_ln_g', 'conv_ln_b', 'w_a_out', 'sgu_ln_g', 'sgu_ln_b', 'sgu_w', 'sgu_b', 'w_b_out', 'w_out', 'xattn_norm', 'mem_norm', 'w_q', 'w_kv', 'w_o', 'ffn2_norm', 'ffn2_w_gu', 'ffn2_w_down', 'final_norm']
TWIN_DIFF_INPUT = 'x'
TWIN_INPUTS = ['x', 'mem', 'ffn1_norm', 'ffn1_w_gu', 'ffn1_w_down', 'mix_norm', 'w_in', 'b_in', 'conv_w', 'conv_b', 'conv_ln_g', 'conv_ln_b', 'w_a_out', 'sgu_ln_g', 'sgu_ln_b', 'sgu_w', 'sgu_b', 'w_b_out', 'w_out', 'xattn_norm', 'mem_norm', 'w_q', 'w_kv', 'w_o', 'ffn2_norm', 'ffn2_w_gu', 'ffn2_w_down', 'final_norm', 'loss_target', 'm_ffn1_norm', 'm_ffn1_w_gu', 'm_ffn1_w_down', 'm_mix_norm', 'm_w_in', 'm_b_in', 'm_conv_w', 'm_conv_b', 'm_conv_ln_g', 'm_conv_ln_b', 'm_w_a_out', 'm_sgu_ln_g', 'm_sgu_ln_b', 'm_sgu_w', 'm_sgu_b', 'm_w_b_out', 'm_w_out', 'm_xattn_norm', 'm_mem_norm', 'm_w_q', 'm_w_kv', 'm_w_o', 'm_ffn2_norm', 'm_ffn2_w_gu', 'm_ffn2_w_down', 'm_final_norm', 'v_ffn1_norm', 'v_ffn1_w_gu', 'v_ffn1_w_down', 'v_mix_norm', 'v_w_in', 'v_b_in', 'v_conv_w', 'v_conv_b', 'v_conv_ln_g', 'v_conv_ln_b', 'v_w_a_out', 'v_sgu_ln_g', 'v_sgu_ln_b', 'v_sgu_w', 'v_sgu_b', 'v_w_b_out', 'v_w_out', 'v_xattn_norm', 'v_mem_norm', 'v_w_q', 'v_w_kv', 'v_w_o', 'v_ffn2_norm', 'v_ffn2_w_gu', 'v_ffn2_w_down', 'v_final_norm']
TWIN_OUTPUTS = ['loss', 'grad_x', 'grad_ffn1_norm', 'grad_ffn1_w_gu', 'grad_ffn1_w_down', 'grad_mix_norm', 'grad_w_in', 'grad_b_in', 'grad_conv_w', 'grad_conv_b', 'grad_conv_ln_g', 'grad_conv_ln_b', 'grad_w_a_out', 'grad_sgu_ln_g', 'grad_sgu_ln_b', 'grad_sgu_w', 'grad_sgu_b', 'grad_w_b_out', 'grad_w_out', 'grad_xattn_norm', 'grad_mem_norm', 'grad_w_q', 'grad_w_kv', 'grad_w_o', 'grad_ffn2_norm', 'grad_ffn2_w_gu', 'grad_ffn2_w_down', 'grad_final_norm', 'delta_ffn1_norm', 'delta_ffn1_w_gu', 'delta_ffn1_w_down', 'delta_mix_norm', 'delta_w_in', 'delta_b_in', 'delta_conv_w', 'delta_conv_b', 'delta_conv_ln_g', 'delta_conv_ln_b', 'delta_w_a_out', 'delta_sgu_ln_g', 'delta_sgu_ln_b', 'delta_sgu_w', 'delta_sgu_b', 'delta_w_b_out', 'delta_w_out', 'delta_xattn_norm', 'delta_mem_norm', 'delta_w_q', 'delta_w_kv', 'delta_w_o', 'delta_ffn2_norm', 'delta_ffn2_w_gu', 'delta_ffn2_w_down', 'delta_final_norm', 'new_m_ffn1_norm', 'new_m_ffn1_w_gu', 'new_m_ffn1_w_down', 'new_m_mix_norm', 'new_m_w_in', 'new_m_b_in', 'new_m_conv_w', 'new_m_conv_b', 'new_m_conv_ln_g', 'new_m_conv_ln_b', 'new_m_w_a_out', 'new_m_sgu_ln_g', 'new_m_sgu_ln_b', 'new_m_sgu_w', 'new_m_sgu_b', 'new_m_w_b_out', 'new_m_w_out', 'new_m_xattn_norm', 'new_m_mem_norm', 'new_m_w_q', 'new_m_w_kv', 'new_m_w_o', 'new_m_ffn2_norm', 'new_m_ffn2_w_gu', 'new_m_ffn2_w_down', 'new_m_final_norm', 'new_v_ffn1_norm', 'new_v_ffn1_w_gu', 'new_v_ffn1_w_down', 'new_v_mix_norm', 'new_v_w_in', 'new_v_b_in', 'new_v_conv_w', 'new_v_conv_b', 'new_v_conv_ln_g', 'new_v_conv_ln_b', 'new_v_w_a_out', 'new_v_sgu_ln_g', 'new_v_sgu_ln_b', 'new_v_sgu_w', 'new_v_sgu_b', 'new_v_w_b_out', 'new_v_w_out', 'new_v_xattn_norm', 'new_v_mem_norm', 'new_v_w_q', 'new_v_w_kv', 'new_v_w_o', 'new_v_ffn2_norm', 'new_v_ffn2_w_gu', 'new_v_ffn2_w_down', 'new_v_final_norm']
TWIN_LEAF_KINDS = {'loss': 'loss', 'grad_x': 'grad_x', 'grad_ffn1_norm': 'grad_w', 'grad_ffn1_w_gu': 'grad_w', 'grad_ffn1_w_down': 'grad_w', 'grad_mix_norm': 'grad_w', 'grad_w_in': 'grad_w', 'grad_b_in': 'grad_w', 'grad_conv_w': 'grad_w', 'grad_conv_b': 'grad_w', 'grad_conv_ln_g': 'grad_w', 'grad_conv_ln_b': 'grad_w', 'grad_w_a_out': 'grad_w', 'grad_sgu_ln_g': 'grad_w', 'grad_sgu_ln_b': 'grad_w', 'grad_sgu_w': 'grad_w', 'grad_sgu_b': 'grad_w', 'grad_w_b_out': 'grad_w', 'grad_w_out': 'grad_w', 'grad_xattn_norm': 'grad_w', 'grad_mem_norm': 'grad_w', 'grad_w_q': 'grad_w', 'grad_w_kv': 'grad_w', 'grad_w_o': 'grad_w', 'grad_ffn2_norm': 'grad_w', 'grad_ffn2_w_gu': 'grad_w', 'grad_ffn2_w_down': 'grad_w', 'grad_final_norm': 'grad_w', 'delta_ffn1_norm': 'delta_w', 'delta_ffn1_w_gu': 'delta_w', 'delta_ffn1_w_down': 'delta_w', 'delta_mix_norm': 'delta_w', 'delta_w_in': 'delta_w', 'delta_b_in': 'delta_w', 'delta_conv_w': 'delta_w', 'delta_conv_b': 'delta_w', 'delta_conv_ln_g': 'delta_w', 'delta_conv_ln_b': 'delta_w', 'delta_w_a_out': 'delta_w', 'delta_sgu_ln_g': 'delta_w', 'delta_sgu_ln_b': 'delta_w', 'delta_sgu_w': 'delta_w', 'delta_sgu_b': 'delta_w', 'delta_w_b_out': 'delta_w', 'delta_w_out': 'delta_w', 'delta_xattn_norm': 'delta_w', 'delta_mem_norm': 'delta_w', 'delta_w_q': 'delta_w', 'delta_w_kv': 'delta_w', 'delta_w_o': 'delta_w', 'delta_ffn2_norm': 'delta_w', 'delta_ffn2_w_gu': 'delta_w', 'delta_ffn2_w_down': 'delta_w', 'delta_final_norm': 'delta_w', 'new_m_ffn1_norm': 'new_m', 'new_m_ffn1_w_gu': 'new_m', 'new_m_ffn1_w_down': 'new_m', 'new_m_mix_norm': 'new_m', 'new_m_w_in': 'new_m', 'new_m_b_in': 'new_m', 'new_m_conv_w': 'new_m', 'new_m_conv_b': 'new_m', 'new_m_conv_ln_g': 'new_m', 'new_m_conv_ln_b': 'new_m', 'new_m_w_a_out': 'new_m', 'new_m_sgu_ln_g': 'new_m', 'new_m_sgu_ln_b': 'new_m', 'new_m_sgu_w': 'new_m', 'new_m_sgu_b': 'new_m', 'new_m_w_b_out': 'new_m', 'new_m_w_out': 'new_m', 'new_m_xattn_norm': 'new_m', 'new_m_mem_norm': 'new_m', 'new_m_w_q': 'new_m', 'new_m_w_kv': 'new_m', 'new_m_w_o': 'new_m', 'new_m_ffn2_norm': 'new_m', 'new_m_ffn2_w_gu': 'new_m', 'new_m_ffn2_w_down': 'new_m', 'new_m_final_norm': 'new_m', 'new_v_ffn1_norm': 'new_v', 'new_v_ffn1_w_gu': 'new_v', 'new_v_ffn1_w_down': 'new_v', 'new_v_mix_norm': 'new_v', 'new_v_w_in': 'new_v', 'new_v_b_in': 'new_v', 'new_v_conv_w': 'new_v', 'new_v_conv_b': 'new_v', 'new_v_conv_ln_g': 'new_v', 'new_v_conv_ln_b': 'new_v', 'new_v_w_a_out': 'new_v', 'new_v_sgu_ln_g': 'new_v', 'new_v_sgu_ln_b': 'new_v', 'new_v_sgu_w': 'new_v', 'new_v_sgu_b': 'new_v', 'new_v_w_b_out': 'new_v', 'new_v_w_out': 'new_v', 'new_v_xattn_norm': 'new_v', 'new_v_mem_norm': 'new_v', 'new_v_w_q': 'new_v', 'new_v_w_kv': 'new_v', 'new_v_w_o': 'new_v', 'new_v_ffn2_norm': 'new_v', 'new_v_ffn2_w_gu': 'new_v', 'new_v_ffn2_w_down': 'new_v', 'new_v_final_norm': 'new_v'}


def _forward(args):
    return _fwd_reference(*[args[k] for k in FWD_PARAMS])


def _output_shape():
    def fwd():
        inp = _fwd_setup_inputs(0)
        return _fwd_reference(*[inp[k] for k in FWD_PARAMS])
    out = _jax.eval_shape(fwd)
    return out.shape, out.dtype

N_MICROBATCH = 1
ADAM_LR = 0.001
ADAM_B1 = 0.9
ADAM_B2 = 0.999
ADAM_EPS = 1e-08
ADAM_WD = 0.01
ADAM_STEP = 10
PER_EXAMPLE_BATCH_AXIS = {'x': 0, 'mem': 0, 'loss_target': 0}
SHARED_INPUTS = []
_WEIGHT_DTYPES = {'ffn1_norm': _jnp.float32, 'ffn1_w_gu': _jnp.float32, 'ffn1_w_down': _jnp.float32, 'mix_norm': _jnp.float32, 'w_in': _jnp.float32, 'b_in': _jnp.float32, 'conv_w': _jnp.float32, 'conv_b': _jnp.float32, 'conv_ln_g': _jnp.float32, 'conv_ln_b': _jnp.float32, 'w_a_out': _jnp.float32, 'sgu_ln_g': _jnp.float32, 'sgu_ln_b': _jnp.float32, 'sgu_w': _jnp.float32, 'sgu_b': _jnp.float32, 'w_b_out': _jnp.float32, 'w_out': _jnp.float32, 'xattn_norm': _jnp.float32, 'mem_norm': _jnp.float32, 'w_q': _jnp.float32, 'w_kv': _jnp.float32, 'w_o': _jnp.float32, 'ffn2_norm': _jnp.float32, 'ffn2_w_gu': _jnp.float32, 'ffn2_w_down': _jnp.float32, 'final_norm': _jnp.float32}
MOMENT_SCALE = {'ffn1_norm': 1.622077e-01, 'ffn1_w_gu': 6.769036e-02, 'ffn1_w_down': 1.107214e-01, 'mix_norm': 2.099770e-01, 'w_in': 8.573338e-02, 'b_in': 9.236341e-02, 'conv_w': 1.046721e-01, 'conv_b': 2.115981e-01, 'conv_ln_g': 1.319874e-01, 'conv_ln_b': 1.279823e-01, 'w_a_out': 1.037460e-01, 'sgu_ln_g': 8.086344e-02, 'sgu_ln_b': 7.516210e-02, 'sgu_w': 1.096625e-01, 'sgu_b': 1.699250e-01, 'w_b_out': 1.400922e-01, 'w_out': 1.748514e-01, 'xattn_norm': 3.290355e-02, 'mem_norm': 4.811616e-02, 'w_q': 3.246667e-02, 'w_kv': 3.261725e-02, 'w_o': 3.298825e-02, 'ffn2_norm': 1.374704e-01, 'ffn2_w_gu': 5.348158e-02, 'ffn2_w_down': 8.707098e-02, 'final_norm': 1.280407e+02}


def _to_microbatches(a, axis):
    t = _jnp.moveaxis(a, axis, 0)
    t = t.reshape((N_MICROBATCH, t.shape[0] // N_MICROBATCH) + t.shape[1:])
    return _jnp.moveaxis(t, 1, axis + 1)


def setup_inputs(seed: int = 0) -> dict:
    inp = _fwd_setup_inputs(seed)
    key = _jax.random.fold_in(_jax.random.key(seed), 7919)
    shape, _ = _output_shape()
    out = dict(inp)
    out["loss_target"] = _jax.random.normal(_jax.random.fold_in(key, 0), shape, _jnp.float32)
    for i, name in enumerate(TWIN_WEIGHTS):
        w = inp[name].astype(_jnp.float32)
        if MOMENT_SCALE is None:
            s = _jnp.sqrt(_jnp.mean(_jnp.square(w)) + 1e-30)
        else:
            s = MOMENT_SCALE[name]
        km, kv = _jax.random.split(_jax.random.fold_in(key, i + 1))
        out[name] = w
        out["m_" + name] = s * _jax.random.normal(km, w.shape, _jnp.float32)
        out["v_" + name] = (s * s) * _jax.random.uniform(kv, w.shape, _jnp.float32, 0.5, 1.5)
    if N_MICROBATCH > 1:
        for name, axis in PER_EXAMPLE_BATCH_AXIS.items():
            out[name] = _to_microbatches(out[name], axis)
    return {'x': out['x'], 'mem': out['mem'], 'ffn1_norm': out['ffn1_norm'], 'ffn1_w_gu': out['ffn1_w_gu'], 'ffn1_w_down': out['ffn1_w_down'], 'mix_norm': out['mix_norm'], 'w_in': out['w_in'], 'b_in': out['b_in'], 'conv_w': out['conv_w'], 'conv_b': out['conv_b'], 'conv_ln_g': out['conv_ln_g'], 'conv_ln_b': out['conv_ln_b'], 'w_a_out': out['w_a_out'], 'sgu_ln_g': out['sgu_ln_g'], 'sgu_ln_b': out['sgu_ln_b'], 'sgu_w': out['sgu_w'], 'sgu_b': out['sgu_b'], 'w_b_out': out['w_b_out'], 'w_out': out['w_out'], 'xattn_norm': out['xattn_norm'], 'mem_norm': out['mem_norm'], 'w_q': out['w_q'], 'w_kv': out['w_kv'], 'w_o': out['w_o'], 'ffn2_norm': out['ffn2_norm'], 'ffn2_w_gu': out['ffn2_w_gu'], 'ffn2_w_down': out['ffn2_w_down'], 'final_norm': out['final_norm'], 'loss_target': out['loss_target'], 'm_ffn1_norm': out['m_ffn1_norm'], 'm_ffn1_w_gu': out['m_ffn1_w_gu'], 'm_ffn1_w_down': out['m_ffn1_w_down'], 'm_mix_norm': out['m_mix_norm'], 'm_w_in': out['m_w_in'], 'm_b_in': out['m_b_in'], 'm_conv_w': out['m_conv_w'], 'm_conv_b': out['m_conv_b'], 'm_conv_ln_g': out['m_conv_ln_g'], 'm_conv_ln_b': out['m_conv_ln_b'], 'm_w_a_out': out['m_w_a_out'], 'm_sgu_ln_g': out['m_sgu_ln_g'], 'm_sgu_ln_b': out['m_sgu_ln_b'], 'm_sgu_w': out['m_sgu_w'], 'm_sgu_b': out['m_sgu_b'], 'm_w_b_out': out['m_w_b_out'], 'm_w_out': out['m_w_out'], 'm_xattn_norm': out['m_xattn_norm'], 'm_mem_norm': out['m_mem_norm'], 'm_w_q': out['m_w_q'], 'm_w_kv': out['m_w_kv'], 'm_w_o': out['m_w_o'], 'm_ffn2_norm': out['m_ffn2_norm'], 'm_ffn2_w_gu': out['m_ffn2_w_gu'], 'm_ffn2_w_down': out['m_ffn2_w_down'], 'm_final_norm': out['m_final_norm'], 'v_ffn1_norm': out['v_ffn1_norm'], 'v_ffn1_w_gu': out['v_ffn1_w_gu'], 'v_ffn1_w_down': out['v_ffn1_w_down'], 'v_mix_norm': out['v_mix_norm'], 'v_w_in': out['v_w_in'], 'v_b_in': out['v_b_in'], 'v_conv_w': out['v_conv_w'], 'v_conv_b': out['v_conv_b'], 'v_conv_ln_g': out['v_conv_ln_g'], 'v_conv_ln_b': out['v_conv_ln_b'], 'v_w_a_out': out['v_w_a_out'], 'v_sgu_ln_g': out['v_sgu_ln_g'], 'v_sgu_ln_b': out['v_sgu_ln_b'], 'v_sgu_w': out['v_sgu_w'], 'v_sgu_b': out['v_sgu_b'], 'v_w_b_out': out['v_w_b_out'], 'v_w_out': out['v_w_out'], 'v_xattn_norm': out['v_xattn_norm'], 'v_mem_norm': out['v_mem_norm'], 'v_w_q': out['v_w_q'], 'v_w_kv': out['v_w_kv'], 'v_w_o': out['v_w_o'], 'v_ffn2_norm': out['v_ffn2_norm'], 'v_ffn2_w_gu': out['v_ffn2_w_gu'], 'v_ffn2_w_down': out['v_ffn2_w_down'], 'v_final_norm': out['v_final_norm']}


def _loss(weights, diff, rest, loss_target):
    with _jax.named_scope("forward"):
        args = {**rest, TWIN_DIFF_INPUT: diff, **{k: w.astype(_WEIGHT_DTYPES[k]) for k, w in weights.items()}}
        y = _forward(args)
    with _jax.named_scope("loss_head"):
        err = _jnp.square(y.astype(_jnp.float32) - loss_target)
        return 0.5 * _jnp.sum(_jnp.mean(err, axis=-1)) if err.ndim else 0.5 * err


def _adamw(w, g, m, v):
    m = ADAM_B1 * m + (1.0 - ADAM_B1) * g
    v = ADAM_B2 * v + (1.0 - ADAM_B2) * _jnp.square(g)
    m_hat = m / (1.0 - ADAM_B1 ** ADAM_STEP)
    v_hat = v / (1.0 - ADAM_B2 ** ADAM_STEP)
    delta = -ADAM_LR * (m_hat / (_jnp.sqrt(v_hat) + ADAM_EPS) + ADAM_WD * w)
    return delta, m, v


def reference(x, mem, ffn1_norm, ffn1_w_gu, ffn1_w_down, mix_norm, w_in, b_in, conv_w, conv_b, conv_ln_g, conv_ln_b, w_a_out, sgu_ln_g, sgu_ln_b, sgu_w, sgu_b, w_b_out, w_out, xattn_norm, mem_norm, w_q, w_kv, w_o, ffn2_norm, ffn2_w_gu, ffn2_w_down, final_norm, loss_target, m_ffn1_norm, m_ffn1_w_gu, m_ffn1_w_down, m_mix_norm, m_w_in, m_b_in, m_conv_w, m_conv_b, m_conv_ln_g, m_conv_ln_b, m_w_a_out, m_sgu_ln_g, m_sgu_ln_b, m_sgu_w, m_sgu_b, m_w_b_out, m_w_out, m_xattn_norm, m_mem_norm, m_w_q, m_w_kv, m_w_o, m_ffn2_norm, m_ffn2_w_gu, m_ffn2_w_down, m_final_norm, v_ffn1_norm, v_ffn1_w_gu, v_ffn1_w_down, v_mix_norm, v_w_in, v_b_in, v_conv_w, v_conv_b, v_conv_ln_g, v_conv_ln_b, v_w_a_out, v_sgu_ln_g, v_sgu_ln_b, v_sgu_w, v_sgu_b, v_w_b_out, v_w_out, v_xattn_norm, v_mem_norm, v_w_q, v_w_kv, v_w_o, v_ffn2_norm, v_ffn2_w_gu, v_ffn2_w_down, v_final_norm):
    given = dict(x=x, mem=mem, ffn1_norm=ffn1_norm, ffn1_w_gu=ffn1_w_gu, ffn1_w_down=ffn1_w_down, mix_norm=mix_norm, w_in=w_in, b_in=b_in, conv_w=conv_w, conv_b=conv_b, conv_ln_g=conv_ln_g, conv_ln_b=conv_ln_b, w_a_out=w_a_out, sgu_ln_g=sgu_ln_g, sgu_ln_b=sgu_ln_b, sgu_w=sgu_w, sgu_b=sgu_b, w_b_out=w_b_out, w_out=w_out, xattn_norm=xattn_norm, mem_norm=mem_norm, w_q=w_q, w_kv=w_kv, w_o=w_o, ffn2_norm=ffn2_norm, ffn2_w_gu=ffn2_w_gu, ffn2_w_down=ffn2_w_down, final_norm=final_norm, loss_target=loss_target, m_ffn1_norm=m_ffn1_norm, m_ffn1_w_gu=m_ffn1_w_gu, m_ffn1_w_down=m_ffn1_w_down, m_mix_norm=m_mix_norm, m_w_in=m_w_in, m_b_in=m_b_in, m_conv_w=m_conv_w, m_conv_b=m_conv_b, m_conv_ln_g=m_conv_ln_g, m_conv_ln_b=m_conv_ln_b, m_w_a_out=m_w_a_out, m_sgu_ln_g=m_sgu_ln_g, m_sgu_ln_b=m_sgu_ln_b, m_sgu_w=m_sgu_w, m_sgu_b=m_sgu_b, m_w_b_out=m_w_b_out, m_w_out=m_w_out, m_xattn_norm=m_xattn_norm, m_mem_norm=m_mem_norm, m_w_q=m_w_q, m_w_kv=m_w_kv, m_w_o=m_w_o, m_ffn2_norm=m_ffn2_norm, m_ffn2_w_gu=m_ffn2_w_gu, m_ffn2_w_down=m_ffn2_w_down, m_final_norm=m_final_norm, v_ffn1_norm=v_ffn1_norm, v_ffn1_w_gu=v_ffn1_w_gu, v_ffn1_w_down=v_ffn1_w_down, v_mix_norm=v_mix_norm, v_w_in=v_w_in, v_b_in=v_b_in, v_conv_w=v_conv_w, v_conv_b=v_conv_b, v_conv_ln_g=v_conv_ln_g, v_conv_ln_b=v_conv_ln_b, v_w_a_out=v_w_a_out, v_sgu_ln_g=v_sgu_ln_g, v_sgu_ln_b=v_sgu_ln_b, v_sgu_w=v_sgu_w, v_sgu_b=v_sgu_b, v_w_b_out=v_w_b_out, v_w_out=v_w_out, v_xattn_norm=v_xattn_norm, v_mem_norm=v_mem_norm, v_w_q=v_w_q, v_w_kv=v_w_kv, v_w_o=v_w_o, v_ffn2_norm=v_ffn2_norm, v_ffn2_w_gu=v_ffn2_w_gu, v_ffn2_w_down=v_ffn2_w_down, v_final_norm=v_final_norm)
    weights = {n: given[n] for n in TWIN_WEIGHTS}
    shared = {n: given[n] for n in SHARED_INPUTS}
    per_example = {n: given[n] for n in ['x', 'mem']}
    grad_fn = _jax.value_and_grad(_loss, argnums=(0, 1))

    def one_microbatch(ex, loss_target):
        ex = dict(ex)
        diff = ex.pop(TWIN_DIFF_INPUT)
        return grad_fn(weights, diff, {**shared, **ex}, loss_target)

    if N_MICROBATCH == 1:
        loss, (grad_w, grad_x) = one_microbatch(per_example, given["loss_target"])
    else:
        def body(carry, xs):
            loss_sum, grad_sum = carry
            l_k, (gw_k, gx_k) = one_microbatch(xs[0], xs[1])
            with _jax.named_scope("update"):
                return (loss_sum + l_k, _jax.tree.map(_jnp.add, grad_sum, gw_k)), gx_k

        init = (_jnp.zeros((), _jnp.float32), _jax.tree.map(_jnp.zeros_like, weights))
        (loss, grad_w), grad_x = _jax.lax.scan(body, init, (per_example, given["loss_target"]))
    with _jax.named_scope("update"):
        delta_w, new_m, new_v = {}, {}, {}
        for n in TWIN_WEIGHTS:
            delta_w[n], new_m[n], new_v[n] = _adamw(weights[n], grad_w[n], given["m_" + n], given["v_" + n])
    return (loss, grad_x, *[grad_w[n] for n in TWIN_WEIGHTS], *[delta_w[n] for n in TWIN_WEIGHTS],
            *[new_m[n] for n in TWIN_WEIGHTS], *[new_v[n] for n in TWIN_WEIGHTS])
```

```python
import functools
import math

import jax
import jax.numpy as jnp
from jax import lax
from jax.experimental import pallas as pl
from jax.experimental.pallas import tpu as pltpu

F32 = jnp.float32
BF16 = jnp.bfloat16
MESH = pl.DeviceIdType.MESH
AXES = ("x", "y", "c")

D = 1024
DFF = 2816
NMEM = 256
HEADS = 4
HD = D // HEADS
CW = 31
HALO = 32
CHUNK = 128
GROUPS = 4
GD = D // GROUPS
EPS_RMS = 1e-6
EPS_LN = 1e-5
LR, B1, B2, EPS_ADAM, WD, STEP = 0.001, 0.9, 0.999, 1e-08, 0.01, 10
NDEV = 8
VMEM_LIMIT = 56 * 1024 * 1024


def _params(*sem):
    return pltpu.CompilerParams(dimension_semantics=sem, vmem_limit_bytes=VMEM_LIMIT)


def _dot(a, b):
    return jnp.dot(a, b, preferred_element_type=F32)


def _dot_nt(a, b):
    return lax.dot_general(a, b, (((1,), (1,)), ((), ())), preferred_element_type=F32)


def _dot_tn(a, b):
    return lax.dot_general(a, b, (((0,), (0,)), ((), ())), preferred_element_type=F32)


def _sigmoid(x):
    return 1.0 / (1.0 + jnp.exp(-x))


_GELU_C = math.sqrt(2.0 / math.pi)


def _gelu(x):
    return 0.5 * x * (1.0 + jnp.tanh(_GELU_C * (x + 0.044715 * (x * x * x))))


def _gelu_grad(x):
    t = jnp.tanh(_GELU_C * (x + 0.044715 * (x * x * x)))
    return 0.5 * (1.0 + t) + 0.5 * x * (1.0 - t * t) * (_GELU_C * (1.0 + 3.0 * 0.044715 * x * x))


def _rms_stats(h):
    r = lax.rsqrt(jnp.mean(h * h, axis=-1, keepdims=True) + EPS_RMS)
    return r, h * r


def _rms_bwd(dxn, h, gain):
    r, xh = _rms_stats(h)
    dgain = jnp.sum(dxn * xh, axis=0, keepdims=True)
    dxh = dxn * gain
    dh = r * (dxh - xh * jnp.mean(dxh * xh, axis=-1, keepdims=True))
    return dh, dgain


def _ln_stats(c):
    mu = jnp.mean(c, axis=-1, keepdims=True)
    xc = c - mu
    rstd = lax.rsqrt(jnp.mean(xc * xc, axis=-1, keepdims=True) + EPS_LN)
    return rstd, xc * rstd


def _ln_bwd(dy, xhat, rstd, g):
    dxh = dy * g
    return rstd * (dxh - jnp.mean(dxh, axis=-1, keepdims=True)
                   - xhat * jnp.mean(dxh * xhat, axis=-1, keepdims=True))


def _row(tm, cols, cb=0):
    return pl.BlockSpec((tm, cols), lambda i, _cb=cb: (i, _cb))


def _const(shape):
    n = len(shape)
    return pl.BlockSpec(shape, lambda *_: (0,) * n)


def _sds(shape, dtype):
    return jax.ShapeDtypeStruct(shape, dtype)


def _acc(ref, first, val):
    @pl.when(first)
    def _():
        ref[...] = jnp.zeros_like(ref)
    ref[...] += val


def _all_gather(name, blk):
    r, cdim = blk.shape

    def body(x_ref, out_ref, send_sems, recv_sems, local_sem):
        x, y, c = lax.axis_index("x"), lax.axis_index("y"), lax.axis_index("c")
        me, sibling = (x, y, c), (x, y, 1 - c)
        chips = [(1 - x, y), (x, 1 - y), (1 - x, 1 - y)]

        def slot(px, py, pc):
            return out_ref.at[4 * px + 2 * py + pc]

        def copy(k, block, to, src=None):
            return pltpu.make_async_remote_copy(
                src_ref=slot(*block) if src is None else src, dst_ref=slot(*block),
                send_sem=send_sems.at[k], recv_sem=recv_sems.at[k],
                device_id=to, device_id_type=MESH)

        mine = pltpu.make_async_copy(x_ref, slot(*me), local_sem)
        mine.start()
        first = [copy(0, me, sibling, src=x_ref)]
        first += [copy(1 + j, me, (*chip, c), src=x_ref) for j, chip in enumerate(chips)]
        for cp in first:
            cp.start()
        passed = [copy(4 + j, (*chip, c), sibling) for j, chip in enumerate(chips)]
        for j, chip in enumerate(chips):
            copy(1 + j, (*chip, c), me).wait_recv()
            passed[j].start()
        copy(0, sibling, me).wait_recv()
        for j, chip in enumerate(chips):
            copy(4 + j, (*chip, 1 - c), me).wait_recv()
        for cp in first + passed:
            cp.wait_send()
        mine.wait()

    return pl.pallas_call(
        body, name=name,
        out_shape=_sds((NDEV, r, cdim), blk.dtype),
        in_specs=[pl.BlockSpec(memory_space=pl.ANY)],
        out_specs=pl.BlockSpec(memory_space=pl.ANY),
        scratch_shapes=[pltpu.SemaphoreType.DMA((7,)), pltpu.SemaphoreType.DMA((7,)),
                        pltpu.SemaphoreType.DMA],
    )(blk)


def _pair_exchange(name, g4):
    _, _, r, cdim = g4.shape

    def body(g_ref, out_ref, send_sems, recv_sems):
        x, y, c = lax.axis_index("x"), lax.axis_index("y"), lax.axis_index("c")
        copies = [pltpu.make_async_remote_copy(
            src_ref=g_ref.at[k, 1 - c], dst_ref=out_ref.at[k],
            send_sem=send_sems.at[k], recv_sem=recv_sems.at[k],
            device_id=(x, y, 1 - c), device_id_type=MESH) for k in range(4)]
        for cp in copies:
            cp.start()
        for cp in copies:
            cp.wait_recv()
        for cp in copies:
            cp.wait_send()

    return pl.pallas_call(
        body, name=name,
        out_shape=_sds((4, r, cdim), g4.dtype),
        in_specs=[pl.BlockSpec(memory_space=pl.ANY)],
        out_specs=pl.BlockSpec(memory_space=pl.ANY),
        scratch_shapes=[pltpu.SemaphoreType.DMA((4,)), pltpu.SemaphoreType.DMA((4,))],
    )(g4)


def _chip_exchange(name, part):
    _, r, cdim = part.shape

    def body(p_ref, out_ref, send_sems, recv_sems):
        x, y, c = lax.axis_index("x"), lax.axis_index("y"), lax.axis_index("c")
        copies = []
        for k in (1, 2, 3):
            px = x if (k >> 1) == 0 else 1 - x
            py = y if (k & 1) == 0 else 1 - y
            copies.append(pltpu.make_async_remote_copy(
                src_ref=p_ref.at[2 * px + py], dst_ref=out_ref.at[k - 1],
                send_sem=send_sems.at[k - 1], recv_sem=recv_sems.at[k - 1],
                device_id=(px, py, c), device_id_type=MESH))
        for cp in copies:
            cp.start()
        for cp in copies:
            cp.wait_recv()
        for cp in copies:
            cp.wait_send()

    return pl.pallas_call(
        body, name=name,
        out_shape=_sds((3, r, cdim), part.dtype),
        in_specs=[pl.BlockSpec(memory_space=pl.ANY)],
        out_specs=pl.BlockSpec(memory_space=pl.ANY),
        scratch_shapes=[pltpu.SemaphoreType.DMA((3,)), pltpu.SemaphoreType.DMA((3,))],
    )(part)


def _pair_sum(name, g4, recv, core, tr):
    _, _, r, cdim = g4.shape

    def body(core_ref, a_ref, b_ref, o_ref):
        o_ref[...] = (a_ref[...].astype(F32) + b_ref[...].astype(F32)).astype(o_ref.dtype)

    return pl.pallas_call(
        body, name=name,
        grid_spec=pltpu.PrefetchScalarGridSpec(
            num_scalar_prefetch=1, grid=(4, r // tr),
            in_specs=[pl.BlockSpec((None, None, tr, cdim), lambda k, i, cr: (k, cr[0], i, 0)),
                      pl.BlockSpec((None, tr, cdim), lambda k, i, cr: (k, i, 0))],
            out_specs=pl.BlockSpec((None, tr, cdim), lambda k, i, cr: (k, i, 0))),
        out_shape=_sds((4, r, cdim), BF16),
        compiler_params=_params("parallel", "parallel"),
    )(core, g4, recv)


def _chip_sum(name, part, recv, chip, tr):
    _, r, cdim = part.shape

    def body(chip_ref, a_ref, b_ref, o_ref):
        s = a_ref[...].astype(F32)
        for k in range(3):
            s = s + b_ref[k].astype(F32)
        o_ref[...] = s

    return pl.pallas_call(
        body, name=name,
        grid_spec=pltpu.PrefetchScalarGridSpec(
            num_scalar_prefetch=1, grid=(r // tr,),
            in_specs=[pl.BlockSpec((None, tr, cdim), lambda i, cr: (cr[0], i, 0)),
                      pl.BlockSpec((3, tr, cdim), lambda i, cr: (0, i, 0))],
            out_specs=pl.BlockSpec((tr, cdim), lambda i, cr: (i, 0))),
        out_shape=_sds((r, cdim), F32),
        compiler_params=_params("parallel"),
    )(chip, part, recv)


def _rms_cast(name, h, gain, tm):
    t = h.shape[0]

    def body(h_ref, g_ref, o_ref):
        _, xh = _rms_stats(h_ref[...])
        o_ref[...] = (xh * g_ref[...]).astype(BF16)

    return pl.pallas_call(
        body, name=name, grid=(t // tm,),
        in_specs=[_row(tm, D), _const((1, D))], out_specs=_row(tm, D),
        out_shape=_sds((t, D), BF16), compiler_params=_params("parallel"),
    )(h, gain)


def _ffn_up(name, xn, wgu_t, tm, tn):
    t = xn.shape[0]
    nh = DFF // tn

    def body(x_ref, wg_ref, wu_ref, g_ref, u_ref, a_ref):
        x = x_ref[...]
        g = _dot_nt(x, wg_ref[...])
        u = _dot_nt(x, wu_ref[...])
        g_ref[...] = g.astype(BF16)
        u_ref[...] = u.astype(BF16)
        a_ref[...] = (g * _sigmoid(g) * u).astype(BF16)

    o = pl.BlockSpec((tm, tn), lambda i, j: (i, j))
    return pl.pallas_call(
        body, name=name, grid=(t // tm, nh),
        in_specs=[pl.BlockSpec((tm, D), lambda i, j: (i, 0)),
                  pl.BlockSpec((tn, D), lambda i, j: (j, 0)),
                  pl.BlockSpec((tn, D), lambda i, j: (j + nh, 0))],
        out_specs=[o, o, o],
        out_shape=[_sds((t, DFF), BF16)] * 3,
        compiler_params=_params("parallel", "arbitrary"),
    )(xn, wgu_t, wgu_t)


def _ffn_down(name, a, wd, h, gain, tm):
    t = a.shape[0]
    with_norm = gain is not None

    def body(*refs):
        if with_norm:
            a_ref, w_ref, h_ref, g_ref, o_ref, n_ref = refs
        else:
            a_ref, w_ref, h_ref, o_ref = refs
        hn = h_ref[...] + 0.5 * _dot(a_ref[...], w_ref[...])
        o_ref[...] = hn
        if with_norm:
            _, xh = _rms_stats(hn)
            n_ref[...] = (xh * g_ref[...]).astype(BF16)

    ins = [_row(tm, DFF), _const((DFF, D)), _row(tm, D)]
    args = [a, wd, h]
    outs, shapes = [_row(tm, D)], [_sds((t, D), F32)]
    if with_norm:
        ins.append(_const((1, D)))
        args.append(gain)
        outs.append(_row(tm, D))
        shapes.append(_sds((t, D), BF16))
    return pl.pallas_call(
        body, name=name, grid=(t // tm,), in_specs=ins, out_specs=outs, out_shape=shapes,
        compiler_params=_params("parallel"),
    )(*args)


def _mix_in(name, n, win_t, b_in, tm, tn):
    t = n.shape[0]

    def body(n_ref, w_ref, b_ref, p_ref):
        p_ref[...] = (_dot_nt(n_ref[...], w_ref[...]) + b_ref[...]).astype(BF16)

    return pl.pallas_call(
        body, name=name, grid=(t // tm, 6 * D // tn),
        in_specs=[pl.BlockSpec((tm, D), lambda i, j: (i, 0)),
                  pl.BlockSpec((tn, D), lambda i, j: (j, 0)),
                  pl.BlockSpec((1, tn), lambda i, j: (0, j))],
        out_specs=pl.BlockSpec((tm, tn), lambda i, j: (i, j)),
        out_shape=_sds((t, 6 * D), BF16),
        compiler_params=_params("parallel", "arbitrary"),
    )(n, win_t, b_in)


def _conv_fwd(name, p, conv_w, conv_b, ln_g, ln_b, tm):
    t = p.shape[0]

    def body(av_ref, ag_ref, w_ref, cb_ref, lg_ref, lb_ref, c_ref, a_ref, ext):
        i = pl.program_id(0)

        @pl.when(i == 0)
        def _():
            ext[pl.ds(0, HALO), :] = jnp.zeros((HALO, D), F32)

        ext[pl.ds(HALO, tm), :] = av_ref[...].astype(F32) * _sigmoid(ag_ref[...].astype(F32))
        acc = jnp.zeros((tm, D), F32) + cb_ref[...]
        for k in range(CW):
            acc = acc + w_ref[pl.ds(k, 1), :] * ext[pl.ds(HALO - (CW - 1) + k, tm), :]
        c_ref[...] = acc
        rstd, chat = _ln_stats(acc)
        ca = chat * lg_ref[...] + lb_ref[...]
        a_ref[...] = (ca * _sigmoid(ca)).astype(BF16)
        ext[pl.ds(0, HALO), :] = ext[pl.ds(tm, HALO), :]

    return pl.pallas_call(
        body, name=name, grid=(t // tm,),
        in_specs=[_row(tm, D, 0), _row(tm, D, 1), _const((HALO, D)), _const((1, D)),
                  _const((1, D)), _const((1, D))],
        out_specs=[_row(tm, D), _row(tm, D)],
        out_shape=[_sds((t, D), F32), _sds((t, D), BF16)],
        scratch_shapes=[pltpu.VMEM((tm + HALO, D), F32)],
        compiler_params=_params("arbitrary"),
    )(p, p, conv_w, conv_b, ln_g, ln_b)


def _sgu_mask():
    rows = lax.broadcasted_iota(jnp.int32, (CHUNK, CHUNK), 0)
    cols = lax.broadcasted_iota(jnp.int32, (CHUNK, CHUNK), 1)
    return cols <= rows


def _sgu_fwd(name, p, ln_g, ln_b, sgu_w, bias_full, tm):
    t = p.shape[0]

    def body(bu_ref, bv_ref, lg_ref, lb_ref, ws_ref, bias_ref, o_ref):
        mask = _sgu_mask()
        _, vhat = _ln_stats(_gelu(bv_ref[...].astype(F32)))
        vn = (vhat * lg_ref[...] + lb_ref[...]).astype(BF16)
        ub = _gelu(bu_ref[...].astype(F32))
        for g in range(GROUPS):
            wm = jnp.where(mask, ws_ref[g], 0.0).astype(BF16)
            cs = slice(g * GD, (g + 1) * GD)
            for cc in range(tm // CHUNK):
                rs = slice(cc * CHUNK, (cc + 1) * CHUNK)
                mixed = _dot(wm, vn[rs, cs]) + bias_ref[:, cs]
                o_ref[rs, cs] = (ub[rs, cs] * mixed).astype(BF16)

    return pl.pallas_call(
        body, name=name, grid=(t // tm,),
        in_specs=[_row(tm, D, 2), _row(tm, D, 3), _const((1, D)), _const((1, D)),
                  _const((GROUPS, CHUNK, CHUNK)), _const((CHUNK, D))],
        out_specs=_row(tm, D), out_shape=_sds((t, D), BF16),
        compiler_params=_params("parallel"),
    )(p, p, ln_g, ln_b, sgu_w, bias_full)


def _merge_fwd(name, act_a, act_b, p, w_a, w_b, w_out, h, gain, tm):
    t = h.shape[0]

    def body(a_ref, b_ref, ga_ref, gb_ref, wa_ref, wb_ref, wo_ref, h_ref, g_ref,
             ya_ref, yb_ref, mg_ref, ho_ref, xn_ref):
        ya = _dot(a_ref[...], wa_ref[...])
        yb = _dot(b_ref[...], wb_ref[...])
        ya_ref[...] = ya.astype(BF16)
        yb_ref[...] = yb.astype(BF16)
        merged = (_sigmoid(ga_ref[...].astype(F32)) * ya
                  + _sigmoid(gb_ref[...].astype(F32)) * yb).astype(BF16)
        mg_ref[...] = merged
        hn = h_ref[...] + _dot(merged, wo_ref[...])
        ho_ref[...] = hn
        _, xh = _rms_stats(hn)
        xn_ref[...] = (xh * g_ref[...]).astype(BF16)

    rb = _row(tm, D)
    return pl.pallas_call(
        body, name=name, grid=(t // tm,),
        in_specs=[rb, rb, _row(tm, D, 4), _row(tm, D, 5), _const((D, D)), _const((D, D)),
                  _const((D, D)), rb, _const((1, D))],
        out_specs=[rb] * 5,
        out_shape=[_sds((t, D), BF16)] * 3 + [_sds((t, D), F32), _sds((t, D), BF16)],
        compiler_params=_params("parallel"),
    )(act_a, act_b, p, p, w_a, w_b, w_out, h, gain)


def _kv_fwd(name, mem, gain, wkv_t):
    def body(m_ref, g_ref, w_ref, mn_ref, k_ref, v_ref):
        _, xh = _rms_stats(m_ref[...])
        mn = (xh * g_ref[...]).astype(BF16)
        mn_ref[...] = mn
        kv = _dot_nt(mn, w_ref[...])
        k_ref[...] = kv[:, :D].astype(BF16)
        v_ref[...] = kv[:, D:].astype(BF16)

    return pl.pallas_call(
        body, name=name,
        out_shape=[_sds((NMEM, D), BF16)] * 3,
        compiler_params=pltpu.CompilerParams(vmem_limit_bytes=VMEM_LIMIT),
    )(mem, gain, wkv_t)


def _softmax_rows(s):
    e = jnp.exp(s - jnp.max(s, axis=-1, keepdims=True))
    return e / jnp.sum(e, axis=-1, keepdims=True)


def _attn_fwd(name, xq, w_q, kb, vb, w_o, h, gain, tm):
    t = h.shape[0]
    scale = 1.0 / math.sqrt(HD)

    def body(x_ref, wq_ref, k_ref, v_ref, wo_ref, h_ref, g_ref, q_ref, o_ref, ho_ref, xn_ref):
        q_ref[...] = _dot(x_ref[...], wq_ref[...]).astype(BF16)
        for hd in range(HEADS):
            cs = slice(hd * HD, (hd + 1) * HD)
            p = _softmax_rows(_dot_nt(q_ref[:, cs], k_ref[:, cs]) * scale)
            o_ref[:, cs] = _dot(p.astype(BF16), v_ref[:, cs]).astype(BF16)
        hn = h_ref[...] + _dot(o_ref[...], wo_ref[...])
        ho_ref[...] = hn
        _, xh = _rms_stats(hn)
        xn_ref[...] = (xh * g_ref[...]).astype(BF16)

    rb = _row(tm, D)
    return pl.pallas_call(
        body, name=name, grid=(t // tm,),
        in_specs=[rb, _const((D, D)), _const((NMEM, D)), _const((NMEM, D)), _const((D, D)), rb,
                  _const((1, D))],
        out_specs=[rb] * 4,
        out_shape=[_sds((t, D), BF16), _sds((t, D), BF16), _sds((t, D), F32), _sds((t, D), BF16)],
        compiler_params=_params("parallel"),
    )(xq, w_q, kb, vb, w_o, h, gain)


def _loss_bwd(name, h, gain, target, tm):
    t = h.shape[0]
    steps = t // tm

    def body(h_ref, g_ref, t_ref, dh_ref, dhb_ref, loss_ref, dg_ref, lacc):
        i = pl.program_id(0)
        hv = h_ref[...]
        r, xh = _rms_stats(hv)
        err = xh * g_ref[...] - t_ref[...]
        _acc(lacc, i == 0, jnp.sum(err * err, axis=0, keepdims=True))
        dy = err * (1.0 / D)
        _acc(dg_ref, i == 0, jnp.sum(dy * xh, axis=0, keepdims=True))
        dxh = dy * g_ref[...]
        dh = r * (dxh - xh * jnp.mean(dxh * xh, axis=-1, keepdims=True))
        dh_ref[...] = dh
        dhb_ref[...] = dh.astype(BF16)

        @pl.when(i == steps - 1)
        def _():
            loss_ref[...] = jnp.zeros((8, 128), F32) + (0.5 / D) * jnp.sum(lacc[...])

    rb = _row(tm, D)
    return pl.pallas_call(
        body, name=name, grid=(steps,),
        in_specs=[rb, _const((1, D)), rb],
        out_specs=[rb, rb, _const((8, 128)), _const((1, D))],
        out_shape=[_sds((t, D), F32), _sds((t, D), BF16), _sds((8, 128), F32), _sds((1, D), F32)],
        scratch_shapes=[pltpu.VMEM((1, D), F32)],
        compiler_params=_params("arbitrary"),
    )(h, gain, target)


def _ffn_bwd_act(name, dhb, wd, g, u, tm, tn):
    t = dhb.shape[0]

    def body(d_ref, w_ref, g_ref, u_ref, dg_ref, du_ref):
        da = 0.5 * _dot_nt(d_ref[...], w_ref[...])
        gv = g_ref[...].astype(F32)
        sg = _sigmoid(gv)
        dg_ref[...] = (da * u_ref[...].astype(F32) * (sg * (1.0 + gv * (1.0 - sg)))).astype(BF16)
        du_ref[...] = (da * (gv * sg)).astype(BF16)

    o = pl.BlockSpec((tm, tn), lambda i, j: (i, j))
    return pl.pallas_call(
        body, name=name, grid=(t // tm, DFF // tn),
        in_specs=[pl.BlockSpec((tm, D), lambda i, j: (i, 0)),
                  pl.BlockSpec((tn, D), lambda i, j: (j, 0)), o, o],
        out_specs=[o, o], out_shape=[_sds((t, DFF), BF16)] * 2,
        compiler_params=_params("parallel", "arbitrary"),
    )(dhb, wd, g, u)


def _dx_rms_bwd(name, pairs, h, gain, dh_in, tm):
    t = h.shape[0]
    np_ = len(pairs)

    def body(*refs):
        a_refs = refs[:np_]
        b_refs = refs[np_:2 * np_]
        h_ref, g_ref, d_ref, o_ref, ob_ref, dg_ref = refs[2 * np_:]
        dxn = None
        for (a_ref, b_ref, pr) in zip(a_refs, b_refs, pairs):
            y = _dot_nt(a_ref[...], b_ref[...]) if pr[4] else _dot(a_ref[...], b_ref[...])
            dxn = y if dxn is None else dxn + y
        dh, dgain = _rms_bwd(dxn, h_ref[...], g_ref[...])
        _acc(dg_ref, pl.program_id(0) == 0, dgain)
        out = d_ref[...] + dh
        o_ref[...] = out
        ob_ref[...] = out.astype(BF16)

    ins, args = [], []
    for (a, b, blk, rows, tr) in pairs:
        ins.append(_row(tm, a.shape[1]))
        args.append(a)
    for (a, b, blk, rows, tr) in pairs:
        ins.append(pl.BlockSpec((rows, b.shape[1]), lambda i, _b=blk: (_b, 0)))
        args.append(b)
    rb = _row(tm, D)
    ins += [rb, _const((1, D)), rb]
    args += [h, gain, dh_in]
    return pl.pallas_call(
        body, name=name, grid=(t // tm,), in_specs=ins,
        out_specs=[rb, rb, _const((1, D))],
        out_shape=[_sds((t, D), F32), _sds((t, D), BF16), _sds((1, D), F32)],
        compiler_params=_params("arbitrary"),
    )(*args)


def _mm_tn(name, a, b, scale, tmo, tk):
    t, m = a.shape
    n = b.shape[1]
    tk = min(tk, t)
    steps = t // tk

    def body(a_ref, b_ref, o_ref, acc):
        k = pl.program_id(1)
        _acc(acc, k == 0, _dot_tn(a_ref[...], b_ref[...]))

        @pl.when(k == steps - 1)
        def _():
            o_ref[...] = (acc[...] * scale).astype(o_ref.dtype)

    return pl.pallas_call(
        body, name=name, grid=(m // tmo, steps),
        in_specs=[pl.BlockSpec((tk, tmo), lambda i, k: (k, i)),
                  pl.BlockSpec((tk, n), lambda i, k: (k, 0))],
        out_specs=pl.BlockSpec((tmo, n), lambda i, k: (i, 0)),
        out_shape=_sds((m, n), BF16),
        scratch_shapes=[pltpu.VMEM((tmo, n), F32)],
        compiler_params=_params("parallel", "arbitrary"),
    )(a, b)


def _attn_bwd(name, dhb, w_o, qb, kb, vb, tm):
    t = dhb.shape[0]
    scale = 1.0 / math.sqrt(HD)

    def body(d_ref, wo_ref, q_ref, k_ref, v_ref, dq_ref, dk_ref, dv_ref, do_s):
        i = pl.program_id(0)

        @pl.when(i == 0)
        def _():
            dk_ref[...] = jnp.zeros_like(dk_ref)
            dv_ref[...] = jnp.zeros_like(dv_ref)

        do_s[...] = _dot_nt(d_ref[...], wo_ref[...]).astype(BF16)
        for hd in range(HEADS):
            cs = slice(hd * HD, (hd + 1) * HD)
            q = q_ref[:, cs]
            p = _softmax_rows(_dot_nt(q, k_ref[:, cs]) * scale)
            do = do_s[:, cs]
            dp = _dot_nt(do, v_ref[:, cs])
            ds = (p * (dp - jnp.sum(dp * p, axis=-1, keepdims=True)) * scale).astype(BF16)
            dq_ref[:, cs] = _dot(ds, k_ref[:, cs]).astype(BF16)
            dk_ref[:, cs] += _dot_tn(ds, q)
            dv_ref[:, cs] += _dot_tn(p.astype(BF16), do)

    rb = _row(tm, D)
    return pl.pallas_call(
        body, name=name, grid=(t // tm,),
        in_specs=[rb, _const((D, D)), rb, _const((NMEM, D)), _const((NMEM, D))],
        out_specs=[rb, _const((NMEM, D)), _const((NMEM, D))],
        out_shape=[_sds((t, D), BF16), _sds((NMEM, D), F32), _sds((NMEM, D), F32)],
        scratch_shapes=[pltpu.VMEM((tm, D), BF16)],
        compiler_params=_params("arbitrary"),
    )(dhb, w_o, qb, kb, vb)


def _kv_bwd(name, dk, dv, memn, wkv_t, mem, gain):
    def body(dk_ref, dv_ref, mn_ref, w_ref, m_ref, g_ref, dw_ref, dg_ref):
        dkb = dk_ref[...].astype(BF16)
        dvb = dv_ref[...].astype(BF16)
        mn = mn_ref[...]
        dw_ref[pl.ds(0, D), :] = _dot_tn(dkb, mn).astype(BF16)
        dw_ref[pl.ds(D, D), :] = _dot_tn(dvb, mn).astype(BF16)
        dmn = _dot(dkb, w_ref[pl.ds(0, D), :]) + _dot(dvb, w_ref[pl.ds(D, D), :])
        _, xh = _rms_stats(m_ref[...])
        dg_ref[...] = jnp.sum(dmn * xh, axis=0, keepdims=True)

    return pl.pallas_call(
        body, name=name,
        out_shape=[_sds((2 * D, D), BF16), _sds((1, D), F32)],
        compiler_params=pltpu.CompilerParams(vmem_limit_bytes=VMEM_LIMIT),
    )(dk, dv, memn, wkv_t, mem, gain)


def _merge_bwd(name, dhb, w_out, ya, yb, p, tm):
    t = dhb.shape[0]

    def body(d_ref, w_ref, ya_ref, yb_ref, ga_ref, gb_ref, dya_ref, dyb_ref, dp_ref, cs_ref):
        dm = _dot_nt(d_ref[...], w_ref[...])
        sa = _sigmoid(ga_ref[...].astype(F32))
        sb = _sigmoid(gb_ref[...].astype(F32))
        dya_ref[...] = (dm * sa).astype(BF16)
        dyb_ref[...] = (dm * sb).astype(BF16)
        dga = dm * ya_ref[...].astype(F32) * (sa * (1.0 - sa))
        dgb = dm * yb_ref[...].astype(F32) * (sb * (1.0 - sb))
        dp_ref[:, pl.ds(0, D)] = dga.astype(BF16)
        dp_ref[:, pl.ds(D, D)] = dgb.astype(BF16)
        first = pl.program_id(0) == 0

        @pl.when(first)
        def _():
            cs_ref[...] = jnp.zeros_like(cs_ref)
        cs_ref[:, pl.ds(0, D)] += jnp.sum(dga, axis=0, keepdims=True)
        cs_ref[:, pl.ds(D, D)] += jnp.sum(dgb, axis=0, keepdims=True)

    rb = _row(tm, D)
    return pl.pallas_call(
        body, name=name, grid=(t // tm,),
        in_specs=[rb, _const((D, D)), rb, rb, _row(tm, D, 4), _row(tm, D, 5)],
        out_specs=[rb, rb, _row(tm, 2 * D), _const((1, 2 * D))],
        out_shape=[_sds((t, D), BF16), _sds((t, D), BF16), _sds((t, 2 * D), BF16),
                   _sds((1, 2 * D), F32)],
        compiler_params=_params("arbitrary"),
    )(dhb, w_out, ya, yb, p, p)


def _conv_bwd(name, dya, w_a, c, p, conv_w, ln_g, ln_b, tm):
    t = dya.shape[0]
    steps = t // tm
    hb = tm // HALO

    def rev(i):
        return steps - 1 - i

    def body(dy_ref, wa_ref, c_ref, av_ref, ag_ref, avh_ref, agh_ref, w_ref, lg_ref, lb_ref,
             dp_ref, cs_ref, dw_ref, dcb_ref, dlg_ref, dlb_ref, dc_ext, a_ext):
        i = pl.program_id(0)
        first = i == 0

        @pl.when(first)
        def _():
            dc_ext[pl.ds(tm, HALO), :] = jnp.zeros((HALO, D), F32)
            dw_ref[...] = jnp.zeros_like(dw_ref)
            cs_ref[...] = jnp.zeros_like(cs_ref)

        d_act = _dot_nt(dy_ref[...], wa_ref[...])
        rstd, chat = _ln_stats(c_ref[...])
        ca = chat * lg_ref[...] + lb_ref[...]
        sc = _sigmoid(ca)
        dca = d_act * (sc * (1.0 + ca * (1.0 - sc)))
        _acc(dlg_ref, first, jnp.sum(dca * chat, axis=0, keepdims=True))
        _acc(dlb_ref, first, jnp.sum(dca, axis=0, keepdims=True))
        dc = _ln_bwd(dca, chat, rstd, lg_ref[...])
        _acc(dcb_ref, first, jnp.sum(dc, axis=0, keepdims=True))
        dc_ext[pl.ds(0, tm), :] = dc

        av = av_ref[...].astype(F32)
        sg = _sigmoid(ag_ref[...].astype(F32))
        a_ext[pl.ds(HALO, tm), :] = av * sg
        halo = avh_ref[...].astype(F32) * _sigmoid(agh_ref[...].astype(F32))
        a_ext[pl.ds(0, HALO), :] = jnp.where(i == steps - 1, 0.0, halo)

        da0 = jnp.zeros((tm, D), F32)
        for k in range(CW):
            da0 = da0 + w_ref[pl.ds(k, 1), :] * dc_ext[pl.ds(CW - 1 - k, tm), :]
            dw_ref[pl.ds(k, 1), :] += jnp.sum(
                dc * a_ext[pl.ds(HALO - (CW - 1) + k, tm), :], axis=0, keepdims=True)
        dav = da0 * sg
        dag = da0 * av * (sg * (1.0 - sg))
        dp_ref[:, pl.ds(0, D)] = dav.astype(BF16)
        dp_ref[:, pl.ds(D, D)] = dag.astype(BF16)
        cs_ref[:, pl.ds(0, D)] += jnp.sum(dav, axis=0, keepdims=True)
        cs_ref[:, pl.ds(D, D)] += jnp.sum(dag, axis=0, keepdims=True)
        dc_ext[pl.ds(tm, HALO), :] = dc_ext[pl.ds(0, HALO), :]

    def rrow(cols, cb=0):
        return pl.BlockSpec((tm, cols), lambda i, _cb=cb: (rev(i), _cb))

    def halo_spec(cb):
        return pl.BlockSpec((HALO, D), lambda i, _cb=cb: (jnp.maximum(rev(i) * hb - 1, 0), _cb))

    return pl.pallas_call(
        body, name=name, grid=(steps,),
        in_specs=[rrow(D), _const((D, D)), rrow(D), rrow(D, 0), rrow(D, 1), halo_spec(0),
                  halo_spec(1), _const((HALO, D)), _const((1, D)), _const((1, D))],
        out_specs=[rrow(2 * D), _const((1, 2 * D)), _const((HALO, D)), _const((1, D)),
                   _const((1, D)), _const((1, D))],
        out_shape=[_sds((t, 2 * D), BF16), _sds((1, 2 * D), F32), _sds((HALO, D), F32),
                   _sds((1, D), F32), _sds((1, D), F32), _sds((1, D), F32)],
        scratch_shapes=[pltpu.VMEM((tm + HALO, D), F32), pltpu.VMEM((tm + HALO, D), F32)],
        compiler_params=_params("arbitrary"),
    )(dya, w_a, c, p, p, p, p, conv_w, ln_g, ln_b)


def _sgu_bwd(name, dyb, w_b, p, ln_g, ln_b, sgu_w, bias_full, tm):
    t = dyb.shape[0]
    steps = t // tm

    def body(dy_ref, wb_ref, bu_ref, bv_ref, lg_ref, lb_ref, ws_ref, bias_ref,
             dp_ref, cs_ref, dws_ref, dsb_ref, dlg_ref, dlb_ref, dub_s, dvn_s, dbias_s):
        i = pl.program_id(0)
        first = i == 0
        mask = _sgu_mask()

        @pl.when(first)
        def _():
            dws_ref[...] = jnp.zeros_like(dws_ref)
            dbias_s[...] = jnp.zeros_like(dbias_s)
            cs_ref[...] = jnp.zeros_like(cs_ref)

        dob = _dot_nt(dy_ref[...], wb_ref[...])
        bu = bu_ref[...].astype(F32)
        bv = bv_ref[...].astype(F32)
        ub = _gelu(bu)
        rstd, vhat = _ln_stats(_gelu(bv))
        vn = (vhat * lg_ref[...] + lb_ref[...]).astype(BF16)
        for g in range(GROUPS):
            wm = jnp.where(mask, ws_ref[g], 0.0).astype(BF16)
            cs = slice(g * GD, (g + 1) * GD)
            for cc in range(tm // CHUNK):
                rs = slice(cc * CHUNK, (cc + 1) * CHUNK)
                vblk = vn[rs, cs]
                mixed = _dot(wm, vblk) + bias_ref[:, cs]
                dob_blk = dob[rs, cs]
                dub_s[rs, cs] = dob_blk * mixed
                dmixed = dob_blk * ub[rs, cs]
                dbias_s[:, cs] += dmixed
                dmb = dmixed.astype(BF16)
                dws_ref[g] += _dot_nt(dmb, vblk)
                dvn_s[rs, cs] = _dot_tn(wm, dmb)
        dbu = dub_s[...] * _gelu_grad(bu)
        dvn = dvn_s[...]
        _acc(dlg_ref, first, jnp.sum(dvn * vhat, axis=0, keepdims=True))
        _acc(dlb_ref, first, jnp.sum(dvn, axis=0, keepdims=True))
        dbv = _ln_bwd(dvn, vhat, rstd, lg_ref[...]) * _gelu_grad(bv)
        dp_ref[:, pl.ds(0, D)] = dbu.astype(BF16)
        dp_ref[:, pl.ds(D, D)] = dbv.astype(BF16)
        cs_ref[:, pl.ds(0, D)] += jnp.sum(dbu, axis=0, keepdims=True)
        cs_ref[:, pl.ds(D, D)] += jnp.sum(dbv, axis=0, keepdims=True)

        @pl.when(i == steps - 1)
        def _():
            lane = lax.broadcasted_iota(jnp.int32, (CHUNK, CHUNK), 1)
            dsb = jnp.zeros((CHUNK, CHUNK), F32)
            for g in range(GROUPS):
                dws_ref[g] = jnp.where(mask, dws_ref[g], 0.0)
                dsb = jnp.where(lane == g, jnp.sum(dbias_s[:, g * GD:(g + 1) * GD], axis=1, keepdims=True), dsb)
            dsb_ref[...] = dsb

    rb = _row(tm, D)
    return pl.pallas_call(
        body, name=name, grid=(steps,),
        in_specs=[rb, _const((D, D)), _row(tm, D, 2), _row(tm, D, 3), _const((1, D)), _const((1, D)),
                  _const((GROUPS, CHUNK, CHUNK)), _const((CHUNK, D))],
        out_specs=[_row(tm, 2 * D), _const((1, 2 * D)), _const((GROUPS, CHUNK, CHUNK)),
                   _const((CHUNK, CHUNK)), _const((1, D)), _const((1, D))],
        out_shape=[_sds((t, 2 * D), BF16), _sds((1, 2 * D), F32), _sds((GROUPS, CHUNK, CHUNK), F32),
                   _sds((CHUNK, CHUNK), F32), _sds((1, D), F32), _sds((1, D), F32)],
        scratch_shapes=[pltpu.VMEM((tm, D), F32), pltpu.VMEM((tm, D), F32), pltpu.VMEM((CHUNK, D), F32)],
        compiler_params=_params("arbitrary"),
    )(dyb, w_b, p, p, ln_g, ln_b, sgu_w, bias_full)


def _adam_math(w, g, m, v):
    m = B1 * m + (1.0 - B1) * g
    v = B2 * v + (1.0 - B2) * (g * g)
    m_hat = m / (1.0 - B1 ** STEP)
    v_hat = v / (1.0 - B2 ** STEP)
    delta = -LR * (m_hat / (jnp.sqrt(v_hat) + EPS_ADAM) + WD * w)
    return delta, m, v


def _adamw(name, w, g, m, v, tr):
    r, cdim = w.shape

    def body(w_ref, g_ref, m_ref, v_ref, d_ref, mo_ref, vo_ref):
        d, mn, vn = _adam_math(w_ref[...], g_ref[...], m_ref[...], v_ref[...])
        d_ref[...] = d
        mo_ref[...] = mn
        vo_ref[...] = vn

    blk = pl.BlockSpec((tr, cdim), lambda i: (i, 0))
    return pl.pallas_call(
        body, name=name, grid=(r // tr,), in_specs=[blk] * 4, out_specs=[blk] * 3,
        out_shape=[_sds((r, cdim), F32)] * 3, compiler_params=_params("parallel"),
    )(w, g, m, v)


def _adamw_small(name, w, g8, m, v):
    r, cdim = w.shape

    def body(w_ref, g_ref, m_ref, v_ref, go_ref, d_ref, mo_ref, vo_ref):
        g = g_ref[0]
        for k in range(1, NDEV):
            g = g + g_ref[k]
        go_ref[...] = g
        d, mn, vn = _adam_math(w_ref[...], g, m_ref[...], v_ref[...])
        d_ref[...] = d
        mo_ref[...] = mn
        vo_ref[...] = vn

    return pl.pallas_call(
        body, name=name, out_shape=[_sds((r, cdim), F32)] * 4,
        compiler_params=pltpu.CompilerParams(vmem_limit_bytes=VMEM_LIMIT),
    )(w, g8, m, v)


_BIG = [("ffn1_w_gu", 704, True), ("ffn1_w_down", 352, False), ("w_in", 768, True),
        ("w_a_out", 128, False), ("w_b_out", 128, False), ("w_out", 128, False),
        ("w_q", 128, False), ("w_kv", 256, True), ("w_o", 128, False),
        ("ffn2_w_gu", 704, True), ("ffn2_w_down", 352, False)]
_BIG_ROWS = sum(r for _, r, _ in _BIG)

_SMALL = [("ffn1_norm", 1), ("mix_norm", 1), ("b_in", 6), ("conv_w", HALO), ("conv_b", 1),
          ("conv_ln_g", 1), ("conv_ln_b", 1), ("sgu_ln_g", 1), ("sgu_ln_b", 1), ("sgu_w", 64),
          ("sgu_b", 1), ("xattn_norm", 1), ("mem_norm", 1), ("ffn2_norm", 1), ("final_norm", 1)]
_SMALL_ROWS = 120


def _pack_small(vals, my_dev):
    rows = []
    for name, nrows in _SMALL:
        a = vals[name].astype(F32)
        if name == "conv_w":
            if a.shape[-1] != D:
                slab = jnp.zeros((HALO, D), F32)
                a = lax.dynamic_update_slice(slab, jnp.pad(a.reshape(CW, -1), ((0, HALO - CW), (0, 0))),
                                             (0, my_dev * (D // NDEV)))
            else:
                a = jnp.pad(a.reshape(CW, D), ((0, HALO - CW), (0, 0)))
        elif name == "sgu_b":
            a = jnp.pad(a.reshape(1, -1), ((0, 0), (0, D - GROUPS * CHUNK)))
        else:
            a = a.reshape(nrows, D)
        rows.append(a)
    packed = jnp.concatenate(rows, axis=0)
    return jnp.pad(packed, ((0, _SMALL_ROWS - packed.shape[0]), (0, 0)))


def _unpack_small(packed, shapes, my_dev):
    out, off = {}, 0
    for name, nrows in _SMALL:
        a = packed[off:off + nrows]
        off += nrows
        if name == "conv_w":
            a = lax.dynamic_slice(a, (0, my_dev * (D // NDEV)), (CW, D // NDEV))
        elif name == "sgu_b":
            a = a[:, :GROUPS * CHUNK]
        out[name] = a.reshape(shapes[name])
    return out


def kernel(x, mem, ffn1_norm, ffn1_w_gu, ffn1_w_down, mix_norm, w_in, b_in, conv_w, conv_b, conv_ln_g, conv_ln_b, w_a_out, sgu_ln_g, sgu_ln_b, sgu_w, sgu_b, w_b_out, w_out, xattn_norm, mem_norm, w_q, w_kv, w_o, ffn2_norm, ffn2_w_gu, ffn2_w_down, final_norm, loss_target, m_ffn1_norm, m_ffn1_w_gu, m_ffn1_w_down, m_mix_norm, m_w_in, m_b_in, m_conv_w, m_conv_b, m_conv_ln_g, m_conv_ln_b, m_w_a_out, m_sgu_ln_g, m_sgu_ln_b, m_sgu_w, m_sgu_b, m_w_b_out, m_w_out, m_xattn_norm, m_mem_norm, m_w_q, m_w_kv, m_w_o, m_ffn2_norm, m_ffn2_w_gu, m_ffn2_w_down, m_final_norm, v_ffn1_norm, v_ffn1_w_gu, v_ffn1_w_down, v_mix_norm, v_w_in, v_b_in, v_conv_w, v_conv_b, v_conv_ln_g, v_conv_ln_b, v_w_a_out, v_sgu_ln_g, v_sgu_ln_b, v_sgu_w, v_sgu_b, v_w_b_out, v_w_out, v_xattn_norm, v_mem_norm, v_w_q, v_w_kv, v_w_o, v_ffn2_norm, v_ffn2_w_gu, v_ffn2_w_down, v_final_norm):
    env = dict(locals())
    names = [n for n, _, _ in _BIG] + [n for n, _ in _SMALL]
    w = {n: env[n] for n in names}
    mom = {n: env["m_" + n] for n in names}
    vel = {n: env["v_" + n] for n in names}

    ax, ay, ac = lax.axis_index("x"), lax.axis_index("y"), lax.axis_index("c")
    my_chip = 2 * ax + ay
    my_dev = 2 * my_chip + ac

    t = x.shape[1]
    tm = min(512, t)
    tm_s = min(256, t)
    xs = x.reshape(t, D)
    tgt = loss_target.reshape(t, D)
    mem2 = mem.reshape(NMEM, D)

    shards = []
    for name, rows, transposed in _BIG:
        a = w[name][0]
        shards.append((a.T if transposed else a).astype(BF16))
    full = _all_gather("gather_weights", jnp.concatenate(shards, axis=0))
    wf, off = {}, 0
    for name, rows, _ in _BIG:
        wf[name] = full[:, off:off + rows, :].reshape(NDEV * rows, D)
        off += rows

    conv_slab = lax.dynamic_update_slice(
        jnp.zeros((HALO, D), F32), jnp.pad(conv_w[0], ((0, HALO - CW), (0, 0))), (0, my_dev * (D // NDEV)))
    conv_w_pad = jnp.sum(_all_gather("gather_conv_w", conv_slab), axis=0)
    bias_full = jnp.repeat(sgu_b[0].T, GD, axis=1)
    b_in2 = b_in.reshape(1, 6 * D)

    xn1 = _rms_cast("norm_x", xs, ffn1_norm, tm)
    g1, u1, a1 = _ffn_up("ffn1_up", xn1, wf["ffn1_w_gu"], tm, 1408)
    h1, n_mix = _ffn_down("ffn1_down", a1, wf["ffn1_w_down"], xs, mix_norm, tm)
    p = _mix_in("mix_in", n_mix, wf["w_in"], b_in2, tm, 1536)
    c_conv, act_a = _conv_fwd("conv_fwd", p, conv_w_pad, conv_b, conv_ln_g, conv_ln_b, tm_s)
    act_b = _sgu_fwd("sgu_fwd", p, sgu_ln_g, sgu_ln_b, sgu_w[0], bias_full, tm_s)
    ya, yb, merged, h2, xq = _merge_fwd("merge_fwd", act_a, act_b, p, wf["w_a_out"], wf["w_b_out"],
                                        wf["w_out"], h1, xattn_norm, tm_s)
    memn, kb, vb = _kv_fwd("kv_fwd", mem2, mem_norm, wf["w_kv"])
    qb, ob, h3, xn4 = _attn_fwd("attn_fwd", xq, wf["w_q"], kb, vb, wf["w_o"], h2, ffn2_norm, tm_s)
    g2, u2, a2 = _ffn_up("ffn2_up", xn4, wf["ffn2_w_gu"], tm, 1408)
    (h4,) = _ffn_down("ffn2_down", a2, wf["ffn2_w_down"], h3, None, tm)
    dh4, dh4b, loss_blk, d_final = _loss_bwd("loss_bwd", h4, final_norm.reshape(1, D), tgt, tm)

    gb = {}
    gs = {}
    gs["final_norm"] = d_final

    def ffn_bwd(tag, dhb, dh, g, u, a, xn, h_in, gain, wgu_t, wd):
        dg, du = _ffn_bwd_act(tag + "_bwd_act", dhb, wd, g, u, tm, 1408)
        gwd = _mm_tn(tag + "_dw_down", a, dhb, 0.5, 1408, 1024)
        gwg = _mm_tn(tag + "_dw_gate", dg, xn, 1.0, 1408, 1024)
        gwu = _mm_tn(tag + "_dw_up", du, xn, 1.0, 1408, 1024)
        dh_o, dhb_o, dgain = _dx_rms_bwd(
            tag + "_bwd_dx", [(dg, wgu_t, 0, DFF, False), (du, wgu_t, 1, DFF, False)], h_in, gain, dh, tm_s)
        return dh_o, dhb_o, dgain, jnp.concatenate([gwg, gwu], axis=0), gwd

    dh3, dh3b, gs["ffn2_norm"], gb["ffn2_w_gu"], gb["ffn2_w_down"] = ffn_bwd(
        "ffn2", dh4b, dh4, g2, u2, a2, xn4, h3, ffn2_norm, wf["ffn2_w_gu"], wf["ffn2_w_down"])

    gb["w_o"] = _mm_tn("dw_o", ob, dh3b, 1.0, 1024, 1024)
    dq, dk, dv = _attn_bwd("attn_bwd", dh3b, wf["w_o"], qb, kb, vb, tm_s)
    gb["w_kv"], gs["mem_norm"] = _kv_bwd("kv_bwd", dk, dv, memn, wf["w_kv"], mem2, mem_norm)
    gb["w_q"] = _mm_tn("dw_q", xq, dq, 1.0, 1024, 1024)
    dh2, dh2b, gs["xattn_norm"] = _dx_rms_bwd(
        "attn_bwd_dx", [(dq, wf["w_q"], 0, D, True)], h2, xattn_norm, dh3, tm_s)

    gb["w_out"] = _mm_tn("dw_out", merged, dh2b, 1.0, 1024, 1024)
    dya, dyb, dp_g, cs_g = _merge_bwd("merge_bwd", dh2b, wf["w_out"], ya, yb, p, tm_s)
    gb["w_a_out"] = _mm_tn("dw_a", act_a, dya, 1.0, 1024, 1024)
    gb["w_b_out"] = _mm_tn("dw_b", act_b, dyb, 1.0, 1024, 1024)
    dp_a, cs_a, d_convw, gs["conv_b"], gs["conv_ln_g"], gs["conv_ln_b"] = _conv_bwd(
        "conv_bwd", dya, wf["w_a_out"], c_conv, p, conv_w_pad, conv_ln_g, conv_ln_b, tm_s)
    dp_b, cs_b, d_sguw, d_sgub, gs["sgu_ln_g"], gs["sgu_ln_b"] = _sgu_bwd(
        "sgu_bwd", dyb, wf["w_b_out"], p, sgu_ln_g, sgu_ln_b, sgu_w[0], bias_full, tm_s)
    gs["conv_w"] = d_convw[:CW].reshape(1, CW, D)
    gs["sgu_w"] = d_sguw
    gs["sgu_b"] = d_sgub[:, :GROUPS].T
    gs["b_in"] = jnp.concatenate([cs_a, cs_b, cs_g], axis=1)
    gb["w_in"] = jnp.concatenate([_mm_tn("dw_in_a", dp_a, n_mix, 1.0, 1024, 1024),
                                  _mm_tn("dw_in_b", dp_b, n_mix, 1.0, 1024, 1024),
                                  _mm_tn("dw_in_g", dp_g, n_mix, 1.0, 1024, 1024)], axis=0)
    dh1, dh1b, gs["mix_norm"] = _dx_rms_bwd(
        "mix_bwd_dx", [(dp_a, wf["w_in"], 0, 2 * D, False), (dp_b, wf["w_in"], 1, 2 * D, False),
                       (dp_g, wf["w_in"], 2, 2 * D, False)], h1, mix_norm, dh2, tm_s)

    dx, _, gs["ffn1_norm"], gb["ffn1_w_gu"], gb["ffn1_w_down"] = ffn_bwd(
        "ffn1", dh1b, dh1, g1, u1, a1, xn1, xs, ffn1_norm, wf["ffn1_w_gu"], wf["ffn1_w_down"])

    g4 = jnp.concatenate([gb[name].reshape(4, 2, rows, D) for name, rows, _ in _BIG], axis=2)
    core = ac.astype(jnp.int32).reshape(1)
    chip = my_chip.astype(jnp.int32).reshape(1)
    from_sibling = _pair_exchange("grads_pair_exchange", g4)
    part = _pair_sum("grads_pair_sum", g4, from_sibling, core, 944)
    from_chips = _chip_exchange("grads_chip_exchange", part)
    gsum = _chip_sum("grads_chip_sum", part, from_chips, chip, 944)

    grads, deltas, new_m, new_v = {}, {}, {}, {}
    off = 0
    for name, rows, transposed in _BIG:
        gsh = gsum[off:off + rows]
        off += rows
        gsh = gsh.T if transposed else gsh
        d, mo, vo = _adamw("adamw_" + name, w[name][0], gsh, mom[name][0], vel[name][0], gsh.shape[0] // 2)
        grads[name], deltas[name], new_m[name], new_v[name] = gsh[None], d[None], mo[None], vo[None]

    shapes = {n: w[n].shape for n, _ in _SMALL}
    g8 = _all_gather("gather_small_grads", _pack_small(gs, my_dev))
    w_small = dict(w)
    m_small = dict(mom)
    v_small = dict(vel)
    sg, sd, sm, sv = _adamw_small("adamw_small", _pack_small(w_small, my_dev), g8,
                                  _pack_small(m_small, my_dev), _pack_small(v_small, my_dev))
    for dst, src in ((grads, sg), (deltas, sd), (new_m, sm), (new_v, sv)):
        dst.update(_unpack_small(src, shapes, my_dev))

    loss = lax.psum(loss_blk[0, 0], AXES)
    order = ["ffn1_norm", "ffn1_w_gu", "ffn1_w_down", "mix_norm", "w_in", "b_in", "conv_w", "conv_b",
             "conv_ln_g", "conv_ln_b", "w_a_out", "sgu_ln_g", "sgu_ln_b", "sgu_w", "sgu_b", "w_b_out",
             "w_out", "xattn_norm", "mem_norm", "w_q", "w_kv", "w_o", "ffn2_norm", "ffn2_w_gu",
             "ffn2_w_down", "final_norm"]
    return (loss, dx.reshape(x.shape), *[grads[n] for n in order], *[deltas[n] for n in order],
            *[new_m[n] for n in order], *[new_v[n] for n in order])
```

```python
import functools
import math

import jax
import jax.numpy as jnp
from jax import lax
from jax.experimental import pallas as pl
from jax.experimental.pallas import tpu as pltpu

F32 = jnp.float32
BF16 = jnp.bfloat16
MESH = pl.DeviceIdType.MESH
AXES = ("x", "y", "c")

D = 1024
DFF = 2816
NMEM = 256
HEADS = 4
HD = D // HEADS
CW = 31
HALO = 32
CHUNK = 128
GROUPS = 4
GD = D // GROUPS
EPS_RMS = 1e-6
EPS_LN = 1e-5
LR, B1, B2, EPS_ADAM, WD, STEP = 0.001, 0.9, 0.999, 1e-08, 0.01, 10
NDEV = 8
VMEM_LIMIT = 56 * 1024 * 1024
TK = 2048


def _params(*sem):
    return pltpu.CompilerParams(dimension_semantics=sem, vmem_limit_bytes=VMEM_LIMIT)


def _dot(a, b):
    return jnp.dot(a, b, preferred_element_type=F32)


def _dot_nt(a, b):
    return lax.dot_general(a, b, (((1,), (1,)), ((), ())), preferred_element_type=F32)


def _dot_tn(a, b):
    return lax.dot_general(a, b, (((0,), (0,)), ((), ())), preferred_element_type=F32)


def _sigmoid(x):
    return 0.5 * jnp.tanh(0.5 * x) + 0.5


_GELU_C = math.sqrt(2.0 / math.pi)


def _gelu(x):
    return 0.5 * x * (1.0 + jnp.tanh(_GELU_C * (x + 0.044715 * (x * x * x))))


def _gelu_grad(x):
    t = jnp.tanh(_GELU_C * (x + 0.044715 * (x * x * x)))
    return 0.5 * (1.0 + t) + 0.5 * x * (1.0 - t * t) * (_GELU_C * (1.0 + 3.0 * 0.044715 * x * x))


def _rms_stats(h):
    r = lax.rsqrt(jnp.mean(h * h, axis=-1, keepdims=True) + EPS_RMS)
    return r, h * r


def _rms_bwd(dxn, h, gain):
    r, xh = _rms_stats(h)
    dgain = jnp.sum(dxn * xh, axis=0, keepdims=True)
    dxh = dxn * gain
    dh = r * (dxh - xh * jnp.mean(dxh * xh, axis=-1, keepdims=True))
    return dh, dgain


def _ln_stats(c):
    mu = jnp.mean(c, axis=-1, keepdims=True)
    xc = c - mu
    rstd = lax.rsqrt(jnp.mean(xc * xc, axis=-1, keepdims=True) + EPS_LN)
    return rstd, xc * rstd


def _ln_bwd(dy, xhat, rstd, g):
    dxh = dy * g
    return rstd * (dxh - jnp.mean(dxh, axis=-1, keepdims=True)
                   - xhat * jnp.mean(dxh * xhat, axis=-1, keepdims=True))


def _row(tm, cols, cb=0):
    return pl.BlockSpec((tm, cols), lambda i, _cb=cb: (i, _cb))


def _const(shape):
    n = len(shape)
    return pl.BlockSpec(shape, lambda *_: (0,) * n)


def _sds(shape, dtype):
    return jax.ShapeDtypeStruct(shape, dtype)


def _acc(ref, first, val):
    @pl.when(first)
    def _():
        ref[...] = jnp.zeros_like(ref)
    ref[...] += val


def _all_gather(name, blk):
    r, cdim = blk.shape

    def body(x_ref, out_ref, send_sems, recv_sems, local_sem):
        x, y, c = lax.axis_index("x"), lax.axis_index("y"), lax.axis_index("c")
        me, sibling = (x, y, c), (x, y, 1 - c)
        chips = [(1 - x, y), (x, 1 - y), (1 - x, 1 - y)]

        def slot(px, py, pc):
            return out_ref.at[4 * px + 2 * py + pc]

        def copy(k, block, to, src=None):
            return pltpu.make_async_remote_copy(
                src_ref=slot(*block) if src is None else src, dst_ref=slot(*block),
                send_sem=send_sems.at[k], recv_sem=recv_sems.at[k],
                device_id=to, device_id_type=MESH)

        mine = pltpu.make_async_copy(x_ref, slot(*me), local_sem)
        mine.start()
        first = [copy(0, me, sibling, src=x_ref)]
        first += [copy(1 + j, me, (*chip, c), src=x_ref) for j, chip in enumerate(chips)]
        for cp in first:
            cp.start()
        passed = [copy(4 + j, (*chip, c), sibling) for j, chip in enumerate(chips)]
        for j, chip in enumerate(chips):
            copy(1 + j, (*chip, c), me).wait_recv()
            passed[j].start()
        copy(0, sibling, me).wait_recv()
        for j, chip in enumerate(chips):
            copy(4 + j, (*chip, 1 - c), me).wait_recv()
        for cp in first + passed:
            cp.wait_send()
        mine.wait()

    return pl.pallas_call(
        body, name=name,
        out_shape=_sds((NDEV, r, cdim), blk.dtype),
        in_specs=[pl.BlockSpec(memory_space=pl.ANY)],
        out_specs=pl.BlockSpec(memory_space=pl.ANY),
        scratch_shapes=[pltpu.SemaphoreType.DMA((7,)), pltpu.SemaphoreType.DMA((7,)),
                        pltpu.SemaphoreType.DMA],
    )(blk)


def _pair_exchange(name, g4):
    _, _, r, cdim = g4.shape

    def body(g_ref, out_ref, send_sems, recv_sems):
        x, y, c = lax.axis_index("x"), lax.axis_index("y"), lax.axis_index("c")
        copies = [pltpu.make_async_remote_copy(
            src_ref=g_ref.at[k, 1 - c], dst_ref=out_ref.at[k],
            send_sem=send_sems.at[k], recv_sem=recv_sems.at[k],
            device_id=(x, y, 1 - c), device_id_type=MESH) for k in range(4)]
        for cp in copies:
            cp.start()
        for cp in copies:
            cp.wait_recv()
        for cp in copies:
            cp.wait_send()

    return pl.pallas_call(
        body, name=name,
        out_shape=_sds((4, r, cdim), g4.dtype),
        in_specs=[pl.BlockSpec(memory_space=pl.ANY)],
        out_specs=pl.BlockSpec(memory_space=pl.ANY),
        scratch_shapes=[pltpu.SemaphoreType.DMA((4,)), pltpu.SemaphoreType.DMA((4,))],
    )(g4)


def _chip_exchange(name, part):
    _, r, cdim = part.shape

    def body(p_ref, out_ref, send_sems, recv_sems):
        x, y, c = lax.axis_index("x"), lax.axis_index("y"), lax.axis_index("c")
        copies = []
        for k in (1, 2, 3):
            px = x if (k >> 1) == 0 else 1 - x
            py = y if (k & 1) == 0 else 1 - y
            copies.append(pltpu.make_async_remote_copy(
                src_ref=p_ref.at[2 * px + py], dst_ref=out_ref.at[k - 1],
                send_sem=send_sems.at[k - 1], recv_sem=recv_sems.at[k - 1],
                device_id=(px, py, c), device_id_type=MESH))
        for cp in copies:
            cp.start()
        for cp in copies:
            cp.wait_recv()
        for cp in copies:
            cp.wait_send()

    return pl.pallas_call(
        body, name=name,
        out_shape=_sds((3, r, cdim), part.dtype),
        in_specs=[pl.BlockSpec(memory_space=pl.ANY)],
        out_specs=pl.BlockSpec(memory_space=pl.ANY),
        scratch_shapes=[pltpu.SemaphoreType.DMA((3,)), pltpu.SemaphoreType.DMA((3,))],
    )(part)


def _pair_sum(name, g4, recv, core, tr):
    _, _, r, cdim = g4.shape

    def body(core_ref, a_ref, b_ref, o_ref):
        o_ref[...] = (a_ref[...].astype(F32) + b_ref[...].astype(F32)).astype(o_ref.dtype)

    return pl.pallas_call(
        body, name=name,
        grid_spec=pltpu.PrefetchScalarGridSpec(
            num_scalar_prefetch=1, grid=(4, r // tr),
            in_specs=[pl.BlockSpec((None, None, tr, cdim), lambda k, i, cr: (k, cr[0], i, 0)),
                      pl.BlockSpec((None, tr, cdim), lambda k, i, cr: (k, i, 0))],
            out_specs=pl.BlockSpec((None, tr, cdim), lambda k, i, cr: (k, i, 0))),
        out_shape=_sds((4, r, cdim), BF16),
        compiler_params=_params("parallel", "parallel"),
    )(core, g4, recv)


def _chip_sum(name, part, recv, chip, tr):
    _, r, cdim = part.shape

    def body(chip_ref, a_ref, b_ref, o_ref):
        s = a_ref[...].astype(F32)
        for k in range(3):
            s = s + b_ref[k].astype(F32)
        o_ref[...] = s

    return pl.pallas_call(
        body, name=name,
        grid_spec=pltpu.PrefetchScalarGridSpec(
            num_scalar_prefetch=1, grid=(r // tr,),
            in_specs=[pl.BlockSpec((None, tr, cdim), lambda i, cr: (cr[0], i, 0)),
                      pl.BlockSpec((3, tr, cdim), lambda i, cr: (0, i, 0))],
            out_specs=pl.BlockSpec((tr, cdim), lambda i, cr: (i, 0))),
        out_shape=_sds((r, cdim), F32),
        compiler_params=_params("parallel"),
    )(chip, part, recv)


def _rms_cast(name, h, gain, tm):
    t = h.shape[0]

    def body(h_ref, g_ref, o_ref):
        _, xh = _rms_stats(h_ref[...])
        o_ref[...] = (xh * g_ref[...]).astype(BF16)

    return pl.pallas_call(
        body, name=name, grid=(t // tm,),
        in_specs=[_row(tm, D), _const((1, D))], out_specs=_row(tm, D),
        out_shape=_sds((t, D), BF16), compiler_params=_params("parallel"),
    )(h, gain)


def _ffn_up(name, xn, wgu_t, tm, tn):
    t = xn.shape[0]
    nh = DFF // tn

    def body(x_ref, wg_ref, wu_ref, g_ref, u_ref, a_ref):
        x = x_ref[...]
        g = _dot_nt(x, wg_ref[...])
        u = _dot_nt(x, wu_ref[...])
        g_ref[...] = g.astype(BF16)
        u_ref[...] = u.astype(BF16)
        a_ref[...] = (g * _sigmoid(g) * u).astype(BF16)

    o = pl.BlockSpec((tm, tn), lambda j, i: (i, j))
    return pl.pallas_call(
        body, name=name, grid=(nh, t // tm),
        in_specs=[pl.BlockSpec((tm, D), lambda j, i: (i, 0)),
                  pl.BlockSpec((tn, D), lambda j, i: (j, 0)),
                  pl.BlockSpec((tn, D), lambda j, i: (j + nh, 0))],
        out_specs=[o, o, o],
        out_shape=[_sds((t, DFF), BF16)] * 3,
        compiler_params=_params("parallel", "parallel"),
    )(xn, wgu_t, wgu_t)


def _ffn_down(name, a, wd, h, gain, tm):
    t = a.shape[0]
    with_norm = gain is not None

    def body(*refs):
        if with_norm:
            a_ref, w_ref, h_ref, g_ref, o_ref, n_ref = refs
        else:
            a_ref, w_ref, h_ref, o_ref = refs
        hn = h_ref[...] + 0.5 * _dot(a_ref[...], w_ref[...])
        o_ref[...] = hn
        if with_norm:
            _, xh = _rms_stats(hn)
            n_ref[...] = (xh * g_ref[...]).astype(BF16)

    ins = [_row(tm, DFF), _const((DFF, D)), _row(tm, D)]
    args = [a, wd, h]
    outs, shapes = [_row(tm, D)], [_sds((t, D), F32)]
    if with_norm:
        ins.append(_const((1, D)))
        args.append(gain)
        outs.append(_row(tm, D))
        shapes.append(_sds((t, D), BF16))
    return pl.pallas_call(
        body, name=name, grid=(t // tm,), in_specs=ins, out_specs=outs, out_shape=shapes,
        compiler_params=_params("parallel"),
    )(*args)


def _mix_in(name, n, win_t, b_in, tm, tn):
    t = n.shape[0]

    def body(n_ref, w_ref, b_ref, p_ref):
        p_ref[...] = (_dot_nt(n_ref[...], w_ref[...]) + b_ref[...]).astype(BF16)

    return pl.pallas_call(
        body, name=name, grid=(6 * D // tn, t // tm),
        in_specs=[pl.BlockSpec((tm, D), lambda j, i: (i, 0)),
                  pl.BlockSpec((tn, D), lambda j, i: (j, 0)),
                  pl.BlockSpec((1, tn), lambda j, i: (0, j))],
        out_specs=pl.BlockSpec((tm, tn), lambda j, i: (i, j)),
        out_shape=_sds((t, 6 * D), BF16),
        compiler_params=_params("parallel", "parallel"),
    )(n, win_t, b_in)


RC = 64
LANES = 128


def _shift_copies(ext, shifted, tm):
    n = tm + HALO - 8
    for m in range(1, 8):
        shifted[m - 1] = ext[pl.ds(m, n), :]


def _by_residue(offs):
    groups = {}
    for k, off in enumerate(offs):
        q, m = divmod(off, 8)
        groups.setdefault(m, []).append((k, q))
    return groups


def _residue_window(ext, shifted, m, taps, base, cs):
    src = ext if m == 0 else shifted.at[m - 1]
    return src[pl.ds(base, RC + 8 * max(q for _, q in taps)), cs]


def _tap_sum(out_ref, bias_ref, w_ref, ext, shifted, offs, tm):
    groups = _by_residue(offs)

    def chunk(j, carry):
        base = pl.multiple_of(j * RC, RC)
        for c in range(D // LANES):
            cs = pl.ds(c * LANES, LANES)
            acc = jnp.zeros((RC, LANES), F32)
            if bias_ref is not None:
                acc = acc + bias_ref[:, cs]
            for m, taps in groups.items():
                big = _residue_window(ext, shifted, m, taps, base, cs)
                for k, q in taps:
                    acc = acc + w_ref[pl.ds(k, 1), cs] * big[8 * q:8 * q + RC]
            out_ref[pl.ds(base, RC), cs] = acc
        return carry

    lax.fori_loop(0, tm // RC, chunk, 0)


def _tap_corr(dw_ref, dc_ext, ext, shifted, offs, tm):
    groups = _by_residue(offs)
    for c in range(D // LANES):
        cs = pl.ds(c * LANES, LANES)

        def chunk(j, accs, cs=cs):
            base = pl.multiple_of(j * RC, RC)
            dcv = dc_ext[pl.ds(base, RC), cs]
            out = list(accs)
            for m, taps in groups.items():
                big = _residue_window(ext, shifted, m, taps, base, cs)
                for k, q in taps:
                    prod = dcv * big[8 * q:8 * q + RC]
                    part = prod[0:8]
                    for s in range(1, RC // 8):
                        part = part + prod[8 * s:8 * s + 8]
                    out[k] = accs[k] + part
            return tuple(out)

        accs = lax.fori_loop(0, tm // RC, chunk, tuple(jnp.zeros((8, LANES), F32) for _ in offs))
        for k in range(len(offs)):
            dw_ref[pl.ds(k, 1), cs] += jnp.sum(accs[k], axis=0, keepdims=True)


def _conv_fwd(name, p, conv_w, conv_b, ln_g, ln_b, tm):
    t = p.shape[0]

    def body(av_ref, ag_ref, w_ref, cb_ref, lg_ref, lb_ref, c_ref, a_ref, ext, shifted):
        i = pl.program_id(0)

        @pl.when(i == 0)
        def _():
            ext[pl.ds(0, HALO), :] = jnp.zeros((HALO, D), F32)

        ext[pl.ds(HALO, tm), :] = av_ref[...].astype(F32) * _sigmoid(ag_ref[...].astype(F32))
        _shift_copies(ext, shifted, tm)
        _tap_sum(c_ref, cb_ref, w_ref, ext, shifted, [HALO - (CW - 1) + k for k in range(CW)], tm)
        rstd, chat = _ln_stats(c_ref[...])
        ca = chat * lg_ref[...] + lb_ref[...]
        a_ref[...] = (ca * _sigmoid(ca)).astype(BF16)
        ext[pl.ds(0, HALO), :] = ext[pl.ds(tm, HALO), :]

    return pl.pallas_call(
        body, name=name, grid=(t // tm,),
        in_specs=[_row(tm, D, 0), _row(tm, D, 1), _const((HALO, D)), _const((1, D)),
                  _const((1, D)), _const((1, D))],
        out_specs=[_row(tm, D), _row(tm, D)],
        out_shape=[_sds((t, D), F32), _sds((t, D), BF16)],
        scratch_shapes=[pltpu.VMEM((tm + HALO, D), F32), pltpu.VMEM((7, tm + HALO - 8, D), F32)],
        compiler_params=_params("arbitrary"),
    )(p, p, conv_w, conv_b, ln_g, ln_b)


def _sgu_mask():
    rows = lax.broadcasted_iota(jnp.int32, (CHUNK, CHUNK), 0)
    cols = lax.broadcasted_iota(jnp.int32, (CHUNK, CHUNK), 1)
    return cols <= rows


def _sgu_fwd(name, p, ln_g, ln_b, sgu_w, bias_full, tm):
    t = p.shape[0]

    def body(bu_ref, bv_ref, lg_ref, lb_ref, ws_ref, bias_ref, o_ref):
        mask = _sgu_mask()
        _, vhat = _ln_stats(_gelu(bv_ref[...].astype(F32)))
        vn = (vhat * lg_ref[...] + lb_ref[...]).astype(BF16)
        ub = _gelu(bu_ref[...].astype(F32))
        for g in range(GROUPS):
            wm = jnp.where(mask, ws_ref[g], 0.0).astype(BF16)
            cs = slice(g * GD, (g + 1) * GD)
            for cc in range(tm // CHUNK):
                rs = slice(cc * CHUNK, (cc + 1) * CHUNK)
                mixed = _dot(wm, vn[rs, cs]) + bias_ref[:, cs]
                o_ref[rs, cs] = (ub[rs, cs] * mixed).astype(BF16)

    return pl.pallas_call(
        body, name=name, grid=(t // tm,),
        in_specs=[_row(tm, D, 2), _row(tm, D, 3), _const((1, D)), _const((1, D)),
                  _const((GROUPS, CHUNK, CHUNK)), _const((CHUNK, D))],
        out_specs=_row(tm, D), out_shape=_sds((t, D), BF16),
        compiler_params=_params("parallel"),
    )(p, p, ln_g, ln_b, sgu_w, bias_full)


def _merge_fwd(name, act_a, act_b, p, w_a, w_b, w_out, h, gain, tm):
    t = h.shape[0]

    def body(a_ref, b_ref, ga_ref, gb_ref, wa_ref, wb_ref, wo_ref, h_ref, g_ref,
             ya_ref, yb_ref, mg_ref, ho_ref, xn_ref):
        ya = _dot(a_ref[...], wa_ref[...])
        yb = _dot(b_ref[...], wb_ref[...])
        ya_ref[...] = ya.astype(BF16)
        yb_ref[...] = yb.astype(BF16)
        merged = (_sigmoid(ga_ref[...].astype(F32)) * ya
                  + _sigmoid(gb_ref[...].astype(F32)) * yb).astype(BF16)
        mg_ref[...] = merged
        hn = h_ref[...] + _dot(merged, wo_ref[...])
        ho_ref[...] = hn
        _, xh = _rms_stats(hn)
        xn_ref[...] = (xh * g_ref[...]).astype(BF16)

    rb = _row(tm, D)
    return pl.pallas_call(
        body, name=name, grid=(t // tm,),
        in_specs=[rb, rb, _row(tm, D, 4), _row(tm, D, 5), _const((D, D)), _const((D, D)),
                  _const((D, D)), rb, _const((1, D))],
        out_specs=[rb] * 5,
        out_shape=[_sds((t, D), BF16)] * 3 + [_sds((t, D), F32), _sds((t, D), BF16)],
        compiler_params=_params("parallel"),
    )(act_a, act_b, p, p, w_a, w_b, w_out, h, gain)


def _kv_fwd(name, mem, gain, wkv_t):
    def body(m_ref, g_ref, w_ref, mn_ref, k_ref, v_ref):
        _, xh = _rms_stats(m_ref[...])
        mn = (xh * g_ref[...]).astype(BF16)
        mn_ref[...] = mn
        kv = _dot_nt(mn, w_ref[...])
        k_ref[...] = kv[:, :D].astype(BF16)
        v_ref[...] = kv[:, D:].astype(BF16)

    return pl.pallas_call(
        body, name=name,
        out_shape=[_sds((NMEM, D), BF16)] * 3,
        compiler_params=pltpu.CompilerParams(vmem_limit_bytes=VMEM_LIMIT),
    )(mem, gain, wkv_t)


def _softmax_rows(s):
    e = jnp.exp(s - jnp.max(s, axis=-1, keepdims=True))
    return e / jnp.sum(e, axis=-1, keepdims=True)


def _attn_fwd(name, xq, w_q, kb, vb, w_o, h, gain, tm):
    t = h.shape[0]
    scale = 1.0 / math.sqrt(HD)

    def body(x_ref, wq_ref, k_ref, v_ref, wo_ref, h_ref, g_ref, q_ref, o_ref, ho_ref, xn_ref):
        q_ref[...] = _dot(x_ref[...], wq_ref[...]).astype(BF16)
        for hd in range(HEADS):
            cs = slice(hd * HD, (hd + 1) * HD)
            p = _softmax_rows(_dot_nt(q_ref[:, cs], k_ref[:, cs]) * scale)
            o_ref[:, cs] = _dot(p.astype(BF16), v_ref[:, cs]).astype(BF16)
        hn = h_ref[...] + _dot(o_ref[...], wo_ref[...])
        ho_ref[...] = hn
        _, xh = _rms_stats(hn)
        xn_ref[...] = (xh * g_ref[...]).astype(BF16)

    rb = _row(tm, D)
    return pl.pallas_call(
        body, name=name, grid=(t // tm,),
        in_specs=[rb, _const((D, D)), _const((NMEM, D)), _const((NMEM, D)), _const((D, D)), rb,
                  _const((1, D))],
        out_specs=[rb] * 4,
        out_shape=[_sds((t, D), BF16), _sds((t, D), BF16), _sds((t, D), F32), _sds((t, D), BF16)],
        compiler_params=_params("parallel"),
    )(xq, w_q, kb, vb, w_o, h, gain)


def _loss_bwd(name, h, gain, target, tm):
    t = h.shape[0]
    steps = t // tm

    def body(h_ref, g_ref, t_ref, dh_ref, dhb_ref, loss_ref, dg_ref, lacc):
        i = pl.program_id(0)
        hv = h_ref[...]
        r, xh = _rms_stats(hv)
        err = xh * g_ref[...] - t_ref[...]
        _acc(lacc, i == 0, jnp.sum(err * err, axis=0, keepdims=True))
        dy = err * (1.0 / D)
        _acc(dg_ref, i == 0, jnp.sum(dy * xh, axis=0, keepdims=True))
        dxh = dy * g_ref[...]
        dh = r * (dxh - xh * jnp.mean(dxh * xh, axis=-1, keepdims=True))
        dh_ref[...] = dh
        dhb_ref[...] = dh.astype(BF16)

        @pl.when(i == steps - 1)
        def _():
            loss_ref[...] = jnp.zeros((8, 128), F32) + (0.5 / D) * jnp.sum(lacc[...])

    rb = _row(tm, D)
    return pl.pallas_call(
        body, name=name, grid=(steps,),
        in_specs=[rb, _const((1, D)), rb],
        out_specs=[rb, rb, _const((8, 128)), _const((1, D))],
        out_shape=[_sds((t, D), F32), _sds((t, D), BF16), _sds((8, 128), F32), _sds((1, D), F32)],
        scratch_shapes=[pltpu.VMEM((1, D), F32)],
        compiler_params=_params("arbitrary"),
    )(h, gain, target)


def _ffn_bwd_act(name, dhb, wd, g, u, tm, tn):
    t = dhb.shape[0]

    def body(d_ref, w_ref, g_ref, u_ref, dg_ref, du_ref):
        da = 0.5 * _dot_nt(d_ref[...], w_ref[...])
        gv = g_ref[...].astype(F32)
        sg = _sigmoid(gv)
        dg_ref[...] = (da * u_ref[...].astype(F32) * (sg * (1.0 + gv * (1.0 - sg)))).astype(BF16)
        du_ref[...] = (da * (gv * sg)).astype(BF16)

    o = pl.BlockSpec((tm, tn), lambda j, i: (i, j))
    return pl.pallas_call(
        body, name=name, grid=(DFF // tn, t // tm),
        in_specs=[pl.BlockSpec((tm, D), lambda j, i: (i, 0)),
                  pl.BlockSpec((tn, D), lambda j, i: (j, 0)), o, o],
        out_specs=[o, o], out_shape=[_sds((t, DFF), BF16)] * 2,
        compiler_params=_params("parallel", "parallel"),
    )(dhb, wd, g, u)


def _dx_rms_bwd(name, pairs, h, gain, dh_in, tm):
    t = h.shape[0]
    np_ = len(pairs)

    def body(*refs):
        a_refs = refs[:np_]
        b_refs = refs[np_:2 * np_]
        h_ref, g_ref, d_ref, o_ref, ob_ref, dg_ref = refs[2 * np_:]
        dxn = None
        for (a_ref, b_ref, pr) in zip(a_refs, b_refs, pairs):
            y = _dot_nt(a_ref[...], b_ref[...]) if pr[4] else _dot(a_ref[...], b_ref[...])
            dxn = y if dxn is None else dxn + y
        dh, dgain = _rms_bwd(dxn, h_ref[...], g_ref[...])
        _acc(dg_ref, pl.program_id(0) == 0, dgain)
        out = d_ref[...] + dh
        o_ref[...] = out
        ob_ref[...] = out.astype(BF16)

    ins, args = [], []
    for (a, b, blk, rows, tr) in pairs:
        ins.append(_row(tm, a.shape[1]))
        args.append(a)
    for (a, b, blk, rows, tr) in pairs:
        ins.append(pl.BlockSpec((rows, b.shape[1]), lambda i, _b=blk: (_b, 0)))
        args.append(b)
    rb = _row(tm, D)
    ins += [rb, _const((1, D)), rb]
    args += [h, gain, dh_in]
    return pl.pallas_call(
        body, name=name, grid=(t // tm,), in_specs=ins,
        out_specs=[rb, rb, _const((1, D))],
        out_shape=[_sds((t, D), F32), _sds((t, D), BF16), _sds((1, D), F32)],
        compiler_params=_params("arbitrary"),
    )(*args)


def _mm_tn(name, a, b, scale, tmo, tk):
    t, m = a.shape
    n = b.shape[1]
    tk = min(tk, t)
    steps = t // tk

    def body(a_ref, b_ref, o_ref, acc):
        k = pl.program_id(1)
        _acc(acc, k == 0, _dot_tn(a_ref[...], b_ref[...]))

        @pl.when(k == steps - 1)
        def _():
            o_ref[...] = (acc[...] * scale).astype(o_ref.dtype)

    return pl.pallas_call(
        body, name=name, grid=(m // tmo, steps),
        in_specs=[pl.BlockSpec((tk, tmo), lambda i, k: (k, i)),
                  pl.BlockSpec((tk, n), lambda i, k: (k, 0))],
        out_specs=pl.BlockSpec((tmo, n), lambda i, k: (i, 0)),
        out_shape=_sds((m, n), BF16),
        scratch_shapes=[pltpu.VMEM((tmo, n), F32)],
        compiler_params=_params("parallel", "arbitrary"),
    )(a, b)


def _attn_bwd(name, dhb, w_o, qb, kb, vb, tm):
    t = dhb.shape[0]
    scale = 1.0 / math.sqrt(HD)

    def body(d_ref, wo_ref, q_ref, k_ref, v_ref, dq_ref, dk_ref, dv_ref, do_s):
        i = pl.program_id(0)

        @pl.when(i == 0)
        def _():
            dk_ref[...] = jnp.zeros_like(dk_ref)
            dv_ref[...] = jnp.zeros_like(dv_ref)

        do_s[...] = _dot_nt(d_ref[...], wo_ref[...]).astype(BF16)
        for hd in range(HEADS):
            cs = slice(hd * HD, (hd + 1) * HD)
            q = q_ref[:, cs]
            p = _softmax_rows(_dot_nt(q, k_ref[:, cs]) * scale)
            do = do_s[:, cs]
            dp = _dot_nt(do, v_ref[:, cs])
            ds = (p * (dp - jnp.sum(dp * p, axis=-1, keepdims=True)) * scale).astype(BF16)
            dq_ref[:, cs] = _dot(ds, k_ref[:, cs]).astype(BF16)
            dk_ref[:, cs] += _dot_tn(ds, q)
            dv_ref[:, cs] += _dot_tn(p.astype(BF16), do)

    rb = _row(tm, D)
    return pl.pallas_call(
        body, name=name, grid=(t // tm,),
        in_specs=[rb, _const((D, D)), rb, _const((NMEM, D)), _const((NMEM, D))],
        out_specs=[rb, _const((NMEM, D)), _const((NMEM, D))],
        out_shape=[_sds((t, D), BF16), _sds((NMEM, D), F32), _sds((NMEM, D), F32)],
        scratch_shapes=[pltpu.VMEM((tm, D), BF16)],
        compiler_params=_params("arbitrary"),
    )(dhb, w_o, qb, kb, vb)


def _kv_bwd(name, dk, dv, memn, wkv_t, mem, gain):
    def body(dk_ref, dv_ref, mn_ref, w_ref, m_ref, g_ref, dw_ref, dg_ref):
        dkb = dk_ref[...].astype(BF16)
        dvb = dv_ref[...].astype(BF16)
        mn = mn_ref[...]
        dw_ref[pl.ds(0, D), :] = _dot_tn(dkb, mn).astype(BF16)
        dw_ref[pl.ds(D, D), :] = _dot_tn(dvb, mn).astype(BF16)
        dmn = _dot(dkb, w_ref[pl.ds(0, D), :]) + _dot(dvb, w_ref[pl.ds(D, D), :])
        _, xh = _rms_stats(m_ref[...])
        dg_ref[...] = jnp.sum(dmn * xh, axis=0, keepdims=True)

    return pl.pallas_call(
        body, name=name,
        out_shape=[_sds((2 * D, D), BF16), _sds((1, D), F32)],
        compiler_params=pltpu.CompilerParams(vmem_limit_bytes=VMEM_LIMIT),
    )(dk, dv, memn, wkv_t, mem, gain)


def _merge_bwd(name, dhb, w_out, ya, yb, p, tm):
    t = dhb.shape[0]

    def body(d_ref, w_ref, ya_ref, yb_ref, ga_ref, gb_ref, dya_ref, dyb_ref, dp_ref, cs_ref):
        dm = _dot_nt(d_ref[...], w_ref[...])
        sa = _sigmoid(ga_ref[...].astype(F32))
        sb = _sigmoid(gb_ref[...].astype(F32))
        dya_ref[...] = (dm * sa).astype(BF16)
        dyb_ref[...] = (dm * sb).astype(BF16)
        dga = dm * ya_ref[...].astype(F32) * (sa * (1.0 - sa))
        dgb = dm * yb_ref[...].astype(F32) * (sb * (1.0 - sb))
        dp_ref[:, pl.ds(0, D)] = dga.astype(BF16)
        dp_ref[:, pl.ds(D, D)] = dgb.astype(BF16)
        first = pl.program_id(0) == 0

        @pl.when(first)
        def _():
            cs_ref[...] = jnp.zeros_like(cs_ref)
        cs_ref[:, pl.ds(0, D)] += jnp.sum(dga, axis=0, keepdims=True)
        cs_ref[:, pl.ds(D, D)] += jnp.sum(dgb, axis=0, keepdims=True)

    rb = _row(tm, D)
    return pl.pallas_call(
        body, name=name, grid=(t // tm,),
        in_specs=[rb, _const((D, D)), rb, rb, _row(tm, D, 4), _row(tm, D, 5)],
        out_specs=[rb, rb, _row(tm, 2 * D), _const((1, 2 * D))],
        out_shape=[_sds((t, D), BF16), _sds((t, D), BF16), _sds((t, 2 * D), BF16),
                   _sds((1, 2 * D), F32)],
        compiler_params=_params("arbitrary"),
    )(dhb, w_out, ya, yb, p, p)


def _conv_bwd(name, dya, w_a, c, p, conv_w, ln_g, ln_b, tm):
    t = dya.shape[0]
    steps = t // tm
    hb = tm // HALO

    def rev(i):
        return steps - 1 - i

    def body(dy_ref, wa_ref, c_ref, av_ref, ag_ref, avh_ref, agh_ref, w_ref, lg_ref, lb_ref,
             dp_ref, cs_ref, dw_ref, dcb_ref, dlg_ref, dlb_ref, dc_ext, a_ext, dc_sh, a_sh, da0_s):
        i = pl.program_id(0)
        first = i == 0

        @pl.when(first)
        def _():
            dc_ext[pl.ds(tm, HALO), :] = jnp.zeros((HALO, D), F32)
            dw_ref[...] = jnp.zeros_like(dw_ref)
            cs_ref[...] = jnp.zeros_like(cs_ref)

        d_act = _dot_nt(dy_ref[...], wa_ref[...])
        rstd, chat = _ln_stats(c_ref[...])
        ca = chat * lg_ref[...] + lb_ref[...]
        sc = _sigmoid(ca)
        dca = d_act * (sc * (1.0 + ca * (1.0 - sc)))
        _acc(dlg_ref, first, jnp.sum(dca * chat, axis=0, keepdims=True))
        _acc(dlb_ref, first, jnp.sum(dca, axis=0, keepdims=True))
        dc = _ln_bwd(dca, chat, rstd, lg_ref[...])
        _acc(dcb_ref, first, jnp.sum(dc, axis=0, keepdims=True))
        dc_ext[pl.ds(0, tm), :] = dc

        av = av_ref[...].astype(F32)
        sg = _sigmoid(ag_ref[...].astype(F32))
        a_ext[pl.ds(HALO, tm), :] = av * sg
        halo = avh_ref[...].astype(F32) * _sigmoid(agh_ref[...].astype(F32))
        a_ext[pl.ds(0, HALO), :] = jnp.where(i == steps - 1, 0.0, halo)

        _shift_copies(dc_ext, dc_sh, tm)
        _shift_copies(a_ext, a_sh, tm)
        _tap_sum(da0_s, None, w_ref, dc_ext, dc_sh, [CW - 1 - k for k in range(CW)], tm)
        _tap_corr(dw_ref, dc_ext, a_ext, a_sh, [HALO - (CW - 1) + k for k in range(CW)], tm)
        da0 = da0_s[...]
        dav = da0 * sg
        dag = da0 * av * (sg * (1.0 - sg))
        dp_ref[:, pl.ds(0, D)] = dav.astype(BF16)
        dp_ref[:, pl.ds(D, D)] = dag.astype(BF16)
        cs_ref[:, pl.ds(0, D)] += jnp.sum(dav, axis=0, keepdims=True)
        cs_ref[:, pl.ds(D, D)] += jnp.sum(dag, axis=0, keepdims=True)
        dc_ext[pl.ds(tm, HALO), :] = dc_ext[pl.ds(0, HALO), :]

    def rrow(cols, cb=0):
        return pl.BlockSpec((tm, cols), lambda i, _cb=cb: (rev(i), _cb))

    def halo_spec(cb):
        return pl.BlockSpec((HALO, D), lambda i, _cb=cb: (jnp.maximum(rev(i) * hb - 1, 0), _cb))

    return pl.pallas_call(
        body, name=name, grid=(steps,),
        in_specs=[rrow(D), _const((D, D)), rrow(D), rrow(D, 0), rrow(D, 1), halo_spec(0),
                  halo_spec(1), _const((HALO, D)), _const((1, D)), _const((1, D))],
        out_specs=[rrow(2 * D), _const((1, 2 * D)), _const((HALO, D)), _const((1, D)),
                   _const((1, D)), _const((1, D))],
        out_shape=[_sds((t, 2 * D), BF16), _sds((1, 2 * D), F32), _sds((HALO, D), F32),
                   _sds((1, D), F32), _sds((1, D), F32), _sds((1, D), F32)],
        scratch_shapes=[pltpu.VMEM((tm + HALO, D), F32), pltpu.VMEM((tm + HALO, D), F32),
                        pltpu.VMEM((7, tm + HALO - 8, D), F32), pltpu.VMEM((7, tm + HALO - 8, D), F32),
                        pltpu.VMEM((tm, D), F32)],
        compiler_params=_params("arbitrary"),
    )(dya, w_a, c, p, p, p, p, conv_w, ln_g, ln_b)


def _sgu_bwd(name, dyb, w_b, p, ln_g, ln_b, sgu_w, bias_full, tm):
    t = dyb.shape[0]
    steps = t // tm

    def body(dy_ref, wb_ref, bu_ref, bv_ref, lg_ref, lb_ref, ws_ref, bias_ref,
             dp_ref, cs_ref, dws_ref, dsb_ref, dlg_ref, dlb_ref, dub_s, dvn_s, dbias_s):
        i = pl.program_id(0)
        first = i == 0
        mask = _sgu_mask()

        @pl.when(first)
        def _():
            dws_ref[...] = jnp.zeros_like(dws_ref)
            dbias_s[...] = jnp.zeros_like(dbias_s)
            cs_ref[...] = jnp.zeros_like(cs_ref)

        dob = _dot_nt(dy_ref[...], wb_ref[...])
        bu = bu_ref[...].astype(F32)
        bv = bv_ref[...].astype(F32)
        ub = _gelu(bu)
        rstd, vhat = _ln_stats(_gelu(bv))
        vn = (vhat * lg_ref[...] + lb_ref[...]).astype(BF16)
        for g in range(GROUPS):
            wm = jnp.where(mask, ws_ref[g], 0.0).astype(BF16)
            cs = slice(g * GD, (g + 1) * GD)
            for cc in range(tm // CHUNK):
                rs = slice(cc * CHUNK, (cc + 1) * CHUNK)
                vblk = vn[rs, cs]
                mixed = _dot(wm, vblk) + bias_ref[:, cs]
                dob_blk = dob[rs, cs]
                dub_s[rs, cs] = dob_blk * mixed
                dmixed = dob_blk * ub[rs, cs]
                dbias_s[:, cs] += dmixed
                dmb = dmixed.astype(BF16)
                dws_ref[g] += _dot_nt(dmb, vblk)
                dvn_s[rs, cs] = _dot_tn(wm, dmb)
        dbu = dub_s[...] * _gelu_grad(bu)
        dvn = dvn_s[...]
        _acc(dlg_ref, first, jnp.sum(dvn * vhat, axis=0, keepdims=True))
        _acc(dlb_ref, first, jnp.sum(dvn, axis=0, keepdims=True))
        dbv = _ln_bwd(dvn, vhat, rstd, lg_ref[...]) * _gelu_grad(bv)
        dp_ref[:, pl.ds(0, D)] = dbu.astype(BF16)
        dp_ref[:, pl.ds(D, D)] = dbv.astype(BF16)
        cs_ref[:, pl.ds(0, D)] += jnp.sum(dbu, axis=0, keepdims=True)
        cs_ref[:, pl.ds(D, D)] += jnp.sum(dbv, axis=0, keepdims=True)

        @pl.when(i == steps - 1)
        def _():
            lane = lax.broadcasted_iota(jnp.int32, (CHUNK, CHUNK), 1)
            dsb = jnp.zeros((CHUNK, CHUNK), F32)
            for g in range(GROUPS):
                dws_ref[g] = jnp.where(mask, dws_ref[g], 0.0)
                dsb = jnp.where(lane == g, jnp.sum(dbias_s[:, g * GD:(g + 1) * GD], axis=1, keepdims=True), dsb)
            dsb_ref[...] = dsb

    rb = _row(tm, D)
    return pl.pallas_call(
        body, name=name, grid=(steps,),
        in_specs=[rb, _const((D, D)), _row(tm, D, 2), _row(tm, D, 3), _const((1, D)), _const((1, D)),
                  _const((GROUPS, CHUNK, CHUNK)), _const((CHUNK, D))],
        out_specs=[_row(tm, 2 * D), _const((1, 2 * D)), _const((GROUPS, CHUNK, CHUNK)),
                   _const((CHUNK, CHUNK)), _const((1, D)), _const((1, D))],
        out_shape=[_sds((t, 2 * D), BF16), _sds((1, 2 * D), F32), _sds((GROUPS, CHUNK, CHUNK), F32),
                   _sds((CHUNK, CHUNK), F32), _sds((1, D), F32), _sds((1, D), F32)],
        scratch_shapes=[pltpu.VMEM((tm, D), F32), pltpu.VMEM((tm, D), F32), pltpu.VMEM((CHUNK, D), F32)],
        compiler_params=_params("arbitrary"),
    )(dyb, w_b, p, p, ln_g, ln_b, sgu_w, bias_full)


def _adam_math(w, g, m, v):
    m = B1 * m + (1.0 - B1) * g
    v = B2 * v + (1.0 - B2) * (g * g)
    m_hat = m / (1.0 - B1 ** STEP)
    v_hat = v / (1.0 - B2 ** STEP)
    delta = -LR * (m_hat / (jnp.sqrt(v_hat) + EPS_ADAM) + WD * w)
    return delta, m, v


def _adamw(name, w, g, m, v, tr):
    r, cdim = w.shape

    def body(w_ref, g_ref, m_ref, v_ref, d_ref, mo_ref, vo_ref):
        d, mn, vn = _adam_math(w_ref[...], g_ref[...], m_ref[...], v_ref[...])
        d_ref[...] = d
        mo_ref[...] = mn
        vo_ref[...] = vn

    blk = pl.BlockSpec((tr, cdim), lambda i: (i, 0))
    return pl.pallas_call(
        body, name=name, grid=(r // tr,), in_specs=[blk] * 4, out_specs=[blk] * 3,
        out_shape=[_sds((r, cdim), F32)] * 3, compiler_params=_params("parallel"),
    )(w, g, m, v)


def _adamw_small(name, w, g8, m, v):
    r, cdim = w.shape

    def body(w_ref, g_ref, m_ref, v_ref, go_ref, d_ref, mo_ref, vo_ref):
        g = g_ref[0]
        for k in range(1, NDEV):
            g = g + g_ref[k]
        go_ref[...] = g
        d, mn, vn = _adam_math(w_ref[...], g, m_ref[...], v_ref[...])
        d_ref[...] = d
        mo_ref[...] = mn
        vo_ref[...] = vn

    return pl.pallas_call(
        body, name=name, out_shape=[_sds((r, cdim), F32)] * 4,
        compiler_params=pltpu.CompilerParams(vmem_limit_bytes=VMEM_LIMIT),
    )(w, g8, m, v)


_BIG = [("ffn1_w_gu", 704, True), ("ffn1_w_down", 352, False), ("w_in", 768, True),
        ("w_a_out", 128, False), ("w_b_out", 128, False), ("w_out", 128, False),
        ("w_q", 128, False), ("w_kv", 256, True), ("w_o", 128, False),
        ("ffn2_w_gu", 704, True), ("ffn2_w_down", 352, False)]
_BIG_ROWS = sum(r for _, r, _ in _BIG)

_SMALL = [("ffn1_norm", 1), ("mix_norm", 1), ("b_in", 6), ("conv_w", HALO), ("conv_b", 1),
          ("conv_ln_g", 1), ("conv_ln_b", 1), ("sgu_ln_g", 1), ("sgu_ln_b", 1), ("sgu_w", 64),
          ("sgu_b", 1), ("xattn_norm", 1), ("mem_norm", 1), ("ffn2_norm", 1), ("final_norm", 1)]
_SMALL_ROWS = 120


def _pack_small(vals, my_dev):
    rows = []
    for name, nrows in _SMALL:
        a = vals[name].astype(F32)
        if name == "conv_w":
            if a.shape[-1] != D:
                slab = jnp.zeros((HALO, D), F32)
                a = lax.dynamic_update_slice(slab, jnp.pad(a.reshape(CW, -1), ((0, HALO - CW), (0, 0))),
                                             (0, my_dev * (D // NDEV)))
            else:
                a = jnp.pad(a.reshape(CW, D), ((0, HALO - CW), (0, 0)))
        elif name == "sgu_b":
            a = jnp.pad(a.reshape(1, -1), ((0, 0), (0, D - GROUPS * CHUNK)))
        else:
            a = a.reshape(nrows, D)
        rows.append(a)
    packed = jnp.concatenate(rows, axis=0)
    return jnp.pad(packed, ((0, _SMALL_ROWS - packed.shape[0]), (0, 0)))


def _unpack_small(packed, shapes, my_dev):
    out, off = {}, 0
    for name, nrows in _SMALL:
        a = packed[off:off + nrows]
        off += nrows
        if name == "conv_w":
            a = lax.dynamic_slice(a, (0, my_dev * (D // NDEV)), (CW, D // NDEV))
        elif name == "sgu_b":
            a = a[:, :GROUPS * CHUNK]
        out[name] = a.reshape(shapes[name])
    return out


def kernel(x, mem, ffn1_norm, ffn1_w_gu, ffn1_w_down, mix_norm, w_in, b_in, conv_w, conv_b, conv_ln_g, conv_ln_b, w_a_out, sgu_ln_g, sgu_ln_b, sgu_w, sgu_b, w_b_out, w_out, xattn_norm, mem_norm, w_q, w_kv, w_o, ffn2_norm, ffn2_w_gu, ffn2_w_down, final_norm, loss_target, m_ffn1_norm, m_ffn1_w_gu, m_ffn1_w_down, m_mix_norm, m_w_in, m_b_in, m_conv_w, m_conv_b, m_conv_ln_g, m_conv_ln_b, m_w_a_out, m_sgu_ln_g, m_sgu_ln_b, m_sgu_w, m_sgu_b, m_w_b_out, m_w_out, m_xattn_norm, m_mem_norm, m_w_q, m_w_kv, m_w_o, m_ffn2_norm, m_ffn2_w_gu, m_ffn2_w_down, m_final_norm, v_ffn1_norm, v_ffn1_w_gu, v_ffn1_w_down, v_mix_norm, v_w_in, v_b_in, v_conv_w, v_conv_b, v_conv_ln_g, v_conv_ln_b, v_w_a_out, v_sgu_ln_g, v_sgu_ln_b, v_sgu_w, v_sgu_b, v_w_b_out, v_w_out, v_xattn_norm, v_mem_norm, v_w_q, v_w_kv, v_w_o, v_ffn2_norm, v_ffn2_w_gu, v_ffn2_w_down, v_final_norm):
    env = dict(locals())
    names = [n for n, _, _ in _BIG] + [n for n, _ in _SMALL]
    w = {n: env[n] for n in names}
    mom = {n: env["m_" + n] for n in names}
    vel = {n: env["v_" + n] for n in names}

    ax, ay, ac = lax.axis_index("x"), lax.axis_index("y"), lax.axis_index("c")
    my_chip = 2 * ax + ay
    my_dev = 2 * my_chip + ac

    t = x.shape[1]
    tm = min(512, t)
    tm_s = min(256, t)
    xs = x.reshape(t, D)
    tgt = loss_target.reshape(t, D)
    mem2 = mem.reshape(NMEM, D)

    shards = []
    for name, rows, transposed in _BIG:
        a = w[name][0]
        shards.append((a.T if transposed else a).astype(BF16))
    full = _all_gather("gather_weights", jnp.concatenate(shards, axis=0))
    wf, off = {}, 0
    for name, rows, _ in _BIG:
        wf[name] = full[:, off:off + rows, :].reshape(NDEV * rows, D)
        off += rows

    conv_slab = lax.dynamic_update_slice(
        jnp.zeros((HALO, D), F32), jnp.pad(conv_w[0], ((0, HALO - CW), (0, 0))), (0, my_dev * (D // NDEV)))
    conv_w_pad = jnp.sum(_all_gather("gather_conv_w", conv_slab), axis=0)
    bias_full = jnp.repeat(sgu_b[0].T, GD, axis=1)
    b_in2 = b_in.reshape(1, 6 * D)

    xn1 = _rms_cast("norm_x", xs, ffn1_norm, tm)
    g1, u1, a1 = _ffn_up("ffn1_up", xn1, wf["ffn1_w_gu"], tm, 1408)
    h1, n_mix = _ffn_down("ffn1_down", a1, wf["ffn1_w_down"], xs, mix_norm, tm)
    p = _mix_in("mix_in", n_mix, wf["w_in"], b_in2, tm, 1536)
    c_conv, act_a = _conv_fwd("conv_fwd", p, conv_w_pad, conv_b, conv_ln_g, conv_ln_b, tm_s)
    act_b = _sgu_fwd("sgu_fwd", p, sgu_ln_g, sgu_ln_b, sgu_w[0], bias_full, tm_s)
    ya, yb, merged, h2, xq = _merge_fwd("merge_fwd", act_a, act_b, p, wf["w_a_out"], wf["w_b_out"],
                                        wf["w_out"], h1, xattn_norm, tm_s)
    memn, kb, vb = _kv_fwd("kv_fwd", mem2, mem_norm, wf["w_kv"])
    qb, ob, h3, xn4 = _attn_fwd("attn_fwd", xq, wf["w_q"], kb, vb, wf["w_o"], h2, ffn2_norm, tm_s)
    g2, u2, a2 = _ffn_up("ffn2_up", xn4, wf["ffn2_w_gu"], tm, 1408)
    (h4,) = _ffn_down("ffn2_down", a2, wf["ffn2_w_down"], h3, None, tm)
    dh4, dh4b, loss_blk, d_final = _loss_bwd("loss_bwd", h4, final_norm.reshape(1, D), tgt, tm)

    gb = {}
    gs = {}
    gs["final_norm"] = d_final

    def ffn_bwd(tag, dhb, dh, g, u, a, xn, h_in, gain, wgu_t, wd):
        dg, du = _ffn_bwd_act(tag + "_bwd_act", dhb, wd, g, u, tm, 1408)
        gwd = _mm_tn(tag + "_dw_down", a, dhb, 0.5, 1408, TK)
        gwg = _mm_tn(tag + "_dw_gate", dg, xn, 1.0, 1408, TK)
        gwu = _mm_tn(tag + "_dw_up", du, xn, 1.0, 1408, TK)
        dh_o, dhb_o, dgain = _dx_rms_bwd(
            tag + "_bwd_dx", [(dg, wgu_t, 0, DFF, False), (du, wgu_t, 1, DFF, False)], h_in, gain, dh, tm_s)
        return dh_o, dhb_o, dgain, jnp.concatenate([gwg, gwu], axis=0), gwd

    dh3, dh3b, gs["ffn2_norm"], gb["ffn2_w_gu"], gb["ffn2_w_down"] = ffn_bwd(
        "ffn2", dh4b, dh4, g2, u2, a2, xn4, h3, ffn2_norm, wf["ffn2_w_gu"], wf["ffn2_w_down"])

    gb["w_o"] = _mm_tn("dw_o", ob, dh3b, 1.0, 1024, TK)
    dq, dk, dv = _attn_bwd("attn_bwd", dh3b, wf["w_o"], qb, kb, vb, tm_s)
    gb["w_kv"], gs["mem_norm"] = _kv_bwd("kv_bwd", dk, dv, memn, wf["w_kv"], mem2, mem_norm)
    gb["w_q"] = _mm_tn("dw_q", xq, dq, 1.0, 1024, TK)
    dh2, dh2b, gs["xattn_norm"] = _dx_rms_bwd(
        "attn_bwd_dx", [(dq, wf["w_q"], 0, D, True)], h2, xattn_norm, dh3, tm_s)

    gb["w_out"] = _mm_tn("dw_out", merged, dh2b, 1.0, 1024, TK)
    dya, dyb, dp_g, cs_g = _merge_bwd("merge_bwd", dh2b, wf["w_out"], ya, yb, p, tm_s)
    gb["w_a_out"] = _mm_tn("dw_a", act_a, dya, 1.0, 1024, TK)
    gb["w_b_out"] = _mm_tn("dw_b", act_b, dyb, 1.0, 1024, TK)
    dp_a, cs_a, d_convw, gs["conv_b"], gs["conv_ln_g"], gs["conv_ln_b"] = _conv_bwd(
        "conv_bwd", dya, wf["w_a_out"], c_conv, p, conv_w_pad, conv_ln_g, conv_ln_b, tm_s)
    dp_b, cs_b, d_sguw, d_sgub, gs["sgu_ln_g"], gs["sgu_ln_b"] = _sgu_bwd(
        "sgu_bwd", dyb, wf["w_b_out"], p, sgu_ln_g, sgu_ln_b, sgu_w[0], bias_full, tm_s)
    gs["conv_w"] = d_convw[:CW].reshape(1, CW, D)
    gs["sgu_w"] = d_sguw
    gs["sgu_b"] = d_sgub[:, :GROUPS].T
    gs["b_in"] = jnp.concatenate([cs_a, cs_b, cs_g], axis=1)
    gb["w_in"] = jnp.concatenate([_mm_tn("dw_in_a", dp_a, n_mix, 1.0, 1024, TK),
                                  _mm_tn("dw_in_b", dp_b, n_mix, 1.0, 1024, TK),
                                  _mm_tn("dw_in_g", dp_g, n_mix, 1.0, 1024, TK)], axis=0)
    dh1, dh1b, gs["mix_norm"] = _dx_rms_bwd(
        "mix_bwd_dx", [(dp_a, wf["w_in"], 0, 2 * D, False), (dp_b, wf["w_in"], 1, 2 * D, False),
                       (dp_g, wf["w_in"], 2, 2 * D, False)], h1, mix_norm, dh2, tm_s)

    dx, _, gs["ffn1_norm"], gb["ffn1_w_gu"], gb["ffn1_w_down"] = ffn_bwd(
        "ffn1", dh1b, dh1, g1, u1, a1, xn1, xs, ffn1_norm, wf["ffn1_w_gu"], wf["ffn1_w_down"])

    g4 = jnp.concatenate([gb[name].reshape(4, 2, rows, D) for name, rows, _ in _BIG], axis=2)
    core = ac.astype(jnp.int32).reshape(1)
    chip = my_chip.astype(jnp.int32).reshape(1)
    from_sibling = _pair_exchange("grads_pair_exchange", g4)
    part = _pair_sum("grads_pair_sum", g4, from_sibling, core, 944)
    from_chips = _chip_exchange("grads_chip_exchange", part)
    gsum = _chip_sum("grads_chip_sum", part, from_chips, chip, 944)

    grads, deltas, new_m, new_v = {}, {}, {}, {}
    off = 0
    for name, rows, transposed in _BIG:
        gsh = gsum[off:off + rows]
        off += rows
        gsh = gsh.T if transposed else gsh
        d, mo, vo = _adamw("adamw_" + name, w[name][0], gsh, mom[name][0], vel[name][0], gsh.shape[0] // 2)
        grads[name], deltas[name], new_m[name], new_v[name] = gsh[None], d[None], mo[None], vo[None]

    shapes = {n: w[n].shape for n, _ in _SMALL}
    g8 = _all_gather("gather_small_grads", _pack_small(gs, my_dev))
    w_small = dict(w)
    m_small = dict(mom)
    v_small = dict(vel)
    sg, sd, sm, sv = _adamw_small("adamw_small", _pack_small(w_small, my_dev), g8,
                                  _pack_small(m_small, my_dev), _pack_small(v_small, my_dev))
    for dst, src in ((grads, sg), (deltas, sd), (new_m, sm), (new_v, sv)):
        dst.update(_unpack_small(src, shapes, my_dev))

    loss = lax.psum(loss_blk[0, 0], AXES)
    order = ["ffn1_norm", "ffn1_w_gu", "ffn1_w_down", "mix_norm", "w_in", "b_in", "conv_w", "conv_b",
             "conv_ln_g", "conv_ln_b", "w_a_out", "sgu_ln_g", "sgu_ln_b", "sgu_w", "sgu_b", "w_b_out",
             "w_out", "xattn_norm", "mem_norm", "w_q", "w_kv", "w_o", "ffn2_norm", "ffn2_w_gu",
             "ffn2_w_down", "final_norm"]
    return (loss, dx.reshape(x.shape), *[grads[n] for n in order], *[deltas[n] for n in order],
            *[new_m[n] for n in order], *[new_v[n] for n in order])
```

```python
import functools
import math

import jax
import jax.numpy as jnp
from jax import lax
from jax.experimental import pallas as pl
from jax.experimental.pallas import tpu as pltpu

F32 = jnp.float32
BF16 = jnp.bfloat16
MESH = pl.DeviceIdType.MESH
AXES = ("x", "y", "c")

D = 1024
DFF = 2816
NMEM = 256
HEADS = 4
HD = D // HEADS
CW = 31
HALO = 32
CHUNK = 128
GROUPS = 4
GD = D // GROUPS
EPS_RMS = 1e-6
EPS_LN = 1e-5
LR, B1, B2, EPS_ADAM, WD, STEP = 0.001, 0.9, 0.999, 1e-08, 0.01, 10
NDEV = 8
VMEM_LIMIT = 56 * 1024 * 1024
TK = 2048


def _params(*sem):
    return pltpu.CompilerParams(dimension_semantics=sem, vmem_limit_bytes=VMEM_LIMIT)


def _dot(a, b):
    return jnp.dot(a, b, preferred_element_type=F32)


def _dot_nt(a, b):
    return lax.dot_general(a, b, (((1,), (1,)), ((), ())), preferred_element_type=F32)


def _dot_tn(a, b):
    return lax.dot_general(a, b, (((0,), (0,)), ((), ())), preferred_element_type=F32)


def _sigmoid(x):
    return 0.5 * jnp.tanh(0.5 * x) + 0.5


_GELU_C = math.sqrt(2.0 / math.pi)


def _gelu(x):
    return 0.5 * x * (1.0 + jnp.tanh(_GELU_C * (x + 0.044715 * (x * x * x))))


def _gelu_grad(x):
    t = jnp.tanh(_GELU_C * (x + 0.044715 * (x * x * x)))
    return 0.5 * (1.0 + t) + 0.5 * x * (1.0 - t * t) * (_GELU_C * (1.0 + 3.0 * 0.044715 * x * x))


def _rms_stats(h):
    r = lax.rsqrt(jnp.mean(h * h, axis=-1, keepdims=True) + EPS_RMS)
    return r, h * r


def _rms_bwd(dxn, h, gain):
    r, xh = _rms_stats(h)
    dgain = jnp.sum(dxn * xh, axis=0, keepdims=True)
    dxh = dxn * gain
    dh = r * (dxh - xh * jnp.mean(dxh * xh, axis=-1, keepdims=True))
    return dh, dgain


def _ln_stats(c):
    mu = jnp.mean(c, axis=-1, keepdims=True)
    xc = c - mu
    rstd = lax.rsqrt(jnp.mean(xc * xc, axis=-1, keepdims=True) + EPS_LN)
    return rstd, xc * rstd


def _ln_bwd(dy, xhat, rstd, g):
    dxh = dy * g
    return rstd * (dxh - jnp.mean(dxh, axis=-1, keepdims=True)
                   - xhat * jnp.mean(dxh * xhat, axis=-1, keepdims=True))


def _row(tm, cols, cb=0):
    return pl.BlockSpec((tm, cols), lambda i, _cb=cb: (i, _cb))


def _const(shape):
    n = len(shape)
    return pl.BlockSpec(shape, lambda *_: (0,) * n)


def _sds(shape, dtype):
    return jax.ShapeDtypeStruct(shape, dtype)


def _acc(ref, first, val):
    @pl.when(first)
    def _():
        ref[...] = jnp.zeros_like(ref)
    ref[...] += val


class _Comm:
    def __init__(self, args, out_shapes, scratch, start, finish):
        self.args, self.out_shapes, self.scratch = args, out_shapes, scratch
        self.start, self.finish = start, finish


_ANY = pl.BlockSpec(memory_space=pl.ANY)


def _run_comm(name, comm):
    ni, no = len(comm.args), len(comm.out_shapes)

    def body(*refs):
        ins, outs, sems = refs[:ni], refs[ni:ni + no], refs[ni + no:]
        comm.start(ins, outs, sems)
        comm.finish(ins, outs, sems)

    return pl.pallas_call(
        body, name=name, out_shape=list(comm.out_shapes), in_specs=[_ANY] * ni, out_specs=[_ANY] * no,
        scratch_shapes=list(comm.scratch),
    )(*comm.args)


def _call(name, body, grid, in_specs, out_specs, out_shape, args, scratch=(), sem=None, hosted=None):
    n_in, n_out, n_scr = len(in_specs), len(out_specs), len(scratch)
    if hosted is None:
        outs = pl.pallas_call(
            body, name=name, grid=grid, in_specs=list(in_specs), out_specs=list(out_specs),
            out_shape=list(out_shape), scratch_shapes=list(scratch), compiler_params=_params(*sem),
        )(*args)
        return outs, []
    hi, ho = len(hosted.args), len(hosted.out_shapes)

    def wrapped(*refs):
        ins, h_in = refs[:n_in], refs[n_in:n_in + hi]
        o0 = n_in + hi
        outs, h_out = refs[o0:o0 + n_out], refs[o0 + n_out:o0 + n_out + ho]
        s0 = o0 + n_out + ho
        scr, h_sems = refs[s0:s0 + n_scr], refs[s0 + n_scr:]
        ids = [pl.program_id(a) for a in range(len(grid))]
        first = functools.reduce(jnp.logical_and, [i == 0 for i in ids])
        last = functools.reduce(jnp.logical_and, [i == g - 1 for i, g in zip(ids, grid)])

        @pl.when(first)
        def _():
            hosted.start(h_in, h_out, h_sems)

        body(*ins, *outs, *scr)

        @pl.when(last)
        def _():
            hosted.finish(h_in, h_out, h_sems)

    res = pl.pallas_call(
        wrapped, name=name, grid=grid, in_specs=list(in_specs) + [_ANY] * hi,
        out_specs=list(out_specs) + [_ANY] * ho, out_shape=list(out_shape) + list(hosted.out_shapes),
        scratch_shapes=list(scratch) + list(hosted.scratch),
        compiler_params=_params(*(["arbitrary"] * len(grid))),
    )(*args, *hosted.args)
    return res[:n_out], res[n_out:]


def _gather_comm(blk):
    r, cdim = blk.shape

    def copies(x_ref, out_ref, send_sems, recv_sems, local_sem):
        x, y, c = lax.axis_index("x"), lax.axis_index("y"), lax.axis_index("c")
        me, sibling = (x, y, c), (x, y, 1 - c)
        chips = [(1 - x, y), (x, 1 - y), (1 - x, 1 - y)]

        def slot(px, py, pc):
            return out_ref.at[4 * px + 2 * py + pc]

        def copy(k, block, to, src=None):
            return pltpu.make_async_remote_copy(
                src_ref=slot(*block) if src is None else src, dst_ref=slot(*block),
                send_sem=send_sems.at[k], recv_sem=recv_sems.at[k],
                device_id=to, device_id_type=MESH)

        mine = pltpu.make_async_copy(x_ref, slot(*me), local_sem)
        first = [copy(0, me, sibling, src=x_ref)]
        first += [copy(1 + j, me, (*chip, c), src=x_ref) for j, chip in enumerate(chips)]
        passed = [copy(4 + j, (*chip, c), sibling) for j, chip in enumerate(chips)]
        landed = [copy(1 + j, (*chip, c), me) for j, chip in enumerate(chips)]
        from_sibling = [copy(0, sibling, me)] + [copy(4 + j, (*chip, 1 - c), me) for j, chip in enumerate(chips)]
        return mine, first, passed, landed, from_sibling

    def start(ins, outs, sems):
        mine, first, _, _, _ = copies(ins[0], outs[0], *sems)
        mine.start()
        for cp in first:
            cp.start()

    def finish(ins, outs, sems):
        mine, first, passed, landed, from_sibling = copies(ins[0], outs[0], *sems)
        for arrived, forward in zip(landed, passed):
            arrived.wait_recv()
            forward.start()
        for cp in from_sibling:
            cp.wait_recv()
        for cp in first + passed:
            cp.wait_send()
        mine.wait()

    return _Comm([blk], [_sds((NDEV, r, cdim), blk.dtype)],
                 [pltpu.SemaphoreType.DMA((7,)), pltpu.SemaphoreType.DMA((7,)), pltpu.SemaphoreType.DMA],
                 start, finish)


def _exchange_comm(src, n, out_rows, make):
    r, cdim = out_rows

    def copies(src_ref, out_ref, send_sems, recv_sems):
        out = []
        for k in range(n):
            s, d, to = make(k, src_ref, out_ref)
            out.append(pltpu.make_async_remote_copy(
                src_ref=s, dst_ref=d, send_sem=send_sems.at[k], recv_sem=recv_sems.at[k],
                device_id=to, device_id_type=MESH))
        return out

    def start(ins, outs, sems):
        for cp in copies(ins[0], outs[0], *sems):
            cp.start()

    def finish(ins, outs, sems):
        cps = copies(ins[0], outs[0], *sems)
        for cp in cps:
            cp.wait_recv()
        for cp in cps:
            cp.wait_send()

    return _Comm([src], [_sds((n, r, cdim), src.dtype)],
                 [pltpu.SemaphoreType.DMA((n,)), pltpu.SemaphoreType.DMA((n,))], start, finish)


def _pair_exchange_comm(g4):
    def make(k, g_ref, out_ref):
        x, y, c = lax.axis_index("x"), lax.axis_index("y"), lax.axis_index("c")
        return g_ref.at[k, 1 - c], out_ref.at[k], (x, y, 1 - c)

    return _exchange_comm(g4, 4, g4.shape[2:], make)


def _chip_exchange_comm(part):
    def make(k, p_ref, out_ref):
        x, y, c = lax.axis_index("x"), lax.axis_index("y"), lax.axis_index("c")
        px = x if ((k + 1) >> 1) == 0 else 1 - x
        py = y if ((k + 1) & 1) == 0 else 1 - y
        return p_ref.at[2 * px + py], out_ref.at[k], (px, py, c)

    return _exchange_comm(part, 3, part.shape[1:], make)


def _pair_sum(name, g4, recv, core, tr):
    _, _, r, cdim = g4.shape

    def body(core_ref, a_ref, b_ref, o_ref):
        o_ref[...] = (a_ref[...].astype(F32) + b_ref[...].astype(F32)).astype(o_ref.dtype)

    return pl.pallas_call(
        body, name=name,
        grid_spec=pltpu.PrefetchScalarGridSpec(
            num_scalar_prefetch=1, grid=(4, r // tr),
            in_specs=[pl.BlockSpec((None, None, tr, cdim), lambda k, i, cr: (k, cr[0], i, 0)),
                      pl.BlockSpec((None, tr, cdim), lambda k, i, cr: (k, i, 0))],
            out_specs=pl.BlockSpec((None, tr, cdim), lambda k, i, cr: (k, i, 0))),
        out_shape=_sds((4, r, cdim), BF16),
        compiler_params=_params("parallel", "parallel"),
    )(core, g4, recv)


def _chip_sum(name, part, recv, chip, tr):
    _, r, cdim = part.shape

    def body(chip_ref, a_ref, b_ref, o_ref):
        s = a_ref[...].astype(F32)
        for k in range(3):
            s = s + b_ref[k].astype(F32)
        o_ref[...] = s

    return pl.pallas_call(
        body, name=name,
        grid_spec=pltpu.PrefetchScalarGridSpec(
            num_scalar_prefetch=1, grid=(r // tr,),
            in_specs=[pl.BlockSpec((None, tr, cdim), lambda i, cr: (cr[0], i, 0)),
                      pl.BlockSpec((3, tr, cdim), lambda i, cr: (0, i, 0))],
            out_specs=pl.BlockSpec((tr, cdim), lambda i, cr: (i, 0))),
        out_shape=_sds((r, cdim), F32),
        compiler_params=_params("parallel"),
    )(chip, part, recv)


def _rms_cast(name, h, gain, tm, hosted=None):
    t = h.shape[0]

    def body(h_ref, g_ref, o_ref):
        _, xh = _rms_stats(h_ref[...])
        o_ref[...] = (xh * g_ref[...]).astype(BF16)

    return _call(name, body, (t // tm,), [_row(tm, D), _const((1, D))], [_row(tm, D)],
                 [_sds((t, D), BF16)], (h, gain), sem=("parallel",), hosted=hosted)


def _ffn_up(name, xn, wgu_t, tm, tn, hosted=None):
    t = xn.shape[0]
    nh = DFF // tn

    def body(x_ref, wg_ref, wu_ref, g_ref, u_ref, a_ref):
        x = x_ref[...]
        g = _dot_nt(x, wg_ref[...])
        u = _dot_nt(x, wu_ref[...])
        g_ref[...] = g.astype(BF16)
        u_ref[...] = u.astype(BF16)
        a_ref[...] = (g * _sigmoid(g) * u).astype(BF16)

    o = pl.BlockSpec((tm, tn), lambda j, i: (i, j))
    return _call(name, body, (nh, t // tm),
                 [pl.BlockSpec((tm, D), lambda j, i: (i, 0)),
                  pl.BlockSpec((tn, D), lambda j, i: (j, 0)),
                  pl.BlockSpec((tn, D), lambda j, i: (j + nh, 0))],
                 [o, o, o], [_sds((t, DFF), BF16)] * 3, (xn, wgu_t, wgu_t),
                 sem=("parallel", "parallel"), hosted=hosted)


def _ffn_down(name, a, wd, h, gain, tm):
    t = a.shape[0]
    with_norm = gain is not None

    def body(*refs):
        if with_norm:
            a_ref, w_ref, h_ref, g_ref, o_ref, n_ref = refs
        else:
            a_ref, w_ref, h_ref, o_ref = refs
        hn = h_ref[...] + 0.5 * _dot(a_ref[...], w_ref[...])
        o_ref[...] = hn
        if with_norm:
            _, xh = _rms_stats(hn)
            n_ref[...] = (xh * g_ref[...]).astype(BF16)

    ins = [_row(tm, DFF), _const((DFF, D)), _row(tm, D)]
    args = [a, wd, h]
    outs, shapes = [_row(tm, D)], [_sds((t, D), F32)]
    if with_norm:
        ins.append(_const((1, D)))
        args.append(gain)
        outs.append(_row(tm, D))
        shapes.append(_sds((t, D), BF16))
    return pl.pallas_call(
        body, name=name, grid=(t // tm,), in_specs=ins, out_specs=outs, out_shape=shapes,
        compiler_params=_params("parallel"),
    )(*args)


def _mix_in(name, n, win_t, b_in, tm, tn):
    t = n.shape[0]

    def body(n_ref, w_ref, b_ref, p_ref):
        p_ref[...] = (_dot_nt(n_ref[...], w_ref[...]) + b_ref[...]).astype(BF16)

    return pl.pallas_call(
        body, name=name, grid=(6 * D // tn, t // tm),
        in_specs=[pl.BlockSpec((tm, D), lambda j, i: (i, 0)),
                  pl.BlockSpec((tn, D), lambda j, i: (j, 0)),
                  pl.BlockSpec((1, tn), lambda j, i: (0, j))],
        out_specs=pl.BlockSpec((tm, tn), lambda j, i: (i, j)),
        out_shape=_sds((t, 6 * D), BF16),
        compiler_params=_params("parallel", "parallel"),
    )(n, win_t, b_in)


RC = 64
LANES = 128


def _shift_copies(ext, shifted, tm):
    n = tm + HALO - 8
    for m in range(1, 8):
        shifted[m - 1] = ext[pl.ds(m, n), :]


def _by_residue(offs):
    groups = {}
    for k, off in enumerate(offs):
        q, m = divmod(off, 8)
        groups.setdefault(m, []).append((k, q))
    return groups


def _residue_window(ext, shifted, m, taps, base, cs):
    src = ext if m == 0 else shifted.at[m - 1]
    return src[pl.ds(base, RC + 8 * max(q for _, q in taps)), cs]


def _tap_sum(out_ref, bias_ref, w_ref, ext, shifted, offs, tm):
    groups = _by_residue(offs)

    def chunk(j, carry):
        base = pl.multiple_of(j * RC, RC)
        for c in range(D // LANES):
            cs = pl.ds(c * LANES, LANES)
            acc = jnp.zeros((RC, LANES), F32)
            if bias_ref is not None:
                acc = acc + bias_ref[:, cs]
            for m, taps in groups.items():
                big = _residue_window(ext, shifted, m, taps, base, cs)
                for k, q in taps:
                    acc = acc + w_ref[pl.ds(k, 1), cs] * big[8 * q:8 * q + RC]
            out_ref[pl.ds(base, RC), cs] = acc
        return carry

    lax.fori_loop(0, tm // RC, chunk, 0)


def _tap_corr(dw_ref, dc_ext, ext, shifted, offs, tm):
    groups = _by_residue(offs)
    for c in range(D // LANES):
        cs = pl.ds(c * LANES, LANES)

        def chunk(j, accs, cs=cs):
            base = pl.multiple_of(j * RC, RC)
            dcv = dc_ext[pl.ds(base, RC), cs]
            out = list(accs)
            for m, taps in groups.items():
                big = _residue_window(ext, shifted, m, taps, base, cs)
                for k, q in taps:
                    prod = dcv * big[8 * q:8 * q + RC]
                    part = prod[0:8]
                    for s in range(1, RC // 8):
                        part = part + prod[8 * s:8 * s + 8]
                    out[k] = accs[k] + part
            return tuple(out)

        accs = lax.fori_loop(0, tm // RC, chunk, tuple(jnp.zeros((8, LANES), F32) for _ in offs))
        for k in range(len(offs)):
            dw_ref[pl.ds(k, 1), cs] += jnp.sum(accs[k], axis=0, keepdims=True)


def _conv_fwd(name, p, conv_w, conv_b, ln_g, ln_b, tm):
    t = p.shape[0]

    def body(av_ref, ag_ref, w_ref, cb_ref, lg_ref, lb_ref, c_ref, a_ref, ext, shifted):
        i = pl.program_id(0)

        @pl.when(i == 0)
        def _():
            ext[pl.ds(0, HALO), :] = jnp.zeros((HALO, D), F32)

        ext[pl.ds(HALO, tm), :] = av_ref[...].astype(F32) * _sigmoid(ag_ref[...].astype(F32))
        _shift_copies(ext, shifted, tm)
        _tap_sum(c_ref, cb_ref, w_ref, ext, shifted, [HALO - (CW - 1) + k for k in range(CW)], tm)
        rstd, chat = _ln_stats(c_ref[...])
        ca = chat * lg_ref[...] + lb_ref[...]
        a_ref[...] = (ca * _sigmoid(ca)).astype(BF16)
        ext[pl.ds(0, HALO), :] = ext[pl.ds(tm, HALO), :]

    return pl.pallas_call(
        body, name=name, grid=(t // tm,),
        in_specs=[_row(tm, D, 0), _row(tm, D, 1), _const((HALO, D)), _const((1, D)),
                  _const((1, D)), _const((1, D))],
        out_specs=[_row(tm, D), _row(tm, D)],
        out_shape=[_sds((t, D), F32), _sds((t, D), BF16)],
        scratch_shapes=[pltpu.VMEM((tm + HALO, D), F32), pltpu.VMEM((7, tm + HALO - 8, D), F32)],
        compiler_params=_params("arbitrary"),
    )(p, p, conv_w, conv_b, ln_g, ln_b)


def _sgu_mask():
    rows = lax.broadcasted_iota(jnp.int32, (CHUNK, CHUNK), 0)
    cols = lax.broadcasted_iota(jnp.int32, (CHUNK, CHUNK), 1)
    return cols <= rows


def _sgu_fwd(name, p, ln_g, ln_b, sgu_w, bias_full, tm):
    t = p.shape[0]

    def body(bu_ref, bv_ref, lg_ref, lb_ref, ws_ref, bias_ref, o_ref):
        mask = _sgu_mask()
        _, vhat = _ln_stats(_gelu(bv_ref[...].astype(F32)))
        vn = (vhat * lg_ref[...] + lb_ref[...]).astype(BF16)
        ub = _gelu(bu_ref[...].astype(F32))
        for g in range(GROUPS):
            wm = jnp.where(mask, ws_ref[g], 0.0).astype(BF16)
            cs = slice(g * GD, (g + 1) * GD)
            for cc in range(tm // CHUNK):
                rs = slice(cc * CHUNK, (cc + 1) * CHUNK)
                mixed = _dot(wm, vn[rs, cs]) + bias_ref[:, cs]
                o_ref[rs, cs] = (ub[rs, cs] * mixed).astype(BF16)

    return pl.pallas_call(
        body, name=name, grid=(t // tm,),
        in_specs=[_row(tm, D, 2), _row(tm, D, 3), _const((1, D)), _const((1, D)),
                  _const((GROUPS, CHUNK, CHUNK)), _const((CHUNK, D))],
        out_specs=_row(tm, D), out_shape=_sds((t, D), BF16),
        compiler_params=_params("parallel"),
    )(p, p, ln_g, ln_b, sgu_w, bias_full)


def _merge_fwd(name, act_a, act_b, p, w_a, w_b, w_out, h, gain, tm):
    t = h.shape[0]

    def body(a_ref, b_ref, ga_ref, gb_ref, wa_ref, wb_ref, wo_ref, h_ref, g_ref,
             ya_ref, yb_ref, mg_ref, ho_ref, xn_ref):
        ya = _dot(a_ref[...], wa_ref[...])
        yb = _dot(b_ref[...], wb_ref[...])
        ya_ref[...] = ya.astype(BF16)
        yb_ref[...] = yb.astype(BF16)
        merged = (_sigmoid(ga_ref[...].astype(F32)) * ya
                  + _sigmoid(gb_ref[...].astype(F32)) * yb).astype(BF16)
        mg_ref[...] = merged
        hn = h_ref[...] + _dot(merged, wo_ref[...])
        ho_ref[...] = hn
        _, xh = _rms_stats(hn)
        xn_ref[...] = (xh * g_ref[...]).astype(BF16)

    rb = _row(tm, D)
    return pl.pallas_call(
        body, name=name, grid=(t // tm,),
        in_specs=[rb, rb, _row(tm, D, 4), _row(tm, D, 5), _const((D, D)), _const((D, D)),
                  _const((D, D)), rb, _const((1, D))],
        out_specs=[rb] * 5,
        out_shape=[_sds((t, D), BF16)] * 3 + [_sds((t, D), F32), _sds((t, D), BF16)],
        compiler_params=_params("parallel"),
    )(act_a, act_b, p, p, w_a, w_b, w_out, h, gain)


def _kv_fwd(name, mem, gain, wkv_t):
    def body(m_ref, g_ref, w_ref, mn_ref, k_ref, v_ref):
        _, xh = _rms_stats(m_ref[...])
        mn = (xh * g_ref[...]).astype(BF16)
        mn_ref[...] = mn
        kv = _dot_nt(mn, w_ref[...])
        k_ref[...] = kv[:, :D].astype(BF16)
        v_ref[...] = kv[:, D:].astype(BF16)

    return pl.pallas_call(
        body, name=name,
        out_shape=[_sds((NMEM, D), BF16)] * 3,
        compiler_params=pltpu.CompilerParams(vmem_limit_bytes=VMEM_LIMIT),
    )(mem, gain, wkv_t)


def _softmax_rows(s):
    e = jnp.exp(s - jnp.max(s, axis=-1, keepdims=True))
    return e / jnp.sum(e, axis=-1, keepdims=True)


def _attn_fwd(name, xq, w_q, kb, vb, w_o, h, gain, tm):
    t = h.shape[0]
    scale = 1.0 / math.sqrt(HD)

    def body(x_ref, wq_ref, k_ref, v_ref, wo_ref, h_ref, g_ref, q_ref, o_ref, ho_ref, xn_ref):
        q_ref[...] = _dot(x_ref[...], wq_ref[...]).astype(BF16)
        for hd in range(HEADS):
            cs = slice(hd * HD, (hd + 1) * HD)
            p = _softmax_rows(_dot_nt(q_ref[:, cs], k_ref[:, cs]) * scale)
            o_ref[:, cs] = _dot(p.astype(BF16), v_ref[:, cs]).astype(BF16)
        hn = h_ref[...] + _dot(o_ref[...], wo_ref[...])
        ho_ref[...] = hn
        _, xh = _rms_stats(hn)
        xn_ref[...] = (xh * g_ref[...]).astype(BF16)

    rb = _row(tm, D)
    return pl.pallas_call(
        body, name=name, grid=(t // tm,),
        in_specs=[rb, _const((D, D)), _const((NMEM, D)), _const((NMEM, D)), _const((D, D)), rb,
                  _const((1, D))],
        out_specs=[rb] * 4,
        out_shape=[_sds((t, D), BF16), _sds((t, D), BF16), _sds((t, D), F32), _sds((t, D), BF16)],
        compiler_params=_params("parallel"),
    )(xq, w_q, kb, vb, w_o, h, gain)


def _loss_bwd(name, h, gain, target, tm):
    t = h.shape[0]
    steps = t // tm

    def body(h_ref, g_ref, t_ref, dh_ref, dhb_ref, loss_ref, dg_ref, lacc):
        i = pl.program_id(0)
        hv = h_ref[...]
        r, xh = _rms_stats(hv)
        err = xh * g_ref[...] - t_ref[...]
        _acc(lacc, i == 0, jnp.sum(err * err, axis=0, keepdims=True))
        dy = err * (1.0 / D)
        _acc(dg_ref, i == 0, jnp.sum(dy * xh, axis=0, keepdims=True))
        dxh = dy * g_ref[...]
        dh = r * (dxh - xh * jnp.mean(dxh * xh, axis=-1, keepdims=True))
        dh_ref[...] = dh
        dhb_ref[...] = dh.astype(BF16)

        @pl.when(i == steps - 1)
        def _():
            loss_ref[...] = jnp.zeros((8, 128), F32) + (0.5 / D) * jnp.sum(lacc[...])

    rb = _row(tm, D)
    return pl.pallas_call(
        body, name=name, grid=(steps,),
        in_specs=[rb, _const((1, D)), rb],
        out_specs=[rb, rb, _const((8, 128)), _const((1, D))],
        out_shape=[_sds((t, D), F32), _sds((t, D), BF16), _sds((8, 128), F32), _sds((1, D), F32)],
        scratch_shapes=[pltpu.VMEM((1, D), F32)],
        compiler_params=_params("arbitrary"),
    )(h, gain, target)


def _ffn_bwd_act(name, dhb, wd, g, u, tm, tn, hosted=None):
    t = dhb.shape[0]

    def body(d_ref, w_ref, g_ref, u_ref, dg_ref, du_ref):
        da = 0.5 * _dot_nt(d_ref[...], w_ref[...])
        gv = g_ref[...].astype(F32)
        sg = _sigmoid(gv)
        dg_ref[...] = (da * u_ref[...].astype(F32) * (sg * (1.0 + gv * (1.0 - sg)))).astype(BF16)
        du_ref[...] = (da * (gv * sg)).astype(BF16)

    o = pl.BlockSpec((tm, tn), lambda j, i: (i, j))
    return _call(name, body, (DFF // tn, t // tm),
                 [pl.BlockSpec((tm, D), lambda j, i: (i, 0)),
                  pl.BlockSpec((tn, D), lambda j, i: (j, 0)), o, o],
                 [o, o], [_sds((t, DFF), BF16)] * 2, (dhb, wd, g, u),
                 sem=("parallel", "parallel"), hosted=hosted)


def _dx_rms_bwd(name, pairs, h, gain, dh_in, tm, hosted=None):
    t = h.shape[0]
    np_ = len(pairs)

    def body(*refs):
        a_refs = refs[:np_]
        b_refs = refs[np_:2 * np_]
        h_ref, g_ref, d_ref, o_ref, ob_ref, dg_ref = refs[2 * np_:]
        dxn = None
        for (a_ref, b_ref, pr) in zip(a_refs, b_refs, pairs):
            y = _dot_nt(a_ref[...], b_ref[...]) if pr[4] else _dot(a_ref[...], b_ref[...])
            dxn = y if dxn is None else dxn + y
        dh, dgain = _rms_bwd(dxn, h_ref[...], g_ref[...])
        _acc(dg_ref, pl.program_id(0) == 0, dgain)
        out = d_ref[...] + dh
        o_ref[...] = out
        ob_ref[...] = out.astype(BF16)

    ins, args = [], []
    for (a, b, blk, rows, tr) in pairs:
        ins.append(_row(tm, a.shape[1]))
        args.append(a)
    for (a, b, blk, rows, tr) in pairs:
        ins.append(pl.BlockSpec((rows, b.shape[1]), lambda i, _b=blk: (_b, 0)))
        args.append(b)
    rb = _row(tm, D)
    ins += [rb, _const((1, D)), rb]
    args += [h, gain, dh_in]
    return _call(name, body, (t // tm,), ins, [rb, rb, _const((1, D))],
                 [_sds((t, D), F32), _sds((t, D), BF16), _sds((1, D), F32)], args,
                 sem=("arbitrary",), hosted=hosted)


def _mm_tn(name, a, b, scale, tmo, tk):
    t, m = a.shape
    n = b.shape[1]
    tk = min(tk, t)
    steps = t // tk

    def body(a_ref, b_ref, o_ref, acc):
        k = pl.program_id(1)
        _acc(acc, k == 0, _dot_tn(a_ref[...], b_ref[...]))

        @pl.when(k == steps - 1)
        def _():
            o_ref[...] = (acc[...] * scale).astype(o_ref.dtype)

    return pl.pallas_call(
        body, name=name, grid=(m // tmo, steps),
        in_specs=[pl.BlockSpec((tk, tmo), lambda i, k: (k, i)),
                  pl.BlockSpec((tk, n), lambda i, k: (k, 0))],
        out_specs=pl.BlockSpec((tmo, n), lambda i, k: (i, 0)),
        out_shape=_sds((m, n), BF16),
        scratch_shapes=[pltpu.VMEM((tmo, n), F32)],
        compiler_params=_params("parallel", "arbitrary"),
    )(a, b)


def _attn_bwd(name, dhb, w_o, qb, kb, vb, tm):
    t = dhb.shape[0]
    scale = 1.0 / math.sqrt(HD)

    def body(d_ref, wo_ref, q_ref, k_ref, v_ref, dq_ref, dk_ref, dv_ref, do_s):
        i = pl.program_id(0)

        @pl.when(i == 0)
        def _():
            dk_ref[...] = jnp.zeros_like(dk_ref)
            dv_ref[...] = jnp.zeros_like(dv_ref)

        do_s[...] = _dot_nt(d_ref[...], wo_ref[...]).astype(BF16)
        for hd in range(HEADS):
            cs = slice(hd * HD, (hd + 1) * HD)
            q = q_ref[:, cs]
            p = _softmax_rows(_dot_nt(q, k_ref[:, cs]) * scale)
            do = do_s[:, cs]
            dp = _dot_nt(do, v_ref[:, cs])
            ds = (p * (dp - jnp.sum(dp * p, axis=-1, keepdims=True)) * scale).astype(BF16)
            dq_ref[:, cs] = _dot(ds, k_ref[:, cs]).astype(BF16)
            dk_ref[:, cs] += _dot_tn(ds, q)
            dv_ref[:, cs] += _dot_tn(p.astype(BF16), do)

    rb = _row(tm, D)
    return pl.pallas_call(
        body, name=name, grid=(t // tm,),
        in_specs=[rb, _const((D, D)), rb, _const((NMEM, D)), _const((NMEM, D))],
        out_specs=[rb, _const((NMEM, D)), _const((NMEM, D))],
        out_shape=[_sds((t, D), BF16), _sds((NMEM, D), F32), _sds((NMEM, D), F32)],
        scratch_shapes=[pltpu.VMEM((tm, D), BF16)],
        compiler_params=_params("arbitrary"),
    )(dhb, w_o, qb, kb, vb)


def _kv_bwd(name, dk, dv, memn, wkv_t, mem, gain):
    def body(dk_ref, dv_ref, mn_ref, w_ref, m_ref, g_ref, dw_ref, dg_ref):
        dkb = dk_ref[...].astype(BF16)
        dvb = dv_ref[...].astype(BF16)
        mn = mn_ref[...]
        dw_ref[pl.ds(0, D), :] = _dot_tn(dkb, mn).astype(BF16)
        dw_ref[pl.ds(D, D), :] = _dot_tn(dvb, mn).astype(BF16)
        dmn = _dot(dkb, w_ref[pl.ds(0, D), :]) + _dot(dvb, w_ref[pl.ds(D, D), :])
        _, xh = _rms_stats(m_ref[...])
        dg_ref[...] = jnp.sum(dmn * xh, axis=0, keepdims=True)

    return pl.pallas_call(
        body, name=name,
        out_shape=[_sds((2 * D, D), BF16), _sds((1, D), F32)],
        compiler_params=pltpu.CompilerParams(vmem_limit_bytes=VMEM_LIMIT),
    )(dk, dv, memn, wkv_t, mem, gain)


def _merge_bwd(name, dhb, w_out, ya, yb, p, tm):
    t = dhb.shape[0]

    def body(d_ref, w_ref, ya_ref, yb_ref, ga_ref, gb_ref, dya_ref, dyb_ref, dp_ref, cs_ref):
        dm = _dot_nt(d_ref[...], w_ref[...])
        sa = _sigmoid(ga_ref[...].astype(F32))
        sb = _sigmoid(gb_ref[...].astype(F32))
        dya_ref[...] = (dm * sa).astype(BF16)
        dyb_ref[...] = (dm * sb).astype(BF16)
        dga = dm * ya_ref[...].astype(F32) * (sa * (1.0 - sa))
        dgb = dm * yb_ref[...].astype(F32) * (sb * (1.0 - sb))
        dp_ref[:, pl.ds(0, D)] = dga.astype(BF16)
        dp_ref[:, pl.ds(D, D)] = dgb.astype(BF16)
        first = pl.program_id(0) == 0

        @pl.when(first)
        def _():
            cs_ref[...] = jnp.zeros_like(cs_ref)
        cs_ref[:, pl.ds(0, D)] += jnp.sum(dga, axis=0, keepdims=True)
        cs_ref[:, pl.ds(D, D)] += jnp.sum(dgb, axis=0, keepdims=True)

    rb = _row(tm, D)
    return pl.pallas_call(
        body, name=name, grid=(t // tm,),
        in_specs=[rb, _const((D, D)), rb, rb, _row(tm, D, 4), _row(tm, D, 5)],
        out_specs=[rb, rb, _row(tm, 2 * D), _const((1, 2 * D))],
        out_shape=[_sds((t, D), BF16), _sds((t, D), BF16), _sds((t, 2 * D), BF16),
                   _sds((1, 2 * D), F32)],
        compiler_params=_params("arbitrary"),
    )(dhb, w_out, ya, yb, p, p)


def _conv_bwd(name, dya, w_a, c, p, conv_w, ln_g, ln_b, tm):
    t = dya.shape[0]
    steps = t // tm
    hb = tm // HALO

    def rev(i):
        return steps - 1 - i

    def body(dy_ref, wa_ref, c_ref, av_ref, ag_ref, avh_ref, agh_ref, w_ref, lg_ref, lb_ref,
             dp_ref, cs_ref, dw_ref, dcb_ref, dlg_ref, dlb_ref, dc_ext, a_ext, dc_sh, a_sh, da0_s):
        i = pl.program_id(0)
        first = i == 0

        @pl.when(first)
        def _():
            dc_ext[pl.ds(tm, HALO), :] = jnp.zeros((HALO, D), F32)
            dw_ref[...] = jnp.zeros_like(dw_ref)
            cs_ref[...] = jnp.zeros_like(cs_ref)

        d_act = _dot_nt(dy_ref[...], wa_ref[...])
        rstd, chat = _ln_stats(c_ref[...])
        ca = chat * lg_ref[...] + lb_ref[...]
        sc = _sigmoid(ca)
        dca = d_act * (sc * (1.0 + ca * (1.0 - sc)))
        _acc(dlg_ref, first, jnp.sum(dca * chat, axis=0, keepdims=True))
        _acc(dlb_ref, first, jnp.sum(dca, axis=0, keepdims=True))
        dc = _ln_bwd(dca, chat, rstd, lg_ref[...])
        _acc(dcb_ref, first, jnp.sum(dc, axis=0, keepdims=True))
        dc_ext[pl.ds(0, tm), :] = dc

        av = av_ref[...].astype(F32)
        sg = _sigmoid(ag_ref[...].astype(F32))
        a_ext[pl.ds(HALO, tm), :] = av * sg
        halo = avh_ref[...].astype(F32) * _sigmoid(agh_ref[...].astype(F32))
        a_ext[pl.ds(0, HALO), :] = jnp.where(i == steps - 1, 0.0, halo)

        _shift_copies(dc_ext, dc_sh, tm)
        _shift_copies(a_ext, a_sh, tm)
        _tap_sum(da0_s, None, w_ref, dc_ext, dc_sh, [CW - 1 - k for k in range(CW)], tm)
        _tap_corr(dw_ref, dc_ext, a_ext, a_sh, [HALO - (CW - 1) + k for k in range(CW)], tm)
        da0 = da0_s[...]
        dav = da0 * sg
        dag = da0 * av * (sg * (1.0 - sg))
        dp_ref[:, pl.ds(0, D)] = dav.astype(BF16)
        dp_ref[:, pl.ds(D, D)] = dag.astype(BF16)
        cs_ref[:, pl.ds(0, D)] += jnp.sum(dav, axis=0, keepdims=True)
        cs_ref[:, pl.ds(D, D)] += jnp.sum(dag, axis=0, keepdims=True)
        dc_ext[pl.ds(tm, HALO), :] = dc_ext[pl.ds(0, HALO), :]

    def rrow(cols, cb=0):
        return pl.BlockSpec((tm, cols), lambda i, _cb=cb: (rev(i), _cb))

    def halo_spec(cb):
        return pl.BlockSpec((HALO, D), lambda i, _cb=cb: (jnp.maximum(rev(i) * hb - 1, 0), _cb))

    return pl.pallas_call(
        body, name=name, grid=(steps,),
        in_specs=[rrow(D), _const((D, D)), rrow(D), rrow(D, 0), rrow(D, 1), halo_spec(0),
                  halo_spec(1), _const((HALO, D)), _const((1, D)), _const((1, D))],
        out_specs=[rrow(2 * D), _const((1, 2 * D)), _const((HALO, D)), _const((1, D)),
                   _const((1, D)), _const((1, D))],
        out_shape=[_sds((t, 2 * D), BF16), _sds((1, 2 * D), F32), _sds((HALO, D), F32),
                   _sds((1, D), F32), _sds((1, D), F32), _sds((1, D), F32)],
        scratch_shapes=[pltpu.VMEM((tm + HALO, D), F32), pltpu.VMEM((tm + HALO, D), F32),
                        pltpu.VMEM((7, tm + HALO - 8, D), F32), pltpu.VMEM((7, tm + HALO - 8, D), F32),
                        pltpu.VMEM((tm, D), F32)],
        compiler_params=_params("arbitrary"),
    )(dya, w_a, c, p, p, p, p, conv_w, ln_g, ln_b)


def _sgu_bwd(name, dyb, w_b, p, ln_g, ln_b, sgu_w, bias_full, tm):
    t = dyb.shape[0]
    steps = t // tm

    def body(dy_ref, wb_ref, bu_ref, bv_ref, lg_ref, lb_ref, ws_ref, bias_ref,
             dp_ref, cs_ref, dws_ref, dsb_ref, dlg_ref, dlb_ref, dub_s, dvn_s, dbias_s):
        i = pl.program_id(0)
        first = i == 0
        mask = _sgu_mask()

        @pl.when(first)
        def _():
            dws_ref[...] = jnp.zeros_like(dws_ref)
            dbias_s[...] = jnp.zeros_like(dbias_s)
            cs_ref[...] = jnp.zeros_like(cs_ref)

        dob = _dot_nt(dy_ref[...], wb_ref[...])
        bu = bu_ref[...].astype(F32)
        bv = bv_ref[...].astype(F32)
        ub = _gelu(bu)
        rstd, vhat = _ln_stats(_gelu(bv))
        vn = (vhat * lg_ref[...] + lb_ref[...]).astype(BF16)
        for g in range(GROUPS):
            wm = jnp.where(mask, ws_ref[g], 0.0).astype(BF16)
            cs = slice(g * GD, (g + 1) * GD)
            for cc in range(tm // CHUNK):
                rs = slice(cc * CHUNK, (cc + 1) * CHUNK)
                vblk = vn[rs, cs]
                mixed = _dot(wm, vblk) + bias_ref[:, cs]
                dob_blk = dob[rs, cs]
                dub_s[rs, cs] = dob_blk * mixed
                dmixed = dob_blk * ub[rs, cs]
                dbias_s[:, cs] += dmixed
                dmb = dmixed.astype(BF16)
                dws_ref[g] += _dot_nt(dmb, vblk)
                dvn_s[rs, cs] = _dot_tn(wm, dmb)
        dbu = dub_s[...] * _gelu_grad(bu)
        dvn = dvn_s[...]
        _acc(dlg_ref, first, jnp.sum(dvn * vhat, axis=0, keepdims=True))
        _acc(dlb_ref, first, jnp.sum(dvn, axis=0, keepdims=True))
        dbv = _ln_bwd(dvn, vhat, rstd, lg_ref[...]) * _gelu_grad(bv)
        dp_ref[:, pl.ds(0, D)] = dbu.astype(BF16)
        dp_ref[:, pl.ds(D, D)] = dbv.astype(BF16)
        cs_ref[:, pl.ds(0, D)] += jnp.sum(dbu, axis=0, keepdims=True)
        cs_ref[:, pl.ds(D, D)] += jnp.sum(dbv, axis=0, keepdims=True)

        @pl.when(i == steps - 1)
        def _():
            lane = lax.broadcasted_iota(jnp.int32, (CHUNK, CHUNK), 1)
            dsb = jnp.zeros((CHUNK, CHUNK), F32)
            for g in range(GROUPS):
                dws_ref[g] = jnp.where(mask, dws_ref[g], 0.0)
                dsb = jnp.where(lane == g, jnp.sum(dbias_s[:, g * GD:(g + 1) * GD], axis=1, keepdims=True), dsb)
            dsb_ref[...] = dsb

    rb = _row(tm, D)
    return pl.pallas_call(
        body, name=name, grid=(steps,),
        in_specs=[rb, _const((D, D)), _row(tm, D, 2), _row(tm, D, 3), _const((1, D)), _const((1, D)),
                  _const((GROUPS, CHUNK, CHUNK)), _const((CHUNK, D))],
        out_specs=[_row(tm, 2 * D), _const((1, 2 * D)), _const((GROUPS, CHUNK, CHUNK)),
                   _const((CHUNK, CHUNK)), _const((1, D)), _const((1, D))],
        out_shape=[_sds((t, 2 * D), BF16), _sds((1, 2 * D), F32), _sds((GROUPS, CHUNK, CHUNK), F32),
                   _sds((CHUNK, CHUNK), F32), _sds((1, D), F32), _sds((1, D), F32)],
        scratch_shapes=[pltpu.VMEM((tm, D), F32), pltpu.VMEM((tm, D), F32), pltpu.VMEM((CHUNK, D), F32)],
        compiler_params=_params("arbitrary"),
    )(dyb, w_b, p, p, ln_g, ln_b, sgu_w, bias_full)


def _adam_math(w, g, m, v):
    m = B1 * m + (1.0 - B1) * g
    v = B2 * v + (1.0 - B2) * (g * g)
    m_hat = m / (1.0 - B1 ** STEP)
    v_hat = v / (1.0 - B2 ** STEP)
    delta = -LR * (m_hat / (jnp.sqrt(v_hat) + EPS_ADAM) + WD * w)
    return delta, m, v


def _adamw(name, w, g, m, v, tr):
    r, cdim = w.shape

    def body(w_ref, g_ref, m_ref, v_ref, d_ref, mo_ref, vo_ref):
        d, mn, vn = _adam_math(w_ref[...], g_ref[...], m_ref[...], v_ref[...])
        d_ref[...] = d
        mo_ref[...] = mn
        vo_ref[...] = vn

    blk = pl.BlockSpec((tr, cdim), lambda i: (i, 0))
    return pl.pallas_call(
        body, name=name, grid=(r // tr,), in_specs=[blk] * 4, out_specs=[blk] * 3,
        out_shape=[_sds((r, cdim), F32)] * 3, compiler_params=_params("parallel"),
    )(w, g, m, v)


def _adamw_small(name, w, g8, m, v):
    r, cdim = w.shape

    def body(w_ref, g_ref, m_ref, v_ref, go_ref, d_ref, mo_ref, vo_ref):
        g = g_ref[0]
        for k in range(1, NDEV):
            g = g + g_ref[k]
        go_ref[...] = g
        d, mn, vn = _adam_math(w_ref[...], g, m_ref[...], v_ref[...])
        d_ref[...] = d
        mo_ref[...] = mn
        vo_ref[...] = vn

    return pl.pallas_call(
        body, name=name, out_shape=[_sds((r, cdim), F32)] * 4,
        compiler_params=pltpu.CompilerParams(vmem_limit_bytes=VMEM_LIMIT),
    )(w, g8, m, v)


_BIG = [("ffn1_w_gu", 704, True), ("ffn1_w_down", 352, False), ("w_in", 768, True),
        ("w_a_out", 128, False), ("w_b_out", 128, False), ("w_out", 128, False),
        ("w_q", 128, False), ("w_kv", 256, True), ("w_o", 128, False),
        ("ffn2_w_gu", 704, True), ("ffn2_w_down", 352, False)]
_BIG_ROWS = sum(r for _, r, _ in _BIG)

_SMALL = [("ffn1_norm", 1), ("mix_norm", 1), ("b_in", 6), ("conv_w", HALO), ("conv_b", 1),
          ("conv_ln_g", 1), ("conv_ln_b", 1), ("sgu_ln_g", 1), ("sgu_ln_b", 1), ("sgu_w", 64),
          ("sgu_b", 1), ("xattn_norm", 1), ("mem_norm", 1), ("ffn2_norm", 1), ("final_norm", 1)]
_SMALL_ROWS = 120


def _pack_small(vals, my_dev):
    rows = []
    for name, nrows in _SMALL:
        a = vals[name].astype(F32)
        if name == "conv_w":
            if a.shape[-1] != D:
                slab = jnp.zeros((HALO, D), F32)
                a = lax.dynamic_update_slice(slab, jnp.pad(a.reshape(CW, -1), ((0, HALO - CW), (0, 0))),
                                             (0, my_dev * (D // NDEV)))
            else:
                a = jnp.pad(a.reshape(CW, D), ((0, HALO - CW), (0, 0)))
        elif name == "sgu_b":
            a = jnp.pad(a.reshape(1, -1), ((0, 0), (0, D - GROUPS * CHUNK)))
        else:
            a = a.reshape(nrows, D)
        rows.append(a)
    packed = jnp.concatenate(rows, axis=0)
    return jnp.pad(packed, ((0, _SMALL_ROWS - packed.shape[0]), (0, 0)))


def _unpack_small(packed, shapes, my_dev):
    out, off = {}, 0
    for name, nrows in _SMALL:
        a = packed[off:off + nrows]
        off += nrows
        if name == "conv_w":
            a = lax.dynamic_slice(a, (0, my_dev * (D // NDEV)), (CW, D // NDEV))
        elif name == "sgu_b":
            a = a[:, :GROUPS * CHUNK]
        out[name] = a.reshape(shapes[name])
    return out


def kernel(x, mem, ffn1_norm, ffn1_w_gu, ffn1_w_down, mix_norm, w_in, b_in, conv_w, conv_b, conv_ln_g, conv_ln_b, w_a_out, sgu_ln_g, sgu_ln_b, sgu_w, sgu_b, w_b_out, w_out, xattn_norm, mem_norm, w_q, w_kv, w_o, ffn2_norm, ffn2_w_gu, ffn2_w_down, final_norm, loss_target, m_ffn1_norm, m_ffn1_w_gu, m_ffn1_w_down, m_mix_norm, m_w_in, m_b_in, m_conv_w, m_conv_b, m_conv_ln_g, m_conv_ln_b, m_w_a_out, m_sgu_ln_g, m_sgu_ln_b, m_sgu_w, m_sgu_b, m_w_b_out, m_w_out, m_xattn_norm, m_mem_norm, m_w_q, m_w_kv, m_w_o, m_ffn2_norm, m_ffn2_w_gu, m_ffn2_w_down, m_final_norm, v_ffn1_norm, v_ffn1_w_gu, v_ffn1_w_down, v_mix_norm, v_w_in, v_b_in, v_conv_w, v_conv_b, v_conv_ln_g, v_conv_ln_b, v_w_a_out, v_sgu_ln_g, v_sgu_ln_b, v_sgu_w, v_sgu_b, v_w_b_out, v_w_out, v_xattn_norm, v_mem_norm, v_w_q, v_w_kv, v_w_o, v_ffn2_norm, v_ffn2_w_gu, v_ffn2_w_down, v_final_norm):
    env = dict(locals())
    names = [n for n, _, _ in _BIG] + [n for n, _ in _SMALL]
    w = {n: env[n] for n in names}
    mom = {n: env["m_" + n] for n in names}
    vel = {n: env["v_" + n] for n in names}

    ax, ay, ac = lax.axis_index("x"), lax.axis_index("y"), lax.axis_index("c")
    my_chip = 2 * ax + ay
    my_dev = 2 * my_chip + ac

    t = x.shape[1]
    tm = min(512, t)
    tm_s = min(256, t)
    xs = x.reshape(t, D)
    tgt = loss_target.reshape(t, D)
    mem2 = mem.reshape(NMEM, D)

    first, rest = _BIG[:2], _BIG[2:]

    def shard_block(entries):
        return jnp.concatenate(
            [(w[n][0].T if tr else w[n][0]).astype(BF16) for n, _, tr in entries], axis=0)

    def split(full, entries):
        out, off = {}, 0
        for n, rows, _ in entries:
            out[n] = full[:, off:off + rows, :].reshape(NDEV * rows, D)
            off += rows
        return out

    conv_slab = lax.dynamic_update_slice(
        jnp.zeros((HALO, D), F32), jnp.pad(conv_w[0], ((0, HALO - CW), (0, 0))), (0, my_dev * (D // NDEV)))
    (conv_w8,) = _run_comm("gather_conv_w", _gather_comm(conv_slab))
    conv_w_pad = jnp.sum(conv_w8, axis=0)
    bias_full = jnp.repeat(sgu_b[0].T, GD, axis=1)
    b_in2 = b_in.reshape(1, 6 * D)

    (xn1,), (full_first,) = _rms_cast("norm_x", xs, ffn1_norm, tm, hosted=_gather_comm(shard_block(first)))
    wf = split(full_first, first)
    (g1, u1, a1), (full_rest,) = _ffn_up("ffn1_up", xn1, wf["ffn1_w_gu"], tm, 1408,
                                         hosted=_gather_comm(shard_block(rest)))
    wf.update(split(full_rest, rest))
    h1, n_mix = _ffn_down("ffn1_down", a1, wf["ffn1_w_down"], xs, mix_norm, tm)
    p = _mix_in("mix_in", n_mix, wf["w_in"], b_in2, tm, 1536)
    c_conv, act_a = _conv_fwd("conv_fwd", p, conv_w_pad, conv_b, conv_ln_g, conv_ln_b, tm_s)
    act_b = _sgu_fwd("sgu_fwd", p, sgu_ln_g, sgu_ln_b, sgu_w[0], bias_full, tm_s)
    ya, yb, merged, h2, xq = _merge_fwd("merge_fwd", act_a, act_b, p, wf["w_a_out"], wf["w_b_out"],
                                        wf["w_out"], h1, xattn_norm, tm_s)
    memn, kb, vb = _kv_fwd("kv_fwd", mem2, mem_norm, wf["w_kv"])
    qb, ob, h3, xn4 = _attn_fwd("attn_fwd", xq, wf["w_q"], kb, vb, wf["w_o"], h2, ffn2_norm, tm_s)
    (g2, u2, a2), _ = _ffn_up("ffn2_up", xn4, wf["ffn2_w_gu"], tm, 1408)
    (h4,) = _ffn_down("ffn2_down", a2, wf["ffn2_w_down"], h3, None, tm)
    dh4, dh4b, loss_blk, d_final = _loss_bwd("loss_bwd", h4, final_norm.reshape(1, D), tgt, tm)

    gb = {}
    gs = {}
    gs["final_norm"] = d_final

    def ffn_bwd(tag, dhb, dh, g, u, a, xn, h_in, gain, wgu_t, wd, hosted=None):
        (dg, du), hosted_out = _ffn_bwd_act(tag + "_bwd_act", dhb, wd, g, u, tm, 1408, hosted=hosted)
        gwd = _mm_tn(tag + "_dw_down", a, dhb, 0.5, 1408, TK)
        gwg = _mm_tn(tag + "_dw_gate", dg, xn, 1.0, 1408, TK)
        gwu = _mm_tn(tag + "_dw_up", du, xn, 1.0, 1408, TK)
        (dh_o, dhb_o, dgain), _ = _dx_rms_bwd(
            tag + "_bwd_dx", [(dg, wgu_t, 0, DFF, False), (du, wgu_t, 1, DFF, False)], h_in, gain, dh, tm_s)
        return dh_o, dhb_o, dgain, jnp.concatenate([gwg, gwu], axis=0), gwd, hosted_out

    dh3, dh3b, gs["ffn2_norm"], gb["ffn2_w_gu"], gb["ffn2_w_down"], _ = ffn_bwd(
        "ffn2", dh4b, dh4, g2, u2, a2, xn4, h3, ffn2_norm, wf["ffn2_w_gu"], wf["ffn2_w_down"])

    gb["w_o"] = _mm_tn("dw_o", ob, dh3b, 1.0, 1024, TK)
    dq, dk, dv = _attn_bwd("attn_bwd", dh3b, wf["w_o"], qb, kb, vb, tm_s)
    gb["w_kv"], gs["mem_norm"] = _kv_bwd("kv_bwd", dk, dv, memn, wf["w_kv"], mem2, mem_norm)
    gb["w_q"] = _mm_tn("dw_q", xq, dq, 1.0, 1024, TK)
    (dh2, dh2b, gs["xattn_norm"]), _ = _dx_rms_bwd(
        "attn_bwd_dx", [(dq, wf["w_q"], 0, D, True)], h2, xattn_norm, dh3, tm_s)

    gb["w_out"] = _mm_tn("dw_out", merged, dh2b, 1.0, 1024, TK)
    dya, dyb, dp_g, cs_g = _merge_bwd("merge_bwd", dh2b, wf["w_out"], ya, yb, p, tm_s)
    gb["w_a_out"] = _mm_tn("dw_a", act_a, dya, 1.0, 1024, TK)
    gb["w_b_out"] = _mm_tn("dw_b", act_b, dyb, 1.0, 1024, TK)
    dp_a, cs_a, d_convw, gs["conv_b"], gs["conv_ln_g"], gs["conv_ln_b"] = _conv_bwd(
        "conv_bwd", dya, wf["w_a_out"], c_conv, p, conv_w_pad, conv_ln_g, conv_ln_b, tm_s)
    dp_b, cs_b, d_sguw, d_sgub, gs["sgu_ln_g"], gs["sgu_ln_b"] = _sgu_bwd(
        "sgu_bwd", dyb, wf["w_b_out"], p, sgu_ln_g, sgu_ln_b, sgu_w[0], bias_full, tm_s)
    gs["conv_w"] = d_convw[:CW].reshape(1, CW, D)
    gs["sgu_w"] = d_sguw
    gs["sgu_b"] = d_sgub[:, :GROUPS].T
    gs["b_in"] = jnp.concatenate([cs_a, cs_b, cs_g], axis=1)
    gb["w_in"] = jnp.concatenate([_mm_tn("dw_in_a", dp_a, n_mix, 1.0, 1024, TK),
                                  _mm_tn("dw_in_b", dp_b, n_mix, 1.0, 1024, TK),
                                  _mm_tn("dw_in_g", dp_g, n_mix, 1.0, 1024, TK)], axis=0)
    core = ac.astype(jnp.int32).reshape(1)
    chip = my_chip.astype(jnp.int32).reshape(1)

    def pack(entries):
        return jnp.concatenate([gb[n].reshape(4, 2, rows, D) for n, rows, _ in entries], axis=2)

    g4_rest = pack(rest)
    (dh1, dh1b, gs["mix_norm"]), (sib_rest,) = _dx_rms_bwd(
        "mix_bwd_dx", [(dp_a, wf["w_in"], 0, 2 * D, False), (dp_b, wf["w_in"], 1, 2 * D, False),
                       (dp_g, wf["w_in"], 2, 2 * D, False)], h1, mix_norm, dh2, tm_s,
        hosted=_pair_exchange_comm(g4_rest))
    part_rest = _pair_sum("grads_pair_sum_rest", g4_rest, sib_rest, core, 544)

    dx, _, gs["ffn1_norm"], gb["ffn1_w_gu"], gb["ffn1_w_down"], (chips_rest,) = ffn_bwd(
        "ffn1", dh1b, dh1, g1, u1, a1, xn1, xs, ffn1_norm, wf["ffn1_w_gu"], wf["ffn1_w_down"],
        hosted=_chip_exchange_comm(part_rest))
    gsum_rest = _chip_sum("grads_chip_sum_rest", part_rest, chips_rest, chip, 544)

    g4_first = pack(first)
    (sib_first,) = _run_comm("grads_pair_exchange_ffn1", _pair_exchange_comm(g4_first))
    part_first = _pair_sum("grads_pair_sum_ffn1", g4_first, sib_first, core, 528)
    (chips_first,) = _run_comm("grads_chip_exchange_ffn1", _chip_exchange_comm(part_first))
    gsum_first = _chip_sum("grads_chip_sum_ffn1", part_first, chips_first, chip, 528)

    grads, deltas, new_m, new_v = {}, {}, {}, {}
    for entries, gsum in ((first, gsum_first), (rest, gsum_rest)):
        off = 0
        for name, rows, transposed in entries:
            gsh = gsum[off:off + rows]
            off += rows
            gsh = gsh.T if transposed else gsh
            d, mo, vo = _adamw("adamw_" + name, w[name][0], gsh, mom[name][0], vel[name][0], gsh.shape[0] // 2)
            grads[name], deltas[name], new_m[name], new_v[name] = gsh[None], d[None], mo[None], vo[None]

    shapes = {n: w[n].shape for n, _ in _SMALL}
    (g8,) = _run_comm("gather_small_grads", _gather_comm(_pack_small(gs, my_dev)))
    sg, sd, sm, sv = _adamw_small("adamw_small", _pack_small(w, my_dev), g8,
                                  _pack_small(mom, my_dev), _pack_small(vel, my_dev))
    for dst, src in ((grads, sg), (deltas, sd), (new_m, sm), (new_v, sv)):
        dst.update(_unpack_small(src, shapes, my_dev))

    loss = lax.psum(loss_blk[0, 0], AXES)
    order = ["ffn1_norm", "ffn1_w_gu", "ffn1_w_down", "mix_norm", "w_in", "b_in", "conv_w", "conv_b",
             "conv_ln_g", "conv_ln_b", "w_a_out", "sgu_ln_g", "sgu_ln_b", "sgu_w", "sgu_b", "w_b_out",
             "w_out", "xattn_norm", "mem_norm", "w_q", "w_kv", "w_o", "ffn2_norm", "ffn2_w_gu",
             "ffn2_w_down", "final_norm"]
    return (loss, dx.reshape(x.shape), *[grads[n] for n in order], *[deltas[n] for n in order],
            *[new_m[n] for n in order], *[new_v[n] for n in order])
```

```python
import functools
import math

import jax
import jax.numpy as jnp
from jax import lax
from jax.experimental import pallas as pl
from jax.experimental.pallas import tpu as pltpu

F32 = jnp.float32
BF16 = jnp.bfloat16
MESH = pl.DeviceIdType.MESH
AXES = ("x", "y", "c")

D = 1024
DFF = 2816
NMEM = 256
HEADS = 4
HD = D // HEADS
CW = 31
HALO = 32
CHUNK = 128
GROUPS = 4
GD = D // GROUPS
EPS_RMS = 1e-6
EPS_LN = 1e-5
LR, B1, B2, EPS_ADAM, WD, STEP = 0.001, 0.9, 0.999, 1e-08, 0.01, 10
NDEV = 8
VMEM_LIMIT = 56 * 1024 * 1024
TK = 2048


def _params(*sem):
    return pltpu.CompilerParams(dimension_semantics=sem, vmem_limit_bytes=VMEM_LIMIT)


def _dot(a, b):
    return jnp.dot(a, b, preferred_element_type=F32)


def _dot_nt(a, b):
    return lax.dot_general(a, b, (((1,), (1,)), ((), ())), preferred_element_type=F32)


def _dot_tn(a, b):
    return lax.dot_general(a, b, (((0,), (0,)), ((), ())), preferred_element_type=F32)


def _sigmoid(x):
    return 0.5 * jnp.tanh(0.5 * x) + 0.5


_GELU_C = math.sqrt(2.0 / math.pi)


def _gelu(x):
    return 0.5 * x * (1.0 + jnp.tanh(_GELU_C * (x + 0.044715 * (x * x * x))))


def _gelu_grad(x):
    t = jnp.tanh(_GELU_C * (x + 0.044715 * (x * x * x)))
    return 0.5 * (1.0 + t) + 0.5 * x * (1.0 - t * t) * (_GELU_C * (1.0 + 3.0 * 0.044715 * x * x))


def _rms_stats(h):
    r = lax.rsqrt(jnp.mean(h * h, axis=-1, keepdims=True) + EPS_RMS)
    return r, h * r


def _rms_bwd(dxn, h, gain):
    r, xh = _rms_stats(h)
    dgain = jnp.sum(dxn * xh, axis=0, keepdims=True)
    dxh = dxn * gain
    dh = r * (dxh - xh * jnp.mean(dxh * xh, axis=-1, keepdims=True))
    return dh, dgain


def _ln_stats(c):
    mu = jnp.mean(c, axis=-1, keepdims=True)
    xc = c - mu
    rstd = lax.rsqrt(jnp.mean(xc * xc, axis=-1, keepdims=True) + EPS_LN)
    return rstd, xc * rstd


def _ln_bwd(dy, xhat, rstd, g):
    dxh = dy * g
    return rstd * (dxh - jnp.mean(dxh, axis=-1, keepdims=True)
                   - xhat * jnp.mean(dxh * xhat, axis=-1, keepdims=True))


def _row(tm, cols, cb=0):
    return pl.BlockSpec((tm, cols), lambda i, _cb=cb: (i, _cb))


def _const(shape):
    n = len(shape)
    return pl.BlockSpec(shape, lambda *_: (0,) * n)


def _sds(shape, dtype):
    return jax.ShapeDtypeStruct(shape, dtype)


def _acc(ref, first, val):
    @pl.when(first)
    def _():
        ref[...] = jnp.zeros_like(ref)
    ref[...] += val


class _Comm:
    def __init__(self, args, out_shapes, scratch, start, finish):
        self.args, self.out_shapes, self.scratch = args, out_shapes, scratch
        self.start, self.finish = start, finish


_ANY = pl.BlockSpec(memory_space=pl.ANY)


def _run_comm(name, comm):
    ni, no = len(comm.args), len(comm.out_shapes)

    def body(*refs):
        ins, outs, sems = refs[:ni], refs[ni:ni + no], refs[ni + no:]
        comm.start(ins, outs, sems)
        comm.finish(ins, outs, sems)

    return pl.pallas_call(
        body, name=name, out_shape=list(comm.out_shapes), in_specs=[_ANY] * ni, out_specs=[_ANY] * no,
        scratch_shapes=list(comm.scratch),
    )(*comm.args)


def _call(name, body, grid, in_specs, out_specs, out_shape, args, scratch=(), sem=None, hosted=None):
    n_in, n_out, n_scr = len(in_specs), len(out_specs), len(scratch)
    if not hosted:
        outs = pl.pallas_call(
            body, name=name, grid=grid, in_specs=list(in_specs), out_specs=list(out_specs),
            out_shape=list(out_shape), scratch_shapes=list(scratch), compiler_params=_params(*sem),
        )(*args)
        return outs, []
    comms = list(hosted) if isinstance(hosted, (list, tuple)) else [hosted]
    hi = sum(len(cm.args) for cm in comms)
    ho = sum(len(cm.out_shapes) for cm in comms)

    def wrapped(*refs):
        ins, h_in = refs[:n_in], refs[n_in:n_in + hi]
        o0 = n_in + hi
        outs, h_out = refs[o0:o0 + n_out], refs[o0 + n_out:o0 + n_out + ho]
        s0 = o0 + n_out + ho
        scr, h_sems = refs[s0:s0 + n_scr], refs[s0 + n_scr:]
        ids = [pl.program_id(a) for a in range(len(grid))]
        first = functools.reduce(jnp.logical_and, [i == 0 for i in ids])
        last = functools.reduce(jnp.logical_and, [i == g - 1 for i, g in zip(ids, grid)])
        parts, a0, b0, c0 = [], 0, 0, 0
        for cm in comms:
            na, nb, nc = len(cm.args), len(cm.out_shapes), len(cm.scratch)
            parts.append((cm, h_in[a0:a0 + na], h_out[b0:b0 + nb], h_sems[c0:c0 + nc]))
            a0, b0, c0 = a0 + na, b0 + nb, c0 + nc

        @pl.when(first)
        def _():
            for cm, ci, co, cs in parts:
                cm.start(ci, co, cs)

        body(*ins, *outs, *scr)

        @pl.when(last)
        def _():
            for cm, ci, co, cs in parts:
                cm.finish(ci, co, cs)

    res = pl.pallas_call(
        wrapped, name=name, grid=grid, in_specs=list(in_specs) + [_ANY] * hi,
        out_specs=list(out_specs) + [_ANY] * ho,
        out_shape=list(out_shape) + [s for cm in comms for s in cm.out_shapes],
        scratch_shapes=list(scratch) + [s for cm in comms for s in cm.scratch],
        compiler_params=_params(*(["arbitrary"] * len(grid))),
    )(*args, *[a for cm in comms for a in cm.args])
    return res[:n_out], res[n_out:]


def _gather_comm(blk):
    r, cdim = blk.shape

    def copies(x_ref, out_ref, send_sems, recv_sems, local_sem):
        x, y, c = lax.axis_index("x"), lax.axis_index("y"), lax.axis_index("c")
        me, sibling = (x, y, c), (x, y, 1 - c)
        chips = [(1 - x, y), (x, 1 - y), (1 - x, 1 - y)]

        def slot(px, py, pc):
            return out_ref.at[4 * px + 2 * py + pc]

        def copy(k, block, to, src=None):
            return pltpu.make_async_remote_copy(
                src_ref=slot(*block) if src is None else src, dst_ref=slot(*block),
                send_sem=send_sems.at[k], recv_sem=recv_sems.at[k],
                device_id=to, device_id_type=MESH)

        mine = pltpu.make_async_copy(x_ref, slot(*me), local_sem)
        first = [copy(0, me, sibling, src=x_ref)]
        first += [copy(1 + j, me, (*chip, c), src=x_ref) for j, chip in enumerate(chips)]
        passed = [copy(4 + j, (*chip, c), sibling) for j, chip in enumerate(chips)]
        landed = [copy(1 + j, (*chip, c), me) for j, chip in enumerate(chips)]
        from_sibling = [copy(0, sibling, me)] + [copy(4 + j, (*chip, 1 - c), me) for j, chip in enumerate(chips)]
        return mine, first, passed, landed, from_sibling

    def start(ins, outs, sems):
        mine, first, _, _, _ = copies(ins[0], outs[0], *sems)
        mine.start()
        for cp in first:
            cp.start()

    def finish(ins, outs, sems):
        mine, first, passed, landed, from_sibling = copies(ins[0], outs[0], *sems)
        for arrived, forward in zip(landed, passed):
            arrived.wait_recv()
            forward.start()
        for cp in from_sibling:
            cp.wait_recv()
        for cp in first + passed:
            cp.wait_send()
        mine.wait()

    return _Comm([blk], [_sds((NDEV, r, cdim), blk.dtype)],
                 [pltpu.SemaphoreType.DMA((7,)), pltpu.SemaphoreType.DMA((7,)), pltpu.SemaphoreType.DMA],
                 start, finish)


def _exchange_comm(src, n, out_rows, make):
    r, cdim = out_rows

    def copies(src_ref, out_ref, send_sems, recv_sems):
        out = []
        for k in range(n):
            s, d, to = make(k, src_ref, out_ref)
            out.append(pltpu.make_async_remote_copy(
                src_ref=s, dst_ref=d, send_sem=send_sems.at[k], recv_sem=recv_sems.at[k],
                device_id=to, device_id_type=MESH))
        return out

    def start(ins, outs, sems):
        for cp in copies(ins[0], outs[0], *sems):
            cp.start()

    def finish(ins, outs, sems):
        cps = copies(ins[0], outs[0], *sems)
        for cp in cps:
            cp.wait_recv()
        for cp in cps:
            cp.wait_send()

    return _Comm([src], [_sds((n, r, cdim), src.dtype)],
                 [pltpu.SemaphoreType.DMA((n,)), pltpu.SemaphoreType.DMA((n,))], start, finish)


def _pair_exchange_comm(g4):
    def make(k, g_ref, out_ref):
        x, y, c = lax.axis_index("x"), lax.axis_index("y"), lax.axis_index("c")
        return g_ref.at[k, 1 - c], out_ref.at[k], (x, y, 1 - c)

    return _exchange_comm(g4, 4, g4.shape[2:], make)


def _chip_exchange_comm(part):
    def make(k, p_ref, out_ref):
        x, y, c = lax.axis_index("x"), lax.axis_index("y"), lax.axis_index("c")
        px = x if ((k + 1) >> 1) == 0 else 1 - x
        py = y if ((k + 1) & 1) == 0 else 1 - y
        return p_ref.at[2 * px + py], out_ref.at[k], (px, py, c)

    return _exchange_comm(part, 3, part.shape[1:], make)


def _all_to_all_comm(g8):
    _, r, cdim = g8.shape

    def copies(g_ref, out_ref, send_sems, recv_sems, local_sem):
        x, y, c = lax.axis_index("x"), lax.axis_index("y"), lax.axis_index("c")
        me = 4 * x + 2 * y + c
        remote = []
        for k in range(1, NDEV):
            px = 1 - x if k & 4 else x
            py = 1 - y if k & 2 else y
            pc = 1 - c if k & 1 else c
            remote.append(pltpu.make_async_remote_copy(
                src_ref=g_ref.at[4 * px + 2 * py + pc], dst_ref=out_ref.at[me],
                send_sem=send_sems.at[k - 1], recv_sem=recv_sems.at[k - 1],
                device_id=(px, py, pc), device_id_type=MESH))
        return pltpu.make_async_copy(g_ref.at[me], out_ref.at[me], local_sem), remote

    def start(ins, outs, sems):
        mine, remote = copies(ins[0], outs[0], *sems)
        mine.start()
        for cp in remote:
            cp.start()

    def finish(ins, outs, sems):
        mine, remote = copies(ins[0], outs[0], *sems)
        for cp in remote:
            cp.wait_recv()
        for cp in remote:
            cp.wait_send()
        mine.wait()

    return _Comm([g8], [_sds((NDEV, r, cdim), g8.dtype)],
                 [pltpu.SemaphoreType.DMA((NDEV - 1,)), pltpu.SemaphoreType.DMA((NDEV - 1,)),
                  pltpu.SemaphoreType.DMA], start, finish)


def _slot_sum(name, slots, tr):
    _, r, cdim = slots.shape

    def body(s_ref, o_ref):
        s = s_ref[0].astype(F32)
        for k in range(1, NDEV):
            s = s + s_ref[k].astype(F32)
        o_ref[...] = s

    return pl.pallas_call(
        body, name=name, grid=(r // tr,),
        in_specs=[pl.BlockSpec((NDEV, tr, cdim), lambda i: (0, i, 0))],
        out_specs=pl.BlockSpec((tr, cdim), lambda i: (i, 0)),
        out_shape=_sds((r, cdim), F32), compiler_params=_params("parallel"),
    )(slots)


def _pair_sum(name, g4, recv, core, tr):
    _, _, r, cdim = g4.shape

    def body(core_ref, a_ref, b_ref, o_ref):
        o_ref[...] = (a_ref[...].astype(F32) + b_ref[...].astype(F32)).astype(o_ref.dtype)

    return pl.pallas_call(
        body, name=name,
        grid_spec=pltpu.PrefetchScalarGridSpec(
            num_scalar_prefetch=1, grid=(4, r // tr),
            in_specs=[pl.BlockSpec((None, None, tr, cdim), lambda k, i, cr: (k, cr[0], i, 0)),
                      pl.BlockSpec((None, tr, cdim), lambda k, i, cr: (k, i, 0))],
            out_specs=pl.BlockSpec((None, tr, cdim), lambda k, i, cr: (k, i, 0))),
        out_shape=_sds((4, r, cdim), BF16),
        compiler_params=_params("parallel", "parallel"),
    )(core, g4, recv)


def _chip_sum(name, part, recv, chip, tr):
    _, r, cdim = part.shape

    def body(chip_ref, a_ref, b_ref, o_ref):
        s = a_ref[...].astype(F32)
        for k in range(3):
            s = s + b_ref[k].astype(F32)
        o_ref[...] = s

    return pl.pallas_call(
        body, name=name,
        grid_spec=pltpu.PrefetchScalarGridSpec(
            num_scalar_prefetch=1, grid=(r // tr,),
            in_specs=[pl.BlockSpec((None, tr, cdim), lambda i, cr: (cr[0], i, 0)),
                      pl.BlockSpec((3, tr, cdim), lambda i, cr: (0, i, 0))],
            out_specs=pl.BlockSpec((tr, cdim), lambda i, cr: (i, 0))),
        out_shape=_sds((r, cdim), F32),
        compiler_params=_params("parallel"),
    )(chip, part, recv)


def _rms_cast(name, h, gain, tm, hosted=None):
    t = h.shape[0]

    def body(h_ref, g_ref, o_ref):
        _, xh = _rms_stats(h_ref[...])
        o_ref[...] = (xh * g_ref[...]).astype(BF16)

    return _call(name, body, (t // tm,), [_row(tm, D), _const((1, D))], [_row(tm, D)],
                 [_sds((t, D), BF16)], (h, gain), sem=("parallel",), hosted=hosted)


def _ffn_up(name, xn, wgu_t, tm, tn, hosted=None):
    t = xn.shape[0]
    nh = DFF // tn

    def body(x_ref, wg_ref, wu_ref, g_ref, u_ref, a_ref):
        x = x_ref[...]
        g = _dot_nt(x, wg_ref[...])
        u = _dot_nt(x, wu_ref[...])
        g_ref[...] = g.astype(BF16)
        u_ref[...] = u.astype(BF16)
        a_ref[...] = (g * _sigmoid(g) * u).astype(BF16)

    o = pl.BlockSpec((tm, tn), lambda j, i: (i, j))
    return _call(name, body, (nh, t // tm),
                 [pl.BlockSpec((tm, D), lambda j, i: (i, 0)),
                  pl.BlockSpec((tn, D), lambda j, i: (j, 0)),
                  pl.BlockSpec((tn, D), lambda j, i: (j + nh, 0))],
                 [o, o, o], [_sds((t, DFF), BF16)] * 3, (xn, wgu_t, wgu_t),
                 sem=("parallel", "parallel"), hosted=hosted)


def _ffn_down(name, a, wd, h, gain, tm):
    t = a.shape[0]

    def body(a_ref, w_ref, h_ref, g_ref, o_ref, n_ref):
        hn = h_ref[...] + 0.5 * _dot(a_ref[...], w_ref[...])
        o_ref[...] = hn
        _, xh = _rms_stats(hn)
        n_ref[...] = (xh * g_ref[...]).astype(BF16)

    return pl.pallas_call(
        body, name=name, grid=(t // tm,),
        in_specs=[_row(tm, DFF), _const((DFF, D)), _row(tm, D), _const((1, D))],
        out_specs=[_row(tm, D), _row(tm, D)],
        out_shape=[_sds((t, D), F32), _sds((t, D), BF16)],
        compiler_params=_params("parallel"),
    )(a, wd, h, gain)


def _mix_in(name, n, win_t, b_in, tm, tn):
    t = n.shape[0]

    def body(n_ref, w_ref, b_ref, p_ref):
        p_ref[...] = (_dot_nt(n_ref[...], w_ref[...]) + b_ref[...]).astype(BF16)

    return pl.pallas_call(
        body, name=name, grid=(6 * D // tn, t // tm),
        in_specs=[pl.BlockSpec((tm, D), lambda j, i: (i, 0)),
                  pl.BlockSpec((tn, D), lambda j, i: (j, 0)),
                  pl.BlockSpec((1, tn), lambda j, i: (0, j))],
        out_specs=pl.BlockSpec((tm, tn), lambda j, i: (i, j)),
        out_shape=_sds((t, 6 * D), BF16),
        compiler_params=_params("parallel", "parallel"),
    )(n, win_t, b_in)


RC = 64
LANES = 128


def _shift_copies(ext, shifted, tm):
    n = tm + HALO - 8
    for m in range(1, 8):
        shifted[m - 1] = ext[pl.ds(m, n), :]


def _by_residue(offs):
    groups = {}
    for k, off in enumerate(offs):
        q, m = divmod(off, 8)
        groups.setdefault(m, []).append((k, q))
    return groups


def _residue_window(ext, shifted, m, taps, base, cs):
    src = ext if m == 0 else shifted.at[m - 1]
    return src[pl.ds(base, RC + 8 * max(q for _, q in taps)), cs]


def _tap_sum(out_ref, bias_ref, w_ref, ext, shifted, offs, tm):
    groups = _by_residue(offs)

    def chunk(j, carry):
        base = pl.multiple_of(j * RC, RC)
        for c in range(D // LANES):
            cs = pl.ds(c * LANES, LANES)
            acc = jnp.zeros((RC, LANES), F32)
            if bias_ref is not None:
                acc = acc + bias_ref[:, cs]
            for m, taps in groups.items():
                big = _residue_window(ext, shifted, m, taps, base, cs)
                for k, q in taps:
                    acc = acc + w_ref[pl.ds(k, 1), cs] * big[8 * q:8 * q + RC]
            out_ref[pl.ds(base, RC), cs] = acc
        return carry

    lax.fori_loop(0, tm // RC, chunk, 0)


def _tap_corr(dw_ref, dc_ext, ext, shifted, offs, tm):
    groups = _by_residue(offs)
    for c in range(D // LANES):
        cs = pl.ds(c * LANES, LANES)

        def chunk(j, accs, cs=cs):
            base = pl.multiple_of(j * RC, RC)
            dcv = dc_ext[pl.ds(base, RC), cs]
            out = list(accs)
            for m, taps in groups.items():
                big = _residue_window(ext, shifted, m, taps, base, cs)
                for k, q in taps:
                    prod = dcv * big[8 * q:8 * q + RC]
                    part = prod[0:8]
                    for s in range(1, RC // 8):
                        part = part + prod[8 * s:8 * s + 8]
                    out[k] = accs[k] + part
            return tuple(out)

        accs = lax.fori_loop(0, tm // RC, chunk, tuple(jnp.zeros((8, LANES), F32) for _ in offs))
        for k in range(len(offs)):
            dw_ref[pl.ds(k, 1), cs] += jnp.sum(accs[k], axis=0, keepdims=True)


def _conv_fwd(name, p, conv_w, conv_b, ln_g, ln_b, tm):
    t = p.shape[0]

    def body(av_ref, ag_ref, w_ref, cb_ref, lg_ref, lb_ref, c_ref, a_ref, ext, shifted):
        i = pl.program_id(0)

        @pl.when(i == 0)
        def _():
            ext[pl.ds(0, HALO), :] = jnp.zeros((HALO, D), F32)

        ext[pl.ds(HALO, tm), :] = av_ref[...].astype(F32) * _sigmoid(ag_ref[...].astype(F32))
        _shift_copies(ext, shifted, tm)
        _tap_sum(c_ref, cb_ref, w_ref, ext, shifted, [HALO - (CW - 1) + k for k in range(CW)], tm)
        rstd, chat = _ln_stats(c_ref[...])
        ca = chat * lg_ref[...] + lb_ref[...]
        a_ref[...] = (ca * _sigmoid(ca)).astype(BF16)
        ext[pl.ds(0, HALO), :] = ext[pl.ds(tm, HALO), :]

    return pl.pallas_call(
        body, name=name, grid=(t // tm,),
        in_specs=[_row(tm, D, 0), _row(tm, D, 1), _const((HALO, D)), _const((1, D)),
                  _const((1, D)), _const((1, D))],
        out_specs=[_row(tm, D), _row(tm, D)],
        out_shape=[_sds((t, D), F32), _sds((t, D), BF16)],
        scratch_shapes=[pltpu.VMEM((tm + HALO, D), F32), pltpu.VMEM((7, tm + HALO - 8, D), F32)],
        compiler_params=_params("arbitrary"),
    )(p, p, conv_w, conv_b, ln_g, ln_b)


def _sgu_mask():
    rows = lax.broadcasted_iota(jnp.int32, (CHUNK, CHUNK), 0)
    cols = lax.broadcasted_iota(jnp.int32, (CHUNK, CHUNK), 1)
    return cols <= rows


def _sgu_fwd(name, p, ln_g, ln_b, sgu_w, bias_full, tm):
    t = p.shape[0]

    def body(bu_ref, bv_ref, lg_ref, lb_ref, ws_ref, bias_ref, o_ref):
        mask = _sgu_mask()
        _, vhat = _ln_stats(_gelu(bv_ref[...].astype(F32)))
        vn = (vhat * lg_ref[...] + lb_ref[...]).astype(BF16)
        ub = _gelu(bu_ref[...].astype(F32))
        for g in range(GROUPS):
            wm = jnp.where(mask, ws_ref[g], 0.0).astype(BF16)
            cs = slice(g * GD, (g + 1) * GD)
            for cc in range(tm // CHUNK):
                rs = slice(cc * CHUNK, (cc + 1) * CHUNK)
                mixed = _dot(wm, vn[rs, cs]) + bias_ref[:, cs]
                o_ref[rs, cs] = (ub[rs, cs] * mixed).astype(BF16)

    return pl.pallas_call(
        body, name=name, grid=(t // tm,),
        in_specs=[_row(tm, D, 2), _row(tm, D, 3), _const((1, D)), _const((1, D)),
                  _const((GROUPS, CHUNK, CHUNK)), _const((CHUNK, D))],
        out_specs=_row(tm, D), out_shape=_sds((t, D), BF16),
        compiler_params=_params("parallel"),
    )(p, p, ln_g, ln_b, sgu_w, bias_full)


def _merge_fwd(name, act_a, act_b, p, w_a, w_b, w_out, h, gain, tm):
    t = h.shape[0]

    def body(a_ref, b_ref, ga_ref, gb_ref, wa_ref, wb_ref, wo_ref, h_ref, g_ref,
             ya_ref, yb_ref, mg_ref, ho_ref, xn_ref):
        ya = _dot(a_ref[...], wa_ref[...])
        yb = _dot(b_ref[...], wb_ref[...])
        ya_ref[...] = ya.astype(BF16)
        yb_ref[...] = yb.astype(BF16)
        merged = (_sigmoid(ga_ref[...].astype(F32)) * ya
                  + _sigmoid(gb_ref[...].astype(F32)) * yb).astype(BF16)
        mg_ref[...] = merged
        hn = h_ref[...] + _dot(merged, wo_ref[...])
        ho_ref[...] = hn
        _, xh = _rms_stats(hn)
        xn_ref[...] = (xh * g_ref[...]).astype(BF16)

    rb = _row(tm, D)
    return pl.pallas_call(
        body, name=name, grid=(t // tm,),
        in_specs=[rb, rb, _row(tm, D, 4), _row(tm, D, 5), _const((D, D)), _const((D, D)),
                  _const((D, D)), rb, _const((1, D))],
        out_specs=[rb] * 5,
        out_shape=[_sds((t, D), BF16)] * 3 + [_sds((t, D), F32), _sds((t, D), BF16)],
        compiler_params=_params("parallel"),
    )(act_a, act_b, p, p, w_a, w_b, w_out, h, gain)


def _kv_fwd(name, mem, gain, wkv_t):
    def body(m_ref, g_ref, w_ref, mn_ref, k_ref, v_ref):
        _, xh = _rms_stats(m_ref[...])
        mn = (xh * g_ref[...]).astype(BF16)
        mn_ref[...] = mn
        kv = _dot_nt(mn, w_ref[...])
        k_ref[...] = kv[:, :D].astype(BF16)
        v_ref[...] = kv[:, D:].astype(BF16)

    return pl.pallas_call(
        body, name=name,
        out_shape=[_sds((NMEM, D), BF16)] * 3,
        compiler_params=pltpu.CompilerParams(vmem_limit_bytes=VMEM_LIMIT),
    )(mem, gain, wkv_t)


def _softmax_rows(s):
    e = jnp.exp(s - jnp.max(s, axis=-1, keepdims=True))
    return e / jnp.sum(e, axis=-1, keepdims=True)


def _attn_fwd(name, xq, w_q, kb, vb, w_o, h, gain, tm):
    t = h.shape[0]
    scale = 1.0 / math.sqrt(HD)

    def body(x_ref, wq_ref, k_ref, v_ref, wo_ref, h_ref, g_ref, q_ref, o_ref, ho_ref, xn_ref):
        q_ref[...] = _dot(x_ref[...], wq_ref[...]).astype(BF16)
        for hd in range(HEADS):
            cs = slice(hd * HD, (hd + 1) * HD)
            p = _softmax_rows(_dot_nt(q_ref[:, cs], k_ref[:, cs]) * scale)
            o_ref[:, cs] = _dot(p.astype(BF16), v_ref[:, cs]).astype(BF16)
        hn = h_ref[...] + _dot(o_ref[...], wo_ref[...])
        ho_ref[...] = hn
        _, xh = _rms_stats(hn)
        xn_ref[...] = (xh * g_ref[...]).astype(BF16)

    rb = _row(tm, D)
    return pl.pallas_call(
        body, name=name, grid=(t // tm,),
        in_specs=[rb, _const((D, D)), _const((NMEM, D)), _const((NMEM, D)), _const((D, D)), rb,
                  _const((1, D))],
        out_specs=[rb] * 4,
        out_shape=[_sds((t, D), BF16), _sds((t, D), BF16), _sds((t, D), F32), _sds((t, D), BF16)],
        compiler_params=_params("parallel"),
    )(xq, w_q, kb, vb, w_o, h, gain)


def _ffn_down_loss(name, a, wd, h, gain, target, tm):
    t = h.shape[0]
    steps = t // tm

    def body(a_ref, w_ref, h_ref, g_ref, t_ref, dh_ref, dhb_ref, loss_ref, dg_ref, lacc):
        i = pl.program_id(0)
        hv = h_ref[...] + 0.5 * _dot(a_ref[...], w_ref[...])
        r, xh = _rms_stats(hv)
        err = xh * g_ref[...] - t_ref[...]
        _acc(lacc, i == 0, jnp.sum(err * err, axis=0, keepdims=True))
        dy = err * (1.0 / D)
        _acc(dg_ref, i == 0, jnp.sum(dy * xh, axis=0, keepdims=True))
        dxh = dy * g_ref[...]
        dh = r * (dxh - xh * jnp.mean(dxh * xh, axis=-1, keepdims=True))
        dh_ref[...] = dh
        dhb_ref[...] = dh.astype(BF16)

        @pl.when(i == steps - 1)
        def _():
            loss_ref[...] = jnp.zeros((8, 128), F32) + (0.5 / D) * jnp.sum(lacc[...])

    rb = _row(tm, D)
    return pl.pallas_call(
        body, name=name, grid=(steps,),
        in_specs=[_row(tm, DFF), _const((DFF, D)), rb, _const((1, D)), rb],
        out_specs=[rb, rb, _const((8, 128)), _const((1, D))],
        out_shape=[_sds((t, D), F32), _sds((t, D), BF16), _sds((8, 128), F32), _sds((1, D), F32)],
        scratch_shapes=[pltpu.VMEM((1, D), F32)],
        compiler_params=_params("arbitrary"),
    )(a, wd, h, gain, target)


def _ffn_bwd_act(name, dhb, wd, g, u, tm, tn, hosted=None):
    t = dhb.shape[0]

    def body(d_ref, w_ref, g_ref, u_ref, dg_ref, du_ref):
        da = 0.5 * _dot_nt(d_ref[...], w_ref[...])
        gv = g_ref[...].astype(F32)
        sg = _sigmoid(gv)
        dg_ref[...] = (da * u_ref[...].astype(F32) * (sg * (1.0 + gv * (1.0 - sg)))).astype(BF16)
        du_ref[...] = (da * (gv * sg)).astype(BF16)

    o = pl.BlockSpec((tm, tn), lambda j, i: (i, j))
    return _call(name, body, (DFF // tn, t // tm),
                 [pl.BlockSpec((tm, D), lambda j, i: (i, 0)),
                  pl.BlockSpec((tn, D), lambda j, i: (j, 0)), o, o],
                 [o, o], [_sds((t, DFF), BF16)] * 2, (dhb, wd, g, u),
                 sem=("parallel", "parallel"), hosted=hosted)


def _dx_rms_bwd(name, pairs, h, gain, dh_in, tm, hosted=None):
    t = h.shape[0]
    np_ = len(pairs)

    def body(*refs):
        a_refs = refs[:np_]
        b_refs = refs[np_:2 * np_]
        h_ref, g_ref, d_ref, o_ref, ob_ref, dg_ref = refs[2 * np_:]
        dxn = None
        for (a_ref, b_ref, pr) in zip(a_refs, b_refs, pairs):
            y = _dot_nt(a_ref[...], b_ref[...]) if pr[4] else _dot(a_ref[...], b_ref[...])
            dxn = y if dxn is None else dxn + y
        dh, dgain = _rms_bwd(dxn, h_ref[...], g_ref[...])
        _acc(dg_ref, pl.program_id(0) == 0, dgain)
        out = d_ref[...] + dh
        o_ref[...] = out
        ob_ref[...] = out.astype(BF16)

    ins, args = [], []
    for (a, b, blk, rows, tr) in pairs:
        ins.append(_row(tm, a.shape[1]))
        args.append(a)
    for (a, b, blk, rows, tr) in pairs:
        ins.append(pl.BlockSpec((rows, b.shape[1]), lambda i, _b=blk: (_b, 0)))
        args.append(b)
    rb = _row(tm, D)
    ins += [rb, _const((1, D)), rb]
    args += [h, gain, dh_in]
    return _call(name, body, (t // tm,), ins, [rb, rb, _const((1, D))],
                 [_sds((t, D), F32), _sds((t, D), BF16), _sds((1, D), F32)], args,
                 sem=("arbitrary",), hosted=hosted)


def _mm_tn(name, a, b, scale, tmo, tk):
    t, m = a.shape
    n = b.shape[1]
    tk = min(tk, t)
    steps = t // tk

    def body(a_ref, b_ref, o_ref, acc):
        k = pl.program_id(1)
        _acc(acc, k == 0, _dot_tn(a_ref[...], b_ref[...]))

        @pl.when(k == steps - 1)
        def _():
            o_ref[...] = (acc[...] * scale).astype(o_ref.dtype)

    return pl.pallas_call(
        body, name=name, grid=(m // tmo, steps),
        in_specs=[pl.BlockSpec((tk, tmo), lambda i, k: (k, i)),
                  pl.BlockSpec((tk, n), lambda i, k: (k, 0))],
        out_specs=pl.BlockSpec((tmo, n), lambda i, k: (i, 0)),
        out_shape=_sds((m, n), BF16),
        scratch_shapes=[pltpu.VMEM((tmo, n), F32)],
        compiler_params=_params("parallel", "arbitrary"),
    )(a, b)


def _attn_bwd(name, dhb, w_o, qb, kb, vb, tm):
    t = dhb.shape[0]
    scale = 1.0 / math.sqrt(HD)

    def body(d_ref, wo_ref, q_ref, k_ref, v_ref, dq_ref, dk_ref, dv_ref, do_s):
        i = pl.program_id(0)

        @pl.when(i == 0)
        def _():
            dk_ref[...] = jnp.zeros_like(dk_ref)
            dv_ref[...] = jnp.zeros_like(dv_ref)

        do_s[...] = _dot_nt(d_ref[...], wo_ref[...]).astype(BF16)
        for hd in range(HEADS):
            cs = slice(hd * HD, (hd + 1) * HD)
            q = q_ref[:, cs]
            p = _softmax_rows(_dot_nt(q, k_ref[:, cs]) * scale)
            do = do_s[:, cs]
            dp = _dot_nt(do, v_ref[:, cs])
            ds = (p * (dp - jnp.sum(dp * p, axis=-1, keepdims=True)) * scale).astype(BF16)
            dq_ref[:, cs] = _dot(ds, k_ref[:, cs]).astype(BF16)
            dk_ref[:, cs] += _dot_tn(ds, q)
            dv_ref[:, cs] += _dot_tn(p.astype(BF16), do)

    rb = _row(tm, D)
    return pl.pallas_call(
        body, name=name, grid=(t // tm,),
        in_specs=[rb, _const((D, D)), rb, _const((NMEM, D)), _const((NMEM, D))],
        out_specs=[rb, _const((NMEM, D)), _const((NMEM, D))],
        out_shape=[_sds((t, D), BF16), _sds((NMEM, D), F32), _sds((NMEM, D), F32)],
        scratch_shapes=[pltpu.VMEM((tm, D), BF16)],
        compiler_params=_params("arbitrary"),
    )(dhb, w_o, qb, kb, vb)


def _kv_bwd(name, dk, dv, memn, wkv_t, mem, gain):
    def body(dk_ref, dv_ref, mn_ref, w_ref, m_ref, g_ref, dw_ref, dg_ref):
        dkb = dk_ref[...].astype(BF16)
        dvb = dv_ref[...].astype(BF16)
        mn = mn_ref[...]
        dw_ref[pl.ds(0, D), :] = _dot_tn(dkb, mn).astype(BF16)
        dw_ref[pl.ds(D, D), :] = _dot_tn(dvb, mn).astype(BF16)
        dmn = _dot(dkb, w_ref[pl.ds(0, D), :]) + _dot(dvb, w_ref[pl.ds(D, D), :])
        _, xh = _rms_stats(m_ref[...])
        dg_ref[...] = jnp.sum(dmn * xh, axis=0, keepdims=True)

    return pl.pallas_call(
        body, name=name,
        out_shape=[_sds((2 * D, D), BF16), _sds((1, D), F32)],
        compiler_params=pltpu.CompilerParams(vmem_limit_bytes=VMEM_LIMIT),
    )(dk, dv, memn, wkv_t, mem, gain)


def _merge_bwd(name, dhb, w_out, ya, yb, p, tm):
    t = dhb.shape[0]

    def body(d_ref, w_ref, ya_ref, yb_ref, ga_ref, gb_ref, dya_ref, dyb_ref, dp_ref, cs_ref):
        dm = _dot_nt(d_ref[...], w_ref[...])
        sa = _sigmoid(ga_ref[...].astype(F32))
        sb = _sigmoid(gb_ref[...].astype(F32))
        dya_ref[...] = (dm * sa).astype(BF16)
        dyb_ref[...] = (dm * sb).astype(BF16)
        dga = dm * ya_ref[...].astype(F32) * (sa * (1.0 - sa))
        dgb = dm * yb_ref[...].astype(F32) * (sb * (1.0 - sb))
        dp_ref[:, pl.ds(0, D)] = dga.astype(BF16)
        dp_ref[:, pl.ds(D, D)] = dgb.astype(BF16)
        first = pl.program_id(0) == 0

        @pl.when(first)
        def _():
            cs_ref[...] = jnp.zeros_like(cs_ref)
        cs_ref[:, pl.ds(0, D)] += jnp.sum(dga, axis=0, keepdims=True)
        cs_ref[:, pl.ds(D, D)] += jnp.sum(dgb, axis=0, keepdims=True)

    rb = _row(tm, D)
    return pl.pallas_call(
        body, name=name, grid=(t // tm,),
        in_specs=[rb, _const((D, D)), rb, rb, _row(tm, D, 4), _row(tm, D, 5)],
        out_specs=[rb, rb, _row(tm, 2 * D), _const((1, 2 * D))],
        out_shape=[_sds((t, D), BF16), _sds((t, D), BF16), _sds((t, 2 * D), BF16),
                   _sds((1, 2 * D), F32)],
        compiler_params=_params("arbitrary"),
    )(dhb, w_out, ya, yb, p, p)


def _conv_bwd(name, dya, w_a, c, p, conv_w, ln_g, ln_b, tm):
    t = dya.shape[0]
    steps = t // tm
    hb = tm // HALO

    def rev(i):
        return steps - 1 - i

    def body(dy_ref, wa_ref, c_ref, av_ref, ag_ref, avh_ref, agh_ref, w_ref, lg_ref, lb_ref,
             dp_ref, cs_ref, dw_ref, dcb_ref, dlg_ref, dlb_ref, dc_ext, a_ext, dc_sh, a_sh, da0_s):
        i = pl.program_id(0)
        first = i == 0

        @pl.when(first)
        def _():
            dc_ext[pl.ds(tm, HALO), :] = jnp.zeros((HALO, D), F32)
            dw_ref[...] = jnp.zeros_like(dw_ref)
            cs_ref[...] = jnp.zeros_like(cs_ref)

        d_act = _dot_nt(dy_ref[...], wa_ref[...])
        rstd, chat = _ln_stats(c_ref[...])
        ca = chat * lg_ref[...] + lb_ref[...]
        sc = _sigmoid(ca)
        dca = d_act * (sc * (1.0 + ca * (1.0 - sc)))
        _acc(dlg_ref, first, jnp.sum(dca * chat, axis=0, keepdims=True))
        _acc(dlb_ref, first, jnp.sum(dca, axis=0, keepdims=True))
        dc = _ln_bwd(dca, chat, rstd, lg_ref[...])
        _acc(dcb_ref, first, jnp.sum(dc, axis=0, keepdims=True))
        dc_ext[pl.ds(0, tm), :] = dc

        av = av_ref[...].astype(F32)
        sg = _sigmoid(ag_ref[...].astype(F32))
        a_ext[pl.ds(HALO, tm), :] = av * sg
        halo = avh_ref[...].astype(F32) * _sigmoid(agh_ref[...].astype(F32))
        a_ext[pl.ds(0, HALO), :] = jnp.where(i == steps - 1, 0.0, halo)

        _shift_copies(dc_ext, dc_sh, tm)
        _shift_copies(a_ext, a_sh, tm)
        _tap_sum(da0_s, None, w_ref, dc_ext, dc_sh, [CW - 1 - k for k in range(CW)], tm)
        _tap_corr(dw_ref, dc_ext, a_ext, a_sh, [HALO - (CW - 1) + k for k in range(CW)], tm)
        da0 = da0_s[...]
        dav = da0 * sg
        dag = da0 * av * (sg * (1.0 - sg))
        dp_ref[:, pl.ds(0, D)] = dav.astype(BF16)
        dp_ref[:, pl.ds(D, D)] = dag.astype(BF16)
        cs_ref[:, pl.ds(0, D)] += jnp.sum(dav, axis=0, keepdims=True)
        cs_ref[:, pl.ds(D, D)] += jnp.sum(dag, axis=0, keepdims=True)
        dc_ext[pl.ds(tm, HALO), :] = dc_ext[pl.ds(0, HALO), :]

    def rrow(cols, cb=0):
        return pl.BlockSpec((tm, cols), lambda i, _cb=cb: (rev(i), _cb))

    def halo_spec(cb):
        return pl.BlockSpec((HALO, D), lambda i, _cb=cb: (jnp.maximum(rev(i) * hb - 1, 0), _cb))

    return pl.pallas_call(
        body, name=name, grid=(steps,),
        in_specs=[rrow(D), _const((D, D)), rrow(D), rrow(D, 0), rrow(D, 1), halo_spec(0),
                  halo_spec(1), _const((HALO, D)), _const((1, D)), _const((1, D))],
        out_specs=[rrow(2 * D), _const((1, 2 * D)), _const((HALO, D)), _const((1, D)),
                   _const((1, D)), _const((1, D))],
        out_shape=[_sds((t, 2 * D), BF16), _sds((1, 2 * D), F32), _sds((HALO, D), F32),
                   _sds((1, D), F32), _sds((1, D), F32), _sds((1, D), F32)],
        scratch_shapes=[pltpu.VMEM((tm + HALO, D), F32), pltpu.VMEM((tm + HALO, D), F32),
                        pltpu.VMEM((7, tm + HALO - 8, D), F32), pltpu.VMEM((7, tm + HALO - 8, D), F32),
                        pltpu.VMEM((tm, D), F32)],
        compiler_params=_params("arbitrary"),
    )(dya, w_a, c, p, p, p, p, conv_w, ln_g, ln_b)


def _sgu_bwd(name, dyb, w_b, p, ln_g, ln_b, sgu_w, bias_full, tm):
    t = dyb.shape[0]
    steps = t // tm

    def body(dy_ref, wb_ref, bu_ref, bv_ref, lg_ref, lb_ref, ws_ref, bias_ref,
             dp_ref, cs_ref, dws_ref, dsb_ref, dlg_ref, dlb_ref, dub_s, dvn_s, dbias_s):
        i = pl.program_id(0)
        first = i == 0
        mask = _sgu_mask()

        @pl.when(first)
        def _():
            dws_ref[...] = jnp.zeros_like(dws_ref)
            dbias_s[...] = jnp.zeros_like(dbias_s)
            cs_ref[...] = jnp.zeros_like(cs_ref)

        dob = _dot_nt(dy_ref[...], wb_ref[...])
        bu = bu_ref[...].astype(F32)
        bv = bv_ref[...].astype(F32)
        ub = _gelu(bu)
        rstd, vhat = _ln_stats(_gelu(bv))
        vn = (vhat * lg_ref[...] + lb_ref[...]).astype(BF16)
        for g in range(GROUPS):
            wm = jnp.where(mask, ws_ref[g], 0.0).astype(BF16)
            cs = slice(g * GD, (g + 1) * GD)
            for cc in range(tm // CHUNK):
                rs = slice(cc * CHUNK, (cc + 1) * CHUNK)
                vblk = vn[rs, cs]
                mixed = _dot(wm, vblk) + bias_ref[:, cs]
                dob_blk = dob[rs, cs]
                dub_s[rs, cs] = dob_blk * mixed
                dmixed = dob_blk * ub[rs, cs]
                dbias_s[:, cs] += dmixed
                dmb = dmixed.astype(BF16)
                dws_ref[g] += _dot_nt(dmb, vblk)
                dvn_s[rs, cs] = _dot_tn(wm, dmb)
        dbu = dub_s[...] * _gelu_grad(bu)
        dvn = dvn_s[...]
        _acc(dlg_ref, first, jnp.sum(dvn * vhat, axis=0, keepdims=True))
        _acc(dlb_ref, first, jnp.sum(dvn, axis=0, keepdims=True))
        dbv = _ln_bwd(dvn, vhat, rstd, lg_ref[...]) * _gelu_grad(bv)
        dp_ref[:, pl.ds(0, D)] = dbu.astype(BF16)
        dp_ref[:, pl.ds(D, D)] = dbv.astype(BF16)
        cs_ref[:, pl.ds(0, D)] += jnp.sum(dbu, axis=0, keepdims=True)
        cs_ref[:, pl.ds(D, D)] += jnp.sum(dbv, axis=0, keepdims=True)

        @pl.when(i == steps - 1)
        def _():
            lane = lax.broadcasted_iota(jnp.int32, (CHUNK, CHUNK), 1)
            dsb = jnp.zeros((CHUNK, CHUNK), F32)
            for g in range(GROUPS):
                dws_ref[g] = jnp.where(mask, dws_ref[g], 0.0)
                dsb = jnp.where(lane == g, jnp.sum(dbias_s[:, g * GD:(g + 1) * GD], axis=1, keepdims=True), dsb)
            dsb_ref[...] = dsb

    rb = _row(tm, D)
    return pl.pallas_call(
        body, name=name, grid=(steps,),
        in_specs=[rb, _const((D, D)), _row(tm, D, 2), _row(tm, D, 3), _const((1, D)), _const((1, D)),
                  _const((GROUPS, CHUNK, CHUNK)), _const((CHUNK, D))],
        out_specs=[_row(tm, 2 * D), _const((1, 2 * D)), _const((GROUPS, CHUNK, CHUNK)),
                   _const((CHUNK, CHUNK)), _const((1, D)), _const((1, D))],
        out_shape=[_sds((t, 2 * D), BF16), _sds((1, 2 * D), F32), _sds((GROUPS, CHUNK, CHUNK), F32),
                   _sds((CHUNK, CHUNK), F32), _sds((1, D), F32), _sds((1, D), F32)],
        scratch_shapes=[pltpu.VMEM((tm, D), F32), pltpu.VMEM((tm, D), F32), pltpu.VMEM((CHUNK, D), F32)],
        compiler_params=_params("arbitrary"),
    )(dyb, w_b, p, p, ln_g, ln_b, sgu_w, bias_full)


def _adam_math(w, g, m, v):
    m = B1 * m + (1.0 - B1) * g
    v = B2 * v + (1.0 - B2) * (g * g)
    m_hat = m / (1.0 - B1 ** STEP)
    v_hat = v / (1.0 - B2 ** STEP)
    delta = -LR * (m_hat / (jnp.sqrt(v_hat) + EPS_ADAM) + WD * w)
    return delta, m, v


def _adamw(name, w, g, m, v, tr):
    r, cdim = w.shape

    def body(w_ref, g_ref, m_ref, v_ref, d_ref, mo_ref, vo_ref):
        d, mn, vn = _adam_math(w_ref[...], g_ref[...], m_ref[...], v_ref[...])
        d_ref[...] = d
        mo_ref[...] = mn
        vo_ref[...] = vn

    blk = pl.BlockSpec((tr, cdim), lambda i: (i, 0))
    return pl.pallas_call(
        body, name=name, grid=(r // tr,), in_specs=[blk] * 4, out_specs=[blk] * 3,
        out_shape=[_sds((r, cdim), F32)] * 3, compiler_params=_params("parallel"),
    )(w, g, m, v)


def _adamw_small(name, w, g8, m, v):
    r, cdim = w.shape

    def body(w_ref, g_ref, m_ref, v_ref, go_ref, d_ref, mo_ref, vo_ref):
        g = g_ref[0]
        for k in range(1, NDEV):
            g = g + g_ref[k]
        go_ref[...] = g
        d, mn, vn = _adam_math(w_ref[...], g, m_ref[...], v_ref[...])
        d_ref[...] = d
        mo_ref[...] = mn
        vo_ref[...] = vn

    return pl.pallas_call(
        body, name=name, out_shape=[_sds((r, cdim), F32)] * 4,
        compiler_params=pltpu.CompilerParams(vmem_limit_bytes=VMEM_LIMIT),
    )(w, g8, m, v)


_BIG = [("ffn1_w_gu", 704, True), ("ffn1_w_down", 352, False), ("w_in", 768, True),
        ("w_a_out", 128, False), ("w_b_out", 128, False), ("w_out", 128, False),
        ("w_q", 128, False), ("w_kv", 256, True), ("w_o", 128, False),
        ("ffn2_w_gu", 704, True), ("ffn2_w_down", 352, False)]
_BIG_ROWS = sum(r for _, r, _ in _BIG)

_SMALL = [("ffn1_norm", 1), ("mix_norm", 1), ("b_in", 6), ("conv_w", HALO), ("conv_b", 1),
          ("conv_ln_g", 1), ("conv_ln_b", 1), ("sgu_ln_g", 1), ("sgu_ln_b", 1), ("sgu_w", 64),
          ("sgu_b", 1), ("xattn_norm", 1), ("mem_norm", 1), ("ffn2_norm", 1), ("final_norm", 1)]
_SMALL_ROWS = 120


def _pack_small(vals, my_dev):
    rows = []
    for name, nrows in _SMALL:
        a = vals[name].astype(F32)
        if name == "conv_w":
            if a.shape[-1] != D:
                slab = jnp.zeros((HALO, D), F32)
                a = lax.dynamic_update_slice(slab, jnp.pad(a.reshape(CW, -1), ((0, HALO - CW), (0, 0))),
                                             (0, my_dev * (D // NDEV)))
            else:
                a = jnp.pad(a.reshape(CW, D), ((0, HALO - CW), (0, 0)))
        elif name == "sgu_b":
            a = jnp.pad(a.reshape(1, -1), ((0, 0), (0, D - GROUPS * CHUNK)))
        else:
            a = a.reshape(nrows, D)
        rows.append(a)
    packed = jnp.concatenate(rows, axis=0)
    return jnp.pad(packed, ((0, _SMALL_ROWS - packed.shape[0]), (0, 0)))


def _unpack_small(packed, shapes, my_dev):
    out, off = {}, 0
    for name, nrows in _SMALL:
        a = packed[off:off + nrows]
        off += nrows
        if name == "conv_w":
            a = lax.dynamic_slice(a, (0, my_dev * (D // NDEV)), (CW, D // NDEV))
        elif name == "sgu_b":
            a = a[:, :GROUPS * CHUNK]
        out[name] = a.reshape(shapes[name])
    return out


def kernel(x, mem, ffn1_norm, ffn1_w_gu, ffn1_w_down, mix_norm, w_in, b_in, conv_w, conv_b, conv_ln_g, conv_ln_b, w_a_out, sgu_ln_g, sgu_ln_b, sgu_w, sgu_b, w_b_out, w_out, xattn_norm, mem_norm, w_q, w_kv, w_o, ffn2_norm, ffn2_w_gu, ffn2_w_down, final_norm, loss_target, m_ffn1_norm, m_ffn1_w_gu, m_ffn1_w_down, m_mix_norm, m_w_in, m_b_in, m_conv_w, m_conv_b, m_conv_ln_g, m_conv_ln_b, m_w_a_out, m_sgu_ln_g, m_sgu_ln_b, m_sgu_w, m_sgu_b, m_w_b_out, m_w_out, m_xattn_norm, m_mem_norm, m_w_q, m_w_kv, m_w_o, m_ffn2_norm, m_ffn2_w_gu, m_ffn2_w_down, m_final_norm, v_ffn1_norm, v_ffn1_w_gu, v_ffn1_w_down, v_mix_norm, v_w_in, v_b_in, v_conv_w, v_conv_b, v_conv_ln_g, v_conv_ln_b, v_w_a_out, v_sgu_ln_g, v_sgu_ln_b, v_sgu_w, v_sgu_b, v_w_b_out, v_w_out, v_xattn_norm, v_mem_norm, v_w_q, v_w_kv, v_w_o, v_ffn2_norm, v_ffn2_w_gu, v_ffn2_w_down, v_final_norm):
    env = dict(locals())
    names = [n for n, _, _ in _BIG] + [n for n, _ in _SMALL]
    w = {n: env[n] for n in names}
    mom = {n: env["m_" + n] for n in names}
    vel = {n: env["v_" + n] for n in names}

    ax, ay, ac = lax.axis_index("x"), lax.axis_index("y"), lax.axis_index("c")
    my_chip = 2 * ax + ay
    my_dev = 2 * my_chip + ac

    t = x.shape[1]
    tm = min(512, t)
    tm_s = min(512, t)
    tm_c = min(256, t)
    xs = x.reshape(t, D)
    tgt = loss_target.reshape(t, D)
    mem2 = mem.reshape(NMEM, D)

    first, rest = _BIG[:1], _BIG[1:]

    def shard_block(entries):
        return jnp.concatenate(
            [(w[n][0].T if tr else w[n][0]).astype(BF16) for n, _, tr in entries], axis=0)

    def split(full, entries):
        out, off = {}, 0
        for n, rows, _ in entries:
            out[n] = full[:, off:off + rows, :].reshape(NDEV * rows, D)
            off += rows
        return out

    conv_slab = lax.dynamic_update_slice(
        jnp.zeros((HALO, D), F32), jnp.pad(conv_w[0], ((0, HALO - CW), (0, 0))), (0, my_dev * (D // NDEV)))
    bias_full = jnp.repeat(sgu_b[0].T, GD, axis=1)
    b_in2 = b_in.reshape(1, 6 * D)

    (xn1,), (full_first, conv_w8) = _rms_cast(
        "norm_x", xs, ffn1_norm, tm, hosted=[_gather_comm(shard_block(first)), _gather_comm(conv_slab)])
    conv_w_pad = jnp.sum(conv_w8, axis=0)
    wf = split(full_first, first)
    (g1, u1, a1), (full_rest,) = _ffn_up("ffn1_up", xn1, wf["ffn1_w_gu"], tm, 1408,
                                         hosted=_gather_comm(shard_block(rest)))
    wf.update(split(full_rest, rest))
    h1, n_mix = _ffn_down("ffn1_down", a1, wf["ffn1_w_down"], xs, mix_norm, tm)
    p = _mix_in("mix_in", n_mix, wf["w_in"], b_in2, tm, 1536)
    c_conv, act_a = _conv_fwd("conv_fwd", p, conv_w_pad, conv_b, conv_ln_g, conv_ln_b, tm_c)
    act_b = _sgu_fwd("sgu_fwd", p, sgu_ln_g, sgu_ln_b, sgu_w[0], bias_full, tm_s)
    ya, yb, merged, h2, xq = _merge_fwd("merge_fwd", act_a, act_b, p, wf["w_a_out"], wf["w_b_out"],
                                        wf["w_out"], h1, xattn_norm, tm_s)
    memn, kb, vb = _kv_fwd("kv_fwd", mem2, mem_norm, wf["w_kv"])
    qb, ob, h3, xn4 = _attn_fwd("attn_fwd", xq, wf["w_q"], kb, vb, wf["w_o"], h2, ffn2_norm, tm_s)
    (g2, u2, a2), _ = _ffn_up("ffn2_up", xn4, wf["ffn2_w_gu"], tm, 1408)
    dh4, dh4b, loss_blk, d_final = _ffn_down_loss("ffn2_down_loss", a2, wf["ffn2_w_down"], h3,
                                                  final_norm.reshape(1, D), tgt, tm)

    gb = {}
    gs = {}
    gs["final_norm"] = d_final

    def ffn_bwd(tag, dhb, dh, g, u, a, xn, h_in, gain, wgu_t, wd, act_hosted=None, dx_hosted=None):
        (dg, du), act_out = _ffn_bwd_act(tag + "_bwd_act", dhb, wd, g, u, tm, 1408, hosted=act_hosted)
        gwd = _mm_tn(tag + "_dw_down", a, dhb, 0.5, 1408, TK)
        gwgu = jnp.concatenate([_mm_tn(tag + "_dw_gate", dg, xn, 1.0, 1408, TK),
                                _mm_tn(tag + "_dw_up", du, xn, 1.0, 1408, TK)], axis=0)
        (dh_o, dhb_o, dgain), dx_out = _dx_rms_bwd(
            tag + "_bwd_dx", [(dg, wgu_t, 0, DFF, False), (du, wgu_t, 1, DFF, False)], h_in, gain, dh, tm_s,
            hosted=dx_hosted(gwgu, gwd) if dx_hosted else None)
        return dh_o, dhb_o, dgain, gwgu, gwd, act_out, dx_out

    dh3, dh3b, gs["ffn2_norm"], gb["ffn2_w_gu"], gb["ffn2_w_down"], _, _ = ffn_bwd(
        "ffn2", dh4b, dh4, g2, u2, a2, xn4, h3, ffn2_norm, wf["ffn2_w_gu"], wf["ffn2_w_down"])

    gb["w_o"] = _mm_tn("dw_o", ob, dh3b, 1.0, 1024, TK)
    dq, dk, dv = _attn_bwd("attn_bwd", dh3b, wf["w_o"], qb, kb, vb, tm_s)
    gb["w_kv"], gs["mem_norm"] = _kv_bwd("kv_bwd", dk, dv, memn, wf["w_kv"], mem2, mem_norm)
    gb["w_q"] = _mm_tn("dw_q", xq, dq, 1.0, 1024, TK)
    (dh2, dh2b, gs["xattn_norm"]), _ = _dx_rms_bwd(
        "attn_bwd_dx", [(dq, wf["w_q"], 0, D, True)], h2, xattn_norm, dh3, tm_s)

    gb["w_out"] = _mm_tn("dw_out", merged, dh2b, 1.0, 1024, TK)
    dya, dyb, dp_g, cs_g = _merge_bwd("merge_bwd", dh2b, wf["w_out"], ya, yb, p, tm_s)
    gb["w_a_out"] = _mm_tn("dw_a", act_a, dya, 1.0, 1024, TK)
    gb["w_b_out"] = _mm_tn("dw_b", act_b, dyb, 1.0, 1024, TK)
    dp_a, cs_a, d_convw, gs["conv_b"], gs["conv_ln_g"], gs["conv_ln_b"] = _conv_bwd(
        "conv_bwd", dya, wf["w_a_out"], c_conv, p, conv_w_pad, conv_ln_g, conv_ln_b, tm_c)
    dp_b, cs_b, d_sguw, d_sgub, gs["sgu_ln_g"], gs["sgu_ln_b"] = _sgu_bwd(
        "sgu_bwd", dyb, wf["w_b_out"], p, sgu_ln_g, sgu_ln_b, sgu_w[0], bias_full, tm_s)
    gs["conv_w"] = d_convw[:CW].reshape(1, CW, D)
    gs["sgu_w"] = d_sguw
    gs["sgu_b"] = d_sgub[:, :GROUPS].T
    gs["b_in"] = jnp.concatenate([cs_a, cs_b, cs_g], axis=1)
    gb["w_in"] = jnp.concatenate([_mm_tn("dw_in_a", dp_a, n_mix, 1.0, 1024, TK),
                                  _mm_tn("dw_in_b", dp_b, n_mix, 1.0, 1024, TK),
                                  _mm_tn("dw_in_g", dp_g, n_mix, 1.0, 1024, TK)], axis=0)
    core = ac.astype(jnp.int32).reshape(1)
    chip = my_chip.astype(jnp.int32).reshape(1)

    def pack(entries):
        return jnp.concatenate([gb[n].reshape(4, 2, rows, D) for n, rows, _ in entries], axis=2)

    last_g, rest_g = _BIG[:2], _BIG[2:]
    g4_rest = pack(rest_g)
    (dh1, dh1b, gs["mix_norm"]), (sib_rest,) = _dx_rms_bwd(
        "mix_bwd_dx", [(dp_a, wf["w_in"], 0, 2 * D, False), (dp_b, wf["w_in"], 1, 2 * D, False),
                       (dp_g, wf["w_in"], 2, 2 * D, False)], h1, mix_norm, dh2, tm_s,
        hosted=_pair_exchange_comm(g4_rest))
    part_rest = _pair_sum("grads_pair_sum_rest", g4_rest, sib_rest, core, 544)

    def ffn1_exchange(gwgu, gwd):
        return _all_to_all_comm(jnp.concatenate(
            [gwgu.reshape(NDEV, -1, D), gwd.reshape(NDEV, -1, D)], axis=1))

    dx, _, gs["ffn1_norm"], _, _, (chips_rest,), (last_slots,) = ffn_bwd(
        "ffn1", dh1b, dh1, g1, u1, a1, xn1, xs, ffn1_norm, wf["ffn1_w_gu"], wf["ffn1_w_down"],
        act_hosted=_chip_exchange_comm(part_rest), dx_hosted=ffn1_exchange)
    gsum_rest = _chip_sum("grads_chip_sum_rest", part_rest, chips_rest, chip, 544)
    gsum_last = _slot_sum("grads_slot_sum_ffn1", last_slots, 528)

    grads, deltas, new_m, new_v = {}, {}, {}, {}
    for entries, gsum in ((last_g, gsum_last), (rest_g, gsum_rest)):
        off = 0
        for name, rows, transposed in entries:
            gsh = gsum[off:off + rows]
            off += rows
            gsh = gsh.T if transposed else gsh
            d, mo, vo = _adamw("adamw_" + name, w[name][0], gsh, mom[name][0], vel[name][0], gsh.shape[0] // 2)
            grads[name], deltas[name], new_m[name], new_v[name] = gsh[None], d[None], mo[None], vo[None]

    shapes = {n: w[n].shape for n, _ in _SMALL}
    (g8,) = _run_comm("gather_small_grads", _gather_comm(_pack_small(gs, my_dev)))
    sg, sd, sm, sv = _adamw_small("adamw_small", _pack_small(w, my_dev), g8,
                                  _pack_small(mom, my_dev), _pack_small(vel, my_dev))
    for dst, src in ((grads, sg), (deltas, sd), (new_m, sm), (new_v, sv)):
        dst.update(_unpack_small(src, shapes, my_dev))

    loss = lax.psum(loss_blk[0, 0], AXES)
    order = ["ffn1_norm", "ffn1_w_gu", "ffn1_w_down", "mix_norm", "w_in", "b_in", "conv_w", "conv_b",
             "conv_ln_g", "conv_ln_b", "w_a_out", "sgu_ln_g", "sgu_ln_b", "sgu_w", "sgu_b", "w_b_out",
             "w_out", "xattn_norm", "mem_norm", "w_q", "w_kv", "w_o", "ffn2_norm", "ffn2_w_gu",
             "ffn2_w_down", "final_norm"]
    return (loss, dx.reshape(x.shape), *[grads[n] for n in order], *[deltas[n] for n in order],
            *[new_m[n] for n in order], *[new_v[n] for n in order])
```

```python
import functools
import math

import jax
import jax.numpy as jnp
from jax import lax
from jax.experimental import pallas as pl
from jax.experimental.pallas import tpu as pltpu

F32 = jnp.float32
BF16 = jnp.bfloat16
MESH = pl.DeviceIdType.MESH
AXES = ("x", "y", "c")

D = 1024
DFF = 2816
NMEM = 256
HEADS = 4
HD = D // HEADS
CW = 31
HALO = 32
CHUNK = 128
GROUPS = 4
GD = D // GROUPS
EPS_RMS = 1e-6
EPS_LN = 1e-5
LR, B1, B2, EPS_ADAM, WD, STEP = 0.001, 0.9, 0.999, 1e-08, 0.01, 10
NDEV = 8
VMEM_LIMIT = 56 * 1024 * 1024
TK = 2048


def _params(*sem):
    return pltpu.CompilerParams(dimension_semantics=sem, vmem_limit_bytes=VMEM_LIMIT)


def _dot(a, b):
    return jnp.dot(a, b, preferred_element_type=F32)


def _dot_nt(a, b):
    return lax.dot_general(a, b, (((1,), (1,)), ((), ())), preferred_element_type=F32)


def _dot_tn(a, b):
    return lax.dot_general(a, b, (((0,), (0,)), ((), ())), preferred_element_type=F32)


def _sigmoid(x):
    return 0.5 * jnp.tanh(0.5 * x) + 0.5


_GELU_C = math.sqrt(2.0 / math.pi)


def _gelu_with_grad(x):
    x2 = x * x
    t = jnp.tanh(_GELU_C * (x + 0.044715 * (x2 * x)))
    half = 0.5 * (1.0 + t)
    return x * half, half + 0.5 * x * (1.0 - t * t) * (_GELU_C * (1.0 + 3.0 * 0.044715 * x2))


def _gelu(x):
    return _gelu_with_grad(x)[0]


def _rms_stats(h):
    r = lax.rsqrt(jnp.mean(h * h, axis=-1, keepdims=True) + EPS_RMS)
    return r, h * r


def _rms_bwd(dxn, h, gain):
    r, xh = _rms_stats(h)
    dgain = jnp.sum(dxn * xh, axis=0, keepdims=True)
    dxh = dxn * gain
    dh = r * (dxh - xh * jnp.mean(dxh * xh, axis=-1, keepdims=True))
    return dh, dgain


def _ln_stats(c):
    mu = jnp.mean(c, axis=-1, keepdims=True)
    xc = c - mu
    rstd = lax.rsqrt(jnp.mean(xc * xc, axis=-1, keepdims=True) + EPS_LN)
    return rstd, xc * rstd


def _ln_bwd(dy, xhat, rstd, g):
    dxh = dy * g
    return rstd * (dxh - jnp.mean(dxh, axis=-1, keepdims=True)
                   - xhat * jnp.mean(dxh * xhat, axis=-1, keepdims=True))


def _row(tm, cols, cb=0):
    return pl.BlockSpec((tm, cols), lambda i, _cb=cb: (i, _cb))


def _const(shape):
    n = len(shape)
    return pl.BlockSpec(shape, lambda *_: (0,) * n)


def _sds(shape, dtype):
    return jax.ShapeDtypeStruct(shape, dtype)


def _acc(ref, first, val):
    @pl.when(first)
    def _():
        ref[...] = jnp.zeros_like(ref)
    ref[...] += val


class _Comm:
    def __init__(self, args, out_shapes, scratch, start, finish):
        self.args, self.out_shapes, self.scratch = args, out_shapes, scratch
        self.start, self.finish = start, finish


_ANY = pl.BlockSpec(memory_space=pl.ANY)


def _run_comm(name, comm):
    ni, no = len(comm.args), len(comm.out_shapes)

    def body(*refs):
        ins, outs, sems = refs[:ni], refs[ni:ni + no], refs[ni + no:]
        comm.start(ins, outs, sems)
        comm.finish(ins, outs, sems)

    return pl.pallas_call(
        body, name=name, out_shape=list(comm.out_shapes), in_specs=[_ANY] * ni, out_specs=[_ANY] * no,
        scratch_shapes=list(comm.scratch),
    )(*comm.args)


def _call(name, body, grid, in_specs, out_specs, out_shape, args, scratch=(), sem=None, hosted=None):
    n_in, n_out, n_scr = len(in_specs), len(out_specs), len(scratch)
    if not hosted:
        outs = pl.pallas_call(
            body, name=name, grid=grid, in_specs=list(in_specs), out_specs=list(out_specs),
            out_shape=list(out_shape), scratch_shapes=list(scratch), compiler_params=_params(*sem),
        )(*args)
        return outs, []
    comms = list(hosted) if isinstance(hosted, (list, tuple)) else [hosted]
    hi = sum(len(cm.args) for cm in comms)
    ho = sum(len(cm.out_shapes) for cm in comms)

    def wrapped(*refs):
        ins, h_in = refs[:n_in], refs[n_in:n_in + hi]
        o0 = n_in + hi
        outs, h_out = refs[o0:o0 + n_out], refs[o0 + n_out:o0 + n_out + ho]
        s0 = o0 + n_out + ho
        scr, h_sems = refs[s0:s0 + n_scr], refs[s0 + n_scr:]
        ids = [pl.program_id(a) for a in range(len(grid))]
        first = functools.reduce(jnp.logical_and, [i == 0 for i in ids])
        last = functools.reduce(jnp.logical_and, [i == g - 1 for i, g in zip(ids, grid)])
        parts, a0, b0, c0 = [], 0, 0, 0
        for cm in comms:
            na, nb, nc = len(cm.args), len(cm.out_shapes), len(cm.scratch)
            parts.append((cm, h_in[a0:a0 + na], h_out[b0:b0 + nb], h_sems[c0:c0 + nc]))
            a0, b0, c0 = a0 + na, b0 + nb, c0 + nc

        @pl.when(first)
        def _():
            for cm, ci, co, cs in parts:
                cm.start(ci, co, cs)

        body(*ins, *outs, *scr)

        @pl.when(last)
        def _():
            for cm, ci, co, cs in parts:
                cm.finish(ci, co, cs)

    res = pl.pallas_call(
        wrapped, name=name, grid=grid, in_specs=list(in_specs) + [_ANY] * hi,
        out_specs=list(out_specs) + [_ANY] * ho,
        out_shape=list(out_shape) + [s for cm in comms for s in cm.out_shapes],
        scratch_shapes=list(scratch) + [s for cm in comms for s in cm.scratch],
        compiler_params=_params(*(["arbitrary"] * len(grid))),
    )(*args, *[a for cm in comms for a in cm.args])
    return res[:n_out], res[n_out:]


def _gather_comm(blk):
    r, cdim = blk.shape

    def copies(x_ref, out_ref, send_sems, recv_sems, local_sem):
        x, y, c = lax.axis_index("x"), lax.axis_index("y"), lax.axis_index("c")
        me, sibling = (x, y, c), (x, y, 1 - c)
        chips = [(1 - x, y), (x, 1 - y), (1 - x, 1 - y)]

        def slot(px, py, pc):
            return out_ref.at[4 * px + 2 * py + pc]

        def copy(k, block, to, src=None):
            return pltpu.make_async_remote_copy(
                src_ref=slot(*block) if src is None else src, dst_ref=slot(*block),
                send_sem=send_sems.at[k], recv_sem=recv_sems.at[k],
                device_id=to, device_id_type=MESH)

        mine = pltpu.make_async_copy(x_ref, slot(*me), local_sem)
        first = [copy(0, me, sibling, src=x_ref)]
        first += [copy(1 + j, me, (*chip, c), src=x_ref) for j, chip in enumerate(chips)]
        passed = [copy(4 + j, (*chip, c), sibling) for j, chip in enumerate(chips)]
        landed = [copy(1 + j, (*chip, c), me) for j, chip in enumerate(chips)]
        from_sibling = [copy(0, sibling, me)] + [copy(4 + j, (*chip, 1 - c), me) for j, chip in enumerate(chips)]
        return mine, first, passed, landed, from_sibling

    def start(ins, outs, sems):
        mine, first, _, _, _ = copies(ins[0], outs[0], *sems)
        mine.start()
        for cp in first:
            cp.start()

    def finish(ins, outs, sems):
        mine, first, passed, landed, from_sibling = copies(ins[0], outs[0], *sems)
        for arrived, forward in zip(landed, passed):
            arrived.wait_recv()
            forward.start()
        for cp in from_sibling:
            cp.wait_recv()
        for cp in first + passed:
            cp.wait_send()
        mine.wait()

    return _Comm([blk], [_sds((NDEV, r, cdim), blk.dtype)],
                 [pltpu.SemaphoreType.DMA((7,)), pltpu.SemaphoreType.DMA((7,)), pltpu.SemaphoreType.DMA],
                 start, finish)


def _exchange_comm(src, n, out_rows, make):
    r, cdim = out_rows

    def copies(src_ref, out_ref, send_sems, recv_sems):
        out = []
        for k in range(n):
            s, d, to = make(k, src_ref, out_ref)
            out.append(pltpu.make_async_remote_copy(
                src_ref=s, dst_ref=d, send_sem=send_sems.at[k], recv_sem=recv_sems.at[k],
                device_id=to, device_id_type=MESH))
        return out

    def start(ins, outs, sems):
        for cp in copies(ins[0], outs[0], *sems):
            cp.start()

    def finish(ins, outs, sems):
        cps = copies(ins[0], outs[0], *sems)
        for cp in cps:
            cp.wait_recv()
        for cp in cps:
            cp.wait_send()

    return _Comm([src], [_sds((n, r, cdim), src.dtype)],
                 [pltpu.SemaphoreType.DMA((n,)), pltpu.SemaphoreType.DMA((n,))], start, finish)


def _pair_exchange_comm(g4):
    def make(k, g_ref, out_ref):
        x, y, c = lax.axis_index("x"), lax.axis_index("y"), lax.axis_index("c")
        return g_ref.at[k, 1 - c], out_ref.at[k], (x, y, 1 - c)

    return _exchange_comm(g4, 4, g4.shape[2:], make)


def _chip_exchange_comm(part):
    def make(k, p_ref, out_ref):
        x, y, c = lax.axis_index("x"), lax.axis_index("y"), lax.axis_index("c")
        px = x if ((k + 1) >> 1) == 0 else 1 - x
        py = y if ((k + 1) & 1) == 0 else 1 - y
        return p_ref.at[2 * px + py], out_ref.at[k], (px, py, c)

    return _exchange_comm(part, 3, part.shape[1:], make)


def _all_to_all_comm(g8):
    _, r, cdim = g8.shape

    def copies(g_ref, out_ref, send_sems, recv_sems, local_sem):
        x, y, c = lax.axis_index("x"), lax.axis_index("y"), lax.axis_index("c")
        me = 4 * x + 2 * y + c
        remote = []
        for k in range(1, NDEV):
            px = 1 - x if k & 4 else x
            py = 1 - y if k & 2 else y
            pc = 1 - c if k & 1 else c
            remote.append(pltpu.make_async_remote_copy(
                src_ref=g_ref.at[4 * px + 2 * py + pc], dst_ref=out_ref.at[me],
                send_sem=send_sems.at[k - 1], recv_sem=recv_sems.at[k - 1],
                device_id=(px, py, pc), device_id_type=MESH))
        return pltpu.make_async_copy(g_ref.at[me], out_ref.at[me], local_sem), remote

    def start(ins, outs, sems):
        mine, remote = copies(ins[0], outs[0], *sems)
        mine.start()
        for cp in remote:
            cp.start()

    def finish(ins, outs, sems):
        mine, remote = copies(ins[0], outs[0], *sems)
        for cp in remote:
            cp.wait_recv()
        for cp in remote:
            cp.wait_send()
        mine.wait()

    return _Comm([g8], [_sds((NDEV, r, cdim), g8.dtype)],
                 [pltpu.SemaphoreType.DMA((NDEV - 1,)), pltpu.SemaphoreType.DMA((NDEV - 1,)),
                  pltpu.SemaphoreType.DMA], start, finish)


def _slot_sum(name, slots, tr):
    _, r, cdim = slots.shape

    def body(s_ref, o_ref):
        s = s_ref[0].astype(F32)
        for k in range(1, NDEV):
            s = s + s_ref[k].astype(F32)
        o_ref[...] = s

    return pl.pallas_call(
        body, name=name, grid=(r // tr,),
        in_specs=[pl.BlockSpec((NDEV, tr, cdim), lambda i: (0, i, 0))],
        out_specs=pl.BlockSpec((tr, cdim), lambda i: (i, 0)),
        out_shape=_sds((r, cdim), F32), compiler_params=_params("parallel"),
    )(slots)


def _pair_sum(name, g4, recv, core, tr):
    _, _, r, cdim = g4.shape

    def body(core_ref, a_ref, b_ref, o_ref):
        o_ref[...] = (a_ref[...].astype(F32) + b_ref[...].astype(F32)).astype(o_ref.dtype)

    return pl.pallas_call(
        body, name=name,
        grid_spec=pltpu.PrefetchScalarGridSpec(
            num_scalar_prefetch=1, grid=(4, r // tr),
            in_specs=[pl.BlockSpec((None, None, tr, cdim), lambda k, i, cr: (k, cr[0], i, 0)),
                      pl.BlockSpec((None, tr, cdim), lambda k, i, cr: (k, i, 0))],
            out_specs=pl.BlockSpec((None, tr, cdim), lambda k, i, cr: (k, i, 0))),
        out_shape=_sds((4, r, cdim), BF16),
        compiler_params=_params("parallel", "parallel"),
    )(core, g4, recv)


def _chip_sum(name, part, recv, chip, tr):
    _, r, cdim = part.shape

    def body(chip_ref, a_ref, b_ref, o_ref):
        s = a_ref[...].astype(F32)
        for k in range(3):
            s = s + b_ref[k].astype(F32)
        o_ref[...] = s

    return pl.pallas_call(
        body, name=name,
        grid_spec=pltpu.PrefetchScalarGridSpec(
            num_scalar_prefetch=1, grid=(r // tr,),
            in_specs=[pl.BlockSpec((None, tr, cdim), lambda i, cr: (cr[0], i, 0)),
                      pl.BlockSpec((3, tr, cdim), lambda i, cr: (0, i, 0))],
            out_specs=pl.BlockSpec((tr, cdim), lambda i, cr: (i, 0))),
        out_shape=_sds((r, cdim), F32),
        compiler_params=_params("parallel"),
    )(chip, part, recv)


def _rms_cast(name, h, gain, tm, hosted=None):
    t = h.shape[0]

    def body(h_ref, g_ref, o_ref):
        _, xh = _rms_stats(h_ref[...])
        o_ref[...] = (xh * g_ref[...]).astype(BF16)

    return _call(name, body, (t // tm,), [_row(tm, D), _const((1, D))], [_row(tm, D)],
                 [_sds((t, D), BF16)], (h, gain), sem=("parallel",), hosted=hosted)


def _ffn_up(name, xn, wgu_t, tm, tn, hosted=None):
    t = xn.shape[0]
    nh = DFF // tn

    def body(x_ref, wg_ref, wu_ref, g_ref, u_ref, a_ref):
        x = x_ref[...]
        g = _dot_nt(x, wg_ref[...])
        u = _dot_nt(x, wu_ref[...])
        g_ref[...] = g.astype(BF16)
        u_ref[...] = u.astype(BF16)
        a_ref[...] = (g * _sigmoid(g) * u).astype(BF16)

    o = pl.BlockSpec((tm, tn), lambda j, i: (i, j))
    return _call(name, body, (nh, t // tm),
                 [pl.BlockSpec((tm, D), lambda j, i: (i, 0)),
                  pl.BlockSpec((tn, D), lambda j, i: (j, 0)),
                  pl.BlockSpec((tn, D), lambda j, i: (j + nh, 0))],
                 [o, o, o], [_sds((t, DFF), BF16)] * 3, (xn, wgu_t, wgu_t),
                 sem=("parallel", "parallel"), hosted=hosted)


def _ffn_down(name, a, wd, h, gain, tm):
    t = a.shape[0]

    def body(a_ref, w_ref, h_ref, g_ref, o_ref, n_ref):
        hn = h_ref[...] + 0.5 * _dot(a_ref[...], w_ref[...])
        o_ref[...] = hn
        _, xh = _rms_stats(hn)
        n_ref[...] = (xh * g_ref[...]).astype(BF16)

    return pl.pallas_call(
        body, name=name, grid=(t // tm,),
        in_specs=[_row(tm, DFF), _const((DFF, D)), _row(tm, D), _const((1, D))],
        out_specs=[_row(tm, D), _row(tm, D)],
        out_shape=[_sds((t, D), F32), _sds((t, D), BF16)],
        compiler_params=_params("parallel"),
    )(a, wd, h, gain)


def _mix_in(name, n, win_t, b_in, tm, tn, hosted=None):
    t = n.shape[0]

    def body(n_ref, w_ref, b_ref, p_ref):
        p_ref[...] = (_dot_nt(n_ref[...], w_ref[...]) + b_ref[...]).astype(BF16)

    return _call(name, body, (6 * D // tn, t // tm),
                 [pl.BlockSpec((tm, D), lambda j, i: (i, 0)),
                  pl.BlockSpec((tn, D), lambda j, i: (j, 0)),
                  pl.BlockSpec((1, tn), lambda j, i: (0, j))],
                 [pl.BlockSpec((tm, tn), lambda j, i: (i, j))], [_sds((t, 6 * D), BF16)],
                 (n, win_t, b_in), sem=("parallel", "parallel"), hosted=hosted)


RC = 64
LANES = 128


def _shift_copies(ext, shifted, tm):
    n = tm + HALO - 8
    for m in range(1, 8):
        shifted[m - 1] = ext[pl.ds(m, n), :]


def _by_residue(offs):
    groups = {}
    for k, off in enumerate(offs):
        q, m = divmod(off, 8)
        groups.setdefault(m, []).append((k, q))
    return groups


def _residue_window(ext, shifted, m, taps, base, cs):
    src = ext if m == 0 else shifted.at[m - 1]
    return src[pl.ds(base, RC + 8 * max(q for _, q in taps)), cs]


def _tap_sum(out_ref, bias_ref, w_ref, ext, shifted, offs, tm):
    groups = _by_residue(offs)

    def chunk(j, carry):
        base = pl.multiple_of(j * RC, RC)
        for c in range(D // LANES):
            cs = pl.ds(c * LANES, LANES)
            acc = jnp.zeros((RC, LANES), F32)
            if bias_ref is not None:
                acc = acc + bias_ref[:, cs]
            for m, taps in groups.items():
                big = _residue_window(ext, shifted, m, taps, base, cs)
                for k, q in taps:
                    acc = acc + w_ref[pl.ds(k, 1), cs] * big[8 * q:8 * q + RC]
            out_ref[pl.ds(base, RC), cs] = acc
        return carry

    lax.fori_loop(0, tm // RC, chunk, 0)


def _tap_corr(dw_ref, dc_ext, ext, shifted, offs, tm):
    groups = _by_residue(offs)
    for c in range(D // LANES):
        cs = pl.ds(c * LANES, LANES)

        def chunk(j, accs, cs=cs):
            base = pl.multiple_of(j * RC, RC)
            dcv = dc_ext[pl.ds(base, RC), cs]
            out = list(accs)
            for m, taps in groups.items():
                big = _residue_window(ext, shifted, m, taps, base, cs)
                for k, q in taps:
                    prod = dcv * big[8 * q:8 * q + RC]
                    part = prod[0:8]
                    for s in range(1, RC // 8):
                        part = part + prod[8 * s:8 * s + 8]
                    out[k] = accs[k] + part
            return tuple(out)

        accs = lax.fori_loop(0, tm // RC, chunk, tuple(jnp.zeros((8, LANES), F32) for _ in offs))
        for k in range(len(offs)):
            dw_ref[pl.ds(k, 1), cs] += jnp.sum(accs[k], axis=0, keepdims=True)


def _conv_fwd(name, p, conv_w, conv_b, ln_g, ln_b, tm):
    t = p.shape[0]

    def body(av_ref, ag_ref, w_ref, cb_ref, lg_ref, lb_ref, c_ref, a_ref, ext, shifted):
        i = pl.program_id(0)

        @pl.when(i == 0)
        def _():
            ext[pl.ds(0, HALO), :] = jnp.zeros((HALO, D), F32)

        ext[pl.ds(HALO, tm), :] = av_ref[...].astype(F32) * _sigmoid(ag_ref[...].astype(F32))
        _shift_copies(ext, shifted, tm)
        _tap_sum(c_ref, cb_ref, w_ref, ext, shifted, [HALO - (CW - 1) + k for k in range(CW)], tm)
        rstd, chat = _ln_stats(c_ref[...])
        ca = chat * lg_ref[...] + lb_ref[...]
        a_ref[...] = (ca * _sigmoid(ca)).astype(BF16)
        ext[pl.ds(0, HALO), :] = ext[pl.ds(tm, HALO), :]

    return pl.pallas_call(
        body, name=name, grid=(t // tm,),
        in_specs=[_row(tm, D, 0), _row(tm, D, 1), _const((HALO, D)), _const((1, D)),
                  _const((1, D)), _const((1, D))],
        out_specs=[_row(tm, D), _row(tm, D)],
        out_shape=[_sds((t, D), F32), _sds((t, D), BF16)],
        scratch_shapes=[pltpu.VMEM((tm + HALO, D), F32), pltpu.VMEM((7, tm + HALO - 8, D), F32)],
        compiler_params=_params("arbitrary"),
    )(p, p, conv_w, conv_b, ln_g, ln_b)


def _sgu_mask():
    rows = lax.broadcasted_iota(jnp.int32, (CHUNK, CHUNK), 0)
    cols = lax.broadcasted_iota(jnp.int32, (CHUNK, CHUNK), 1)
    return cols <= rows


def _sgu_fwd(name, p, ln_g, ln_b, sgu_w, bias_full, tm):
    t = p.shape[0]

    def body(bu_ref, bv_ref, lg_ref, lb_ref, ws_ref, bias_ref, o_ref):
        mask = _sgu_mask()
        _, vhat = _ln_stats(_gelu(bv_ref[...].astype(F32)))
        vn = (vhat * lg_ref[...] + lb_ref[...]).astype(BF16)
        ub = _gelu(bu_ref[...].astype(F32))
        for g in range(GROUPS):
            wm = jnp.where(mask, ws_ref[g], 0.0).astype(BF16)
            cs = slice(g * GD, (g + 1) * GD)
            for cc in range(tm // CHUNK):
                rs = slice(cc * CHUNK, (cc + 1) * CHUNK)
                mixed = _dot(wm, vn[rs, cs]) + bias_ref[:, cs]
                o_ref[rs, cs] = (ub[rs, cs] * mixed).astype(BF16)

    return pl.pallas_call(
        body, name=name, grid=(t // tm,),
        in_specs=[_row(tm, D, 2), _row(tm, D, 3), _const((1, D)), _const((1, D)),
                  _const((GROUPS, CHUNK, CHUNK)), _const((CHUNK, D))],
        out_specs=_row(tm, D), out_shape=_sds((t, D), BF16),
        compiler_params=_params("parallel"),
    )(p, p, ln_g, ln_b, sgu_w, bias_full)


def _merge_fwd(name, act_a, act_b, p, w_a, w_b, w_out, h, gain, tm):
    t = h.shape[0]

    def body(a_ref, b_ref, ga_ref, gb_ref, wa_ref, wb_ref, wo_ref, h_ref, g_ref,
             ya_ref, yb_ref, mg_ref, ho_ref, xn_ref):
        ya = _dot(a_ref[...], wa_ref[...])
        yb = _dot(b_ref[...], wb_ref[...])
        ya_ref[...] = ya.astype(BF16)
        yb_ref[...] = yb.astype(BF16)
        merged = (_sigmoid(ga_ref[...].astype(F32)) * ya
                  + _sigmoid(gb_ref[...].astype(F32)) * yb).astype(BF16)
        mg_ref[...] = merged
        hn = h_ref[...] + _dot(merged, wo_ref[...])
        ho_ref[...] = hn
        _, xh = _rms_stats(hn)
        xn_ref[...] = (xh * g_ref[...]).astype(BF16)

    rb = _row(tm, D)
    return pl.pallas_call(
        body, name=name, grid=(t // tm,),
        in_specs=[rb, rb, _row(tm, D, 4), _row(tm, D, 5), _const((D, D)), _const((D, D)),
                  _const((D, D)), rb, _const((1, D))],
        out_specs=[rb] * 5,
        out_shape=[_sds((t, D), BF16)] * 3 + [_sds((t, D), F32), _sds((t, D), BF16)],
        compiler_params=_params("parallel"),
    )(act_a, act_b, p, p, w_a, w_b, w_out, h, gain)


def _kv_fwd(name, mem, gain, wkv_t):
    def body(m_ref, g_ref, w_ref, mn_ref, k_ref, v_ref):
        _, xh = _rms_stats(m_ref[...])
        mn = (xh * g_ref[...]).astype(BF16)
        mn_ref[...] = mn
        kv = _dot_nt(mn, w_ref[...])
        k_ref[...] = kv[:, :D].astype(BF16)
        v_ref[...] = kv[:, D:].astype(BF16)

    return pl.pallas_call(
        body, name=name,
        out_shape=[_sds((NMEM, D), BF16)] * 3,
        compiler_params=pltpu.CompilerParams(vmem_limit_bytes=VMEM_LIMIT),
    )(mem, gain, wkv_t)


def _softmax_rows(s):
    e = jnp.exp(s - jnp.max(s, axis=-1, keepdims=True))
    return e / jnp.sum(e, axis=-1, keepdims=True)


def _attn_fwd(name, xq, w_q, kb, vb, w_o, h, gain, tm):
    t = h.shape[0]
    scale = 1.0 / math.sqrt(HD)

    def body(x_ref, wq_ref, k_ref, v_ref, wo_ref, h_ref, g_ref, q_ref, o_ref, ho_ref, xn_ref):
        q_ref[...] = _dot(x_ref[...], wq_ref[...]).astype(BF16)
        for hd in range(HEADS):
            cs = slice(hd * HD, (hd + 1) * HD)
            p = _softmax_rows(_dot_nt(q_ref[:, cs], k_ref[:, cs]) * scale)
            o_ref[:, cs] = _dot(p.astype(BF16), v_ref[:, cs]).astype(BF16)
        hn = h_ref[...] + _dot(o_ref[...], wo_ref[...])
        ho_ref[...] = hn
        _, xh = _rms_stats(hn)
        xn_ref[...] = (xh * g_ref[...]).astype(BF16)

    rb = _row(tm, D)
    return pl.pallas_call(
        body, name=name, grid=(t // tm,),
        in_specs=[rb, _const((D, D)), _const((NMEM, D)), _const((NMEM, D)), _const((D, D)), rb,
                  _const((1, D))],
        out_specs=[rb] * 4,
        out_shape=[_sds((t, D), BF16), _sds((t, D), BF16), _sds((t, D), F32), _sds((t, D), BF16)],
        compiler_params=_params("parallel"),
    )(xq, w_q, kb, vb, w_o, h, gain)


def _ffn_down_loss(name, a, wd, h, gain, target, tm):
    t = h.shape[0]
    steps = t // tm

    def body(a_ref, w_ref, h_ref, g_ref, t_ref, dh_ref, dhb_ref, loss_ref, dg_ref, lacc):
        i = pl.program_id(0)
        hv = h_ref[...] + 0.5 * _dot(a_ref[...], w_ref[...])
        r, xh = _rms_stats(hv)
        err = xh * g_ref[...] - t_ref[...]
        _acc(lacc, i == 0, jnp.sum(err * err, axis=0, keepdims=True))
        dy = err * (1.0 / D)
        _acc(dg_ref, i == 0, jnp.sum(dy * xh, axis=0, keepdims=True))
        dxh = dy * g_ref[...]
        dh = r * (dxh - xh * jnp.mean(dxh * xh, axis=-1, keepdims=True))
        dh_ref[...] = dh
        dhb_ref[...] = dh.astype(BF16)

        @pl.when(i == steps - 1)
        def _():
            loss_ref[...] = jnp.zeros((8, 128), F32) + (0.5 / D) * jnp.sum(lacc[...])

    rb = _row(tm, D)
    return pl.pallas_call(
        body, name=name, grid=(steps,),
        in_specs=[_row(tm, DFF), _const((DFF, D)), rb, _const((1, D)), rb],
        out_specs=[rb, rb, _const((8, 128)), _const((1, D))],
        out_shape=[_sds((t, D), F32), _sds((t, D), BF16), _sds((8, 128), F32), _sds((1, D), F32)],
        scratch_shapes=[pltpu.VMEM((1, D), F32)],
        compiler_params=_params("arbitrary"),
    )(a, wd, h, gain, target)


def _ffn_bwd_act(name, dhb, wd, g, u, tm, tn, hosted=None):
    t = dhb.shape[0]

    def body(d_ref, w_ref, g_ref, u_ref, dg_ref, du_ref):
        da = 0.5 * _dot_nt(d_ref[...], w_ref[...])
        gv = g_ref[...].astype(F32)
        sg = _sigmoid(gv)
        dg_ref[...] = (da * u_ref[...].astype(F32) * (sg * (1.0 + gv * (1.0 - sg)))).astype(BF16)
        du_ref[...] = (da * (gv * sg)).astype(BF16)

    o = pl.BlockSpec((tm, tn), lambda j, i: (i, j))
    return _call(name, body, (DFF // tn, t // tm),
                 [pl.BlockSpec((tm, D), lambda j, i: (i, 0)),
                  pl.BlockSpec((tn, D), lambda j, i: (j, 0)), o, o],
                 [o, o], [_sds((t, DFF), BF16)] * 2, (dhb, wd, g, u),
                 sem=("parallel", "parallel"), hosted=hosted)


def _dx_rms_bwd(name, pairs, h, gain, dh_in, tm, hosted=None):
    t = h.shape[0]
    np_ = len(pairs)

    def body(*refs):
        a_refs = refs[:np_]
        b_refs = refs[np_:2 * np_]
        h_ref, g_ref, d_ref, o_ref, ob_ref, dg_ref = refs[2 * np_:]
        dxn = None
        for (a_ref, b_ref, pr) in zip(a_refs, b_refs, pairs):
            y = _dot_nt(a_ref[...], b_ref[...]) if pr[4] else _dot(a_ref[...], b_ref[...])
            dxn = y if dxn is None else dxn + y
        dh, dgain = _rms_bwd(dxn, h_ref[...], g_ref[...])
        _acc(dg_ref, pl.program_id(0) == 0, dgain)
        out = d_ref[...] + dh
        o_ref[...] = out
        ob_ref[...] = out.astype(BF16)

    ins, args = [], []
    for (a, b, blk, rows, tr) in pairs:
        ins.append(_row(tm, a.shape[1]))
        args.append(a)
    for (a, b, blk, rows, tr) in pairs:
        ins.append(pl.BlockSpec((rows, b.shape[1]), lambda i, _b=blk: (_b, 0)))
        args.append(b)
    rb = _row(tm, D)
    ins += [rb, _const((1, D)), rb]
    args += [h, gain, dh_in]
    return _call(name, body, (t // tm,), ins, [rb, rb, _const((1, D))],
                 [_sds((t, D), F32), _sds((t, D), BF16), _sds((1, D), F32)], args,
                 sem=("arbitrary",), hosted=hosted)


def _mm_tn(name, a, b, scale, tmo, tk):
    t, m = a.shape
    n = b.shape[1]
    tk = min(tk, t)
    steps = t // tk

    def body(a_ref, b_ref, o_ref, acc):
        k = pl.program_id(1)
        _acc(acc, k == 0, _dot_tn(a_ref[...], b_ref[...]))

        @pl.when(k == steps - 1)
        def _():
            o_ref[...] = (acc[...] * scale).astype(o_ref.dtype)

    return pl.pallas_call(
        body, name=name, grid=(m // tmo, steps),
        in_specs=[pl.BlockSpec((tk, tmo), lambda i, k: (k, i)),
                  pl.BlockSpec((tk, n), lambda i, k: (k, 0))],
        out_specs=pl.BlockSpec((tmo, n), lambda i, k: (i, 0)),
        out_shape=_sds((m, n), BF16),
        scratch_shapes=[pltpu.VMEM((tmo, n), F32)],
        compiler_params=_params("parallel", "arbitrary"),
    )(a, b)


def _attn_bwd(name, dhb, w_o, qb, kb, vb, tm):
    t = dhb.shape[0]
    scale = 1.0 / math.sqrt(HD)

    def body(d_ref, wo_ref, q_ref, k_ref, v_ref, dq_ref, dk_ref, dv_ref, do_s):
        i = pl.program_id(0)

        @pl.when(i == 0)
        def _():
            dk_ref[...] = jnp.zeros_like(dk_ref)
            dv_ref[...] = jnp.zeros_like(dv_ref)

        do_s[...] = _dot_nt(d_ref[...], wo_ref[...]).astype(BF16)
        for hd in range(HEADS):
            cs = slice(hd * HD, (hd + 1) * HD)
            q = q_ref[:, cs]
            p = _softmax_rows(_dot_nt(q, k_ref[:, cs]) * scale)
            do = do_s[:, cs]
            dp = _dot_nt(do, v_ref[:, cs])
            ds = (p * (dp - jnp.sum(dp * p, axis=-1, keepdims=True)) * scale).astype(BF16)
            dq_ref[:, cs] = _dot(ds, k_ref[:, cs]).astype(BF16)
            dk_ref[:, cs] += _dot_tn(ds, q)
            dv_ref[:, cs] += _dot_tn(p.astype(BF16), do)

    rb = _row(tm, D)
    return pl.pallas_call(
        body, name=name, grid=(t // tm,),
        in_specs=[rb, _const((D, D)), rb, _const((NMEM, D)), _const((NMEM, D))],
        out_specs=[rb, _const((NMEM, D)), _const((NMEM, D))],
        out_shape=[_sds((t, D), BF16), _sds((NMEM, D), F32), _sds((NMEM, D), F32)],
        scratch_shapes=[pltpu.VMEM((tm, D), BF16)],
        compiler_params=_params("arbitrary"),
    )(dhb, w_o, qb, kb, vb)


def _kv_bwd(name, dk, dv, memn, wkv_t, mem, gain):
    def body(dk_ref, dv_ref, mn_ref, w_ref, m_ref, g_ref, dw_ref, dg_ref):
        dkb = dk_ref[...].astype(BF16)
        dvb = dv_ref[...].astype(BF16)
        mn = mn_ref[...]
        dw_ref[pl.ds(0, D), :] = _dot_tn(dkb, mn).astype(BF16)
        dw_ref[pl.ds(D, D), :] = _dot_tn(dvb, mn).astype(BF16)
        dmn = _dot(dkb, w_ref[pl.ds(0, D), :]) + _dot(dvb, w_ref[pl.ds(D, D), :])
        _, xh = _rms_stats(m_ref[...])
        dg_ref[...] = jnp.sum(dmn * xh, axis=0, keepdims=True)

    return pl.pallas_call(
        body, name=name,
        out_shape=[_sds((2 * D, D), BF16), _sds((1, D), F32)],
        compiler_params=pltpu.CompilerParams(vmem_limit_bytes=VMEM_LIMIT),
    )(dk, dv, memn, wkv_t, mem, gain)


def _merge_bwd(name, dhb, w_out, ya, yb, p, tm):
    t = dhb.shape[0]

    def body(d_ref, w_ref, ya_ref, yb_ref, ga_ref, gb_ref, dya_ref, dyb_ref, dp_ref, cs_ref):
        dm = _dot_nt(d_ref[...], w_ref[...])
        sa = _sigmoid(ga_ref[...].astype(F32))
        sb = _sigmoid(gb_ref[...].astype(F32))
        dya_ref[...] = (dm * sa).astype(BF16)
        dyb_ref[...] = (dm * sb).astype(BF16)
        dga = dm * ya_ref[...].astype(F32) * (sa * (1.0 - sa))
        dgb = dm * yb_ref[...].astype(F32) * (sb * (1.0 - sb))
        dp_ref[:, pl.ds(0, D)] = dga.astype(BF16)
        dp_ref[:, pl.ds(D, D)] = dgb.astype(BF16)
        first = pl.program_id(0) == 0

        @pl.when(first)
        def _():
            cs_ref[...] = jnp.zeros_like(cs_ref)
        cs_ref[:, pl.ds(0, D)] += jnp.sum(dga, axis=0, keepdims=True)
        cs_ref[:, pl.ds(D, D)] += jnp.sum(dgb, axis=0, keepdims=True)

    rb = _row(tm, D)
    return pl.pallas_call(
        body, name=name, grid=(t // tm,),
        in_specs=[rb, _const((D, D)), rb, rb, _row(tm, D, 4), _row(tm, D, 5)],
        out_specs=[rb, rb, _row(tm, 2 * D), _const((1, 2 * D))],
        out_shape=[_sds((t, D), BF16), _sds((t, D), BF16), _sds((t, 2 * D), BF16),
                   _sds((1, 2 * D), F32)],
        compiler_params=_params("arbitrary"),
    )(dhb, w_out, ya, yb, p, p)


def _conv_bwd(name, dya, w_a, c, p, conv_w, ln_g, ln_b, tm):
    t = dya.shape[0]
    steps = t // tm
    hb = tm // HALO

    def rev(i):
        return steps - 1 - i

    def body(dy_ref, wa_ref, c_ref, av_ref, ag_ref, avh_ref, agh_ref, w_ref, lg_ref, lb_ref,
             dp_ref, cs_ref, dw_ref, dcb_ref, dlg_ref, dlb_ref, dc_ext, a_ext, dc_sh, a_sh, da0_s):
        i = pl.program_id(0)
        first = i == 0

        @pl.when(first)
        def _():
            dc_ext[pl.ds(tm, HALO), :] = jnp.zeros((HALO, D), F32)
            dw_ref[...] = jnp.zeros_like(dw_ref)
            cs_ref[...] = jnp.zeros_like(cs_ref)

        d_act = _dot_nt(dy_ref[...], wa_ref[...])
        rstd, chat = _ln_stats(c_ref[...])
        ca = chat * lg_ref[...] + lb_ref[...]
        sc = _sigmoid(ca)
        dca = d_act * (sc * (1.0 + ca * (1.0 - sc)))
        _acc(dlg_ref, first, jnp.sum(dca * chat, axis=0, keepdims=True))
        _acc(dlb_ref, first, jnp.sum(dca, axis=0, keepdims=True))
        dc = _ln_bwd(dca, chat, rstd, lg_ref[...])
        _acc(dcb_ref, first, jnp.sum(dc, axis=0, keepdims=True))
        dc_ext[pl.ds(0, tm), :] = dc

        av = av_ref[...].astype(F32)
        sg = _sigmoid(ag_ref[...].astype(F32))
        a_ext[pl.ds(HALO, tm), :] = av * sg
        halo = avh_ref[...].astype(F32) * _sigmoid(agh_ref[...].astype(F32))
        a_ext[pl.ds(0, HALO), :] = jnp.where(i == steps - 1, 0.0, halo)

        _shift_copies(dc_ext, dc_sh, tm)
        _shift_copies(a_ext, a_sh, tm)
        _tap_sum(da0_s, None, w_ref, dc_ext, dc_sh, [CW - 1 - k for k in range(CW)], tm)
        _tap_corr(dw_ref, dc_ext, a_ext, a_sh, [HALO - (CW - 1) + k for k in range(CW)], tm)
        da0 = da0_s[...]
        dav = da0 * sg
        dag = da0 * av * (sg * (1.0 - sg))
        dp_ref[:, pl.ds(0, D)] = dav.astype(BF16)
        dp_ref[:, pl.ds(D, D)] = dag.astype(BF16)
        cs_ref[:, pl.ds(0, D)] += jnp.sum(dav, axis=0, keepdims=True)
        cs_ref[:, pl.ds(D, D)] += jnp.sum(dag, axis=0, keepdims=True)
        dc_ext[pl.ds(tm, HALO), :] = dc_ext[pl.ds(0, HALO), :]

    def rrow(cols, cb=0):
        return pl.BlockSpec((tm, cols), lambda i, _cb=cb: (rev(i), _cb))

    def halo_spec(cb):
        return pl.BlockSpec((HALO, D), lambda i, _cb=cb: (jnp.maximum(rev(i) * hb - 1, 0), _cb))

    return pl.pallas_call(
        body, name=name, grid=(steps,),
        in_specs=[rrow(D), _const((D, D)), rrow(D), rrow(D, 0), rrow(D, 1), halo_spec(0),
                  halo_spec(1), _const((HALO, D)), _const((1, D)), _const((1, D))],
        out_specs=[rrow(2 * D), _const((1, 2 * D)), _const((HALO, D)), _const((1, D)),
                   _const((1, D)), _const((1, D))],
        out_shape=[_sds((t, 2 * D), BF16), _sds((1, 2 * D), F32), _sds((HALO, D), F32),
                   _sds((1, D), F32), _sds((1, D), F32), _sds((1, D), F32)],
        scratch_shapes=[pltpu.VMEM((tm + HALO, D), F32), pltpu.VMEM((tm + HALO, D), F32),
                        pltpu.VMEM((7, tm + HALO - 8, D), F32), pltpu.VMEM((7, tm + HALO - 8, D), F32),
                        pltpu.VMEM((tm, D), F32)],
        compiler_params=_params("arbitrary"),
    )(dya, w_a, c, p, p, p, p, conv_w, ln_g, ln_b)


def _sgu_bwd(name, dyb, w_b, p, ln_g, ln_b, sgu_w, bias_full, tm):
    t = dyb.shape[0]
    steps = t // tm

    def body(dy_ref, wb_ref, bu_ref, bv_ref, lg_ref, lb_ref, ws_ref, bias_ref,
             dp_ref, cs_ref, dws_ref, dsb_ref, dlg_ref, dlb_ref, dub_s, dvn_s, dbias_s):
        i = pl.program_id(0)
        first = i == 0
        mask = _sgu_mask()

        @pl.when(first)
        def _():
            dws_ref[...] = jnp.zeros_like(dws_ref)
            dbias_s[...] = jnp.zeros_like(dbias_s)
            cs_ref[...] = jnp.zeros_like(cs_ref)

        dob = _dot_nt(dy_ref[...], wb_ref[...])
        bu = bu_ref[...].astype(F32)
        bv = bv_ref[...].astype(F32)
        ub, ub_grad = _gelu_with_grad(bu)
        vb, vb_grad = _gelu_with_grad(bv)
        rstd, vhat = _ln_stats(vb)
        vn = (vhat * lg_ref[...] + lb_ref[...]).astype(BF16)
        for g in range(GROUPS):
            wm = jnp.where(mask, ws_ref[g], 0.0).astype(BF16)
            cs = slice(g * GD, (g + 1) * GD)
            for cc in range(tm // CHUNK):
                rs = slice(cc * CHUNK, (cc + 1) * CHUNK)
                vblk = vn[rs, cs]
                mixed = _dot(wm, vblk) + bias_ref[:, cs]
                dob_blk = dob[rs, cs]
                dub_s[rs, cs] = dob_blk * mixed
                dmixed = dob_blk * ub[rs, cs]
                dbias_s[:, cs] += dmixed
                dmb = dmixed.astype(BF16)
                dws_ref[g] += _dot_nt(dmb, vblk)
                dvn_s[rs, cs] = _dot_tn(wm, dmb)
        dbu = dub_s[...] * ub_grad
        dvn = dvn_s[...]
        _acc(dlg_ref, first, jnp.sum(dvn * vhat, axis=0, keepdims=True))
        _acc(dlb_ref, first, jnp.sum(dvn, axis=0, keepdims=True))
        dbv = _ln_bwd(dvn, vhat, rstd, lg_ref[...]) * vb_grad
        dp_ref[:, pl.ds(0, D)] = dbu.astype(BF16)
        dp_ref[:, pl.ds(D, D)] = dbv.astype(BF16)
        cs_ref[:, pl.ds(0, D)] += jnp.sum(dbu, axis=0, keepdims=True)
        cs_ref[:, pl.ds(D, D)] += jnp.sum(dbv, axis=0, keepdims=True)

        @pl.when(i == steps - 1)
        def _():
            lane = lax.broadcasted_iota(jnp.int32, (CHUNK, CHUNK), 1)
            dsb = jnp.zeros((CHUNK, CHUNK), F32)
            for g in range(GROUPS):
                dws_ref[g] = jnp.where(mask, dws_ref[g], 0.0)
                dsb = jnp.where(lane == g, jnp.sum(dbias_s[:, g * GD:(g + 1) * GD], axis=1, keepdims=True), dsb)
            dsb_ref[...] = dsb

    rb = _row(tm, D)
    return pl.pallas_call(
        body, name=name, grid=(steps,),
        in_specs=[rb, _const((D, D)), _row(tm, D, 2), _row(tm, D, 3), _const((1, D)), _const((1, D)),
                  _const((GROUPS, CHUNK, CHUNK)), _const((CHUNK, D))],
        out_specs=[_row(tm, 2 * D), _const((1, 2 * D)), _const((GROUPS, CHUNK, CHUNK)),
                   _const((CHUNK, CHUNK)), _const((1, D)), _const((1, D))],
        out_shape=[_sds((t, 2 * D), BF16), _sds((1, 2 * D), F32), _sds((GROUPS, CHUNK, CHUNK), F32),
                   _sds((CHUNK, CHUNK), F32), _sds((1, D), F32), _sds((1, D), F32)],
        scratch_shapes=[pltpu.VMEM((tm, D), F32), pltpu.VMEM((tm, D), F32), pltpu.VMEM((CHUNK, D), F32)],
        compiler_params=_params("arbitrary"),
    )(dyb, w_b, p, p, ln_g, ln_b, sgu_w, bias_full)


def _adam_math(w, g, m, v):
    m = B1 * m + (1.0 - B1) * g
    v = B2 * v + (1.0 - B2) * (g * g)
    m_hat = m / (1.0 - B1 ** STEP)
    v_hat = v / (1.0 - B2 ** STEP)
    delta = -LR * (m_hat / (jnp.sqrt(v_hat) + EPS_ADAM) + WD * w)
    return delta, m, v


def _adamw(name, w, g, m, v, tr):
    r, cdim = w.shape

    def body(w_ref, g_ref, m_ref, v_ref, d_ref, mo_ref, vo_ref):
        d, mn, vn = _adam_math(w_ref[...], g_ref[...], m_ref[...], v_ref[...])
        d_ref[...] = d
        mo_ref[...] = mn
        vo_ref[...] = vn

    blk = pl.BlockSpec((tr, cdim), lambda i: (i, 0))
    return pl.pallas_call(
        body, name=name, grid=(r // tr,), in_specs=[blk] * 4, out_specs=[blk] * 3,
        out_shape=[_sds((r, cdim), F32)] * 3, compiler_params=_params("parallel"),
    )(w, g, m, v)


def _adamw_small(name, w, g8, m, v):
    r, cdim = w.shape

    def body(w_ref, g_ref, m_ref, v_ref, go_ref, d_ref, mo_ref, vo_ref):
        g = g_ref[0]
        for k in range(1, NDEV):
            g = g + g_ref[k]
        go_ref[...] = g
        d, mn, vn = _adam_math(w_ref[...], g, m_ref[...], v_ref[...])
        d_ref[...] = d
        mo_ref[...] = mn
        vo_ref[...] = vn

    return pl.pallas_call(
        body, name=name, out_shape=[_sds((r, cdim), F32)] * 4,
        compiler_params=pltpu.CompilerParams(vmem_limit_bytes=VMEM_LIMIT),
    )(w, g8, m, v)


_BIG = [("ffn1_w_gu", 704, True), ("ffn1_w_down", 352, False), ("w_in", 768, True),
        ("w_a_out", 128, False), ("w_b_out", 128, False), ("w_out", 128, False),
        ("w_q", 128, False), ("w_kv", 256, True), ("w_o", 128, False),
        ("ffn2_w_gu", 704, True), ("ffn2_w_down", 352, False)]
_BIG_ROWS = sum(r for _, r, _ in _BIG)

_SMALL = [("ffn1_norm", 1), ("mix_norm", 1), ("b_in", 6), ("conv_w", HALO), ("conv_b", 1),
          ("conv_ln_g", 1), ("conv_ln_b", 1), ("sgu_ln_g", 1), ("sgu_ln_b", 1), ("sgu_w", 64),
          ("sgu_b", 1), ("xattn_norm", 1), ("mem_norm", 1), ("ffn2_norm", 1), ("final_norm", 1)]
_SMALL_ROWS = 120


def _pack_small(vals, my_dev):
    rows = []
    for name, nrows in _SMALL:
        a = vals[name].astype(F32)
        if name == "conv_w":
            if a.shape[-1] != D:
                slab = jnp.zeros((HALO, D), F32)
                a = lax.dynamic_update_slice(slab, jnp.pad(a.reshape(CW, -1), ((0, HALO - CW), (0, 0))),
                                             (0, my_dev * (D // NDEV)))
            else:
                a = jnp.pad(a.reshape(CW, D), ((0, HALO - CW), (0, 0)))
        elif name == "sgu_b":
            a = jnp.pad(a.reshape(1, -1), ((0, 0), (0, D - GROUPS * CHUNK)))
        else:
            a = a.reshape(nrows, D)
        rows.append(a)
    packed = jnp.concatenate(rows, axis=0)
    return jnp.pad(packed, ((0, _SMALL_ROWS - packed.shape[0]), (0, 0)))


def _unpack_small(packed, shapes, my_dev):
    out, off = {}, 0
    for name, nrows in _SMALL:
        a = packed[off:off + nrows]
        off += nrows
        if name == "conv_w":
            a = lax.dynamic_slice(a, (0, my_dev * (D // NDEV)), (CW, D // NDEV))
        elif name == "sgu_b":
            a = a[:, :GROUPS * CHUNK]
        out[name] = a.reshape(shapes[name])
    return out


def kernel(x, mem, ffn1_norm, ffn1_w_gu, ffn1_w_down, mix_norm, w_in, b_in, conv_w, conv_b, conv_ln_g, conv_ln_b, w_a_out, sgu_ln_g, sgu_ln_b, sgu_w, sgu_b, w_b_out, w_out, xattn_norm, mem_norm, w_q, w_kv, w_o, ffn2_norm, ffn2_w_gu, ffn2_w_down, final_norm, loss_target, m_ffn1_norm, m_ffn1_w_gu, m_ffn1_w_down, m_mix_norm, m_w_in, m_b_in, m_conv_w, m_conv_b, m_conv_ln_g, m_conv_ln_b, m_w_a_out, m_sgu_ln_g, m_sgu_ln_b, m_sgu_w, m_sgu_b, m_w_b_out, m_w_out, m_xattn_norm, m_mem_norm, m_w_q, m_w_kv, m_w_o, m_ffn2_norm, m_ffn2_w_gu, m_ffn2_w_down, m_final_norm, v_ffn1_norm, v_ffn1_w_gu, v_ffn1_w_down, v_mix_norm, v_w_in, v_b_in, v_conv_w, v_conv_b, v_conv_ln_g, v_conv_ln_b, v_w_a_out, v_sgu_ln_g, v_sgu_ln_b, v_sgu_w, v_sgu_b, v_w_b_out, v_w_out, v_xattn_norm, v_mem_norm, v_w_q, v_w_kv, v_w_o, v_ffn2_norm, v_ffn2_w_gu, v_ffn2_w_down, v_final_norm):
    env = dict(locals())
    names = [n for n, _, _ in _BIG] + [n for n, _ in _SMALL]
    w = {n: env[n] for n in names}
    mom = {n: env["m_" + n] for n in names}
    vel = {n: env["v_" + n] for n in names}

    ax, ay, ac = lax.axis_index("x"), lax.axis_index("y"), lax.axis_index("c")
    my_chip = 2 * ax + ay
    my_dev = 2 * my_chip + ac

    t = x.shape[1]
    tm = min(512, t)
    tm_big = min(1024, t)
    tm_s = min(512, t)
    tm_c = min(256, t)
    xs = x.reshape(t, D)
    tgt = loss_target.reshape(t, D)
    mem2 = mem.reshape(NMEM, D)

    first, mid, late = _BIG[:1], _BIG[1:6], _BIG[6:]

    def shard_block(entries):
        return jnp.concatenate(
            [(w[n][0].T if tr else w[n][0]).astype(BF16) for n, _, tr in entries], axis=0)

    def split(full, entries):
        out, off = {}, 0
        for n, rows, _ in entries:
            out[n] = full[:, off:off + rows, :].reshape(NDEV * rows, D)
            off += rows
        return out

    conv_slab = lax.dynamic_update_slice(
        jnp.zeros((HALO, D), F32), jnp.pad(conv_w[0], ((0, HALO - CW), (0, 0))), (0, my_dev * (D // NDEV)))
    bias_full = jnp.repeat(sgu_b[0].T, GD, axis=1)
    b_in2 = b_in.reshape(1, 6 * D)

    (xn1,), (full_first, conv_w8) = _rms_cast(
        "norm_x", xs, ffn1_norm, tm, hosted=[_gather_comm(shard_block(first)), _gather_comm(conv_slab)])
    conv_w_pad = jnp.sum(conv_w8, axis=0)
    wf = split(full_first, first)
    (g1, u1, a1), (full_mid,) = _ffn_up("ffn1_up", xn1, wf["ffn1_w_gu"], tm_big, 1408,
                                        hosted=_gather_comm(shard_block(mid)))
    wf.update(split(full_mid, mid))
    h1, n_mix = _ffn_down("ffn1_down", a1, wf["ffn1_w_down"], xs, mix_norm, tm)
    (p,), (full_late,) = _mix_in("mix_in", n_mix, wf["w_in"], b_in2, tm_big, 1536,
                                 hosted=_gather_comm(shard_block(late)))
    wf.update(split(full_late, late))
    c_conv, act_a = _conv_fwd("conv_fwd", p, conv_w_pad, conv_b, conv_ln_g, conv_ln_b, tm_c)
    act_b = _sgu_fwd("sgu_fwd", p, sgu_ln_g, sgu_ln_b, sgu_w[0], bias_full, tm_s)
    ya, yb, merged, h2, xq = _merge_fwd("merge_fwd", act_a, act_b, p, wf["w_a_out"], wf["w_b_out"],
                                        wf["w_out"], h1, xattn_norm, tm_s)
    memn, kb, vb = _kv_fwd("kv_fwd", mem2, mem_norm, wf["w_kv"])
    qb, ob, h3, xn4 = _attn_fwd("attn_fwd", xq, wf["w_q"], kb, vb, wf["w_o"], h2, ffn2_norm, tm_s)
    (g2, u2, a2), _ = _ffn_up("ffn2_up", xn4, wf["ffn2_w_gu"], tm_big, 1408)
    dh4, dh4b, loss_blk, d_final = _ffn_down_loss("ffn2_down_loss", a2, wf["ffn2_w_down"], h3,
                                                  final_norm.reshape(1, D), tgt, tm)

    gb = {}
    gs = {}
    gs["final_norm"] = d_final

    def ffn_bwd(tag, dhb, dh, g, u, a, xn, h_in, gain, wgu_t, wd, act_hosted=None, dx_hosted=None):
        (dg, du), act_out = _ffn_bwd_act(tag + "_bwd_act", dhb, wd, g, u, tm_big, 1408, hosted=act_hosted)
        gwd = _mm_tn(tag + "_dw_down", a, dhb, 0.5, 1408, TK)
        gwgu = jnp.concatenate([_mm_tn(tag + "_dw_gate", dg, xn, 1.0, 1408, TK),
                                _mm_tn(tag + "_dw_up", du, xn, 1.0, 1408, TK)], axis=0)
        (dh_o, dhb_o, dgain), dx_out = _dx_rms_bwd(
            tag + "_bwd_dx", [(dg, wgu_t, 0, DFF, False), (du, wgu_t, 1, DFF, False)], h_in, gain, dh, tm_s,
            hosted=dx_hosted(gwgu, gwd) if dx_hosted else None)
        return dh_o, dhb_o, dgain, gwgu, gwd, act_out, dx_out

    dh3, dh3b, gs["ffn2_norm"], gb["ffn2_w_gu"], gb["ffn2_w_down"], _, _ = ffn_bwd(
        "ffn2", dh4b, dh4, g2, u2, a2, xn4, h3, ffn2_norm, wf["ffn2_w_gu"], wf["ffn2_w_down"])

    gb["w_o"] = _mm_tn("dw_o", ob, dh3b, 1.0, 1024, TK)
    dq, dk, dv = _attn_bwd("attn_bwd", dh3b, wf["w_o"], qb, kb, vb, tm_s)
    gb["w_kv"], gs["mem_norm"] = _kv_bwd("kv_bwd", dk, dv, memn, wf["w_kv"], mem2, mem_norm)
    gb["w_q"] = _mm_tn("dw_q", xq, dq, 1.0, 1024, TK)
    (dh2, dh2b, gs["xattn_norm"]), _ = _dx_rms_bwd(
        "attn_bwd_dx", [(dq, wf["w_q"], 0, D, True)], h2, xattn_norm, dh3, tm_s)

    gb["w_out"] = _mm_tn("dw_out", merged, dh2b, 1.0, 1024, TK)
    dya, dyb, dp_g, cs_g = _merge_bwd("merge_bwd", dh2b, wf["w_out"], ya, yb, p, tm_s)
    gb["w_a_out"] = _mm_tn("dw_a", act_a, dya, 1.0, 1024, TK)
    gb["w_b_out"] = _mm_tn("dw_b", act_b, dyb, 1.0, 1024, TK)
    dp_a, cs_a, d_convw, gs["conv_b"], gs["conv_ln_g"], gs["conv_ln_b"] = _conv_bwd(
        "conv_bwd", dya, wf["w_a_out"], c_conv, p, conv_w_pad, conv_ln_g, conv_ln_b, tm_c)
    dp_b, cs_b, d_sguw, d_sgub, gs["sgu_ln_g"], gs["sgu_ln_b"] = _sgu_bwd(
        "sgu_bwd", dyb, wf["w_b_out"], p, sgu_ln_g, sgu_ln_b, sgu_w[0], bias_full, tm_s)
    gs["conv_w"] = d_convw[:CW].reshape(1, CW, D)
    gs["sgu_w"] = d_sguw
    gs["sgu_b"] = d_sgub[:, :GROUPS].T
    gs["b_in"] = jnp.concatenate([cs_a, cs_b, cs_g], axis=1)
    gb["w_in"] = jnp.concatenate([_mm_tn("dw_in_a", dp_a, n_mix, 1.0, 1024, TK),
                                  _mm_tn("dw_in_b", dp_b, n_mix, 1.0, 1024, TK),
                                  _mm_tn("dw_in_g", dp_g, n_mix, 1.0, 1024, TK)], axis=0)
    core = ac.astype(jnp.int32).reshape(1)
    chip = my_chip.astype(jnp.int32).reshape(1)

    def pack(entries):
        return jnp.concatenate([gb[n].reshape(4, 2, rows, D) for n, rows, _ in entries], axis=2)

    last_g, rest_g = _BIG[:2], _BIG[2:]
    g4_rest = pack(rest_g)
    (dh1, dh1b, gs["mix_norm"]), (sib_rest,) = _dx_rms_bwd(
        "mix_bwd_dx", [(dp_a, wf["w_in"], 0, 2 * D, False), (dp_b, wf["w_in"], 1, 2 * D, False),
                       (dp_g, wf["w_in"], 2, 2 * D, False)], h1, mix_norm, dh2, tm_s,
        hosted=_pair_exchange_comm(g4_rest))
    part_rest = _pair_sum("grads_pair_sum_rest", g4_rest, sib_rest, core, 544)

    def ffn1_exchange(gwgu, gwd):
        return _all_to_all_comm(jnp.concatenate(
            [gwgu.reshape(NDEV, -1, D), gwd.reshape(NDEV, -1, D)], axis=1))

    dx, _, gs["ffn1_norm"], _, _, (chips_rest,), (last_slots,) = ffn_bwd(
        "ffn1", dh1b, dh1, g1, u1, a1, xn1, xs, ffn1_norm, wf["ffn1_w_gu"], wf["ffn1_w_down"],
        act_hosted=_chip_exchange_comm(part_rest), dx_hosted=ffn1_exchange)
    gsum_rest = _chip_sum("grads_chip_sum_rest", part_rest, chips_rest, chip, 544)
    gsum_last = _slot_sum("grads_slot_sum_ffn1", last_slots, 528)

    grads, deltas, new_m, new_v = {}, {}, {}, {}
    for entries, gsum in ((last_g, gsum_last), (rest_g, gsum_rest)):
        off = 0
        for name, rows, transposed in entries:
            gsh = gsum[off:off + rows]
            off += rows
            gsh = gsh.T if transposed else gsh
            d, mo, vo = _adamw("adamw_" + name, w[name][0], gsh, mom[name][0], vel[name][0], gsh.shape[0] // 2)
            grads[name], deltas[name], new_m[name], new_v[name] = gsh[None], d[None], mo[None], vo[None]

    shapes = {n: w[n].shape for n, _ in _SMALL}
    (g8,) = _run_comm("gather_small_grads", _gather_comm(_pack_small(gs, my_dev)))
    sg, sd, sm, sv = _adamw_small("adamw_small", _pack_small(w, my_dev), g8,
                                  _pack_small(mom, my_dev), _pack_small(vel, my_dev))
    for dst, src in ((grads, sg), (deltas, sd), (new_m, sm), (new_v, sv)):
        dst.update(_unpack_small(src, shapes, my_dev))

    loss = lax.psum(loss_blk[0, 0], AXES)
    order = ["ffn1_norm", "ffn1_w_gu", "ffn1_w_down", "mix_norm", "w_in", "b_in", "conv_w", "conv_b",
             "conv_ln_g", "conv_ln_b", "w_a_out", "sgu_ln_g", "sgu_ln_b", "sgu_w", "sgu_b", "w_b_out",
             "w_out", "xattn_norm", "mem_norm", "w_q", "w_kv", "w_o", "ffn2_norm", "ffn2_w_gu",
             "ffn2_w_down", "final_norm"]
    return (loss, dx.reshape(x.shape), *[grads[n] for n in order], *[deltas[n] for n in order],
            *[new_m[n] for n in order], *[new_v[n] for n in order])
```

```python
import functools
import math

import jax
import jax.numpy as jnp
from jax import lax
from jax.experimental import pallas as pl
from jax.experimental.pallas import tpu as pltpu

F32 = jnp.float32
BF16 = jnp.bfloat16
MESH = pl.DeviceIdType.MESH
AXES = ("x", "y", "c")

D = 1024
DFF = 2816
NMEM = 256
HEADS = 4
HD = D // HEADS
CW = 31
HALO = 32
CHUNK = 128
GROUPS = 4
GD = D // GROUPS
EPS_RMS = 1e-6
EPS_LN = 1e-5
LR, B1, B2, EPS_ADAM, WD, STEP = 0.001, 0.9, 0.999, 1e-08, 0.01, 10
NDEV = 8
VMEM_LIMIT = 56 * 1024 * 1024
TK = 2048


def _params(*sem):
    return pltpu.CompilerParams(dimension_semantics=sem, vmem_limit_bytes=VMEM_LIMIT)


def _dot(a, b):
    return jnp.dot(a, b, preferred_element_type=F32)


def _dot_nt(a, b):
    return lax.dot_general(a, b, (((1,), (1,)), ((), ())), preferred_element_type=F32)


def _dot_tn(a, b):
    return lax.dot_general(a, b, (((0,), (0,)), ((), ())), preferred_element_type=F32)


def _sigmoid(x):
    return 0.5 * jnp.tanh(0.5 * x) + 0.5


_GELU_C = math.sqrt(2.0 / math.pi)


def _gelu_with_grad(x):
    x2 = x * x
    t = jnp.tanh(_GELU_C * (x + 0.044715 * (x2 * x)))
    half = 0.5 * (1.0 + t)
    return x * half, half + 0.5 * x * (1.0 - t * t) * (_GELU_C * (1.0 + 3.0 * 0.044715 * x2))


def _gelu(x):
    return _gelu_with_grad(x)[0]


def _rms_stats(h):
    r = lax.rsqrt(jnp.mean(h * h, axis=-1, keepdims=True) + EPS_RMS)
    return r, h * r


def _rms_bwd(dxn, h, gain):
    r, xh = _rms_stats(h)
    dgain = jnp.sum(dxn * xh, axis=0, keepdims=True)
    dxh = dxn * gain
    dh = r * (dxh - xh * jnp.mean(dxh * xh, axis=-1, keepdims=True))
    return dh, dgain


def _ln_stats(c):
    mu = jnp.mean(c, axis=-1, keepdims=True)
    xc = c - mu
    rstd = lax.rsqrt(jnp.mean(xc * xc, axis=-1, keepdims=True) + EPS_LN)
    return rstd, xc * rstd


def _ln_bwd(dy, xhat, rstd, g):
    dxh = dy * g
    return rstd * (dxh - jnp.mean(dxh, axis=-1, keepdims=True)
                   - xhat * jnp.mean(dxh * xhat, axis=-1, keepdims=True))


def _row(tm, cols, cb=0):
    return pl.BlockSpec((tm, cols), lambda i, _cb=cb: (i, _cb))


def _const(shape):
    n = len(shape)
    return pl.BlockSpec(shape, lambda *_: (0,) * n)


def _sds(shape, dtype):
    return jax.ShapeDtypeStruct(shape, dtype)


def _acc(ref, first, val):
    @pl.when(first)
    def _():
        ref[...] = jnp.zeros_like(ref)
    ref[...] += val


class _Comm:
    def __init__(self, args, out_shapes, scratch, start, finish):
        self.args, self.out_shapes, self.scratch = args, out_shapes, scratch
        self.start, self.finish = start, finish


_ANY = pl.BlockSpec(memory_space=pl.ANY)


def _run_comm(name, comm):
    ni, no = len(comm.args), len(comm.out_shapes)

    def body(*refs):
        ins, outs, sems = refs[:ni], refs[ni:ni + no], refs[ni + no:]
        comm.start(ins, outs, sems)
        comm.finish(ins, outs, sems)

    return pl.pallas_call(
        body, name=name, out_shape=list(comm.out_shapes), in_specs=[_ANY] * ni, out_specs=[_ANY] * no,
        scratch_shapes=list(comm.scratch),
    )(*comm.args)


def _call(name, body, grid, in_specs, out_specs, out_shape, args, scratch=(), sem=None, hosted=None):
    n_in, n_out, n_scr = len(in_specs), len(out_specs), len(scratch)
    if not hosted:
        outs = pl.pallas_call(
            body, name=name, grid=grid, in_specs=list(in_specs), out_specs=list(out_specs),
            out_shape=list(out_shape), scratch_shapes=list(scratch), compiler_params=_params(*sem),
        )(*args)
        return outs, []
    comms = list(hosted) if isinstance(hosted, (list, tuple)) else [hosted]
    hi = sum(len(cm.args) for cm in comms)
    ho = sum(len(cm.out_shapes) for cm in comms)

    def wrapped(*refs):
        ins, h_in = refs[:n_in], refs[n_in:n_in + hi]
        o0 = n_in + hi
        outs, h_out = refs[o0:o0 + n_out], refs[o0 + n_out:o0 + n_out + ho]
        s0 = o0 + n_out + ho
        scr, h_sems = refs[s0:s0 + n_scr], refs[s0 + n_scr:]
        ids = [pl.program_id(a) for a in range(len(grid))]
        first = functools.reduce(jnp.logical_and, [i == 0 for i in ids])
        last = functools.reduce(jnp.logical_and, [i == g - 1 for i, g in zip(ids, grid)])
        parts, a0, b0, c0 = [], 0, 0, 0
        for cm in comms:
            na, nb, nc = len(cm.args), len(cm.out_shapes), len(cm.scratch)
            parts.append((cm, h_in[a0:a0 + na], h_out[b0:b0 + nb], h_sems[c0:c0 + nc]))
            a0, b0, c0 = a0 + na, b0 + nb, c0 + nc

        @pl.when(first)
        def _():
            for cm, ci, co, cs in parts:
                cm.start(ci, co, cs)

        body(*ins, *outs, *scr)

        @pl.when(last)
        def _():
            for cm, ci, co, cs in parts:
                cm.finish(ci, co, cs)

    res = pl.pallas_call(
        wrapped, name=name, grid=grid, in_specs=list(in_specs) + [_ANY] * hi,
        out_specs=list(out_specs) + [_ANY] * ho,
        out_shape=list(out_shape) + [s for cm in comms for s in cm.out_shapes],
        scratch_shapes=list(scratch) + [s for cm in comms for s in cm.scratch],
        compiler_params=_params(*(["arbitrary"] * len(grid))),
    )(*args, *[a for cm in comms for a in cm.args])
    return res[:n_out], res[n_out:]


def _gather_comm(blk):
    r, cdim = blk.shape

    def copies(x_ref, out_ref, send_sems, recv_sems, local_sem):
        x, y, c = lax.axis_index("x"), lax.axis_index("y"), lax.axis_index("c")
        me, sibling = (x, y, c), (x, y, 1 - c)
        chips = [(1 - x, y), (x, 1 - y), (1 - x, 1 - y)]

        def slot(px, py, pc):
            return out_ref.at[4 * px + 2 * py + pc]

        def copy(k, block, to, src=None):
            return pltpu.make_async_remote_copy(
                src_ref=slot(*block) if src is None else src, dst_ref=slot(*block),
                send_sem=send_sems.at[k], recv_sem=recv_sems.at[k],
                device_id=to, device_id_type=MESH)

        mine = pltpu.make_async_copy(x_ref, slot(*me), local_sem)
        first = [copy(0, me, sibling, src=x_ref)]
        first += [copy(1 + j, me, (*chip, c), src=x_ref) for j, chip in enumerate(chips)]
        passed = [copy(4 + j, (*chip, c), sibling) for j, chip in enumerate(chips)]
        landed = [copy(1 + j, (*chip, c), me) for j, chip in enumerate(chips)]
        from_sibling = [copy(0, sibling, me)] + [copy(4 + j, (*chip, 1 - c), me) for j, chip in enumerate(chips)]
        return mine, first, passed, landed, from_sibling

    def start(ins, outs, sems):
        mine, first, _, _, _ = copies(ins[0], outs[0], *sems)
        mine.start()
        for cp in first:
            cp.start()

    def finish(ins, outs, sems):
        mine, first, passed, landed, from_sibling = copies(ins[0], outs[0], *sems)
        for arrived, forward in zip(landed, passed):
            arrived.wait_recv()
            forward.start()
        for cp in from_sibling:
            cp.wait_recv()
        for cp in first + passed:
            cp.wait_send()
        mine.wait()

    return _Comm([blk], [_sds((NDEV, r, cdim), blk.dtype)],
                 [pltpu.SemaphoreType.DMA((7,)), pltpu.SemaphoreType.DMA((7,)), pltpu.SemaphoreType.DMA],
                 start, finish)


def _exchange_comm(src, n, out_rows, make):
    r, cdim = out_rows

    def copies(src_ref, out_ref, send_sems, recv_sems):
        out = []
        for k in range(n):
            s, d, to = make(k, src_ref, out_ref)
            out.append(pltpu.make_async_remote_copy(
                src_ref=s, dst_ref=d, send_sem=send_sems.at[k], recv_sem=recv_sems.at[k],
                device_id=to, device_id_type=MESH))
        return out

    def start(ins, outs, sems):
        for cp in copies(ins[0], outs[0], *sems):
            cp.start()

    def finish(ins, outs, sems):
        cps = copies(ins[0], outs[0], *sems)
        for cp in cps:
            cp.wait_recv()
        for cp in cps:
            cp.wait_send()

    return _Comm([src], [_sds((n, r, cdim), src.dtype)],
                 [pltpu.SemaphoreType.DMA((n,)), pltpu.SemaphoreType.DMA((n,))], start, finish)


def _pair_exchange_comm(g4):
    def make(k, g_ref, out_ref):
        x, y, c = lax.axis_index("x"), lax.axis_index("y"), lax.axis_index("c")
        return g_ref.at[k, 1 - c], out_ref.at[k], (x, y, 1 - c)

    return _exchange_comm(g4, 4, g4.shape[2:], make)


def _chip_exchange_comm(part):
    def make(k, p_ref, out_ref):
        x, y, c = lax.axis_index("x"), lax.axis_index("y"), lax.axis_index("c")
        px = x if ((k + 1) >> 1) == 0 else 1 - x
        py = y if ((k + 1) & 1) == 0 else 1 - y
        return p_ref.at[2 * px + py], out_ref.at[k], (px, py, c)

    return _exchange_comm(part, 3, part.shape[1:], make)


def _all_to_all_comm(g8):
    _, r, cdim = g8.shape

    def copies(g_ref, out_ref, send_sems, recv_sems, local_sem):
        x, y, c = lax.axis_index("x"), lax.axis_index("y"), lax.axis_index("c")
        me = 4 * x + 2 * y + c
        remote = []
        for k in range(1, NDEV):
            px = 1 - x if k & 4 else x
            py = 1 - y if k & 2 else y
            pc = 1 - c if k & 1 else c
            remote.append(pltpu.make_async_remote_copy(
                src_ref=g_ref.at[4 * px + 2 * py + pc], dst_ref=out_ref.at[me],
                send_sem=send_sems.at[k - 1], recv_sem=recv_sems.at[k - 1],
                device_id=(px, py, pc), device_id_type=MESH))
        return pltpu.make_async_copy(g_ref.at[me], out_ref.at[me], local_sem), remote

    def start(ins, outs, sems):
        mine, remote = copies(ins[0], outs[0], *sems)
        mine.start()
        for cp in remote:
            cp.start()

    def finish(ins, outs, sems):
        mine, remote = copies(ins[0], outs[0], *sems)
        for cp in remote:
            cp.wait_recv()
        for cp in remote:
            cp.wait_send()
        mine.wait()

    return _Comm([g8], [_sds((NDEV, r, cdim), g8.dtype)],
                 [pltpu.SemaphoreType.DMA((NDEV - 1,)), pltpu.SemaphoreType.DMA((NDEV - 1,)),
                  pltpu.SemaphoreType.DMA], start, finish)


def _slot_sum(name, slots, tr):
    _, r, cdim = slots.shape

    def body(s_ref, o_ref):
        s = s_ref[0].astype(F32)
        for k in range(1, NDEV):
            s = s + s_ref[k].astype(F32)
        o_ref[...] = s

    return pl.pallas_call(
        body, name=name, grid=(r // tr,),
        in_specs=[pl.BlockSpec((NDEV, tr, cdim), lambda i: (0, i, 0))],
        out_specs=pl.BlockSpec((tr, cdim), lambda i: (i, 0)),
        out_shape=_sds((r, cdim), F32), compiler_params=_params("parallel"),
    )(slots)


def _pair_sum(name, g4, recv, core, tr):
    _, _, r, cdim = g4.shape

    def body(core_ref, a_ref, b_ref, o_ref):
        o_ref[...] = (a_ref[...].astype(F32) + b_ref[...].astype(F32)).astype(o_ref.dtype)

    return pl.pallas_call(
        body, name=name,
        grid_spec=pltpu.PrefetchScalarGridSpec(
            num_scalar_prefetch=1, grid=(4, r // tr),
            in_specs=[pl.BlockSpec((None, None, tr, cdim), lambda k, i, cr: (k, cr[0], i, 0)),
                      pl.BlockSpec((None, tr, cdim), lambda k, i, cr: (k, i, 0))],
            out_specs=pl.BlockSpec((None, tr, cdim), lambda k, i, cr: (k, i, 0))),
        out_shape=_sds((4, r, cdim), BF16),
        compiler_params=_params("parallel", "parallel"),
    )(core, g4, recv)


def _chip_sum(name, part, recv, chip, tr):
    _, r, cdim = part.shape

    def body(chip_ref, a_ref, b_ref, o_ref):
        s = a_ref[...].astype(F32)
        for k in range(3):
            s = s + b_ref[k].astype(F32)
        o_ref[...] = s

    return pl.pallas_call(
        body, name=name,
        grid_spec=pltpu.PrefetchScalarGridSpec(
            num_scalar_prefetch=1, grid=(r // tr,),
            in_specs=[pl.BlockSpec((None, tr, cdim), lambda i, cr: (cr[0], i, 0)),
                      pl.BlockSpec((3, tr, cdim), lambda i, cr: (0, i, 0))],
            out_specs=pl.BlockSpec((tr, cdim), lambda i, cr: (i, 0))),
        out_shape=_sds((r, cdim), F32),
        compiler_params=_params("parallel"),
    )(chip, part, recv)


def _rms_cast(name, h, gain, tm, hosted=None):
    t = h.shape[0]

    def body(h_ref, g_ref, o_ref):
        _, xh = _rms_stats(h_ref[...])
        o_ref[...] = (xh * g_ref[...]).astype(BF16)

    return _call(name, body, (t // tm,), [_row(tm, D), _const((1, D))], [_row(tm, D)],
                 [_sds((t, D), BF16)], (h, gain), sem=("parallel",), hosted=hosted)


def _ffn_up(name, xn, wgu_t, tm, tn, hosted=None):
    t = xn.shape[0]
    nh = DFF // tn

    def body(x_ref, wg_ref, wu_ref, g_ref, u_ref, a_ref):
        x = x_ref[...]
        g = _dot_nt(x, wg_ref[...])
        u = _dot_nt(x, wu_ref[...])
        g_ref[...] = g.astype(BF16)
        u_ref[...] = u.astype(BF16)
        a_ref[...] = (g * _sigmoid(g) * u).astype(BF16)

    o = pl.BlockSpec((tm, tn), lambda j, i: (i, j))
    return _call(name, body, (nh, t // tm),
                 [pl.BlockSpec((tm, D), lambda j, i: (i, 0)),
                  pl.BlockSpec((tn, D), lambda j, i: (j, 0)),
                  pl.BlockSpec((tn, D), lambda j, i: (j + nh, 0))],
                 [o, o, o], [_sds((t, DFF), BF16)] * 3, (xn, wgu_t, wgu_t),
                 sem=("parallel", "parallel"), hosted=hosted)


def _ffn_down(name, a, wd, h, gain, tm):
    t = a.shape[0]

    def body(a_ref, w_ref, h_ref, g_ref, o_ref, n_ref):
        hn = h_ref[...] + 0.5 * _dot(a_ref[...], w_ref[...])
        o_ref[...] = hn
        _, xh = _rms_stats(hn)
        n_ref[...] = (xh * g_ref[...]).astype(BF16)

    return pl.pallas_call(
        body, name=name, grid=(t // tm,),
        in_specs=[_row(tm, DFF), _const((DFF, D)), _row(tm, D), _const((1, D))],
        out_specs=[_row(tm, D), _row(tm, D)],
        out_shape=[_sds((t, D), F32), _sds((t, D), BF16)],
        compiler_params=_params("parallel"),
    )(a, wd, h, gain)


def _mix_in(name, n, win_t, b_in, tm, tn, hosted=None):
    t = n.shape[0]

    def body(n_ref, w_ref, b_ref, p_ref):
        p_ref[...] = (_dot_nt(n_ref[...], w_ref[...]) + b_ref[...]).astype(BF16)

    return _call(name, body, (6 * D // tn, t // tm),
                 [pl.BlockSpec((tm, D), lambda j, i: (i, 0)),
                  pl.BlockSpec((tn, D), lambda j, i: (j, 0)),
                  pl.BlockSpec((1, tn), lambda j, i: (0, j))],
                 [pl.BlockSpec((tm, tn), lambda j, i: (i, j))], [_sds((t, 6 * D), BF16)],
                 (n, win_t, b_in), sem=("parallel", "parallel"), hosted=hosted)


RC = 64
LANES = 128


def _shift_copies(ext, shifted, tm):
    n = tm + HALO - 8
    for m in range(1, 8):
        shifted[m - 1] = ext[pl.ds(m, n), :]


def _by_residue(offs):
    groups = {}
    for k, off in enumerate(offs):
        q, m = divmod(off, 8)
        groups.setdefault(m, []).append((k, q))
    return groups


def _residue_window(ext, shifted, m, taps, base, cs):
    src = ext if m == 0 else shifted.at[m - 1]
    return src[pl.ds(base, RC + 8 * max(q for _, q in taps)), cs]


def _tap_sum(out_ref, bias_ref, w_ref, ext, shifted, offs, tm):
    groups = _by_residue(offs)

    def chunk(j, carry):
        base = pl.multiple_of(j * RC, RC)
        for c in range(D // LANES):
            cs = pl.ds(c * LANES, LANES)
            acc = jnp.zeros((RC, LANES), F32)
            if bias_ref is not None:
                acc = acc + bias_ref[:, cs]
            for m, taps in groups.items():
                big = _residue_window(ext, shifted, m, taps, base, cs)
                for k, q in taps:
                    acc = acc + w_ref[pl.ds(k, 1), cs] * big[8 * q:8 * q + RC]
            out_ref[pl.ds(base, RC), cs] = acc
        return carry

    lax.fori_loop(0, tm // RC, chunk, 0)


def _tap_corr(dw_ref, dc_ext, ext, shifted, offs, tm):
    groups = _by_residue(offs)
    for c in range(D // LANES):
        cs = pl.ds(c * LANES, LANES)

        def chunk(j, accs, cs=cs):
            base = pl.multiple_of(j * RC, RC)
            dcv = dc_ext[pl.ds(base, RC), cs]
            out = list(accs)
            for m, taps in groups.items():
                big = _residue_window(ext, shifted, m, taps, base, cs)
                for k, q in taps:
                    prod = dcv * big[8 * q:8 * q + RC]
                    part = prod[0:8]
                    for s in range(1, RC // 8):
                        part = part + prod[8 * s:8 * s + 8]
                    out[k] = accs[k] + part
            return tuple(out)

        accs = lax.fori_loop(0, tm // RC, chunk, tuple(jnp.zeros((8, LANES), F32) for _ in offs))
        for k in range(len(offs)):
            dw_ref[pl.ds(k, 1), cs] += jnp.sum(accs[k], axis=0, keepdims=True)


def _conv_fwd(name, p, conv_w, conv_b, ln_g, ln_b, tm):
    t = p.shape[0]

    def body(av_ref, ag_ref, w_ref, cb_ref, lg_ref, lb_ref, c_ref, a_ref, ext, shifted):
        i = pl.program_id(0)

        @pl.when(i == 0)
        def _():
            ext[pl.ds(0, HALO), :] = jnp.zeros((HALO, D), F32)

        ext[pl.ds(HALO, tm), :] = av_ref[...].astype(F32) * _sigmoid(ag_ref[...].astype(F32))
        _shift_copies(ext, shifted, tm)
        _tap_sum(c_ref, cb_ref, w_ref, ext, shifted, [HALO - (CW - 1) + k for k in range(CW)], tm)
        rstd, chat = _ln_stats(c_ref[...])
        ca = chat * lg_ref[...] + lb_ref[...]
        a_ref[...] = (ca * _sigmoid(ca)).astype(BF16)
        ext[pl.ds(0, HALO), :] = ext[pl.ds(tm, HALO), :]

    return pl.pallas_call(
        body, name=name, grid=(t // tm,),
        in_specs=[_row(tm, D, 0), _row(tm, D, 1), _const((HALO, D)), _const((1, D)),
                  _const((1, D)), _const((1, D))],
        out_specs=[_row(tm, D), _row(tm, D)],
        out_shape=[_sds((t, D), F32), _sds((t, D), BF16)],
        scratch_shapes=[pltpu.VMEM((tm + HALO, D), F32), pltpu.VMEM((7, tm + HALO - 8, D), F32)],
        compiler_params=_params("arbitrary"),
    )(p, p, conv_w, conv_b, ln_g, ln_b)


def _sgu_mask():
    rows = lax.broadcasted_iota(jnp.int32, (CHUNK, CHUNK), 0)
    cols = lax.broadcasted_iota(jnp.int32, (CHUNK, CHUNK), 1)
    return cols <= rows


def _sgu_fwd(name, p, ln_g, ln_b, sgu_w, bias_full, tm):
    t = p.shape[0]

    def body(bu_ref, bv_ref, lg_ref, lb_ref, ws_ref, bias_ref, o_ref):
        mask = _sgu_mask()
        _, vhat = _ln_stats(_gelu(bv_ref[...].astype(F32)))
        vn = (vhat * lg_ref[...] + lb_ref[...]).astype(BF16)
        ub = _gelu(bu_ref[...].astype(F32))
        for g in range(GROUPS):
            wm = jnp.where(mask, ws_ref[g], 0.0).astype(BF16)
            cs = slice(g * GD, (g + 1) * GD)
            for cc in range(tm // CHUNK):
                rs = slice(cc * CHUNK, (cc + 1) * CHUNK)
                mixed = _dot(wm, vn[rs, cs]) + bias_ref[:, cs]
                o_ref[rs, cs] = (ub[rs, cs] * mixed).astype(BF16)

    return pl.pallas_call(
        body, name=name, grid=(t // tm,),
        in_specs=[_row(tm, D, 2), _row(tm, D, 3), _const((1, D)), _const((1, D)),
                  _const((GROUPS, CHUNK, CHUNK)), _const((CHUNK, D))],
        out_specs=_row(tm, D), out_shape=_sds((t, D), BF16),
        compiler_params=_params("parallel"),
    )(p, p, ln_g, ln_b, sgu_w, bias_full)


def _merge_fwd(name, act_a, act_b, p, w_a, w_b, w_out, h, gain, tm):
    t = h.shape[0]

    def body(a_ref, b_ref, ga_ref, gb_ref, wa_ref, wb_ref, wo_ref, h_ref, g_ref,
             ya_ref, yb_ref, mg_ref, ho_ref, xn_ref):
        ya = _dot(a_ref[...], wa_ref[...])
        yb = _dot(b_ref[...], wb_ref[...])
        ya_ref[...] = ya.astype(BF16)
        yb_ref[...] = yb.astype(BF16)
        merged = (_sigmoid(ga_ref[...].astype(F32)) * ya
                  + _sigmoid(gb_ref[...].astype(F32)) * yb).astype(BF16)
        mg_ref[...] = merged
        hn = h_ref[...] + _dot(merged, wo_ref[...])
        ho_ref[...] = hn
        _, xh = _rms_stats(hn)
        xn_ref[...] = (xh * g_ref[...]).astype(BF16)

    rb = _row(tm, D)
    return pl.pallas_call(
        body, name=name, grid=(t // tm,),
        in_specs=[rb, rb, _row(tm, D, 4), _row(tm, D, 5), _const((D, D)), _const((D, D)),
                  _const((D, D)), rb, _const((1, D))],
        out_specs=[rb] * 5,
        out_shape=[_sds((t, D), BF16)] * 3 + [_sds((t, D), F32), _sds((t, D), BF16)],
        compiler_params=_params("parallel"),
    )(act_a, act_b, p, p, w_a, w_b, w_out, h, gain)


def _kv_fwd(name, mem, gain, wkv_t):
    def body(m_ref, g_ref, w_ref, mn_ref, k_ref, v_ref):
        _, xh = _rms_stats(m_ref[...])
        mn = (xh * g_ref[...]).astype(BF16)
        mn_ref[...] = mn
        kv = _dot_nt(mn, w_ref[...])
        k_ref[...] = kv[:, :D].astype(BF16)
        v_ref[...] = kv[:, D:].astype(BF16)

    return pl.pallas_call(
        body, name=name,
        out_shape=[_sds((NMEM, D), BF16)] * 3,
        compiler_params=pltpu.CompilerParams(vmem_limit_bytes=VMEM_LIMIT),
    )(mem, gain, wkv_t)


def _softmax_rows(s):
    e = jnp.exp(s - jnp.max(s, axis=-1, keepdims=True))
    return e / jnp.sum(e, axis=-1, keepdims=True)


def _attn_fwd(name, xq, w_q, kb, vb, w_o, h, gain, tm):
    t = h.shape[0]
    scale = 1.0 / math.sqrt(HD)

    def body(x_ref, wq_ref, k_ref, v_ref, wo_ref, h_ref, g_ref, q_ref, o_ref, ho_ref, xn_ref):
        q_ref[...] = _dot(x_ref[...], wq_ref[...]).astype(BF16)
        for hd in range(HEADS):
            cs = slice(hd * HD, (hd + 1) * HD)
            p = _softmax_rows(_dot_nt(q_ref[:, cs], k_ref[:, cs]) * scale)
            o_ref[:, cs] = _dot(p.astype(BF16), v_ref[:, cs]).astype(BF16)
        hn = h_ref[...] + _dot(o_ref[...], wo_ref[...])
        ho_ref[...] = hn
        _, xh = _rms_stats(hn)
        xn_ref[...] = (xh * g_ref[...]).astype(BF16)

    rb = _row(tm, D)
    return pl.pallas_call(
        body, name=name, grid=(t // tm,),
        in_specs=[rb, _const((D, D)), _const((NMEM, D)), _const((NMEM, D)), _const((D, D)), rb,
                  _const((1, D))],
        out_specs=[rb] * 4,
        out_shape=[_sds((t, D), BF16), _sds((t, D), BF16), _sds((t, D), F32), _sds((t, D), BF16)],
        compiler_params=_params("parallel"),
    )(xq, w_q, kb, vb, w_o, h, gain)


def _ffn_down_loss(name, a, wd, h, gain, target, tm):
    t = h.shape[0]
    steps = t // tm

    def body(a_ref, w_ref, h_ref, g_ref, t_ref, dh_ref, dhb_ref, loss_ref, dg_ref, lacc):
        i = pl.program_id(0)
        hv = h_ref[...] + 0.5 * _dot(a_ref[...], w_ref[...])
        r, xh = _rms_stats(hv)
        err = xh * g_ref[...] - t_ref[...]
        _acc(lacc, i == 0, jnp.sum(err * err, axis=0, keepdims=True))
        dy = err * (1.0 / D)
        _acc(dg_ref, i == 0, jnp.sum(dy * xh, axis=0, keepdims=True))
        dxh = dy * g_ref[...]
        dh = r * (dxh - xh * jnp.mean(dxh * xh, axis=-1, keepdims=True))
        dh_ref[...] = dh
        dhb_ref[...] = dh.astype(BF16)

        @pl.when(i == steps - 1)
        def _():
            loss_ref[...] = jnp.zeros((8, 128), F32) + (0.5 / D) * jnp.sum(lacc[...])

    rb = _row(tm, D)
    return pl.pallas_call(
        body, name=name, grid=(steps,),
        in_specs=[_row(tm, DFF), _const((DFF, D)), rb, _const((1, D)), rb],
        out_specs=[rb, rb, _const((8, 128)), _const((1, D))],
        out_shape=[_sds((t, D), F32), _sds((t, D), BF16), _sds((8, 128), F32), _sds((1, D), F32)],
        scratch_shapes=[pltpu.VMEM((1, D), F32)],
        compiler_params=_params("arbitrary"),
    )(a, wd, h, gain, target)


def _ffn_bwd_act(name, dhb, wd, g, u, tm, tn, hosted=None):
    t = dhb.shape[0]

    def body(d_ref, w_ref, g_ref, u_ref, dg_ref, du_ref):
        da = 0.5 * _dot_nt(d_ref[...], w_ref[...])
        gv = g_ref[...].astype(F32)
        sg = _sigmoid(gv)
        dg_ref[...] = (da * u_ref[...].astype(F32) * (sg * (1.0 + gv * (1.0 - sg)))).astype(BF16)
        du_ref[...] = (da * (gv * sg)).astype(BF16)

    o = pl.BlockSpec((tm, tn), lambda j, i: (i, j))
    return _call(name, body, (DFF // tn, t // tm),
                 [pl.BlockSpec((tm, D), lambda j, i: (i, 0)),
                  pl.BlockSpec((tn, D), lambda j, i: (j, 0)), o, o],
                 [o, o], [_sds((t, DFF), BF16)] * 2, (dhb, wd, g, u),
                 sem=("parallel", "parallel"), hosted=hosted)


def _dx_rms_bwd(name, pairs, h, gain, dh_in, tm, hosted=None):
    t = h.shape[0]
    np_ = len(pairs)

    def body(*refs):
        a_refs = refs[:np_]
        b_refs = refs[np_:2 * np_]
        h_ref, g_ref, d_ref, o_ref, ob_ref, dg_ref = refs[2 * np_:]
        dxn = None
        for (a_ref, b_ref, pr) in zip(a_refs, b_refs, pairs):
            y = _dot_nt(a_ref[...], b_ref[...]) if pr[4] else _dot(a_ref[...], b_ref[...])
            dxn = y if dxn is None else dxn + y
        dh, dgain = _rms_bwd(dxn, h_ref[...], g_ref[...])
        _acc(dg_ref, pl.program_id(0) == 0, dgain)
        out = d_ref[...] + dh
        o_ref[...] = out
        ob_ref[...] = out.astype(BF16)

    ins, args = [], []
    for (a, b, blk, rows, tr) in pairs:
        ins.append(_row(tm, a.shape[1]))
        args.append(a)
    for (a, b, blk, rows, tr) in pairs:
        ins.append(pl.BlockSpec((rows, b.shape[1]), lambda i, _b=blk: (_b, 0)))
        args.append(b)
    rb = _row(tm, D)
    ins += [rb, _const((1, D)), rb]
    args += [h, gain, dh_in]
    return _call(name, body, (t // tm,), ins, [rb, rb, _const((1, D))],
                 [_sds((t, D), F32), _sds((t, D), BF16), _sds((1, D), F32)], args,
                 sem=("arbitrary",), hosted=hosted)


def _mm_tn(name, a, b, scale, tmo, tk):
    t, m = a.shape
    n = b.shape[1]
    tk = min(tk, t)
    steps = t // tk

    def body(a_ref, b_ref, o_ref, acc):
        k = pl.program_id(1)
        _acc(acc, k == 0, _dot_tn(a_ref[...], b_ref[...]))

        @pl.when(k == steps - 1)
        def _():
            o_ref[...] = (acc[...] * scale).astype(o_ref.dtype)

    return pl.pallas_call(
        body, name=name, grid=(m // tmo, steps),
        in_specs=[pl.BlockSpec((tk, tmo), lambda i, k: (k, i)),
                  pl.BlockSpec((tk, n), lambda i, k: (k, 0))],
        out_specs=pl.BlockSpec((tmo, n), lambda i, k: (i, 0)),
        out_shape=_sds((m, n), BF16),
        scratch_shapes=[pltpu.VMEM((tmo, n), F32)],
        compiler_params=_params("parallel", "arbitrary"),
    )(a, b)


def _attn_bwd(name, dhb, w_o, qb, kb, vb, tm):
    t = dhb.shape[0]
    scale = 1.0 / math.sqrt(HD)

    def body(d_ref, wo_ref, q_ref, k_ref, v_ref, dq_ref, dk_ref, dv_ref, do_s):
        i = pl.program_id(0)

        @pl.when(i == 0)
        def _():
            dk_ref[...] = jnp.zeros_like(dk_ref)
            dv_ref[...] = jnp.zeros_like(dv_ref)

        do_s[...] = _dot_nt(d_ref[...], wo_ref[...]).astype(BF16)
        for hd in range(HEADS):
            cs = slice(hd * HD, (hd + 1) * HD)
            q = q_ref[:, cs]
            p = _softmax_rows(_dot_nt(q, k_ref[:, cs]) * scale)
            do = do_s[:, cs]
            dp = _dot_nt(do, v_ref[:, cs])
            ds = (p * (dp - jnp.sum(dp * p, axis=-1, keepdims=True)) * scale).astype(BF16)
            dq_ref[:, cs] = _dot(ds, k_ref[:, cs]).astype(BF16)
            dk_ref[:, cs] += _dot_tn(ds, q)
            dv_ref[:, cs] += _dot_tn(p.astype(BF16), do)

    rb = _row(tm, D)
    return pl.pallas_call(
        body, name=name, grid=(t // tm,),
        in_specs=[rb, _const((D, D)), rb, _const((NMEM, D)), _const((NMEM, D))],
        out_specs=[rb, _const((NMEM, D)), _const((NMEM, D))],
        out_shape=[_sds((t, D), BF16), _sds((NMEM, D), F32), _sds((NMEM, D), F32)],
        scratch_shapes=[pltpu.VMEM((tm, D), BF16)],
        compiler_params=_params("arbitrary"),
    )(dhb, w_o, qb, kb, vb)


def _kv_bwd(name, dk, dv, memn, wkv_t, mem, gain):
    def body(dk_ref, dv_ref, mn_ref, w_ref, m_ref, g_ref, dw_ref, dg_ref):
        dkb = dk_ref[...].astype(BF16)
        dvb = dv_ref[...].astype(BF16)
        mn = mn_ref[...]
        dw_ref[pl.ds(0, D), :] = _dot_tn(dkb, mn).astype(BF16)
        dw_ref[pl.ds(D, D), :] = _dot_tn(dvb, mn).astype(BF16)
        dmn = _dot(dkb, w_ref[pl.ds(0, D), :]) + _dot(dvb, w_ref[pl.ds(D, D), :])
        _, xh = _rms_stats(m_ref[...])
        dg_ref[...] = jnp.sum(dmn * xh, axis=0, keepdims=True)

    return pl.pallas_call(
        body, name=name,
        out_shape=[_sds((2 * D, D), BF16), _sds((1, D), F32)],
        compiler_params=pltpu.CompilerParams(vmem_limit_bytes=VMEM_LIMIT),
    )(dk, dv, memn, wkv_t, mem, gain)


def _merge_bwd(name, dhb, w_out, ya, yb, p, tm):
    t = dhb.shape[0]

    def body(d_ref, w_ref, ya_ref, yb_ref, ga_ref, gb_ref, dya_ref, dyb_ref, dp_ref, cs_ref):
        dm = _dot_nt(d_ref[...], w_ref[...])
        sa = _sigmoid(ga_ref[...].astype(F32))
        sb = _sigmoid(gb_ref[...].astype(F32))
        dya_ref[...] = (dm * sa).astype(BF16)
        dyb_ref[...] = (dm * sb).astype(BF16)
        dga = dm * ya_ref[...].astype(F32) * (sa * (1.0 - sa))
        dgb = dm * yb_ref[...].astype(F32) * (sb * (1.0 - sb))
        dp_ref[:, pl.ds(0, D)] = dga.astype(BF16)
        dp_ref[:, pl.ds(D, D)] = dgb.astype(BF16)
        first = pl.program_id(0) == 0

        @pl.when(first)
        def _():
            cs_ref[...] = jnp.zeros_like(cs_ref)
        cs_ref[:, pl.ds(0, D)] += jnp.sum(dga, axis=0, keepdims=True)
        cs_ref[:, pl.ds(D, D)] += jnp.sum(dgb, axis=0, keepdims=True)

    rb = _row(tm, D)
    return pl.pallas_call(
        body, name=name, grid=(t // tm,),
        in_specs=[rb, _const((D, D)), rb, rb, _row(tm, D, 4), _row(tm, D, 5)],
        out_specs=[rb, rb, _row(tm, 2 * D), _const((1, 2 * D))],
        out_shape=[_sds((t, D), BF16), _sds((t, D), BF16), _sds((t, 2 * D), BF16),
                   _sds((1, 2 * D), F32)],
        compiler_params=_params("arbitrary"),
    )(dhb, w_out, ya, yb, p, p)


def _conv_bwd(name, dya, w_a, c, p, conv_w, ln_g, ln_b, tm, hosted=None):
    t = dya.shape[0]
    steps = t // tm
    hb = tm // HALO

    def rev(i):
        return steps - 1 - i

    def body(dy_ref, wa_ref, c_ref, av_ref, ag_ref, avh_ref, agh_ref, w_ref, lg_ref, lb_ref,
             dp_ref, cs_ref, dw_ref, dcb_ref, dlg_ref, dlb_ref, dc_ext, a_ext, dc_sh, a_sh, da0_s):
        i = pl.program_id(0)
        first = i == 0

        @pl.when(first)
        def _():
            dc_ext[pl.ds(tm, HALO), :] = jnp.zeros((HALO, D), F32)
            dw_ref[...] = jnp.zeros_like(dw_ref)
            cs_ref[...] = jnp.zeros_like(cs_ref)

        d_act = _dot_nt(dy_ref[...], wa_ref[...])
        rstd, chat = _ln_stats(c_ref[...])
        ca = chat * lg_ref[...] + lb_ref[...]
        sc = _sigmoid(ca)
        dca = d_act * (sc * (1.0 + ca * (1.0 - sc)))
        _acc(dlg_ref, first, jnp.sum(dca * chat, axis=0, keepdims=True))
        _acc(dlb_ref, first, jnp.sum(dca, axis=0, keepdims=True))
        dc = _ln_bwd(dca, chat, rstd, lg_ref[...])
        _acc(dcb_ref, first, jnp.sum(dc, axis=0, keepdims=True))
        dc_ext[pl.ds(0, tm), :] = dc

        av = av_ref[...].astype(F32)
        sg = _sigmoid(ag_ref[...].astype(F32))
        a_ext[pl.ds(HALO, tm), :] = av * sg
        halo = avh_ref[...].astype(F32) * _sigmoid(agh_ref[...].astype(F32))
        a_ext[pl.ds(0, HALO), :] = jnp.where(i == steps - 1, 0.0, halo)

        _shift_copies(dc_ext, dc_sh, tm)
        _shift_copies(a_ext, a_sh, tm)
        _tap_sum(da0_s, None, w_ref, dc_ext, dc_sh, [CW - 1 - k for k in range(CW)], tm)
        _tap_corr(dw_ref, dc_ext, a_ext, a_sh, [HALO - (CW - 1) + k for k in range(CW)], tm)
        da0 = da0_s[...]
        dav = da0 * sg
        dag = da0 * av * (sg * (1.0 - sg))
        dp_ref[:, pl.ds(0, D)] = dav.astype(BF16)
        dp_ref[:, pl.ds(D, D)] = dag.astype(BF16)
        cs_ref[:, pl.ds(0, D)] += jnp.sum(dav, axis=0, keepdims=True)
        cs_ref[:, pl.ds(D, D)] += jnp.sum(dag, axis=0, keepdims=True)
        dc_ext[pl.ds(tm, HALO), :] = dc_ext[pl.ds(0, HALO), :]

    def rrow(cols, cb=0):
        return pl.BlockSpec((tm, cols), lambda i, _cb=cb: (rev(i), _cb))

    def halo_spec(cb):
        return pl.BlockSpec((HALO, D), lambda i, _cb=cb: (jnp.maximum(rev(i) * hb - 1, 0), _cb))

    return _call(
        name, body, (steps,),
        [rrow(D), _const((D, D)), rrow(D), rrow(D, 0), rrow(D, 1), halo_spec(0),
         halo_spec(1), _const((HALO, D)), _const((1, D)), _const((1, D))],
        [rrow(2 * D), _const((1, 2 * D)), _const((HALO, D)), _const((1, D)),
         _const((1, D)), _const((1, D))],
        [_sds((t, 2 * D), BF16), _sds((1, 2 * D), F32), _sds((HALO, D), F32),
         _sds((1, D), F32), _sds((1, D), F32), _sds((1, D), F32)],
        (dya, w_a, c, p, p, p, p, conv_w, ln_g, ln_b),
        scratch=[pltpu.VMEM((tm + HALO, D), F32), pltpu.VMEM((tm + HALO, D), F32),
                 pltpu.VMEM((7, tm + HALO - 8, D), F32), pltpu.VMEM((7, tm + HALO - 8, D), F32),
                 pltpu.VMEM((tm, D), F32)],
        sem=("arbitrary",), hosted=hosted)


def _sgu_bwd(name, dyb, w_b, p, ln_g, ln_b, sgu_w, bias_full, tm):
    t = dyb.shape[0]
    steps = t // tm

    def body(dy_ref, wb_ref, bu_ref, bv_ref, lg_ref, lb_ref, ws_ref, bias_ref,
             dp_ref, cs_ref, dws_ref, dsb_ref, dlg_ref, dlb_ref, dub_s, dvn_s, dbias_s):
        i = pl.program_id(0)
        first = i == 0
        mask = _sgu_mask()

        @pl.when(first)
        def _():
            dws_ref[...] = jnp.zeros_like(dws_ref)
            dbias_s[...] = jnp.zeros_like(dbias_s)
            cs_ref[...] = jnp.zeros_like(cs_ref)

        dob = _dot_nt(dy_ref[...], wb_ref[...])
        bu = bu_ref[...].astype(F32)
        bv = bv_ref[...].astype(F32)
        ub, ub_grad = _gelu_with_grad(bu)
        vb, vb_grad = _gelu_with_grad(bv)
        rstd, vhat = _ln_stats(vb)
        vn = (vhat * lg_ref[...] + lb_ref[...]).astype(BF16)
        for g in range(GROUPS):
            wm = jnp.where(mask, ws_ref[g], 0.0).astype(BF16)
            cs = slice(g * GD, (g + 1) * GD)
            for cc in range(tm // CHUNK):
                rs = slice(cc * CHUNK, (cc + 1) * CHUNK)
                vblk = vn[rs, cs]
                mixed = _dot(wm, vblk) + bias_ref[:, cs]
                dob_blk = dob[rs, cs]
                dub_s[rs, cs] = dob_blk * mixed
                dmixed = dob_blk * ub[rs, cs]
                dbias_s[:, cs] += dmixed
                dmb = dmixed.astype(BF16)
                dws_ref[g] += _dot_nt(dmb, vblk)
                dvn_s[rs, cs] = _dot_tn(wm, dmb)
        dbu = dub_s[...] * ub_grad
        dvn = dvn_s[...]
        _acc(dlg_ref, first, jnp.sum(dvn * vhat, axis=0, keepdims=True))
        _acc(dlb_ref, first, jnp.sum(dvn, axis=0, keepdims=True))
        dbv = _ln_bwd(dvn, vhat, rstd, lg_ref[...]) * vb_grad
        dp_ref[:, pl.ds(0, D)] = dbu.astype(BF16)
        dp_ref[:, pl.ds(D, D)] = dbv.astype(BF16)
        cs_ref[:, pl.ds(0, D)] += jnp.sum(dbu, axis=0, keepdims=True)
        cs_ref[:, pl.ds(D, D)] += jnp.sum(dbv, axis=0, keepdims=True)

        @pl.when(i == steps - 1)
        def _():
            lane = lax.broadcasted_iota(jnp.int32, (CHUNK, CHUNK), 1)
            dsb = jnp.zeros((CHUNK, CHUNK), F32)
            for g in range(GROUPS):
                dws_ref[g] = jnp.where(mask, dws_ref[g], 0.0)
                dsb = jnp.where(lane == g, jnp.sum(dbias_s[:, g * GD:(g + 1) * GD], axis=1, keepdims=True), dsb)
            dsb_ref[...] = dsb

    rb = _row(tm, D)
    return pl.pallas_call(
        body, name=name, grid=(steps,),
        in_specs=[rb, _const((D, D)), _row(tm, D, 2), _row(tm, D, 3), _const((1, D)), _const((1, D)),
                  _const((GROUPS, CHUNK, CHUNK)), _const((CHUNK, D))],
        out_specs=[_row(tm, 2 * D), _const((1, 2 * D)), _const((GROUPS, CHUNK, CHUNK)),
                   _const((CHUNK, CHUNK)), _const((1, D)), _const((1, D))],
        out_shape=[_sds((t, 2 * D), BF16), _sds((1, 2 * D), F32), _sds((GROUPS, CHUNK, CHUNK), F32),
                   _sds((CHUNK, CHUNK), F32), _sds((1, D), F32), _sds((1, D), F32)],
        scratch_shapes=[pltpu.VMEM((tm, D), F32), pltpu.VMEM((tm, D), F32), pltpu.VMEM((CHUNK, D), F32)],
        compiler_params=_params("arbitrary"),
    )(dyb, w_b, p, p, ln_g, ln_b, sgu_w, bias_full)


def _adam_math(w, g, m, v):
    m = B1 * m + (1.0 - B1) * g
    v = B2 * v + (1.0 - B2) * (g * g)
    m_hat = m / (1.0 - B1 ** STEP)
    v_hat = v / (1.0 - B2 ** STEP)
    delta = -LR * (m_hat / (jnp.sqrt(v_hat) + EPS_ADAM) + WD * w)
    return delta, m, v


def _adamw(name, w, g, m, v, tr):
    r, cdim = w.shape

    def body(w_ref, g_ref, m_ref, v_ref, d_ref, mo_ref, vo_ref):
        d, mn, vn = _adam_math(w_ref[...], g_ref[...], m_ref[...], v_ref[...])
        d_ref[...] = d
        mo_ref[...] = mn
        vo_ref[...] = vn

    blk = pl.BlockSpec((tr, cdim), lambda i: (i, 0))
    return pl.pallas_call(
        body, name=name, grid=(r // tr,), in_specs=[blk] * 4, out_specs=[blk] * 3,
        out_shape=[_sds((r, cdim), F32)] * 3, compiler_params=_params("parallel"),
    )(w, g, m, v)


def _adamw_small(name, w, g8, m, v):
    r, cdim = w.shape

    def body(w_ref, g_ref, m_ref, v_ref, go_ref, d_ref, mo_ref, vo_ref):
        g = g_ref[0]
        for k in range(1, NDEV):
            g = g + g_ref[k]
        go_ref[...] = g
        d, mn, vn = _adam_math(w_ref[...], g, m_ref[...], v_ref[...])
        d_ref[...] = d
        mo_ref[...] = mn
        vo_ref[...] = vn

    return pl.pallas_call(
        body, name=name, out_shape=[_sds((r, cdim), F32)] * 4,
        compiler_params=pltpu.CompilerParams(vmem_limit_bytes=VMEM_LIMIT),
    )(w, g8, m, v)


_BIG = [("ffn1_w_gu", 704, True), ("ffn1_w_down", 352, False), ("w_in", 768, True),
        ("w_a_out", 128, False), ("w_b_out", 128, False), ("w_out", 128, False),
        ("w_q", 128, False), ("w_kv", 256, True), ("w_o", 128, False),
        ("ffn2_w_gu", 704, True), ("ffn2_w_down", 352, False)]
_BIG_ROWS = sum(r for _, r, _ in _BIG)

_SMALL = [("ffn1_norm", 1), ("mix_norm", 1), ("b_in", 6), ("conv_w", HALO), ("conv_b", 1),
          ("conv_ln_g", 1), ("conv_ln_b", 1), ("sgu_ln_g", 1), ("sgu_ln_b", 1), ("sgu_w", 64),
          ("sgu_b", 1), ("xattn_norm", 1), ("mem_norm", 1), ("ffn2_norm", 1), ("final_norm", 1)]
_SMALL_ROWS = 120


def _pack_small(vals, my_dev):
    rows = []
    for name, nrows in _SMALL:
        a = vals[name].astype(F32)
        if name == "conv_w":
            if a.shape[-1] != D:
                slab = jnp.zeros((HALO, D), F32)
                a = lax.dynamic_update_slice(slab, jnp.pad(a.reshape(CW, -1), ((0, HALO - CW), (0, 0))),
                                             (0, my_dev * (D // NDEV)))
            else:
                a = jnp.pad(a.reshape(CW, D), ((0, HALO - CW), (0, 0)))
        elif name == "sgu_b":
            a = jnp.pad(a.reshape(1, -1), ((0, 0), (0, D - GROUPS * CHUNK)))
        else:
            a = a.reshape(nrows, D)
        rows.append(a)
    packed = jnp.concatenate(rows, axis=0)
    return jnp.pad(packed, ((0, _SMALL_ROWS - packed.shape[0]), (0, 0)))


def _unpack_small(packed, shapes, my_dev):
    out, off = {}, 0
    for name, nrows in _SMALL:
        a = packed[off:off + nrows]
        off += nrows
        if name == "conv_w":
            a = lax.dynamic_slice(a, (0, my_dev * (D // NDEV)), (CW, D // NDEV))
        elif name == "sgu_b":
            a = a[:, :GROUPS * CHUNK]
        out[name] = a.reshape(shapes[name])
    return out


def kernel(x, mem, ffn1_norm, ffn1_w_gu, ffn1_w_down, mix_norm, w_in, b_in, conv_w, conv_b, conv_ln_g, conv_ln_b, w_a_out, sgu_ln_g, sgu_ln_b, sgu_w, sgu_b, w_b_out, w_out, xattn_norm, mem_norm, w_q, w_kv, w_o, ffn2_norm, ffn2_w_gu, ffn2_w_down, final_norm, loss_target, m_ffn1_norm, m_ffn1_w_gu, m_ffn1_w_down, m_mix_norm, m_w_in, m_b_in, m_conv_w, m_conv_b, m_conv_ln_g, m_conv_ln_b, m_w_a_out, m_sgu_ln_g, m_sgu_ln_b, m_sgu_w, m_sgu_b, m_w_b_out, m_w_out, m_xattn_norm, m_mem_norm, m_w_q, m_w_kv, m_w_o, m_ffn2_norm, m_ffn2_w_gu, m_ffn2_w_down, m_final_norm, v_ffn1_norm, v_ffn1_w_gu, v_ffn1_w_down, v_mix_norm, v_w_in, v_b_in, v_conv_w, v_conv_b, v_conv_ln_g, v_conv_ln_b, v_w_a_out, v_sgu_ln_g, v_sgu_ln_b, v_sgu_w, v_sgu_b, v_w_b_out, v_w_out, v_xattn_norm, v_mem_norm, v_w_q, v_w_kv, v_w_o, v_ffn2_norm, v_ffn2_w_gu, v_ffn2_w_down, v_final_norm):
    env = dict(locals())
    names = [n for n, _, _ in _BIG] + [n for n, _ in _SMALL]
    w = {n: env[n] for n in names}
    mom = {n: env["m_" + n] for n in names}
    vel = {n: env["v_" + n] for n in names}

    ax, ay, ac = lax.axis_index("x"), lax.axis_index("y"), lax.axis_index("c")
    my_chip = 2 * ax + ay
    my_dev = 2 * my_chip + ac

    t = x.shape[1]
    tm = min(512, t)
    tm_big = min(1024, t)
    tm_s = min(512, t)
    tm_c = min(256, t)
    xs = x.reshape(t, D)
    tgt = loss_target.reshape(t, D)
    mem2 = mem.reshape(NMEM, D)

    first, mid, late = _BIG[:1], _BIG[1:6], _BIG[6:]

    def gathers(entries):
        return [_gather_comm((w[n][0].T if tr else w[n][0]).astype(BF16)) for n, _, tr in entries]

    def whole(gathered, entries):
        return {n: g.reshape(NDEV * rows, D) for g, (n, rows, _) in zip(gathered, entries)}

    conv_slab = lax.dynamic_update_slice(
        jnp.zeros((HALO, D), F32), jnp.pad(conv_w[0], ((0, HALO - CW), (0, 0))), (0, my_dev * (D // NDEV)))
    bias_full = jnp.repeat(sgu_b[0].T, GD, axis=1)
    b_in2 = b_in.reshape(1, 6 * D)

    (xn1,), (*full_first, conv_w8) = _rms_cast(
        "norm_x", xs, ffn1_norm, tm, hosted=gathers(first) + [_gather_comm(conv_slab)])
    conv_w_pad = jnp.sum(conv_w8, axis=0)
    wf = whole(full_first, first)
    (g1, u1, a1), full_mid = _ffn_up("ffn1_up", xn1, wf["ffn1_w_gu"], tm_big, 1408, hosted=gathers(mid))
    wf.update(whole(full_mid, mid))
    h1, n_mix = _ffn_down("ffn1_down", a1, wf["ffn1_w_down"], xs, mix_norm, tm)
    (p,), full_late = _mix_in("mix_in", n_mix, wf["w_in"], b_in2, tm_big, 1536, hosted=gathers(late))
    wf.update(whole(full_late, late))
    c_conv, act_a = _conv_fwd("conv_fwd", p, conv_w_pad, conv_b, conv_ln_g, conv_ln_b, tm_c)
    act_b = _sgu_fwd("sgu_fwd", p, sgu_ln_g, sgu_ln_b, sgu_w[0], bias_full, tm_s)
    ya, yb, merged, h2, xq = _merge_fwd("merge_fwd", act_a, act_b, p, wf["w_a_out"], wf["w_b_out"],
                                        wf["w_out"], h1, xattn_norm, tm_s)
    memn, kb, vb = _kv_fwd("kv_fwd", mem2, mem_norm, wf["w_kv"])
    qb, ob, h3, xn4 = _attn_fwd("attn_fwd", xq, wf["w_q"], kb, vb, wf["w_o"], h2, ffn2_norm, tm_s)
    (g2, u2, a2), _ = _ffn_up("ffn2_up", xn4, wf["ffn2_w_gu"], tm_big, 1408)
    dh4, dh4b, loss_blk, d_final = _ffn_down_loss("ffn2_down_loss", a2, wf["ffn2_w_down"], h3,
                                                  final_norm.reshape(1, D), tgt, tm)

    gb = {}
    gs = {}
    gs["final_norm"] = d_final

    def ffn_bwd(tag, dhb, dh, g, u, a, xn, h_in, gain, wgu_t, wd, act_hosted=None, dx_hosted=None):
        (dg, du), act_out = _ffn_bwd_act(tag + "_bwd_act", dhb, wd, g, u, tm_big, 1408, hosted=act_hosted)
        gwd = _mm_tn(tag + "_dw_down", a, dhb, 0.5, 1408, TK)
        gwgu = jnp.concatenate([_mm_tn(tag + "_dw_gate", dg, xn, 1.0, 1408, TK),
                                _mm_tn(tag + "_dw_up", du, xn, 1.0, 1408, TK)], axis=0)
        (dh_o, dhb_o, dgain), dx_out = _dx_rms_bwd(
            tag + "_bwd_dx", [(dg, wgu_t, 0, DFF, False), (du, wgu_t, 1, DFF, False)], h_in, gain, dh, tm_s,
            hosted=dx_hosted(gwgu, gwd) if dx_hosted else None)
        return dh_o, dhb_o, dgain, gwgu, gwd, act_out, dx_out

    dh3, dh3b, gs["ffn2_norm"], gb["ffn2_w_gu"], gb["ffn2_w_down"], _, _ = ffn_bwd(
        "ffn2", dh4b, dh4, g2, u2, a2, xn4, h3, ffn2_norm, wf["ffn2_w_gu"], wf["ffn2_w_down"])

    gb["w_o"] = _mm_tn("dw_o", ob, dh3b, 1.0, 1024, 2 * TK)
    dq, dk, dv = _attn_bwd("attn_bwd", dh3b, wf["w_o"], qb, kb, vb, tm_s)
    gb["w_kv"], gs["mem_norm"] = _kv_bwd("kv_bwd", dk, dv, memn, wf["w_kv"], mem2, mem_norm)
    gb["w_q"] = _mm_tn("dw_q", xq, dq, 1.0, 1024, 2 * TK)

    core = ac.astype(jnp.int32).reshape(1)
    chip = my_chip.astype(jnp.int32).reshape(1)
    last_g, mixer_g, attn_g = _BIG[:2], _BIG[2:6], _BIG[6:]

    def pack(entries):
        return jnp.concatenate([gb[n].reshape(4, 2, rows, D) for n, rows, _ in entries], axis=2)

    g4_attn = pack(attn_g)
    (dh2, dh2b, gs["xattn_norm"]), (sib_attn,) = _dx_rms_bwd(
        "attn_bwd_dx", [(dq, wf["w_q"], 0, D, True)], h2, xattn_norm, dh3, tm_s,
        hosted=_pair_exchange_comm(g4_attn))
    part_attn = _pair_sum("grads_pair_sum_attn", g4_attn, sib_attn, core, 784)

    gb["w_out"] = _mm_tn("dw_out", merged, dh2b, 1.0, 1024, 2 * TK)
    dya, dyb, dp_g, cs_g = _merge_bwd("merge_bwd", dh2b, wf["w_out"], ya, yb, p, tm_s)
    gb["w_a_out"] = _mm_tn("dw_a", act_a, dya, 1.0, 1024, 2 * TK)
    gb["w_b_out"] = _mm_tn("dw_b", act_b, dyb, 1.0, 1024, 2 * TK)
    (dp_a, cs_a, d_convw, gs["conv_b"], gs["conv_ln_g"], gs["conv_ln_b"]), (chips_attn,) = _conv_bwd(
        "conv_bwd", dya, wf["w_a_out"], c_conv, p, conv_w_pad, conv_ln_g, conv_ln_b, tm_c,
        hosted=_chip_exchange_comm(part_attn))
    gsum_attn = _chip_sum("grads_chip_sum_attn", part_attn, chips_attn, chip, 784)
    dp_b, cs_b, d_sguw, d_sgub, gs["sgu_ln_g"], gs["sgu_ln_b"] = _sgu_bwd(
        "sgu_bwd", dyb, wf["w_b_out"], p, sgu_ln_g, sgu_ln_b, sgu_w[0], bias_full, tm_s)
    gs["conv_w"] = d_convw[:CW].reshape(1, CW, D)
    gs["sgu_w"] = d_sguw
    gs["sgu_b"] = d_sgub[:, :GROUPS].T
    gs["b_in"] = jnp.concatenate([cs_a, cs_b, cs_g], axis=1)
    gb["w_in"] = jnp.concatenate([_mm_tn("dw_in_a", dp_a, n_mix, 1.0, 1024, 2 * TK),
                                  _mm_tn("dw_in_b", dp_b, n_mix, 1.0, 1024, 2 * TK),
                                  _mm_tn("dw_in_g", dp_g, n_mix, 1.0, 1024, 2 * TK)], axis=0)
    g4_mixer = pack(mixer_g)
    (dh1, dh1b, gs["mix_norm"]), (sib_mixer,) = _dx_rms_bwd(
        "mix_bwd_dx", [(dp_a, wf["w_in"], 0, 2 * D, False), (dp_b, wf["w_in"], 1, 2 * D, False),
                       (dp_g, wf["w_in"], 2, 2 * D, False)], h1, mix_norm, dh2, tm_s,
        hosted=_pair_exchange_comm(g4_mixer))
    part_mixer = _pair_sum("grads_pair_sum_mixer", g4_mixer, sib_mixer, core, 576)

    def ffn1_exchange(gwgu, gwd):
        return _all_to_all_comm(jnp.concatenate(
            [gwgu.reshape(NDEV, -1, D), gwd.reshape(NDEV, -1, D)], axis=1))

    dx, _, gs["ffn1_norm"], _, _, (chips_mixer,), (last_slots,) = ffn_bwd(
        "ffn1", dh1b, dh1, g1, u1, a1, xn1, xs, ffn1_norm, wf["ffn1_w_gu"], wf["ffn1_w_down"],
        act_hosted=_chip_exchange_comm(part_mixer), dx_hosted=ffn1_exchange)
    gsum_mixer = _chip_sum("grads_chip_sum_mixer", part_mixer, chips_mixer, chip, 576)
    gsum_last = _slot_sum("grads_slot_sum_ffn1", last_slots, 528)

    grads, deltas, new_m, new_v = {}, {}, {}, {}
    for entries, gsum in ((last_g, gsum_last), (mixer_g, gsum_mixer), (attn_g, gsum_attn)):
        off = 0
        for name, rows, transposed in entries:
            gsh = gsum[off:off + rows]
            off += rows
            gsh = gsh.T if transposed else gsh
            d, mo, vo = _adamw("adamw_" + name, w[name][0], gsh, mom[name][0], vel[name][0], gsh.shape[0] // 2)
            grads[name], deltas[name], new_m[name], new_v[name] = gsh[None], d[None], mo[None], vo[None]

    shapes = {n: w[n].shape for n, _ in _SMALL}
    (g8,) = _run_comm("gather_small_grads", _gather_comm(_pack_small(gs, my_dev)))
    sg, sd, sm, sv = _adamw_small("adamw_small", _pack_small(w, my_dev), g8,
                                  _pack_small(mom, my_dev), _pack_small(vel, my_dev))
    for dst, src in ((grads, sg), (deltas, sd), (new_m, sm), (new_v, sv)):
        dst.update(_unpack_small(src, shapes, my_dev))

    loss = lax.psum(loss_blk[0, 0], AXES)
    order = ["ffn1_norm", "ffn1_w_gu", "ffn1_w_down", "mix_norm", "w_in", "b_in", "conv_w", "conv_b",
             "conv_ln_g", "conv_ln_b", "w_a_out", "sgu_ln_g", "sgu_ln_b", "sgu_w", "sgu_b", "w_b_out",
             "w_out", "xattn_norm", "mem_norm", "w_q", "w_kv", "w_o", "ffn2_norm", "ffn2_w_gu",
             "ffn2_w_down", "final_norm"]
    return (loss, dx.reshape(x.shape), *[grads[n] for n in order], *[deltas[n] for n in order],
            *[new_m[n] for n in order], *[new_v[n] for n in order])
```

```python
import functools
import math

import jax
import jax.numpy as jnp
from jax import lax
from jax.experimental import pallas as pl
from jax.experimental.pallas import tpu as pltpu

F32 = jnp.float32
BF16 = jnp.bfloat16
MESH = pl.DeviceIdType.MESH
AXES = ("x", "y", "c")

D = 1024
DFF = 2816
NMEM = 256
HEADS = 4
HD = D // HEADS
CW = 31
HALO = 32
CHUNK = 128
GROUPS = 4
GD = D // GROUPS
EPS_RMS = 1e-6
EPS_LN = 1e-5
LR, B1, B2, EPS_ADAM, WD, STEP = 0.001, 0.9, 0.999, 1e-08, 0.01, 10
NDEV = 8
VMEM_LIMIT = 56 * 1024 * 1024
TK = 2048


def _params(*sem):
    return pltpu.CompilerParams(dimension_semantics=sem, vmem_limit_bytes=VMEM_LIMIT)


def _dot(a, b):
    return jnp.dot(a, b, preferred_element_type=F32)


def _dot_nt(a, b):
    return lax.dot_general(a, b, (((1,), (1,)), ((), ())), preferred_element_type=F32)


def _dot_tn(a, b):
    return lax.dot_general(a, b, (((0,), (0,)), ((), ())), preferred_element_type=F32)


def _sigmoid(x):
    return 0.5 * jnp.tanh(0.5 * x) + 0.5


_GELU_C = math.sqrt(2.0 / math.pi)


def _gelu_with_grad(x):
    x2 = x * x
    t = jnp.tanh(_GELU_C * (x + 0.044715 * (x2 * x)))
    half = 0.5 * (1.0 + t)
    return x * half, half + 0.5 * x * (1.0 - t * t) * (_GELU_C * (1.0 + 3.0 * 0.044715 * x2))


def _gelu(x):
    return _gelu_with_grad(x)[0]


def _rms_stats(h):
    r = lax.rsqrt(jnp.mean(h * h, axis=-1, keepdims=True) + EPS_RMS)
    return r, h * r


def _rms_bwd(dxn, h, gain):
    r, xh = _rms_stats(h)
    dgain = jnp.sum(dxn * xh, axis=0, keepdims=True)
    dxh = dxn * gain
    dh = r * (dxh - xh * jnp.mean(dxh * xh, axis=-1, keepdims=True))
    return dh, dgain


def _ln_stats(c):
    mu = jnp.mean(c, axis=-1, keepdims=True)
    xc = c - mu
    rstd = lax.rsqrt(jnp.mean(xc * xc, axis=-1, keepdims=True) + EPS_LN)
    return rstd, xc * rstd


def _ln_bwd(dy, xhat, rstd, g):
    dxh = dy * g
    return rstd * (dxh - jnp.mean(dxh, axis=-1, keepdims=True)
                   - xhat * jnp.mean(dxh * xhat, axis=-1, keepdims=True))


def _row(tm, cols, cb=0):
    return pl.BlockSpec((tm, cols), lambda i, _cb=cb: (i, _cb))


def _const(shape):
    n = len(shape)
    return pl.BlockSpec(shape, lambda *_: (0,) * n)


def _sds(shape, dtype):
    return jax.ShapeDtypeStruct(shape, dtype)


def _acc(ref, first, val):
    @pl.when(first)
    def _():
        ref[...] = jnp.zeros_like(ref)
    ref[...] += val


class _Comm:
    def __init__(self, args, out_shapes, scratch, start, finish):
        self.args, self.out_shapes, self.scratch = args, out_shapes, scratch
        self.start, self.finish = start, finish


_ANY = pl.BlockSpec(memory_space=pl.ANY)


def _run_comm(name, comm):
    ni, no = len(comm.args), len(comm.out_shapes)

    def body(*refs):
        ins, outs, sems = refs[:ni], refs[ni:ni + no], refs[ni + no:]
        comm.start(ins, outs, sems)
        comm.finish(ins, outs, sems)

    return pl.pallas_call(
        body, name=name, out_shape=list(comm.out_shapes), in_specs=[_ANY] * ni, out_specs=[_ANY] * no,
        scratch_shapes=list(comm.scratch),
    )(*comm.args)


def _call(name, body, grid, in_specs, out_specs, out_shape, args, scratch=(), sem=None, hosted=None):
    n_in, n_out, n_scr = len(in_specs), len(out_specs), len(scratch)
    if not hosted:
        outs = pl.pallas_call(
            body, name=name, grid=grid, in_specs=list(in_specs), out_specs=list(out_specs),
            out_shape=list(out_shape), scratch_shapes=list(scratch), compiler_params=_params(*sem),
        )(*args)
        return outs, []
    comms = list(hosted) if isinstance(hosted, (list, tuple)) else [hosted]
    hi = sum(len(cm.args) for cm in comms)
    ho = sum(len(cm.out_shapes) for cm in comms)

    def wrapped(*refs):
        ins, h_in = refs[:n_in], refs[n_in:n_in + hi]
        o0 = n_in + hi
        outs, h_out = refs[o0:o0 + n_out], refs[o0 + n_out:o0 + n_out + ho]
        s0 = o0 + n_out + ho
        scr, h_sems = refs[s0:s0 + n_scr], refs[s0 + n_scr:]
        ids = [pl.program_id(a) for a in range(len(grid))]
        first = functools.reduce(jnp.logical_and, [i == 0 for i in ids])
        last = functools.reduce(jnp.logical_and, [i == g - 1 for i, g in zip(ids, grid)])
        parts, a0, b0, c0 = [], 0, 0, 0
        for cm in comms:
            na, nb, nc = len(cm.args), len(cm.out_shapes), len(cm.scratch)
            parts.append((cm, h_in[a0:a0 + na], h_out[b0:b0 + nb], h_sems[c0:c0 + nc]))
            a0, b0, c0 = a0 + na, b0 + nb, c0 + nc

        @pl.when(first)
        def _():
            for cm, ci, co, cs in parts:
                cm.start(ci, co, cs)

        body(*ins, *outs, *scr)

        @pl.when(last)
        def _():
            for cm, ci, co, cs in parts:
                cm.finish(ci, co, cs)

    res = pl.pallas_call(
        wrapped, name=name, grid=grid, in_specs=list(in_specs) + [_ANY] * hi,
        out_specs=list(out_specs) + [_ANY] * ho,
        out_shape=list(out_shape) + [s for cm in comms for s in cm.out_shapes],
        scratch_shapes=list(scratch) + [s for cm in comms for s in cm.scratch],
        compiler_params=_params(*(["arbitrary"] * len(grid))),
    )(*args, *[a for cm in comms for a in cm.args])
    return res[:n_out], res[n_out:]


def _gather_comm(blk):
    r, cdim = blk.shape

    def copies(x_ref, out_ref, send_sems, recv_sems, local_sem):
        x, y, c = lax.axis_index("x"), lax.axis_index("y"), lax.axis_index("c")
        me, sibling = (x, y, c), (x, y, 1 - c)
        chips = [(1 - x, y), (x, 1 - y), (1 - x, 1 - y)]

        def slot(px, py, pc):
            return out_ref.at[4 * px + 2 * py + pc]

        def copy(k, block, to, src=None):
            return pltpu.make_async_remote_copy(
                src_ref=slot(*block) if src is None else src, dst_ref=slot(*block),
                send_sem=send_sems.at[k], recv_sem=recv_sems.at[k],
                device_id=to, device_id_type=MESH)

        mine = pltpu.make_async_copy(x_ref, slot(*me), local_sem)
        first = [copy(0, me, sibling, src=x_ref)]
        first += [copy(1 + j, me, (*chip, c), src=x_ref) for j, chip in enumerate(chips)]
        passed = [copy(4 + j, (*chip, c), sibling) for j, chip in enumerate(chips)]
        landed = [copy(1 + j, (*chip, c), me) for j, chip in enumerate(chips)]
        from_sibling = [copy(0, sibling, me)] + [copy(4 + j, (*chip, 1 - c), me) for j, chip in enumerate(chips)]
        return mine, first, passed, landed, from_sibling

    def start(ins, outs, sems):
        mine, first, _, _, _ = copies(ins[0], outs[0], *sems)
        mine.start()
        for cp in first:
            cp.start()

    def finish(ins, outs, sems):
        mine, first, passed, landed, from_sibling = copies(ins[0], outs[0], *sems)
        for arrived, forward in zip(landed, passed):
            arrived.wait_recv()
            forward.start()
        for cp in from_sibling:
            cp.wait_recv()
        for cp in first + passed:
            cp.wait_send()
        mine.wait()

    return _Comm([blk], [_sds((NDEV, r, cdim), blk.dtype)],
                 [pltpu.SemaphoreType.DMA((7,)), pltpu.SemaphoreType.DMA((7,)), pltpu.SemaphoreType.DMA],
                 start, finish)


def _exchange_comm(src, n, out_rows, make):
    r, cdim = out_rows

    def copies(src_ref, out_ref, send_sems, recv_sems):
        out = []
        for k in range(n):
            s, d, to = make(k, src_ref, out_ref)
            out.append(pltpu.make_async_remote_copy(
                src_ref=s, dst_ref=d, send_sem=send_sems.at[k], recv_sem=recv_sems.at[k],
                device_id=to, device_id_type=MESH))
        return out

    def start(ins, outs, sems):
        for cp in copies(ins[0], outs[0], *sems):
            cp.start()

    def finish(ins, outs, sems):
        cps = copies(ins[0], outs[0], *sems)
        for cp in cps:
            cp.wait_recv()
        for cp in cps:
            cp.wait_send()

    return _Comm([src], [_sds((n, r, cdim), src.dtype)],
                 [pltpu.SemaphoreType.DMA((n,)), pltpu.SemaphoreType.DMA((n,))], start, finish)


def _pair_exchange_comm(g4):
    def make(k, g_ref, out_ref):
        x, y, c = lax.axis_index("x"), lax.axis_index("y"), lax.axis_index("c")
        return g_ref.at[k, 1 - c], out_ref.at[k], (x, y, 1 - c)

    return _exchange_comm(g4, 4, g4.shape[2:], make)


def _chip_exchange_comm(part):
    def make(k, p_ref, out_ref):
        x, y, c = lax.axis_index("x"), lax.axis_index("y"), lax.axis_index("c")
        px = x if ((k + 1) >> 1) == 0 else 1 - x
        py = y if ((k + 1) & 1) == 0 else 1 - y
        return p_ref.at[2 * px + py], out_ref.at[k], (px, py, c)

    return _exchange_comm(part, 3, part.shape[1:], make)


def _all_to_all_comm(g8):
    _, r, cdim = g8.shape

    def copies(g_ref, out_ref, send_sems, recv_sems, local_sem):
        x, y, c = lax.axis_index("x"), lax.axis_index("y"), lax.axis_index("c")
        me = 4 * x + 2 * y + c
        remote = []
        for k in range(1, NDEV):
            px = 1 - x if k & 4 else x
            py = 1 - y if k & 2 else y
            pc = 1 - c if k & 1 else c
            remote.append(pltpu.make_async_remote_copy(
                src_ref=g_ref.at[4 * px + 2 * py + pc], dst_ref=out_ref.at[me],
                send_sem=send_sems.at[k - 1], recv_sem=recv_sems.at[k - 1],
                device_id=(px, py, pc), device_id_type=MESH))
        return pltpu.make_async_copy(g_ref.at[me], out_ref.at[me], local_sem), remote

    def start(ins, outs, sems):
        mine, remote = copies(ins[0], outs[0], *sems)
        mine.start()
        for cp in remote:
            cp.start()

    def finish(ins, outs, sems):
        mine, remote = copies(ins[0], outs[0], *sems)
        for cp in remote:
            cp.wait_recv()
        for cp in remote:
            cp.wait_send()
        mine.wait()

    return _Comm([g8], [_sds((NDEV, r, cdim), g8.dtype)],
                 [pltpu.SemaphoreType.DMA((NDEV - 1,)), pltpu.SemaphoreType.DMA((NDEV - 1,)),
                  pltpu.SemaphoreType.DMA], start, finish)


def _slot_sum(name, slots, tr):
    _, r, cdim = slots.shape

    def body(s_ref, o_ref):
        s = s_ref[0].astype(F32)
        for k in range(1, NDEV):
            s = s + s_ref[k].astype(F32)
        o_ref[...] = s

    return pl.pallas_call(
        body, name=name, grid=(r // tr,),
        in_specs=[pl.BlockSpec((NDEV, tr, cdim), lambda i: (0, i, 0))],
        out_specs=pl.BlockSpec((tr, cdim), lambda i: (i, 0)),
        out_shape=_sds((r, cdim), F32), compiler_params=_params("parallel"),
    )(slots)


def _pair_sum(name, g4, recv, core, tr):
    _, _, r, cdim = g4.shape

    def body(core_ref, a_ref, b_ref, o_ref):
        o_ref[...] = (a_ref[...].astype(F32) + b_ref[...].astype(F32)).astype(o_ref.dtype)

    return pl.pallas_call(
        body, name=name,
        grid_spec=pltpu.PrefetchScalarGridSpec(
            num_scalar_prefetch=1, grid=(4, r // tr),
            in_specs=[pl.BlockSpec((None, None, tr, cdim), lambda k, i, cr: (k, cr[0], i, 0)),
                      pl.BlockSpec((None, tr, cdim), lambda k, i, cr: (k, i, 0))],
            out_specs=pl.BlockSpec((None, tr, cdim), lambda k, i, cr: (k, i, 0))),
        out_shape=_sds((4, r, cdim), BF16),
        compiler_params=_params("parallel", "parallel"),
    )(core, g4, recv)


def _chip_sum(name, part, recv, chip, tr):
    _, r, cdim = part.shape

    def body(chip_ref, a_ref, b_ref, o_ref):
        s = a_ref[...].astype(F32)
        for k in range(3):
            s = s + b_ref[k].astype(F32)
        o_ref[...] = s

    return pl.pallas_call(
        body, name=name,
        grid_spec=pltpu.PrefetchScalarGridSpec(
            num_scalar_prefetch=1, grid=(r // tr,),
            in_specs=[pl.BlockSpec((None, tr, cdim), lambda i, cr: (cr[0], i, 0)),
                      pl.BlockSpec((3, tr, cdim), lambda i, cr: (0, i, 0))],
            out_specs=pl.BlockSpec((tr, cdim), lambda i, cr: (i, 0))),
        out_shape=_sds((r, cdim), F32),
        compiler_params=_params("parallel"),
    )(chip, part, recv)


def _rms_cast(name, h, gain, tm, hosted=None):
    t = h.shape[0]

    def body(h_ref, g_ref, o_ref):
        _, xh = _rms_stats(h_ref[...])
        o_ref[...] = (xh * g_ref[...]).astype(BF16)

    return _call(name, body, (t // tm,), [_row(tm, D), _const((1, D))], [_row(tm, D)],
                 [_sds((t, D), BF16)], (h, gain), sem=("parallel",), hosted=hosted)


def _ffn_up(name, xn, wgu_t, tm, tn, hosted=None):
    t = xn.shape[0]
    nh = DFF // tn

    def body(x_ref, wg_ref, wu_ref, g_ref, u_ref, a_ref):
        x = x_ref[...]
        g = _dot_nt(x, wg_ref[...])
        u = _dot_nt(x, wu_ref[...])
        g_ref[...] = g.astype(BF16)
        u_ref[...] = u.astype(BF16)
        a_ref[...] = (g * _sigmoid(g) * u).astype(BF16)

    o = pl.BlockSpec((tm, tn), lambda j, i: (i, j))
    return _call(name, body, (nh, t // tm),
                 [pl.BlockSpec((tm, D), lambda j, i: (i, 0)),
                  pl.BlockSpec((tn, D), lambda j, i: (j, 0)),
                  pl.BlockSpec((tn, D), lambda j, i: (j + nh, 0))],
                 [o, o, o], [_sds((t, DFF), BF16)] * 3, (xn, wgu_t, wgu_t),
                 sem=("parallel", "parallel"), hosted=hosted)


def _ffn_down(name, a, wd, h, gain, tm):
    t = a.shape[0]

    def body(a_ref, w_ref, h_ref, g_ref, o_ref, n_ref):
        hn = h_ref[...] + 0.5 * _dot(a_ref[...], w_ref[...])
        o_ref[...] = hn
        _, xh = _rms_stats(hn)
        n_ref[...] = (xh * g_ref[...]).astype(BF16)

    return pl.pallas_call(
        body, name=name, grid=(t // tm,),
        in_specs=[_row(tm, DFF), _const((DFF, D)), _row(tm, D), _const((1, D))],
        out_specs=[_row(tm, D), _row(tm, D)],
        out_shape=[_sds((t, D), F32), _sds((t, D), BF16)],
        compiler_params=_params("parallel"),
    )(a, wd, h, gain)


def _mix_in(name, n, win_t, b_in, tm, tn, hosted=None):
    t = n.shape[0]

    def body(n_ref, w_ref, b_ref, p_ref):
        p_ref[...] = (_dot_nt(n_ref[...], w_ref[...]) + b_ref[...]).astype(BF16)

    return _call(name, body, (6 * D // tn, t // tm),
                 [pl.BlockSpec((tm, D), lambda j, i: (i, 0)),
                  pl.BlockSpec((tn, D), lambda j, i: (j, 0)),
                  pl.BlockSpec((1, tn), lambda j, i: (0, j))],
                 [pl.BlockSpec((tm, tn), lambda j, i: (i, j))], [_sds((t, 6 * D), BF16)],
                 (n, win_t, b_in), sem=("parallel", "parallel"), hosted=hosted)


RC = 64
LANES = 128


def _shift_copies(ext, shifted, tm):
    n = tm + HALO - 8
    for m in range(1, 8):
        shifted[m - 1] = ext[pl.ds(m, n), :]


def _by_residue(offs):
    groups = {}
    for k, off in enumerate(offs):
        q, m = divmod(off, 8)
        groups.setdefault(m, []).append((k, q))
    return groups


def _residue_window(ext, shifted, m, taps, base, cs):
    src = ext if m == 0 else shifted.at[m - 1]
    return src[pl.ds(base, RC + 8 * max(q for _, q in taps)), cs]


def _tap_sum(out_ref, bias_ref, w_ref, ext, shifted, offs, tm):
    groups = _by_residue(offs)

    def chunk(j, carry):
        base = pl.multiple_of(j * RC, RC)
        for c in range(D // LANES):
            cs = pl.ds(c * LANES, LANES)
            acc = jnp.zeros((RC, LANES), F32)
            if bias_ref is not None:
                acc = acc + bias_ref[:, cs]
            for m, taps in groups.items():
                big = _residue_window(ext, shifted, m, taps, base, cs)
                for k, q in taps:
                    acc = acc + w_ref[pl.ds(k, 1), cs] * big[8 * q:8 * q + RC]
            out_ref[pl.ds(base, RC), cs] = acc
        return carry

    lax.fori_loop(0, tm // RC, chunk, 0)


def _tap_corr(dw_ref, dc_ext, ext, shifted, offs, tm):
    groups = _by_residue(offs)
    for c in range(D // LANES):
        cs = pl.ds(c * LANES, LANES)

        def chunk(j, accs, cs=cs):
            base = pl.multiple_of(j * RC, RC)
            dcv = dc_ext[pl.ds(base, RC), cs]
            out = list(accs)
            for m, taps in groups.items():
                big = _residue_window(ext, shifted, m, taps, base, cs)
                for k, q in taps:
                    prod = dcv * big[8 * q:8 * q + RC]
                    part = prod[0:8]
                    for s in range(1, RC // 8):
                        part = part + prod[8 * s:8 * s + 8]
                    out[k] = accs[k] + part
            return tuple(out)

        accs = lax.fori_loop(0, tm // RC, chunk, tuple(jnp.zeros((8, LANES), F32) for _ in offs))
        for k in range(len(offs)):
            dw_ref[pl.ds(k, 1), cs] += jnp.sum(accs[k], axis=0, keepdims=True)


def _conv_fwd(name, p, conv_w, conv_b, ln_g, ln_b, tm):
    t = p.shape[0]

    def body(av_ref, ag_ref, w_ref, cb_ref, lg_ref, lb_ref, c_ref, a_ref, ext, shifted):
        i = pl.program_id(0)

        @pl.when(i == 0)
        def _():
            ext[pl.ds(0, HALO), :] = jnp.zeros((HALO, D), F32)

        ext[pl.ds(HALO, tm), :] = av_ref[...].astype(F32) * _sigmoid(ag_ref[...].astype(F32))
        _shift_copies(ext, shifted, tm)
        _tap_sum(c_ref, cb_ref, w_ref, ext, shifted, [HALO - (CW - 1) + k for k in range(CW)], tm)
        rstd, chat = _ln_stats(c_ref[...])
        ca = chat * lg_ref[...] + lb_ref[...]
        a_ref[...] = (ca * _sigmoid(ca)).astype(BF16)
        ext[pl.ds(0, HALO), :] = ext[pl.ds(tm, HALO), :]

    return pl.pallas_call(
        body, name=name, grid=(t // tm,),
        in_specs=[_row(tm, D, 0), _row(tm, D, 1), _const((HALO, D)), _const((1, D)),
                  _const((1, D)), _const((1, D))],
        out_specs=[_row(tm, D), _row(tm, D)],
        out_shape=[_sds((t, D), F32), _sds((t, D), BF16)],
        scratch_shapes=[pltpu.VMEM((tm + HALO, D), F32), pltpu.VMEM((7, tm + HALO - 8, D), F32)],
        compiler_params=_params("arbitrary"),
    )(p, p, conv_w, conv_b, ln_g, ln_b)


def _sgu_mask():
    rows = lax.broadcasted_iota(jnp.int32, (CHUNK, CHUNK), 0)
    cols = lax.broadcasted_iota(jnp.int32, (CHUNK, CHUNK), 1)
    return cols <= rows


def _sgu_fwd(name, p, ln_g, ln_b, sgu_w, bias_full, tm):
    t = p.shape[0]

    def body(bu_ref, bv_ref, lg_ref, lb_ref, ws_ref, bias_ref, o_ref):
        mask = _sgu_mask()
        _, vhat = _ln_stats(_gelu(bv_ref[...].astype(F32)))
        vn = (vhat * lg_ref[...] + lb_ref[...]).astype(BF16)
        ub = _gelu(bu_ref[...].astype(F32))
        for g in range(GROUPS):
            wm = jnp.where(mask, ws_ref[g], 0.0).astype(BF16)
            cs = slice(g * GD, (g + 1) * GD)
            for cc in range(tm // CHUNK):
                rs = slice(cc * CHUNK, (cc + 1) * CHUNK)
                mixed = _dot(wm, vn[rs, cs]) + bias_ref[:, cs]
                o_ref[rs, cs] = (ub[rs, cs] * mixed).astype(BF16)

    return pl.pallas_call(
        body, name=name, grid=(t // tm,),
        in_specs=[_row(tm, D, 2), _row(tm, D, 3), _const((1, D)), _const((1, D)),
                  _const((GROUPS, CHUNK, CHUNK)), _const((CHUNK, D))],
        out_specs=_row(tm, D), out_shape=_sds((t, D), BF16),
        compiler_params=_params("parallel"),
    )(p, p, ln_g, ln_b, sgu_w, bias_full)


def _merge_fwd(name, act_a, act_b, p, w_a, w_b, w_out, h, gain, tm):
    t = h.shape[0]

    def body(a_ref, b_ref, ga_ref, gb_ref, wa_ref, wb_ref, wo_ref, h_ref, g_ref,
             ya_ref, yb_ref, mg_ref, ho_ref, xn_ref):
        ya = _dot(a_ref[...], wa_ref[...])
        yb = _dot(b_ref[...], wb_ref[...])
        ya_ref[...] = ya.astype(BF16)
        yb_ref[...] = yb.astype(BF16)
        merged = (_sigmoid(ga_ref[...].astype(F32)) * ya
                  + _sigmoid(gb_ref[...].astype(F32)) * yb).astype(BF16)
        mg_ref[...] = merged
        hn = h_ref[...] + _dot(merged, wo_ref[...])
        ho_ref[...] = hn
        _, xh = _rms_stats(hn)
        xn_ref[...] = (xh * g_ref[...]).astype(BF16)

    rb = _row(tm, D)
    return pl.pallas_call(
        body, name=name, grid=(t // tm,),
        in_specs=[rb, rb, _row(tm, D, 4), _row(tm, D, 5), _const((D, D)), _const((D, D)),
                  _const((D, D)), rb, _const((1, D))],
        out_specs=[rb] * 5,
        out_shape=[_sds((t, D), BF16)] * 3 + [_sds((t, D), F32), _sds((t, D), BF16)],
        compiler_params=_params("parallel"),
    )(act_a, act_b, p, p, w_a, w_b, w_out, h, gain)


def _kv_fwd(name, mem, gain, wkv_t):
    def body(m_ref, g_ref, w_ref, mn_ref, k_ref, v_ref):
        _, xh = _rms_stats(m_ref[...])
        mn = (xh * g_ref[...]).astype(BF16)
        mn_ref[...] = mn
        kv = _dot_nt(mn, w_ref[...])
        k_ref[...] = kv[:, :D].astype(BF16)
        v_ref[...] = kv[:, D:].astype(BF16)

    return pl.pallas_call(
        body, name=name,
        out_shape=[_sds((NMEM, D), BF16)] * 3,
        compiler_params=pltpu.CompilerParams(vmem_limit_bytes=VMEM_LIMIT),
    )(mem, gain, wkv_t)


def _softmax_rows(s):
    e = jnp.exp(s - jnp.max(s, axis=-1, keepdims=True))
    return e / jnp.sum(e, axis=-1, keepdims=True)


def _attn_fwd(name, xq, w_q, kb, vb, w_o, h, gain, tm):
    t = h.shape[0]
    scale = 1.0 / math.sqrt(HD)

    def body(x_ref, wq_ref, k_ref, v_ref, wo_ref, h_ref, g_ref, q_ref, o_ref, ho_ref, xn_ref):
        q_ref[...] = _dot(x_ref[...], wq_ref[...]).astype(BF16)
        for hd in range(HEADS):
            cs = slice(hd * HD, (hd + 1) * HD)
            p = _softmax_rows(_dot_nt(q_ref[:, cs], k_ref[:, cs]) * scale)
            o_ref[:, cs] = _dot(p.astype(BF16), v_ref[:, cs]).astype(BF16)
        hn = h_ref[...] + _dot(o_ref[...], wo_ref[...])
        ho_ref[...] = hn
        _, xh = _rms_stats(hn)
        xn_ref[...] = (xh * g_ref[...]).astype(BF16)

    rb = _row(tm, D)
    return pl.pallas_call(
        body, name=name, grid=(t // tm,),
        in_specs=[rb, _const((D, D)), _const((NMEM, D)), _const((NMEM, D)), _const((D, D)), rb,
                  _const((1, D))],
        out_specs=[rb] * 4,
        out_shape=[_sds((t, D), BF16), _sds((t, D), BF16), _sds((t, D), F32), _sds((t, D), BF16)],
        compiler_params=_params("parallel"),
    )(xq, w_q, kb, vb, w_o, h, gain)


def _ffn_down_loss(name, a, wd, h, gain, target, tm):
    t = h.shape[0]
    steps = t // tm

    def body(a_ref, w_ref, h_ref, g_ref, t_ref, dh_ref, dhb_ref, loss_ref, dg_ref, lacc):
        i = pl.program_id(0)
        hv = h_ref[...] + 0.5 * _dot(a_ref[...], w_ref[...])
        r, xh = _rms_stats(hv)
        err = xh * g_ref[...] - t_ref[...]
        _acc(lacc, i == 0, jnp.sum(err * err, axis=0, keepdims=True))
        dy = err * (1.0 / D)
        _acc(dg_ref, i == 0, jnp.sum(dy * xh, axis=0, keepdims=True))
        dxh = dy * g_ref[...]
        dh = r * (dxh - xh * jnp.mean(dxh * xh, axis=-1, keepdims=True))
        dh_ref[...] = dh
        dhb_ref[...] = dh.astype(BF16)

        @pl.when(i == steps - 1)
        def _():
            loss_ref[...] = jnp.zeros((8, 128), F32) + (0.5 / D) * jnp.sum(lacc[...])

    rb = _row(tm, D)
    return pl.pallas_call(
        body, name=name, grid=(steps,),
        in_specs=[_row(tm, DFF), _const((DFF, D)), rb, _const((1, D)), rb],
        out_specs=[rb, rb, _const((8, 128)), _const((1, D))],
        out_shape=[_sds((t, D), F32), _sds((t, D), BF16), _sds((8, 128), F32), _sds((1, D), F32)],
        scratch_shapes=[pltpu.VMEM((1, D), F32)],
        compiler_params=_params("arbitrary"),
    )(a, wd, h, gain, target)


def _ffn_bwd_act(name, dhb, wd, g, u, tm, tn, hosted=None):
    t = dhb.shape[0]

    def body(d_ref, w_ref, g_ref, u_ref, dg_ref, du_ref):
        da = 0.5 * _dot_nt(d_ref[...], w_ref[...])
        gv = g_ref[...].astype(F32)
        sg = _sigmoid(gv)
        dg_ref[...] = (da * u_ref[...].astype(F32) * (sg * (1.0 + gv * (1.0 - sg)))).astype(BF16)
        du_ref[...] = (da * (gv * sg)).astype(BF16)

    o = pl.BlockSpec((tm, tn), lambda j, i: (i, j))
    return _call(name, body, (DFF // tn, t // tm),
                 [pl.BlockSpec((tm, D), lambda j, i: (i, 0)),
                  pl.BlockSpec((tn, D), lambda j, i: (j, 0)), o, o],
                 [o, o], [_sds((t, DFF), BF16)] * 2, (dhb, wd, g, u),
                 sem=("parallel", "parallel"), hosted=hosted)


def _dx_rms_bwd(name, pairs, h, gain, dh_in, tm, hosted=None):
    t = h.shape[0]
    np_ = len(pairs)

    def body(*refs):
        a_refs = refs[:np_]
        b_refs = refs[np_:2 * np_]
        h_ref, g_ref, d_ref, o_ref, ob_ref, dg_ref = refs[2 * np_:]
        dxn = None
        for (a_ref, b_ref, pr) in zip(a_refs, b_refs, pairs):
            y = _dot_nt(a_ref[...], b_ref[...]) if pr[4] else _dot(a_ref[...], b_ref[...])
            dxn = y if dxn is None else dxn + y
        dh, dgain = _rms_bwd(dxn, h_ref[...], g_ref[...])
        _acc(dg_ref, pl.program_id(0) == 0, dgain)
        out = d_ref[...] + dh
        o_ref[...] = out
        ob_ref[...] = out.astype(BF16)

    ins, args = [], []
    for (a, b, blk, rows, tr) in pairs:
        ins.append(_row(tm, a.shape[1]))
        args.append(a)
    for (a, b, blk, rows, tr) in pairs:
        ins.append(pl.BlockSpec((rows, b.shape[1]), lambda i, _b=blk: (_b, 0)))
        args.append(b)
    rb = _row(tm, D)
    ins += [rb, _const((1, D)), rb]
    args += [h, gain, dh_in]
    return _call(name, body, (t // tm,), ins, [rb, rb, _const((1, D))],
                 [_sds((t, D), F32), _sds((t, D), BF16), _sds((1, D), F32)], args,
                 sem=("arbitrary",), hosted=hosted)


class _Dest:
    def __init__(self, buf, total_rows, off, rows, row0=0):
        self.buf, self.total_rows, self.off, self.rows, self.row0 = buf, total_rows, off, rows, row0

    def segments(self, lo, hi):
        out = []
        for d in range(NDEV):
            a, b = max(lo + self.row0, d * self.rows), min(hi + self.row0, (d + 1) * self.rows)
            if a < b:
                out.append((a - self.row0 - lo, d, self.off + a - d * self.rows, b - a))
        return out


def _store_segments(stage, buf_ref, sems, segs):
    cps = [pltpu.make_async_copy(stage.at[pl.ds(s0, n)], buf_ref.at[d, pl.ds(r0, n)], sems.at[j])
           for j, (s0, d, r0, n) in enumerate(segs)]
    for cp in cps:
        cp.start()
    for cp in cps:
        cp.wait()


def _mm_tn(name, a, b, scale, tmo, tk, dest):
    t, m = a.shape
    n = b.shape[1]
    tk = min(tk, t)
    steps = t // tk
    tiles = m // tmo
    seg_lists = [dest.segments(i * tmo, (i + 1) * tmo) for i in range(tiles)]
    fresh = dest.buf is None

    def body(*refs):
        a_ref, b_ref = refs[0], refs[1]
        buf_ref, acc, stage, sems = refs[-4:]
        i, k = pl.program_id(0), pl.program_id(1)
        _acc(acc, k == 0, _dot_tn(a_ref[...], b_ref[...]))

        @pl.when(k == steps - 1)
        def _():
            stage[...] = (acc[...] * scale).astype(BF16)
            for ti, segs in enumerate(seg_lists):
                @pl.when(i == ti)
                def _(segs=segs):
                    _store_segments(stage, buf_ref, sems, segs)

    ins = [pl.BlockSpec((tk, tmo), lambda i, k: (k, i)), pl.BlockSpec((tk, n), lambda i, k: (k, 0))]
    args = [a, b]
    if not fresh:
        ins.append(_ANY)
        args.append(dest.buf)
    return pl.pallas_call(
        body, name=name, grid=(tiles, steps), in_specs=ins, out_specs=_ANY,
        out_shape=_sds((NDEV, dest.total_rows, n), BF16),
        input_output_aliases={} if fresh else {2: 0},
        scratch_shapes=[pltpu.VMEM((tmo, n), F32), pltpu.VMEM((tmo, n), BF16),
                        pltpu.SemaphoreType.DMA((max(len(s) for s in seg_lists),))],
        compiler_params=_params("arbitrary", "arbitrary"),
    )(*args)


def _attn_bwd(name, dhb, w_o, qb, kb, vb, tm):
    t = dhb.shape[0]
    scale = 1.0 / math.sqrt(HD)

    def body(d_ref, wo_ref, q_ref, k_ref, v_ref, dq_ref, dk_ref, dv_ref, do_s):
        i = pl.program_id(0)

        @pl.when(i == 0)
        def _():
            dk_ref[...] = jnp.zeros_like(dk_ref)
            dv_ref[...] = jnp.zeros_like(dv_ref)

        do_s[...] = _dot_nt(d_ref[...], wo_ref[...]).astype(BF16)
        for hd in range(HEADS):
            cs = slice(hd * HD, (hd + 1) * HD)
            q = q_ref[:, cs]
            p = _softmax_rows(_dot_nt(q, k_ref[:, cs]) * scale)
            do = do_s[:, cs]
            dp = _dot_nt(do, v_ref[:, cs])
            ds = (p * (dp - jnp.sum(dp * p, axis=-1, keepdims=True)) * scale).astype(BF16)
            dq_ref[:, cs] = _dot(ds, k_ref[:, cs]).astype(BF16)
            dk_ref[:, cs] += _dot_tn(ds, q)
            dv_ref[:, cs] += _dot_tn(p.astype(BF16), do)

    rb = _row(tm, D)
    return pl.pallas_call(
        body, name=name, grid=(t // tm,),
        in_specs=[rb, _const((D, D)), rb, _const((NMEM, D)), _const((NMEM, D))],
        out_specs=[rb, _const((NMEM, D)), _const((NMEM, D))],
        out_shape=[_sds((t, D), BF16), _sds((NMEM, D), F32), _sds((NMEM, D), F32)],
        scratch_shapes=[pltpu.VMEM((tm, D), BF16)],
        compiler_params=_params("arbitrary"),
    )(dhb, w_o, qb, kb, vb)


def _kv_bwd(name, dk, dv, memn, wkv_t, mem, gain, dest):
    segs = dest.segments(0, 2 * D)
    vmem = pl.BlockSpec(memory_space=pltpu.VMEM)

    def body(dk_ref, dv_ref, mn_ref, w_ref, m_ref, g_ref, buf_in, buf_ref, dg_ref, stage, sems):
        dkb = dk_ref[...].astype(BF16)
        dvb = dv_ref[...].astype(BF16)
        mn = mn_ref[...]
        stage[pl.ds(0, D), :] = _dot_tn(dkb, mn).astype(BF16)
        stage[pl.ds(D, D), :] = _dot_tn(dvb, mn).astype(BF16)
        _store_segments(stage, buf_ref, sems, segs)
        dmn = _dot(dkb, w_ref[pl.ds(0, D), :]) + _dot(dvb, w_ref[pl.ds(D, D), :])
        _, xh = _rms_stats(m_ref[...])
        dg_ref[...] = jnp.sum(dmn * xh, axis=0, keepdims=True)

    return pl.pallas_call(
        body, name=name,
        in_specs=[vmem] * 6 + [_ANY], out_specs=[_ANY, vmem],
        out_shape=[_sds(dest.buf.shape, BF16), _sds((1, D), F32)],
        input_output_aliases={6: 0},
        scratch_shapes=[pltpu.VMEM((2 * D, D), BF16), pltpu.SemaphoreType.DMA((len(segs),))],
        compiler_params=pltpu.CompilerParams(vmem_limit_bytes=VMEM_LIMIT),
    )(dk, dv, memn, wkv_t, mem, gain, dest.buf)


def _merge_bwd(name, dhb, w_out, ya, yb, p, tm):
    t = dhb.shape[0]

    def body(d_ref, w_ref, ya_ref, yb_ref, ga_ref, gb_ref, dya_ref, dyb_ref, dp_ref, cs_ref):
        dm = _dot_nt(d_ref[...], w_ref[...])
        sa = _sigmoid(ga_ref[...].astype(F32))
        sb = _sigmoid(gb_ref[...].astype(F32))
        dya_ref[...] = (dm * sa).astype(BF16)
        dyb_ref[...] = (dm * sb).astype(BF16)
        dga = dm * ya_ref[...].astype(F32) * (sa * (1.0 - sa))
        dgb = dm * yb_ref[...].astype(F32) * (sb * (1.0 - sb))
        dp_ref[:, pl.ds(0, D)] = dga.astype(BF16)
        dp_ref[:, pl.ds(D, D)] = dgb.astype(BF16)
        first = pl.program_id(0) == 0

        @pl.when(first)
        def _():
            cs_ref[...] = jnp.zeros_like(cs_ref)
        cs_ref[:, pl.ds(0, D)] += jnp.sum(dga, axis=0, keepdims=True)
        cs_ref[:, pl.ds(D, D)] += jnp.sum(dgb, axis=0, keepdims=True)

    rb = _row(tm, D)
    return pl.pallas_call(
        body, name=name, grid=(t // tm,),
        in_specs=[rb, _const((D, D)), rb, rb, _row(tm, D, 4), _row(tm, D, 5)],
        out_specs=[rb, rb, _row(tm, 2 * D), _const((1, 2 * D))],
        out_shape=[_sds((t, D), BF16), _sds((t, D), BF16), _sds((t, 2 * D), BF16),
                   _sds((1, 2 * D), F32)],
        compiler_params=_params("arbitrary"),
    )(dhb, w_out, ya, yb, p, p)


def _conv_bwd(name, dya, w_a, c, p, conv_w, ln_g, ln_b, tm, hosted=None):
    t = dya.shape[0]
    steps = t // tm
    hb = tm // HALO

    def rev(i):
        return steps - 1 - i

    def body(dy_ref, wa_ref, c_ref, av_ref, ag_ref, avh_ref, agh_ref, w_ref, lg_ref, lb_ref,
             dp_ref, cs_ref, dw_ref, dcb_ref, dlg_ref, dlb_ref, dc_ext, a_ext, dc_sh, a_sh, da0_s):
        i = pl.program_id(0)
        first = i == 0

        @pl.when(first)
        def _():
            dc_ext[pl.ds(tm, HALO), :] = jnp.zeros((HALO, D), F32)
            dw_ref[...] = jnp.zeros_like(dw_ref)
            cs_ref[...] = jnp.zeros_like(cs_ref)

        d_act = _dot_nt(dy_ref[...], wa_ref[...])
        rstd, chat = _ln_stats(c_ref[...])
        ca = chat * lg_ref[...] + lb_ref[...]
        sc = _sigmoid(ca)
        dca = d_act * (sc * (1.0 + ca * (1.0 - sc)))
        _acc(dlg_ref, first, jnp.sum(dca * chat, axis=0, keepdims=True))
        _acc(dlb_ref, first, jnp.sum(dca, axis=0, keepdims=True))
        dc = _ln_bwd(dca, chat, rstd, lg_ref[...])
        _acc(dcb_ref, first, jnp.sum(dc, axis=0, keepdims=True))
        dc_ext[pl.ds(0, tm), :] = dc

        av = av_ref[...].astype(F32)
        sg = _sigmoid(ag_ref[...].astype(F32))
        a_ext[pl.ds(HALO, tm), :] = av * sg
        halo = avh_ref[...].astype(F32) * _sigmoid(agh_ref[...].astype(F32))
        a_ext[pl.ds(0, HALO), :] = jnp.where(i == steps - 1, 0.0, halo)

        _shift_copies(dc_ext, dc_sh, tm)
        _shift_copies(a_ext, a_sh, tm)
        _tap_sum(da0_s, None, w_ref, dc_ext, dc_sh, [CW - 1 - k for k in range(CW)], tm)
        _tap_corr(dw_ref, dc_ext, a_ext, a_sh, [HALO - (CW - 1) + k for k in range(CW)], tm)
        da0 = da0_s[...]
        dav = da0 * sg
        dag = da0 * av * (sg * (1.0 - sg))
        dp_ref[:, pl.ds(0, D)] = dav.astype(BF16)
        dp_ref[:, pl.ds(D, D)] = dag.astype(BF16)
        cs_ref[:, pl.ds(0, D)] += jnp.sum(dav, axis=0, keepdims=True)
        cs_ref[:, pl.ds(D, D)] += jnp.sum(dag, axis=0, keepdims=True)
        dc_ext[pl.ds(tm, HALO), :] = dc_ext[pl.ds(0, HALO), :]

    def rrow(cols, cb=0):
        return pl.BlockSpec((tm, cols), lambda i, _cb=cb: (rev(i), _cb))

    def halo_spec(cb):
        return pl.BlockSpec((HALO, D), lambda i, _cb=cb: (jnp.maximum(rev(i) * hb - 1, 0), _cb))

    return _call(
        name, body, (steps,),
        [rrow(D), _const((D, D)), rrow(D), rrow(D, 0), rrow(D, 1), halo_spec(0),
         halo_spec(1), _const((HALO, D)), _const((1, D)), _const((1, D))],
        [rrow(2 * D), _const((1, 2 * D)), _const((HALO, D)), _const((1, D)),
         _const((1, D)), _const((1, D))],
        [_sds((t, 2 * D), BF16), _sds((1, 2 * D), F32), _sds((HALO, D), F32),
         _sds((1, D), F32), _sds((1, D), F32), _sds((1, D), F32)],
        (dya, w_a, c, p, p, p, p, conv_w, ln_g, ln_b),
        scratch=[pltpu.VMEM((tm + HALO, D), F32), pltpu.VMEM((tm + HALO, D), F32),
                 pltpu.VMEM((7, tm + HALO - 8, D), F32), pltpu.VMEM((7, tm + HALO - 8, D), F32),
                 pltpu.VMEM((tm, D), F32)],
        sem=("arbitrary",), hosted=hosted)


def _sgu_bwd(name, dyb, w_b, p, ln_g, ln_b, sgu_w, bias_full, tm):
    t = dyb.shape[0]
    steps = t // tm

    def body(dy_ref, wb_ref, bu_ref, bv_ref, lg_ref, lb_ref, ws_ref, bias_ref,
             dp_ref, cs_ref, dws_ref, dsb_ref, dlg_ref, dlb_ref, dub_s, dvn_s, dbias_s):
        i = pl.program_id(0)
        first = i == 0
        mask = _sgu_mask()

        @pl.when(first)
        def _():
            dws_ref[...] = jnp.zeros_like(dws_ref)
            dbias_s[...] = jnp.zeros_like(dbias_s)
            cs_ref[...] = jnp.zeros_like(cs_ref)

        dob = _dot_nt(dy_ref[...], wb_ref[...])
        bu = bu_ref[...].astype(F32)
        bv = bv_ref[...].astype(F32)
        ub, ub_grad = _gelu_with_grad(bu)
        vb, vb_grad = _gelu_with_grad(bv)
        rstd, vhat = _ln_stats(vb)
        vn = (vhat * lg_ref[...] + lb_ref[...]).astype(BF16)
        for g in range(GROUPS):
            wm = jnp.where(mask, ws_ref[g], 0.0).astype(BF16)
            cs = slice(g * GD, (g + 1) * GD)
            for cc in range(tm // CHUNK):
                rs = slice(cc * CHUNK, (cc + 1) * CHUNK)
                vblk = vn[rs, cs]
                mixed = _dot(wm, vblk) + bias_ref[:, cs]
                dob_blk = dob[rs, cs]
                dub_s[rs, cs] = dob_blk * mixed
                dmixed = dob_blk * ub[rs, cs]
                dbias_s[:, cs] += dmixed
                dmb = dmixed.astype(BF16)
                dws_ref[g] += _dot_nt(dmb, vblk)
                dvn_s[rs, cs] = _dot_tn(wm, dmb)
        dbu = dub_s[...] * ub_grad
        dvn = dvn_s[...]
        _acc(dlg_ref, first, jnp.sum(dvn * vhat, axis=0, keepdims=True))
        _acc(dlb_ref, first, jnp.sum(dvn, axis=0, keepdims=True))
        dbv = _ln_bwd(dvn, vhat, rstd, lg_ref[...]) * vb_grad
        dp_ref[:, pl.ds(0, D)] = dbu.astype(BF16)
        dp_ref[:, pl.ds(D, D)] = dbv.astype(BF16)
        cs_ref[:, pl.ds(0, D)] += jnp.sum(dbu, axis=0, keepdims=True)
        cs_ref[:, pl.ds(D, D)] += jnp.sum(dbv, axis=0, keepdims=True)

        @pl.when(i == steps - 1)
        def _():
            lane = lax.broadcasted_iota(jnp.int32, (CHUNK, CHUNK), 1)
            dsb = jnp.zeros((CHUNK, CHUNK), F32)
            for g in range(GROUPS):
                dws_ref[g] = jnp.where(mask, dws_ref[g], 0.0)
                dsb = jnp.where(lane == g, jnp.sum(dbias_s[:, g * GD:(g + 1) * GD], axis=1, keepdims=True), dsb)
            dsb_ref[...] = dsb

    rb = _row(tm, D)
    return pl.pallas_call(
        body, name=name, grid=(steps,),
        in_specs=[rb, _const((D, D)), _row(tm, D, 2), _row(tm, D, 3), _const((1, D)), _const((1, D)),
                  _const((GROUPS, CHUNK, CHUNK)), _const((CHUNK, D))],
        out_specs=[_row(tm, 2 * D), _const((1, 2 * D)), _const((GROUPS, CHUNK, CHUNK)),
                   _const((CHUNK, CHUNK)), _const((1, D)), _const((1, D))],
        out_shape=[_sds((t, 2 * D), BF16), _sds((1, 2 * D), F32), _sds((GROUPS, CHUNK, CHUNK), F32),
                   _sds((CHUNK, CHUNK), F32), _sds((1, D), F32), _sds((1, D), F32)],
        scratch_shapes=[pltpu.VMEM((tm, D), F32), pltpu.VMEM((tm, D), F32), pltpu.VMEM((CHUNK, D), F32)],
        compiler_params=_params("arbitrary"),
    )(dyb, w_b, p, p, ln_g, ln_b, sgu_w, bias_full)


def _adam_math(w, g, m, v):
    m = B1 * m + (1.0 - B1) * g
    v = B2 * v + (1.0 - B2) * (g * g)
    m_hat = m / (1.0 - B1 ** STEP)
    v_hat = v / (1.0 - B2 ** STEP)
    delta = -LR * (m_hat / (jnp.sqrt(v_hat) + EPS_ADAM) + WD * w)
    return delta, m, v


def _adamw(name, w, g, m, v, tr):
    r, cdim = w.shape

    def body(w_ref, g_ref, m_ref, v_ref, d_ref, mo_ref, vo_ref):
        d, mn, vn = _adam_math(w_ref[...], g_ref[...], m_ref[...], v_ref[...])
        d_ref[...] = d
        mo_ref[...] = mn
        vo_ref[...] = vn

    blk = pl.BlockSpec((tr, cdim), lambda i: (i, 0))
    return pl.pallas_call(
        body, name=name, grid=(r // tr,), in_specs=[blk] * 4, out_specs=[blk] * 3,
        out_shape=[_sds((r, cdim), F32)] * 3, compiler_params=_params("parallel"),
    )(w, g, m, v)


def _adamw_small(name, w, g8, m, v):
    r, cdim = w.shape

    def body(w_ref, g_ref, m_ref, v_ref, go_ref, d_ref, mo_ref, vo_ref):
        g = g_ref[0]
        for k in range(1, NDEV):
            g = g + g_ref[k]
        go_ref[...] = g
        d, mn, vn = _adam_math(w_ref[...], g, m_ref[...], v_ref[...])
        d_ref[...] = d
        mo_ref[...] = mn
        vo_ref[...] = vn

    return pl.pallas_call(
        body, name=name, out_shape=[_sds((r, cdim), F32)] * 4,
        compiler_params=pltpu.CompilerParams(vmem_limit_bytes=VMEM_LIMIT),
    )(w, g8, m, v)


_BIG = [("ffn1_w_gu", 704, True), ("ffn1_w_down", 352, False), ("w_in", 768, True),
        ("w_a_out", 128, False), ("w_b_out", 128, False), ("w_out", 128, False),
        ("w_q", 128, False), ("w_kv", 256, True), ("w_o", 128, False),
        ("ffn2_w_gu", 704, True), ("ffn2_w_down", 352, False)]
_BIG_ROWS = sum(r for _, r, _ in _BIG)

_SMALL = [("ffn1_norm", 1), ("mix_norm", 1), ("b_in", 6), ("conv_w", HALO), ("conv_b", 1),
          ("conv_ln_g", 1), ("conv_ln_b", 1), ("sgu_ln_g", 1), ("sgu_ln_b", 1), ("sgu_w", 64),
          ("sgu_b", 1), ("xattn_norm", 1), ("mem_norm", 1), ("ffn2_norm", 1), ("final_norm", 1)]
_SMALL_ROWS = 120


def _pack_small(vals, my_dev):
    rows = []
    for name, nrows in _SMALL:
        a = vals[name].astype(F32)
        if name == "conv_w":
            if a.shape[-1] != D:
                slab = jnp.zeros((HALO, D), F32)
                a = lax.dynamic_update_slice(slab, jnp.pad(a.reshape(CW, -1), ((0, HALO - CW), (0, 0))),
                                             (0, my_dev * (D // NDEV)))
            else:
                a = jnp.pad(a.reshape(CW, D), ((0, HALO - CW), (0, 0)))
        elif name == "sgu_b":
            a = jnp.pad(a.reshape(1, -1), ((0, 0), (0, D - GROUPS * CHUNK)))
        else:
            a = a.reshape(nrows, D)
        rows.append(a)
    packed = jnp.concatenate(rows, axis=0)
    return jnp.pad(packed, ((0, _SMALL_ROWS - packed.shape[0]), (0, 0)))


def _unpack_small(packed, shapes, my_dev):
    out, off = {}, 0
    for name, nrows in _SMALL:
        a = packed[off:off + nrows]
        off += nrows
        if name == "conv_w":
            a = lax.dynamic_slice(a, (0, my_dev * (D // NDEV)), (CW, D // NDEV))
        elif name == "sgu_b":
            a = a[:, :GROUPS * CHUNK]
        out[name] = a.reshape(shapes[name])
    return out


def kernel(x, mem, ffn1_norm, ffn1_w_gu, ffn1_w_down, mix_norm, w_in, b_in, conv_w, conv_b, conv_ln_g, conv_ln_b, w_a_out, sgu_ln_g, sgu_ln_b, sgu_w, sgu_b, w_b_out, w_out, xattn_norm, mem_norm, w_q, w_kv, w_o, ffn2_norm, ffn2_w_gu, ffn2_w_down, final_norm, loss_target, m_ffn1_norm, m_ffn1_w_gu, m_ffn1_w_down, m_mix_norm, m_w_in, m_b_in, m_conv_w, m_conv_b, m_conv_ln_g, m_conv_ln_b, m_w_a_out, m_sgu_ln_g, m_sgu_ln_b, m_sgu_w, m_sgu_b, m_w_b_out, m_w_out, m_xattn_norm, m_mem_norm, m_w_q, m_w_kv, m_w_o, m_ffn2_norm, m_ffn2_w_gu, m_ffn2_w_down, m_final_norm, v_ffn1_norm, v_ffn1_w_gu, v_ffn1_w_down, v_mix_norm, v_w_in, v_b_in, v_conv_w, v_conv_b, v_conv_ln_g, v_conv_ln_b, v_w_a_out, v_sgu_ln_g, v_sgu_ln_b, v_sgu_w, v_sgu_b, v_w_b_out, v_w_out, v_xattn_norm, v_mem_norm, v_w_q, v_w_kv, v_w_o, v_ffn2_norm, v_ffn2_w_gu, v_ffn2_w_down, v_final_norm):
    env = dict(locals())
    names = [n for n, _, _ in _BIG] + [n for n, _ in _SMALL]
    w = {n: env[n] for n in names}
    mom = {n: env["m_" + n] for n in names}
    vel = {n: env["v_" + n] for n in names}

    ax, ay, ac = lax.axis_index("x"), lax.axis_index("y"), lax.axis_index("c")
    my_chip = 2 * ax + ay
    my_dev = 2 * my_chip + ac

    t = x.shape[1]
    tm = min(512, t)
    tm_big = min(1024, t)
    tm_s = min(512, t)
    tm_c = min(256, t)
    xs = x.reshape(t, D)
    tgt = loss_target.reshape(t, D)
    mem2 = mem.reshape(NMEM, D)

    first, mid, late = _BIG[:1], _BIG[1:6], _BIG[6:]

    def gathers(entries):
        return [_gather_comm((w[n][0].T if tr else w[n][0]).astype(BF16)) for n, _, tr in entries]

    def whole(gathered, entries):
        return {n: g.reshape(NDEV * rows, D) for g, (n, rows, _) in zip(gathered, entries)}

    conv_slab = lax.dynamic_update_slice(
        jnp.zeros((HALO, D), F32), jnp.pad(conv_w[0], ((0, HALO - CW), (0, 0))), (0, my_dev * (D // NDEV)))
    bias_full = jnp.repeat(sgu_b[0].T, GD, axis=1)
    b_in2 = b_in.reshape(1, 6 * D)

    (xn1,), (*full_first, conv_w8) = _rms_cast(
        "norm_x", xs, ffn1_norm, tm, hosted=gathers(first) + [_gather_comm(conv_slab)])
    conv_w_pad = jnp.sum(conv_w8, axis=0)
    wf = whole(full_first, first)
    (g1, u1, a1), full_mid = _ffn_up("ffn1_up", xn1, wf["ffn1_w_gu"], tm_big, 1408, hosted=gathers(mid))
    wf.update(whole(full_mid, mid))
    h1, n_mix = _ffn_down("ffn1_down", a1, wf["ffn1_w_down"], xs, mix_norm, tm)
    (p,), full_late = _mix_in("mix_in", n_mix, wf["w_in"], b_in2, tm_big, 1536, hosted=gathers(late))
    wf.update(whole(full_late, late))
    c_conv, act_a = _conv_fwd("conv_fwd", p, conv_w_pad, conv_b, conv_ln_g, conv_ln_b, tm_c)
    act_b = _sgu_fwd("sgu_fwd", p, sgu_ln_g, sgu_ln_b, sgu_w[0], bias_full, tm_s)
    ya, yb, merged, h2, xq = _merge_fwd("merge_fwd", act_a, act_b, p, wf["w_a_out"], wf["w_b_out"],
                                        wf["w_out"], h1, xattn_norm, tm_s)
    memn, kb, vb = _kv_fwd("kv_fwd", mem2, mem_norm, wf["w_kv"])
    qb, ob, h3, xn4 = _attn_fwd("attn_fwd", xq, wf["w_q"], kb, vb, wf["w_o"], h2, ffn2_norm, tm_s)
    (g2, u2, a2), _ = _ffn_up("ffn2_up", xn4, wf["ffn2_w_gu"], tm_big, 1408)
    dh4, dh4b, loss_blk, d_final = _ffn_down_loss("ffn2_down_loss", a2, wf["ffn2_w_down"], h3,
                                                  final_norm.reshape(1, D), tgt, tm)

    gs = {}
    gs["final_norm"] = d_final

    core = ac.astype(jnp.int32).reshape(1)
    chip = my_chip.astype(jnp.int32).reshape(1)
    last_g, mixer_g, attn_g = _BIG[:2], _BIG[2:6], _BIG[6:]

    def layout(entries):
        offs, off = {}, 0
        for n, rows, _ in entries:
            offs[n] = (off, rows)
            off += rows
        return offs, off

    def dest(group, buf, name, row0=0):
        offs, total = group
        return _Dest(buf, total, offs[name][0], offs[name][1], row0)

    lay_last, lay_mixer, lay_attn = layout(last_g), layout(mixer_g), layout(attn_g)

    def ffn_bwd(tag, dhb, dh, g, u, a, xn, h_in, gain, wgu_t, wd, group, buf, act_hosted=None, dx_hosted=None):
        (dg, du), act_out = _ffn_bwd_act(tag + "_bwd_act", dhb, wd, g, u, tm_big, 1408, hosted=act_hosted)
        buf = _mm_tn(tag + "_dw_down", a, dhb, 0.5, 1408, TK, dest(group, buf, tag + "_w_down"))
        buf = _mm_tn(tag + "_dw_gate", dg, xn, 1.0, 1408, TK, dest(group, buf, tag + "_w_gu"))
        buf = _mm_tn(tag + "_dw_up", du, xn, 1.0, 1408, TK, dest(group, buf, tag + "_w_gu", DFF))
        (dh_o, dhb_o, dgain), dx_out = _dx_rms_bwd(
            tag + "_bwd_dx", [(dg, wgu_t, 0, DFF, False), (du, wgu_t, 1, DFF, False)], h_in, gain, dh, tm_s,
            hosted=dx_hosted(buf) if dx_hosted else None)
        return dh_o, dhb_o, dgain, buf, act_out, dx_out

    dh3, dh3b, gs["ffn2_norm"], g_attn, _, _ = ffn_bwd(
        "ffn2", dh4b, dh4, g2, u2, a2, xn4, h3, ffn2_norm, wf["ffn2_w_gu"], wf["ffn2_w_down"], lay_attn, None)

    g_attn = _mm_tn("dw_o", ob, dh3b, 1.0, 1024, 2 * TK, dest(lay_attn, g_attn, "w_o"))
    dq, dk, dv = _attn_bwd("attn_bwd", dh3b, wf["w_o"], qb, kb, vb, tm_s)
    g_attn, gs["mem_norm"] = _kv_bwd("kv_bwd", dk, dv, memn, wf["w_kv"], mem2, mem_norm,
                                     dest(lay_attn, g_attn, "w_kv"))
    g_attn = _mm_tn("dw_q", xq, dq, 1.0, 1024, 2 * TK, dest(lay_attn, g_attn, "w_q"))

    g4_attn = g_attn.reshape(4, 2, lay_attn[1], D)
    (dh2, dh2b, gs["xattn_norm"]), (sib_attn,) = _dx_rms_bwd(
        "attn_bwd_dx", [(dq, wf["w_q"], 0, D, True)], h2, xattn_norm, dh3, tm_s,
        hosted=_pair_exchange_comm(g4_attn))
    part_attn = _pair_sum("grads_pair_sum_attn", g4_attn, sib_attn, core, 784)

    g_mixer = _mm_tn("dw_out", merged, dh2b, 1.0, 1024, 2 * TK, dest(lay_mixer, None, "w_out"))
    dya, dyb, dp_g, cs_g = _merge_bwd("merge_bwd", dh2b, wf["w_out"], ya, yb, p, tm_s)
    g_mixer = _mm_tn("dw_a", act_a, dya, 1.0, 1024, 2 * TK, dest(lay_mixer, g_mixer, "w_a_out"))
    g_mixer = _mm_tn("dw_b", act_b, dyb, 1.0, 1024, 2 * TK, dest(lay_mixer, g_mixer, "w_b_out"))
    (dp_a, cs_a, d_convw, gs["conv_b"], gs["conv_ln_g"], gs["conv_ln_b"]), (chips_attn,) = _conv_bwd(
        "conv_bwd", dya, wf["w_a_out"], c_conv, p, conv_w_pad, conv_ln_g, conv_ln_b, tm_c,
        hosted=_chip_exchange_comm(part_attn))
    gsum_attn = _chip_sum("grads_chip_sum_attn", part_attn, chips_attn, chip, 784)
    dp_b, cs_b, d_sguw, d_sgub, gs["sgu_ln_g"], gs["sgu_ln_b"] = _sgu_bwd(
        "sgu_bwd", dyb, wf["w_b_out"], p, sgu_ln_g, sgu_ln_b, sgu_w[0], bias_full, tm_s)
    gs["conv_w"] = d_convw[:CW].reshape(1, CW, D)
    gs["sgu_w"] = d_sguw
    gs["sgu_b"] = d_sgub[:, :GROUPS].T
    gs["b_in"] = jnp.concatenate([cs_a, cs_b, cs_g], axis=1)
    for j, (tag, dpart) in enumerate((("a", dp_a), ("b", dp_b), ("g", dp_g))):
        g_mixer = _mm_tn("dw_in_" + tag, dpart, n_mix, 1.0, 1024, 2 * TK,
                         dest(lay_mixer, g_mixer, "w_in", 2 * D * j))
    g4_mixer = g_mixer.reshape(4, 2, lay_mixer[1], D)
    (dh1, dh1b, gs["mix_norm"]), (sib_mixer,) = _dx_rms_bwd(
        "mix_bwd_dx", [(dp_a, wf["w_in"], 0, 2 * D, False), (dp_b, wf["w_in"], 1, 2 * D, False),
                       (dp_g, wf["w_in"], 2, 2 * D, False)], h1, mix_norm, dh2, tm_s,
        hosted=_pair_exchange_comm(g4_mixer))
    part_mixer = _pair_sum("grads_pair_sum_mixer", g4_mixer, sib_mixer, core, 576)

    dx, _, gs["ffn1_norm"], _, (chips_mixer,), (last_slots,) = ffn_bwd(
        "ffn1", dh1b, dh1, g1, u1, a1, xn1, xs, ffn1_norm, wf["ffn1_w_gu"], wf["ffn1_w_down"], lay_last, None,
        act_hosted=_chip_exchange_comm(part_mixer), dx_hosted=_all_to_all_comm)
    gsum_mixer = _chip_sum("grads_chip_sum_mixer", part_mixer, chips_mixer, chip, 576)
    gsum_last = _slot_sum("grads_slot_sum_ffn1", last_slots, 528)

    grads, deltas, new_m, new_v = {}, {}, {}, {}
    for entries, gsum in ((last_g, gsum_last), (mixer_g, gsum_mixer), (attn_g, gsum_attn)):
        off = 0
        for name, rows, transposed in entries:
            gsh = gsum[off:off + rows]
            off += rows
            gsh = gsh.T if transposed else gsh
            d, mo, vo = _adamw("adamw_" + name, w[name][0], gsh, mom[name][0], vel[name][0], gsh.shape[0] // 2)
            grads[name], deltas[name], new_m[name], new_v[name] = gsh[None], d[None], mo[None], vo[None]

    shapes = {n: w[n].shape for n, _ in _SMALL}
    (g8,) = _run_comm("gather_small_grads", _gather_comm(_pack_small(gs, my_dev)))
    sg, sd, sm, sv = _adamw_small("adamw_small", _pack_small(w, my_dev), g8,
                                  _pack_small(mom, my_dev), _pack_small(vel, my_dev))
    for dst, src in ((grads, sg), (deltas, sd), (new_m, sm), (new_v, sv)):
        dst.update(_unpack_small(src, shapes, my_dev))

    loss = lax.psum(loss_blk[0, 0], AXES)
    order = ["ffn1_norm", "ffn1_w_gu", "ffn1_w_down", "mix_norm", "w_in", "b_in", "conv_w", "conv_b",
             "conv_ln_g", "conv_ln_b", "w_a_out", "sgu_ln_g", "sgu_ln_b", "sgu_w", "sgu_b", "w_b_out",
             "w_out", "xattn_norm", "mem_norm", "w_q", "w_kv", "w_o", "ffn2_norm", "ffn2_w_gu",
             "ffn2_w_down", "final_norm"]
    return (loss, dx.reshape(x.shape), *[grads[n] for n in order], *[deltas[n] for n in order],
            *[new_m[n] for n in order], *[new_v[n] for n in order])
```

```python
import functools
import math

import jax
import jax.numpy as jnp
from jax import lax
from jax.experimental import pallas as pl
from jax.experimental.pallas import tpu as pltpu

F32 = jnp.float32
BF16 = jnp.bfloat16
MESH = pl.DeviceIdType.MESH
AXES = ("x", "y", "c")

D = 1024
DFF = 2816
NMEM = 256
HEADS = 4
HD = D // HEADS
CW = 31
HALO = 32
CHUNK = 128
GROUPS = 4
GD = D // GROUPS
EPS_RMS = 1e-6
EPS_LN = 1e-5
LR, B1, B2, EPS_ADAM, WD, STEP = 0.001, 0.9, 0.999, 1e-08, 0.01, 10
NDEV = 8
VMEM_LIMIT = 56 * 1024 * 1024
TK = 2048


def _params(*sem):
    return pltpu.CompilerParams(dimension_semantics=sem, vmem_limit_bytes=VMEM_LIMIT)


def _dot(a, b):
    return jnp.dot(a, b, preferred_element_type=F32)


def _dot_nt(a, b):
    return lax.dot_general(a, b, (((1,), (1,)), ((), ())), preferred_element_type=F32)


def _dot_tn(a, b):
    return lax.dot_general(a, b, (((0,), (0,)), ((), ())), preferred_element_type=F32)


def _sigmoid(x):
    return 0.5 * jnp.tanh(0.5 * x) + 0.5


_GELU_C = math.sqrt(2.0 / math.pi)


def _gelu_with_grad(x):
    x2 = x * x
    t = jnp.tanh(_GELU_C * (x + 0.044715 * (x2 * x)))
    half = 0.5 * (1.0 + t)
    return x * half, half + 0.5 * x * (1.0 - t * t) * (_GELU_C * (1.0 + 3.0 * 0.044715 * x2))


def _gelu(x):
    return _gelu_with_grad(x)[0]


def _rms_stats(h):
    r = lax.rsqrt(jnp.mean(h * h, axis=-1, keepdims=True) + EPS_RMS)
    return r, h * r


def _rms_bwd(dxn, h, gain):
    r, xh = _rms_stats(h)
    dgain = jnp.sum(dxn * xh, axis=0, keepdims=True)
    dxh = dxn * gain
    dh = r * (dxh - xh * jnp.mean(dxh * xh, axis=-1, keepdims=True))
    return dh, dgain


def _ln_stats(c):
    mu = jnp.mean(c, axis=-1, keepdims=True)
    xc = c - mu
    rstd = lax.rsqrt(jnp.mean(xc * xc, axis=-1, keepdims=True) + EPS_LN)
    return rstd, xc * rstd


def _ln_bwd(dy, xhat, rstd, g):
    dxh = dy * g
    return rstd * (dxh - jnp.mean(dxh, axis=-1, keepdims=True)
                   - xhat * jnp.mean(dxh * xhat, axis=-1, keepdims=True))


def _row(tm, cols, cb=0):
    return pl.BlockSpec((tm, cols), lambda i, _cb=cb: (i, _cb))


def _const(shape):
    n = len(shape)
    return pl.BlockSpec(shape, lambda *_: (0,) * n)


def _sds(shape, dtype):
    return jax.ShapeDtypeStruct(shape, dtype)


def _acc(ref, first, val):
    @pl.when(first)
    def _():
        ref[...] = jnp.zeros_like(ref)
    ref[...] += val


class _Comm:
    def __init__(self, args, out_shapes, scratch, start, finish):
        self.args, self.out_shapes, self.scratch = args, out_shapes, scratch
        self.start, self.finish = start, finish


_ANY = pl.BlockSpec(memory_space=pl.ANY)


def _run_comm(name, comm):
    ni, no = len(comm.args), len(comm.out_shapes)

    def body(*refs):
        ins, outs, sems = refs[:ni], refs[ni:ni + no], refs[ni + no:]
        comm.start(ins, outs, sems)
        comm.finish(ins, outs, sems)

    return pl.pallas_call(
        body, name=name, out_shape=list(comm.out_shapes), in_specs=[_ANY] * ni, out_specs=[_ANY] * no,
        scratch_shapes=list(comm.scratch),
    )(*comm.args)


def _call(name, body, grid, in_specs, out_specs, out_shape, args, scratch=(), sem=None, hosted=None):
    n_in, n_out, n_scr = len(in_specs), len(out_specs), len(scratch)
    if not hosted:
        outs = pl.pallas_call(
            body, name=name, grid=grid, in_specs=list(in_specs), out_specs=list(out_specs),
            out_shape=list(out_shape), scratch_shapes=list(scratch), compiler_params=_params(*sem),
        )(*args)
        return outs, []
    comms = list(hosted) if isinstance(hosted, (list, tuple)) else [hosted]
    hi = sum(len(cm.args) for cm in comms)
    ho = sum(len(cm.out_shapes) for cm in comms)

    def wrapped(*refs):
        ins, h_in = refs[:n_in], refs[n_in:n_in + hi]
        o0 = n_in + hi
        outs, h_out = refs[o0:o0 + n_out], refs[o0 + n_out:o0 + n_out + ho]
        s0 = o0 + n_out + ho
        scr, h_sems = refs[s0:s0 + n_scr], refs[s0 + n_scr:]
        ids = [pl.program_id(a) for a in range(len(grid))]
        first = functools.reduce(jnp.logical_and, [i == 0 for i in ids])
        last = functools.reduce(jnp.logical_and, [i == g - 1 for i, g in zip(ids, grid)])
        parts, a0, b0, c0 = [], 0, 0, 0
        for cm in comms:
            na, nb, nc = len(cm.args), len(cm.out_shapes), len(cm.scratch)
            parts.append((cm, h_in[a0:a0 + na], h_out[b0:b0 + nb], h_sems[c0:c0 + nc]))
            a0, b0, c0 = a0 + na, b0 + nb, c0 + nc

        @pl.when(first)
        def _():
            for cm, ci, co, cs in parts:
                cm.start(ci, co, cs)

        body(*ins, *outs, *scr)

        @pl.when(last)
        def _():
            for cm, ci, co, cs in parts:
                cm.finish(ci, co, cs)

    res = pl.pallas_call(
        wrapped, name=name, grid=grid, in_specs=list(in_specs) + [_ANY] * hi,
        out_specs=list(out_specs) + [_ANY] * ho,
        out_shape=list(out_shape) + [s for cm in comms for s in cm.out_shapes],
        scratch_shapes=list(scratch) + [s for cm in comms for s in cm.scratch],
        compiler_params=_params(*(["arbitrary"] * len(grid))),
    )(*args, *[a for cm in comms for a in cm.args])
    return res[:n_out], res[n_out:]


def _gather_comm(blk):
    r, cdim = blk.shape

    def copies(x_ref, out_ref, send_sems, recv_sems, local_sem):
        x, y, c = lax.axis_index("x"), lax.axis_index("y"), lax.axis_index("c")
        me, sibling = (x, y, c), (x, y, 1 - c)
        chips = [(1 - x, y), (x, 1 - y), (1 - x, 1 - y)]

        def slot(px, py, pc):
            return out_ref.at[4 * px + 2 * py + pc]

        def copy(k, block, to, src=None):
            return pltpu.make_async_remote_copy(
                src_ref=slot(*block) if src is None else src, dst_ref=slot(*block),
                send_sem=send_sems.at[k], recv_sem=recv_sems.at[k],
                device_id=to, device_id_type=MESH)

        mine = pltpu.make_async_copy(x_ref, slot(*me), local_sem)
        first = [copy(0, me, sibling, src=x_ref)]
        first += [copy(1 + j, me, (*chip, c), src=x_ref) for j, chip in enumerate(chips)]
        passed = [copy(4 + j, (*chip, c), sibling) for j, chip in enumerate(chips)]
        landed = [copy(1 + j, (*chip, c), me) for j, chip in enumerate(chips)]
        from_sibling = [copy(0, sibling, me)] + [copy(4 + j, (*chip, 1 - c), me) for j, chip in enumerate(chips)]
        return mine, first, passed, landed, from_sibling

    def start(ins, outs, sems):
        mine, first, _, _, _ = copies(ins[0], outs[0], *sems)
        mine.start()
        for cp in first:
            cp.start()

    def finish(ins, outs, sems):
        mine, first, passed, landed, from_sibling = copies(ins[0], outs[0], *sems)
        for arrived, forward in zip(landed, passed):
            arrived.wait_recv()
            forward.start()
        for cp in from_sibling:
            cp.wait_recv()
        for cp in first + passed:
            cp.wait_send()
        mine.wait()

    return _Comm([blk], [_sds((NDEV, r, cdim), blk.dtype)],
                 [pltpu.SemaphoreType.DMA((7,)), pltpu.SemaphoreType.DMA((7,)), pltpu.SemaphoreType.DMA],
                 start, finish)


def _exchange_comm(src, n, out_rows, make):
    r, cdim = out_rows

    def copies(src_ref, out_ref, send_sems, recv_sems):
        out = []
        for k in range(n):
            s, d, to = make(k, src_ref, out_ref)
            out.append(pltpu.make_async_remote_copy(
                src_ref=s, dst_ref=d, send_sem=send_sems.at[k], recv_sem=recv_sems.at[k],
                device_id=to, device_id_type=MESH))
        return out

    def start(ins, outs, sems):
        for cp in copies(ins[0], outs[0], *sems):
            cp.start()

    def finish(ins, outs, sems):
        cps = copies(ins[0], outs[0], *sems)
        for cp in cps:
            cp.wait_recv()
        for cp in cps:
            cp.wait_send()

    return _Comm([src], [_sds((n, r, cdim), src.dtype)],
                 [pltpu.SemaphoreType.DMA((n,)), pltpu.SemaphoreType.DMA((n,))], start, finish)


def _pair_exchange_comm(g4):
    def make(k, g_ref, out_ref):
        x, y, c = lax.axis_index("x"), lax.axis_index("y"), lax.axis_index("c")
        return g_ref.at[k, 1 - c], out_ref.at[k], (x, y, 1 - c)

    return _exchange_comm(g4, 4, g4.shape[2:], make)


def _chip_exchange_comm(part):
    def make(k, p_ref, out_ref):
        x, y, c = lax.axis_index("x"), lax.axis_index("y"), lax.axis_index("c")
        px = x if ((k + 1) >> 1) == 0 else 1 - x
        py = y if ((k + 1) & 1) == 0 else 1 - y
        return p_ref.at[2 * px + py], out_ref.at[k], (px, py, c)

    return _exchange_comm(part, 3, part.shape[1:], make)


def _all_to_all_comm(g8):
    _, r, cdim = g8.shape

    def copies(g_ref, out_ref, send_sems, recv_sems, local_sem):
        x, y, c = lax.axis_index("x"), lax.axis_index("y"), lax.axis_index("c")
        me = 4 * x + 2 * y + c
        remote = []
        for k in range(1, NDEV):
            px = 1 - x if k & 4 else x
            py = 1 - y if k & 2 else y
            pc = 1 - c if k & 1 else c
            remote.append(pltpu.make_async_remote_copy(
                src_ref=g_ref.at[4 * px + 2 * py + pc], dst_ref=out_ref.at[me],
                send_sem=send_sems.at[k - 1], recv_sem=recv_sems.at[k - 1],
                device_id=(px, py, pc), device_id_type=MESH))
        return pltpu.make_async_copy(g_ref.at[me], out_ref.at[me], local_sem), remote

    def start(ins, outs, sems):
        mine, remote = copies(ins[0], outs[0], *sems)
        mine.start()
        for cp in remote:
            cp.start()

    def finish(ins, outs, sems):
        mine, remote = copies(ins[0], outs[0], *sems)
        for cp in remote:
            cp.wait_recv()
        for cp in remote:
            cp.wait_send()
        mine.wait()

    return _Comm([g8], [_sds((NDEV, r, cdim), g8.dtype)],
                 [pltpu.SemaphoreType.DMA((NDEV - 1,)), pltpu.SemaphoreType.DMA((NDEV - 1,)),
                  pltpu.SemaphoreType.DMA], start, finish)


def _slot_sum(name, slots, tr):
    _, r, cdim = slots.shape

    def body(s_ref, o_ref):
        s = s_ref[0].astype(F32)
        for k in range(1, NDEV):
            s = s + s_ref[k].astype(F32)
        o_ref[...] = s

    return pl.pallas_call(
        body, name=name, grid=(r // tr,),
        in_specs=[pl.BlockSpec((NDEV, tr, cdim), lambda i: (0, i, 0))],
        out_specs=pl.BlockSpec((tr, cdim), lambda i: (i, 0)),
        out_shape=_sds((r, cdim), F32), compiler_params=_params("parallel"),
    )(slots)


def _pair_sum(name, g4, recv, core, tr):
    _, _, r, cdim = g4.shape

    def body(core_ref, a_ref, b_ref, o_ref):
        o_ref[...] = (a_ref[...].astype(F32) + b_ref[...].astype(F32)).astype(o_ref.dtype)

    return pl.pallas_call(
        body, name=name,
        grid_spec=pltpu.PrefetchScalarGridSpec(
            num_scalar_prefetch=1, grid=(4, r // tr),
            in_specs=[pl.BlockSpec((None, None, tr, cdim), lambda k, i, cr: (k, cr[0], i, 0)),
                      pl.BlockSpec((None, tr, cdim), lambda k, i, cr: (k, i, 0))],
            out_specs=pl.BlockSpec((None, tr, cdim), lambda k, i, cr: (k, i, 0))),
        out_shape=_sds((4, r, cdim), BF16),
        compiler_params=_params("parallel", "parallel"),
    )(core, g4, recv)


def _chip_sum(name, part, recv, chip, tr):
    _, r, cdim = part.shape

    def body(chip_ref, a_ref, b_ref, o_ref):
        s = a_ref[...].astype(F32)
        for k in range(3):
            s = s + b_ref[k].astype(F32)
        o_ref[...] = s

    return pl.pallas_call(
        body, name=name,
        grid_spec=pltpu.PrefetchScalarGridSpec(
            num_scalar_prefetch=1, grid=(r // tr,),
            in_specs=[pl.BlockSpec((None, tr, cdim), lambda i, cr: (cr[0], i, 0)),
                      pl.BlockSpec((3, tr, cdim), lambda i, cr: (0, i, 0))],
            out_specs=pl.BlockSpec((tr, cdim), lambda i, cr: (i, 0))),
        out_shape=_sds((r, cdim), F32),
        compiler_params=_params("parallel"),
    )(chip, part, recv)


def _rms_cast(name, h, gain, tm, hosted=None):
    t = h.shape[0]

    def body(h_ref, g_ref, o_ref):
        _, xh = _rms_stats(h_ref[...])
        o_ref[...] = (xh * g_ref[...]).astype(BF16)

    return _call(name, body, (t // tm,), [_row(tm, D), _const((1, D))], [_row(tm, D)],
                 [_sds((t, D), BF16)], (h, gain), sem=("parallel",), hosted=hosted)


def _ffn_up(name, xn, wgu_t, tm, tn, hosted=None):
    t = xn.shape[0]
    nh = DFF // tn

    def body(x_ref, wg_ref, wu_ref, s_ref, f_ref, a_ref):
        x = x_ref[...]
        g = _dot_nt(x, wg_ref[...])
        u = _dot_nt(x, wu_ref[...])
        sg = _sigmoid(g)
        s = g * sg
        s_ref[...] = s.astype(BF16)
        f_ref[...] = (u * (sg + s * (1.0 - sg))).astype(BF16)
        a_ref[...] = (s * u).astype(BF16)

    o = pl.BlockSpec((tm, tn), lambda j, i: (i, j))
    return _call(name, body, (nh, t // tm),
                 [pl.BlockSpec((tm, D), lambda j, i: (i, 0)),
                  pl.BlockSpec((tn, D), lambda j, i: (j, 0)),
                  pl.BlockSpec((tn, D), lambda j, i: (j + nh, 0))],
                 [o, o, o], [_sds((t, DFF), BF16)] * 3, (xn, wgu_t, wgu_t),
                 sem=("parallel", "parallel"), hosted=hosted)


def _ffn_down(name, a, wd, h, gain, tm):
    t = a.shape[0]

    def body(a_ref, w_ref, h_ref, g_ref, o_ref, n_ref):
        hn = h_ref[...] + 0.5 * _dot(a_ref[...], w_ref[...])
        o_ref[...] = hn
        _, xh = _rms_stats(hn)
        n_ref[...] = (xh * g_ref[...]).astype(BF16)

    return pl.pallas_call(
        body, name=name, grid=(t // tm,),
        in_specs=[_row(tm, DFF), _const((DFF, D)), _row(tm, D), _const((1, D))],
        out_specs=[_row(tm, D), _row(tm, D)],
        out_shape=[_sds((t, D), F32), _sds((t, D), BF16)],
        compiler_params=_params("parallel"),
    )(a, wd, h, gain)


def _mix_in(name, n, win_t, b_in, tm, tn, hosted=None):
    t = n.shape[0]

    def body(n_ref, w_ref, b_ref, p_ref):
        p_ref[...] = (_dot_nt(n_ref[...], w_ref[...]) + b_ref[...]).astype(BF16)

    return _call(name, body, (6 * D // tn, t // tm),
                 [pl.BlockSpec((tm, D), lambda j, i: (i, 0)),
                  pl.BlockSpec((tn, D), lambda j, i: (j, 0)),
                  pl.BlockSpec((1, tn), lambda j, i: (0, j))],
                 [pl.BlockSpec((tm, tn), lambda j, i: (i, j))], [_sds((t, 6 * D), BF16)],
                 (n, win_t, b_in), sem=("parallel", "parallel"), hosted=hosted)


RC = 64
LANES = 128


def _shift_copies(ext, shifted, tm):
    n = tm + HALO - 8
    for m in range(1, 8):
        shifted[m - 1] = ext[pl.ds(m, n), :]


def _by_residue(offs):
    groups = {}
    for k, off in enumerate(offs):
        q, m = divmod(off, 8)
        groups.setdefault(m, []).append((k, q))
    return groups


def _residue_window(ext, shifted, m, taps, base, cs):
    src = ext if m == 0 else shifted.at[m - 1]
    return src[pl.ds(base, RC + 8 * max(q for _, q in taps)), cs]


def _tap_sum(out_ref, bias_ref, w_ref, ext, shifted, offs, tm):
    groups = _by_residue(offs)

    def chunk(j, carry):
        base = pl.multiple_of(j * RC, RC)
        for c in range(D // LANES):
            cs = pl.ds(c * LANES, LANES)
            acc = jnp.zeros((RC, LANES), F32)
            if bias_ref is not None:
                acc = acc + bias_ref[:, cs]
            for m, taps in groups.items():
                big = _residue_window(ext, shifted, m, taps, base, cs)
                for k, q in taps:
                    acc = acc + w_ref[pl.ds(k, 1), cs] * big[8 * q:8 * q + RC]
            out_ref[pl.ds(base, RC), cs] = acc
        return carry

    lax.fori_loop(0, tm // RC, chunk, 0)


def _tap_corr(dw_ref, dc_ext, ext, shifted, offs, tm):
    groups = _by_residue(offs)
    for c in range(D // LANES):
        cs = pl.ds(c * LANES, LANES)

        def chunk(j, accs, cs=cs):
            base = pl.multiple_of(j * RC, RC)
            dcv = dc_ext[pl.ds(base, RC), cs]
            out = list(accs)
            for m, taps in groups.items():
                big = _residue_window(ext, shifted, m, taps, base, cs)
                for k, q in taps:
                    prod = dcv * big[8 * q:8 * q + RC]
                    part = prod[0:8]
                    for s in range(1, RC // 8):
                        part = part + prod[8 * s:8 * s + 8]
                    out[k] = accs[k] + part
            return tuple(out)

        accs = lax.fori_loop(0, tm // RC, chunk, tuple(jnp.zeros((8, LANES), F32) for _ in offs))
        for k in range(len(offs)):
            dw_ref[pl.ds(k, 1), cs] += jnp.sum(accs[k], axis=0, keepdims=True)


def _conv_fwd(name, p, conv_w, conv_b, ln_g, ln_b, tm):
    t = p.shape[0]

    def body(av_ref, ag_ref, w_ref, cb_ref, lg_ref, lb_ref, c_ref, a_ref, ext, shifted):
        i = pl.program_id(0)

        @pl.when(i == 0)
        def _():
            ext[pl.ds(0, HALO), :] = jnp.zeros((HALO, D), F32)

        ext[pl.ds(HALO, tm), :] = av_ref[...].astype(F32) * _sigmoid(ag_ref[...].astype(F32))
        _shift_copies(ext, shifted, tm)
        _tap_sum(c_ref, cb_ref, w_ref, ext, shifted, [HALO - (CW - 1) + k for k in range(CW)], tm)
        rstd, chat = _ln_stats(c_ref[...])
        ca = chat * lg_ref[...] + lb_ref[...]
        a_ref[...] = (ca * _sigmoid(ca)).astype(BF16)
        ext[pl.ds(0, HALO), :] = ext[pl.ds(tm, HALO), :]

    return pl.pallas_call(
        body, name=name, grid=(t // tm,),
        in_specs=[_row(tm, D, 0), _row(tm, D, 1), _const((HALO, D)), _const((1, D)),
                  _const((1, D)), _const((1, D))],
        out_specs=[_row(tm, D), _row(tm, D)],
        out_shape=[_sds((t, D), F32), _sds((t, D), BF16)],
        scratch_shapes=[pltpu.VMEM((tm + HALO, D), F32), pltpu.VMEM((7, tm + HALO - 8, D), F32)],
        compiler_params=_params("arbitrary"),
    )(p, p, conv_w, conv_b, ln_g, ln_b)


def _sgu_mask():
    rows = lax.broadcasted_iota(jnp.int32, (CHUNK, CHUNK), 0)
    cols = lax.broadcasted_iota(jnp.int32, (CHUNK, CHUNK), 1)
    return cols <= rows


def _sgu_fwd(name, p, ln_g, ln_b, sgu_w, bias_full, tm):
    t = p.shape[0]

    def body(bu_ref, bv_ref, lg_ref, lb_ref, ws_ref, bias_ref, o_ref):
        mask = _sgu_mask()
        _, vhat = _ln_stats(_gelu(bv_ref[...].astype(F32)))
        vn = (vhat * lg_ref[...] + lb_ref[...]).astype(BF16)
        ub = _gelu(bu_ref[...].astype(F32))
        for g in range(GROUPS):
            wm = jnp.where(mask, ws_ref[g], 0.0).astype(BF16)
            cs = slice(g * GD, (g + 1) * GD)
            for cc in range(tm // CHUNK):
                rs = slice(cc * CHUNK, (cc + 1) * CHUNK)
                mixed = _dot(wm, vn[rs, cs]) + bias_ref[:, cs]
                o_ref[rs, cs] = (ub[rs, cs] * mixed).astype(BF16)

    return pl.pallas_call(
        body, name=name, grid=(t // tm,),
        in_specs=[_row(tm, D, 2), _row(tm, D, 3), _const((1, D)), _const((1, D)),
                  _const((GROUPS, CHUNK, CHUNK)), _const((CHUNK, D))],
        out_specs=_row(tm, D), out_shape=_sds((t, D), BF16),
        compiler_params=_params("parallel"),
    )(p, p, ln_g, ln_b, sgu_w, bias_full)


def _merge_fwd(name, act_a, act_b, p, w_a, w_b, w_out, h, gain, tm):
    t = h.shape[0]

    def body(a_ref, b_ref, ga_ref, gb_ref, wa_ref, wb_ref, wo_ref, h_ref, g_ref,
             ya_ref, yb_ref, mg_ref, ho_ref, xn_ref):
        ya = _dot(a_ref[...], wa_ref[...])
        yb = _dot(b_ref[...], wb_ref[...])
        ya_ref[...] = ya.astype(BF16)
        yb_ref[...] = yb.astype(BF16)
        merged = (_sigmoid(ga_ref[...].astype(F32)) * ya
                  + _sigmoid(gb_ref[...].astype(F32)) * yb).astype(BF16)
        mg_ref[...] = merged
        hn = h_ref[...] + _dot(merged, wo_ref[...])
        ho_ref[...] = hn
        _, xh = _rms_stats(hn)
        xn_ref[...] = (xh * g_ref[...]).astype(BF16)

    rb = _row(tm, D)
    return pl.pallas_call(
        body, name=name, grid=(t // tm,),
        in_specs=[rb, rb, _row(tm, D, 4), _row(tm, D, 5), _const((D, D)), _const((D, D)),
                  _const((D, D)), rb, _const((1, D))],
        out_specs=[rb] * 5,
        out_shape=[_sds((t, D), BF16)] * 3 + [_sds((t, D), F32), _sds((t, D), BF16)],
        compiler_params=_params("parallel"),
    )(act_a, act_b, p, p, w_a, w_b, w_out, h, gain)


def _kv_fwd(name, mem, gain, wkv_t):
    def body(m_ref, g_ref, w_ref, mn_ref, k_ref, v_ref):
        _, xh = _rms_stats(m_ref[...])
        mn = (xh * g_ref[...]).astype(BF16)
        mn_ref[...] = mn
        kv = _dot_nt(mn, w_ref[...])
        k_ref[...] = kv[:, :D].astype(BF16)
        v_ref[...] = kv[:, D:].astype(BF16)

    return pl.pallas_call(
        body, name=name,
        out_shape=[_sds((NMEM, D), BF16)] * 3,
        compiler_params=pltpu.CompilerParams(vmem_limit_bytes=VMEM_LIMIT),
    )(mem, gain, wkv_t)


def _softmax_rows(s):
    e = jnp.exp(s - jnp.max(s, axis=-1, keepdims=True))
    return e / jnp.sum(e, axis=-1, keepdims=True)


def _attn_fwd(name, xq, w_q, kb, vb, w_o, h, gain, tm):
    t = h.shape[0]
    scale = 1.0 / math.sqrt(HD)

    def body(x_ref, wq_ref, k_ref, v_ref, wo_ref, h_ref, g_ref, q_ref, o_ref, ho_ref, xn_ref):
        q_ref[...] = _dot(x_ref[...], wq_ref[...]).astype(BF16)
        for hd in range(HEADS):
            cs = slice(hd * HD, (hd + 1) * HD)
            p = _softmax_rows(_dot_nt(q_ref[:, cs], k_ref[:, cs]) * scale)
            o_ref[:, cs] = _dot(p.astype(BF16), v_ref[:, cs]).astype(BF16)
        hn = h_ref[...] + _dot(o_ref[...], wo_ref[...])
        ho_ref[...] = hn
        _, xh = _rms_stats(hn)
        xn_ref[...] = (xh * g_ref[...]).astype(BF16)

    rb = _row(tm, D)
    return pl.pallas_call(
        body, name=name, grid=(t // tm,),
        in_specs=[rb, _const((D, D)), _const((NMEM, D)), _const((NMEM, D)), _const((D, D)), rb,
                  _const((1, D))],
        out_specs=[rb] * 4,
        out_shape=[_sds((t, D), BF16), _sds((t, D), BF16), _sds((t, D), F32), _sds((t, D), BF16)],
        compiler_params=_params("parallel"),
    )(xq, w_q, kb, vb, w_o, h, gain)


def _ffn_down_loss(name, a, wd, h, gain, target, tm):
    t = h.shape[0]
    steps = t // tm

    def body(a_ref, w_ref, h_ref, g_ref, t_ref, dh_ref, dhb_ref, loss_ref, dg_ref, lacc):
        i = pl.program_id(0)
        hv = h_ref[...] + 0.5 * _dot(a_ref[...], w_ref[...])
        r, xh = _rms_stats(hv)
        err = xh * g_ref[...] - t_ref[...]
        _acc(lacc, i == 0, jnp.sum(err * err, axis=0, keepdims=True))
        dy = err * (1.0 / D)
        _acc(dg_ref, i == 0, jnp.sum(dy * xh, axis=0, keepdims=True))
        dxh = dy * g_ref[...]
        dh = r * (dxh - xh * jnp.mean(dxh * xh, axis=-1, keepdims=True))
        dh_ref[...] = dh
        dhb_ref[...] = dh.astype(BF16)

        @pl.when(i == steps - 1)
        def _():
            loss_ref[...] = jnp.zeros((8, 128), F32) + (0.5 / D) * jnp.sum(lacc[...])

    rb = _row(tm, D)
    return pl.pallas_call(
        body, name=name, grid=(steps,),
        in_specs=[_row(tm, DFF), _const((DFF, D)), rb, _const((1, D)), rb],
        out_specs=[rb, rb, _const((8, 128)), _const((1, D))],
        out_shape=[_sds((t, D), F32), _sds((t, D), BF16), _sds((8, 128), F32), _sds((1, D), F32)],
        scratch_shapes=[pltpu.VMEM((1, D), F32)],
        compiler_params=_params("arbitrary"),
    )(a, wd, h, gain, target)


def _ffn_bwd_act(name, dhb, wd, s, f, tm, tn, hosted=None):
    t = dhb.shape[0]

    def body(d_ref, w_ref, s_ref, f_ref, dg_ref, du_ref):
        da = 0.5 * _dot_nt(d_ref[...], w_ref[...])
        dg_ref[...] = (da * f_ref[...].astype(F32)).astype(BF16)
        du_ref[...] = (da * s_ref[...].astype(F32)).astype(BF16)

    o = pl.BlockSpec((tm, tn), lambda j, i: (i, j))
    return _call(name, body, (DFF // tn, t // tm),
                 [pl.BlockSpec((tm, D), lambda j, i: (i, 0)),
                  pl.BlockSpec((tn, D), lambda j, i: (j, 0)), o, o],
                 [o, o], [_sds((t, DFF), BF16)] * 2, (dhb, wd, s, f),
                 sem=("parallel", "parallel"), hosted=hosted)


def _dx_rms_bwd(name, pairs, h, gain, dh_in, tm, hosted=None):
    t = h.shape[0]
    np_ = len(pairs)

    def body(*refs):
        a_refs = refs[:np_]
        b_refs = refs[np_:2 * np_]
        h_ref, g_ref, d_ref, o_ref, ob_ref, dg_ref = refs[2 * np_:]
        dxn = None
        for (a_ref, b_ref, pr) in zip(a_refs, b_refs, pairs):
            y = _dot_nt(a_ref[...], b_ref[...]) if pr[4] else _dot(a_ref[...], b_ref[...])
            dxn = y if dxn is None else dxn + y
        dh, dgain = _rms_bwd(dxn, h_ref[...], g_ref[...])
        _acc(dg_ref, pl.program_id(0) == 0, dgain)
        out = d_ref[...] + dh
        o_ref[...] = out
        ob_ref[...] = out.astype(BF16)

    ins, args = [], []
    for (a, b, blk, rows, tr) in pairs:
        ins.append(_row(tm, a.shape[1]))
        args.append(a)
    for (a, b, blk, rows, tr) in pairs:
        ins.append(pl.BlockSpec((rows, b.shape[1]), lambda i, _b=blk: (_b, 0)))
        args.append(b)
    rb = _row(tm, D)
    ins += [rb, _const((1, D)), rb]
    args += [h, gain, dh_in]
    return _call(name, body, (t // tm,), ins, [rb, rb, _const((1, D))],
                 [_sds((t, D), F32), _sds((t, D), BF16), _sds((1, D), F32)], args,
                 sem=("arbitrary",), hosted=hosted)


class _Dest:
    def __init__(self, buf, total_rows, off, rows, row0=0):
        self.buf, self.total_rows, self.off, self.rows, self.row0 = buf, total_rows, off, rows, row0

    def segments(self, lo, hi):
        out = []
        for d in range(NDEV):
            a, b = max(lo + self.row0, d * self.rows), min(hi + self.row0, (d + 1) * self.rows)
            if a < b:
                out.append((a - self.row0 - lo, d, self.off + a - d * self.rows, b - a))
        return out


def _store_segments(stage, buf_ref, sems, segs):
    cps = [pltpu.make_async_copy(stage.at[pl.ds(s0, n)], buf_ref.at[d, pl.ds(r0, n)], sems.at[j])
           for j, (s0, d, r0, n) in enumerate(segs)]
    for cp in cps:
        cp.start()
    for cp in cps:
        cp.wait()


def _mm_tn(name, a, b, scale, tmo, tk, dest):
    t, m = a.shape
    n = b.shape[1]
    tk = min(tk, t)
    steps = t // tk
    tiles = m // tmo
    seg_lists = [dest.segments(i * tmo, (i + 1) * tmo) for i in range(tiles)]
    fresh = dest.buf is None

    def body(*refs):
        a_ref, b_ref = refs[0], refs[1]
        buf_ref, acc, stage, sems = refs[-4:]
        i, k = pl.program_id(0), pl.program_id(1)
        _acc(acc, k == 0, _dot_tn(a_ref[...], b_ref[...]))

        @pl.when(k == steps - 1)
        def _():
            stage[...] = (acc[...] * scale).astype(BF16)
            for ti, segs in enumerate(seg_lists):
                @pl.when(i == ti)
                def _(segs=segs):
                    _store_segments(stage, buf_ref, sems, segs)

    ins = [pl.BlockSpec((tk, tmo), lambda i, k: (k, i)), pl.BlockSpec((tk, n), lambda i, k: (k, 0))]
    args = [a, b]
    if not fresh:
        ins.append(_ANY)
        args.append(dest.buf)
    return pl.pallas_call(
        body, name=name, grid=(tiles, steps), in_specs=ins, out_specs=_ANY,
        out_shape=_sds((NDEV, dest.total_rows, n), BF16),
        input_output_aliases={} if fresh else {2: 0},
        scratch_shapes=[pltpu.VMEM((tmo, n), F32), pltpu.VMEM((tmo, n), BF16),
                        pltpu.SemaphoreType.DMA((max(len(s) for s in seg_lists),))],
        compiler_params=_params("arbitrary", "arbitrary"),
    )(*args)


def _attn_bwd(name, dhb, w_o, qb, kb, vb, tm):
    t = dhb.shape[0]
    scale = 1.0 / math.sqrt(HD)

    def body(d_ref, wo_ref, q_ref, k_ref, v_ref, dq_ref, dk_ref, dv_ref, do_s):
        i = pl.program_id(0)

        @pl.when(i == 0)
        def _():
            dk_ref[...] = jnp.zeros_like(dk_ref)
            dv_ref[...] = jnp.zeros_like(dv_ref)

        do_s[...] = _dot_nt(d_ref[...], wo_ref[...]).astype(BF16)
        for hd in range(HEADS):
            cs = slice(hd * HD, (hd + 1) * HD)
            q = q_ref[:, cs]
            p = _softmax_rows(_dot_nt(q, k_ref[:, cs]) * scale)
            do = do_s[:, cs]
            dp = _dot_nt(do, v_ref[:, cs])
            ds = (p * (dp - jnp.sum(dp * p, axis=-1, keepdims=True)) * scale).astype(BF16)
            dq_ref[:, cs] = _dot(ds, k_ref[:, cs]).astype(BF16)
            dk_ref[:, cs] += _dot_tn(ds, q)
            dv_ref[:, cs] += _dot_tn(p.astype(BF16), do)

    rb = _row(tm, D)
    return pl.pallas_call(
        body, name=name, grid=(t // tm,),
        in_specs=[rb, _const((D, D)), rb, _const((NMEM, D)), _const((NMEM, D))],
        out_specs=[rb, _const((NMEM, D)), _const((NMEM, D))],
        out_shape=[_sds((t, D), BF16), _sds((NMEM, D), F32), _sds((NMEM, D), F32)],
        scratch_shapes=[pltpu.VMEM((tm, D), BF16)],
        compiler_params=_params("arbitrary"),
    )(dhb, w_o, qb, kb, vb)


def _kv_bwd(name, dk, dv, memn, wkv_t, mem, gain, dest):
    segs = dest.segments(0, 2 * D)
    vmem = pl.BlockSpec(memory_space=pltpu.VMEM)

    def body(dk_ref, dv_ref, mn_ref, w_ref, m_ref, g_ref, buf_in, buf_ref, dg_ref, stage, sems):
        dkb = dk_ref[...].astype(BF16)
        dvb = dv_ref[...].astype(BF16)
        mn = mn_ref[...]
        stage[pl.ds(0, D), :] = _dot_tn(dkb, mn).astype(BF16)
        stage[pl.ds(D, D), :] = _dot_tn(dvb, mn).astype(BF16)
        _store_segments(stage, buf_ref, sems, segs)
        dmn = _dot(dkb, w_ref[pl.ds(0, D), :]) + _dot(dvb, w_ref[pl.ds(D, D), :])
        _, xh = _rms_stats(m_ref[...])
        dg_ref[...] = jnp.sum(dmn * xh, axis=0, keepdims=True)

    return pl.pallas_call(
        body, name=name,
        in_specs=[vmem] * 6 + [_ANY], out_specs=[_ANY, vmem],
        out_shape=[_sds(dest.buf.shape, BF16), _sds((1, D), F32)],
        input_output_aliases={6: 0},
        scratch_shapes=[pltpu.VMEM((2 * D, D), BF16), pltpu.SemaphoreType.DMA((len(segs),))],
        compiler_params=pltpu.CompilerParams(vmem_limit_bytes=VMEM_LIMIT),
    )(dk, dv, memn, wkv_t, mem, gain, dest.buf)


def _merge_bwd(name, dhb, w_out, ya, yb, p, tm):
    t = dhb.shape[0]

    def body(d_ref, w_ref, ya_ref, yb_ref, ga_ref, gb_ref, dya_ref, dyb_ref, dp_ref, cs_ref):
        dm = _dot_nt(d_ref[...], w_ref[...])
        sa = _sigmoid(ga_ref[...].astype(F32))
        sb = _sigmoid(gb_ref[...].astype(F32))
        dya_ref[...] = (dm * sa).astype(BF16)
        dyb_ref[...] = (dm * sb).astype(BF16)
        dga = dm * ya_ref[...].astype(F32) * (sa * (1.0 - sa))
        dgb = dm * yb_ref[...].astype(F32) * (sb * (1.0 - sb))
        dp_ref[:, pl.ds(0, D)] = dga.astype(BF16)
        dp_ref[:, pl.ds(D, D)] = dgb.astype(BF16)
        first = pl.program_id(0) == 0

        @pl.when(first)
        def _():
            cs_ref[...] = jnp.zeros_like(cs_ref)
        cs_ref[:, pl.ds(0, D)] += jnp.sum(dga, axis=0, keepdims=True)
        cs_ref[:, pl.ds(D, D)] += jnp.sum(dgb, axis=0, keepdims=True)

    rb = _row(tm, D)
    return pl.pallas_call(
        body, name=name, grid=(t // tm,),
        in_specs=[rb, _const((D, D)), rb, rb, _row(tm, D, 4), _row(tm, D, 5)],
        out_specs=[rb, rb, _row(tm, 2 * D), _const((1, 2 * D))],
        out_shape=[_sds((t, D), BF16), _sds((t, D), BF16), _sds((t, 2 * D), BF16),
                   _sds((1, 2 * D), F32)],
        compiler_params=_params("arbitrary"),
    )(dhb, w_out, ya, yb, p, p)


def _conv_bwd(name, dya, w_a, c, p, conv_w, ln_g, ln_b, tm, hosted=None):
    t = dya.shape[0]
    steps = t // tm
    hb = tm // HALO

    def rev(i):
        return steps - 1 - i

    def body(dy_ref, wa_ref, c_ref, av_ref, ag_ref, avh_ref, agh_ref, w_ref, lg_ref, lb_ref,
             dp_ref, cs_ref, dw_ref, dcb_ref, dlg_ref, dlb_ref, dc_ext, a_ext, dc_sh, a_sh, da0_s):
        i = pl.program_id(0)
        first = i == 0

        @pl.when(first)
        def _():
            dc_ext[pl.ds(tm, HALO), :] = jnp.zeros((HALO, D), F32)
            dw_ref[...] = jnp.zeros_like(dw_ref)
            cs_ref[...] = jnp.zeros_like(cs_ref)

        d_act = _dot_nt(dy_ref[...], wa_ref[...])
        rstd, chat = _ln_stats(c_ref[...])
        ca = chat * lg_ref[...] + lb_ref[...]
        sc = _sigmoid(ca)
        dca = d_act * (sc * (1.0 + ca * (1.0 - sc)))
        _acc(dlg_ref, first, jnp.sum(dca * chat, axis=0, keepdims=True))
        _acc(dlb_ref, first, jnp.sum(dca, axis=0, keepdims=True))
        dc = _ln_bwd(dca, chat, rstd, lg_ref[...])
        _acc(dcb_ref, first, jnp.sum(dc, axis=0, keepdims=True))
        dc_ext[pl.ds(0, tm), :] = dc

        av = av_ref[...].astype(F32)
        sg = _sigmoid(ag_ref[...].astype(F32))
        a_ext[pl.ds(HALO, tm), :] = av * sg
        halo = avh_ref[...].astype(F32) * _sigmoid(agh_ref[...].astype(F32))
        a_ext[pl.ds(0, HALO), :] = jnp.where(i == steps - 1, 0.0, halo)

        _shift_copies(dc_ext, dc_sh, tm)
        _shift_copies(a_ext, a_sh, tm)
        _tap_sum(da0_s, None, w_ref, dc_ext, dc_sh, [CW - 1 - k for k in range(CW)], tm)
        _tap_corr(dw_ref, dc_ext, a_ext, a_sh, [HALO - (CW - 1) + k for k in range(CW)], tm)
        da0 = da0_s[...]
        dav = da0 * sg
        dag = da0 * av * (sg * (1.0 - sg))
        dp_ref[:, pl.ds(0, D)] = dav.astype(BF16)
        dp_ref[:, pl.ds(D, D)] = dag.astype(BF16)
        cs_ref[:, pl.ds(0, D)] += jnp.sum(dav, axis=0, keepdims=True)
        cs_ref[:, pl.ds(D, D)] += jnp.sum(dag, axis=0, keepdims=True)
        dc_ext[pl.ds(tm, HALO), :] = dc_ext[pl.ds(0, HALO), :]

    def rrow(cols, cb=0):
        return pl.BlockSpec((tm, cols), lambda i, _cb=cb: (rev(i), _cb))

    def halo_spec(cb):
        return pl.BlockSpec((HALO, D), lambda i, _cb=cb: (jnp.maximum(rev(i) * hb - 1, 0), _cb))

    return _call(
        name, body, (steps,),
        [rrow(D), _const((D, D)), rrow(D), rrow(D, 0), rrow(D, 1), halo_spec(0),
         halo_spec(1), _const((HALO, D)), _const((1, D)), _const((1, D))],
        [rrow(2 * D), _const((1, 2 * D)), _const((HALO, D)), _const((1, D)),
         _const((1, D)), _const((1, D))],
        [_sds((t, 2 * D), BF16), _sds((1, 2 * D), F32), _sds((HALO, D), F32),
         _sds((1, D), F32), _sds((1, D), F32), _sds((1, D), F32)],
        (dya, w_a, c, p, p, p, p, conv_w, ln_g, ln_b),
        scratch=[pltpu.VMEM((tm + HALO, D), F32), pltpu.VMEM((tm + HALO, D), F32),
                 pltpu.VMEM((7, tm + HALO - 8, D), F32), pltpu.VMEM((7, tm + HALO - 8, D), F32),
                 pltpu.VMEM((tm, D), F32)],
        sem=("arbitrary",), hosted=hosted)


def _sgu_bwd(name, dyb, w_b, p, ln_g, ln_b, sgu_w, bias_full, tm):
    t = dyb.shape[0]
    steps = t // tm

    def body(dy_ref, wb_ref, bu_ref, bv_ref, lg_ref, lb_ref, ws_ref, bias_ref,
             dp_ref, cs_ref, dws_ref, dsb_ref, dlg_ref, dlb_ref, dub_s, dvn_s, dbias_s):
        i = pl.program_id(0)
        first = i == 0
        mask = _sgu_mask()

        @pl.when(first)
        def _():
            dws_ref[...] = jnp.zeros_like(dws_ref)
            dbias_s[...] = jnp.zeros_like(dbias_s)
            cs_ref[...] = jnp.zeros_like(cs_ref)

        dob = _dot_nt(dy_ref[...], wb_ref[...])
        bu = bu_ref[...].astype(F32)
        bv = bv_ref[...].astype(F32)
        ub, ub_grad = _gelu_with_grad(bu)
        vb, vb_grad = _gelu_with_grad(bv)
        rstd, vhat = _ln_stats(vb)
        vn = (vhat * lg_ref[...] + lb_ref[...]).astype(BF16)
        for g in range(GROUPS):
            wm = jnp.where(mask, ws_ref[g], 0.0).astype(BF16)
            cs = slice(g * GD, (g + 1) * GD)
            for cc in range(tm // CHUNK):
                rs = slice(cc * CHUNK, (cc + 1) * CHUNK)
                vblk = vn[rs, cs]
                mixed = _dot(wm, vblk) + bias_ref[:, cs]
                dob_blk = dob[rs, cs]
                dub_s[rs, cs] = dob_blk * mixed
                dmixed = dob_blk * ub[rs, cs]
                dbias_s[:, cs] += dmixed
                dmb = dmixed.astype(BF16)
                dws_ref[g] += _dot_nt(dmb, vblk)
                dvn_s[rs, cs] = _dot_tn(wm, dmb)
        dbu = dub_s[...] * ub_grad
        dvn = dvn_s[...]
        _acc(dlg_ref, first, jnp.sum(dvn * vhat, axis=0, keepdims=True))
        _acc(dlb_ref, first, jnp.sum(dvn, axis=0, keepdims=True))
        dbv = _ln_bwd(dvn, vhat, rstd, lg_ref[...]) * vb_grad
        dp_ref[:, pl.ds(0, D)] = dbu.astype(BF16)
        dp_ref[:, pl.ds(D, D)] = dbv.astype(BF16)
        cs_ref[:, pl.ds(0, D)] += jnp.sum(dbu, axis=0, keepdims=True)
        cs_ref[:, pl.ds(D, D)] += jnp.sum(dbv, axis=0, keepdims=True)

        @pl.when(i == steps - 1)
        def _():
            lane = lax.broadcasted_iota(jnp.int32, (CHUNK, CHUNK), 1)
            dsb = jnp.zeros((CHUNK, CHUNK), F32)
            for g in range(GROUPS):
                dws_ref[g] = jnp.where(mask, dws_ref[g], 0.0)
                dsb = jnp.where(lane == g, jnp.sum(dbias_s[:, g * GD:(g + 1) * GD], axis=1, keepdims=True), dsb)
            dsb_ref[...] = dsb

    rb = _row(tm, D)
    return pl.pallas_call(
        body, name=name, grid=(steps,),
        in_specs=[rb, _const((D, D)), _row(tm, D, 2), _row(tm, D, 3), _const((1, D)), _const((1, D)),
                  _const((GROUPS, CHUNK, CHUNK)), _const((CHUNK, D))],
        out_specs=[_row(tm, 2 * D), _const((1, 2 * D)), _const((GROUPS, CHUNK, CHUNK)),
                   _const((CHUNK, CHUNK)), _const((1, D)), _const((1, D))],
        out_shape=[_sds((t, 2 * D), BF16), _sds((1, 2 * D), F32), _sds((GROUPS, CHUNK, CHUNK), F32),
                   _sds((CHUNK, CHUNK), F32), _sds((1, D), F32), _sds((1, D), F32)],
        scratch_shapes=[pltpu.VMEM((tm, D), F32), pltpu.VMEM((tm, D), F32), pltpu.VMEM((CHUNK, D), F32)],
        compiler_params=_params("arbitrary"),
    )(dyb, w_b, p, p, ln_g, ln_b, sgu_w, bias_full)


def _adam_math(w, g, m, v):
    m = B1 * m + (1.0 - B1) * g
    v = B2 * v + (1.0 - B2) * (g * g)
    m_hat = m / (1.0 - B1 ** STEP)
    v_hat = v / (1.0 - B2 ** STEP)
    delta = -LR * (m_hat / (jnp.sqrt(v_hat) + EPS_ADAM) + WD * w)
    return delta, m, v


def _adamw(name, w, g, m, v, tr):
    r, cdim = w.shape

    def body(w_ref, g_ref, m_ref, v_ref, d_ref, mo_ref, vo_ref):
        d, mn, vn = _adam_math(w_ref[...], g_ref[...], m_ref[...], v_ref[...])
        d_ref[...] = d
        mo_ref[...] = mn
        vo_ref[...] = vn

    blk = pl.BlockSpec((tr, cdim), lambda i: (i, 0))
    return pl.pallas_call(
        body, name=name, grid=(r // tr,), in_specs=[blk] * 4, out_specs=[blk] * 3,
        out_shape=[_sds((r, cdim), F32)] * 3, compiler_params=_params("parallel"),
    )(w, g, m, v)


def _adamw_small(name, w, g8, m, v):
    r, cdim = w.shape

    def body(w_ref, g_ref, m_ref, v_ref, go_ref, d_ref, mo_ref, vo_ref):
        g = g_ref[0]
        for k in range(1, NDEV):
            g = g + g_ref[k]
        go_ref[...] = g
        d, mn, vn = _adam_math(w_ref[...], g, m_ref[...], v_ref[...])
        d_ref[...] = d
        mo_ref[...] = mn
        vo_ref[...] = vn

    return pl.pallas_call(
        body, name=name, out_shape=[_sds((r, cdim), F32)] * 4,
        compiler_params=pltpu.CompilerParams(vmem_limit_bytes=VMEM_LIMIT),
    )(w, g8, m, v)


_BIG = [("ffn1_w_gu", 704, True), ("ffn1_w_down", 352, False), ("w_in", 768, True),
        ("w_a_out", 128, False), ("w_b_out", 128, False), ("w_out", 128, False),
        ("w_q", 128, False), ("w_kv", 256, True), ("w_o", 128, False),
        ("ffn2_w_gu", 704, True), ("ffn2_w_down", 352, False)]
_BIG_ROWS = sum(r for _, r, _ in _BIG)

_SMALL = [("ffn1_norm", 1), ("mix_norm", 1), ("b_in", 6), ("conv_w", HALO), ("conv_b", 1),
          ("conv_ln_g", 1), ("conv_ln_b", 1), ("sgu_ln_g", 1), ("sgu_ln_b", 1), ("sgu_w", 64),
          ("sgu_b", 1), ("xattn_norm", 1), ("mem_norm", 1), ("ffn2_norm", 1), ("final_norm", 1)]
_SMALL_ROWS = 120


def _pack_small(vals, my_dev):
    rows = []
    for name, nrows in _SMALL:
        a = vals[name].astype(F32)
        if name == "conv_w":
            if a.shape[-1] != D:
                slab = jnp.zeros((HALO, D), F32)
                a = lax.dynamic_update_slice(slab, jnp.pad(a.reshape(CW, -1), ((0, HALO - CW), (0, 0))),
                                             (0, my_dev * (D // NDEV)))
            else:
                a = jnp.pad(a.reshape(CW, D), ((0, HALO - CW), (0, 0)))
        elif name == "sgu_b":
            a = jnp.pad(a.reshape(1, -1), ((0, 0), (0, D - GROUPS * CHUNK)))
        else:
            a = a.reshape(nrows, D)
        rows.append(a)
    packed = jnp.concatenate(rows, axis=0)
    return jnp.pad(packed, ((0, _SMALL_ROWS - packed.shape[0]), (0, 0)))


def _unpack_small(packed, shapes, my_dev):
    out, off = {}, 0
    for name, nrows in _SMALL:
        a = packed[off:off + nrows]
        off += nrows
        if name == "conv_w":
            a = lax.dynamic_slice(a, (0, my_dev * (D // NDEV)), (CW, D // NDEV))
        elif name == "sgu_b":
            a = a[:, :GROUPS * CHUNK]
        out[name] = a.reshape(shapes[name])
    return out


def kernel(x, mem, ffn1_norm, ffn1_w_gu, ffn1_w_down, mix_norm, w_in, b_in, conv_w, conv_b, conv_ln_g, conv_ln_b, w_a_out, sgu_ln_g, sgu_ln_b, sgu_w, sgu_b, w_b_out, w_out, xattn_norm, mem_norm, w_q, w_kv, w_o, ffn2_norm, ffn2_w_gu, ffn2_w_down, final_norm, loss_target, m_ffn1_norm, m_ffn1_w_gu, m_ffn1_w_down, m_mix_norm, m_w_in, m_b_in, m_conv_w, m_conv_b, m_conv_ln_g, m_conv_ln_b, m_w_a_out, m_sgu_ln_g, m_sgu_ln_b, m_sgu_w, m_sgu_b, m_w_b_out, m_w_out, m_xattn_norm, m_mem_norm, m_w_q, m_w_kv, m_w_o, m_ffn2_norm, m_ffn2_w_gu, m_ffn2_w_down, m_final_norm, v_ffn1_norm, v_ffn1_w_gu, v_ffn1_w_down, v_mix_norm, v_w_in, v_b_in, v_conv_w, v_conv_b, v_conv_ln_g, v_conv_ln_b, v_w_a_out, v_sgu_ln_g, v_sgu_ln_b, v_sgu_w, v_sgu_b, v_w_b_out, v_w_out, v_xattn_norm, v_mem_norm, v_w_q, v_w_kv, v_w_o, v_ffn2_norm, v_ffn2_w_gu, v_ffn2_w_down, v_final_norm):
    env = dict(locals())
    names = [n for n, _, _ in _BIG] + [n for n, _ in _SMALL]
    w = {n: env[n] for n in names}
    mom = {n: env["m_" + n] for n in names}
    vel = {n: env["v_" + n] for n in names}

    ax, ay, ac = lax.axis_index("x"), lax.axis_index("y"), lax.axis_index("c")
    my_chip = 2 * ax + ay
    my_dev = 2 * my_chip + ac

    t = x.shape[1]
    tm = min(512, t)
    tm_big = min(1024, t)
    tm_s = min(512, t)
    tm_c = min(256, t)
    xs = x.reshape(t, D)
    tgt = loss_target.reshape(t, D)
    mem2 = mem.reshape(NMEM, D)

    first, mid, late = _BIG[:1], _BIG[1:6], _BIG[6:]

    def gathers(entries):
        return [_gather_comm((w[n][0].T if tr else w[n][0]).astype(BF16)) for n, _, tr in entries]

    def whole(gathered, entries):
        return {n: g.reshape(NDEV * rows, D) for g, (n, rows, _) in zip(gathered, entries)}

    conv_slab = lax.dynamic_update_slice(
        jnp.zeros((HALO, D), F32), jnp.pad(conv_w[0], ((0, HALO - CW), (0, 0))), (0, my_dev * (D // NDEV)))
    bias_full = jnp.repeat(sgu_b[0].T, GD, axis=1)
    b_in2 = b_in.reshape(1, 6 * D)

    (xn1,), (*full_first, conv_w8) = _rms_cast(
        "norm_x", xs, ffn1_norm, tm, hosted=gathers(first) + [_gather_comm(conv_slab)])
    conv_w_pad = jnp.sum(conv_w8, axis=0)
    wf = whole(full_first, first)
    (g1, u1, a1), full_mid = _ffn_up("ffn1_up", xn1, wf["ffn1_w_gu"], tm_big, 1408, hosted=gathers(mid))
    wf.update(whole(full_mid, mid))
    h1, n_mix = _ffn_down("ffn1_down", a1, wf["ffn1_w_down"], xs, mix_norm, tm)
    (p,), full_late = _mix_in("mix_in", n_mix, wf["w_in"], b_in2, tm_big, 1536, hosted=gathers(late))
    wf.update(whole(full_late, late))
    c_conv, act_a = _conv_fwd("conv_fwd", p, conv_w_pad, conv_b, conv_ln_g, conv_ln_b, tm_c)
    act_b = _sgu_fwd("sgu_fwd", p, sgu_ln_g, sgu_ln_b, sgu_w[0], bias_full, tm_s)
    ya, yb, merged, h2, xq = _merge_fwd("merge_fwd", act_a, act_b, p, wf["w_a_out"], wf["w_b_out"],
                                        wf["w_out"], h1, xattn_norm, tm_s)
    memn, kb, vb = _kv_fwd("kv_fwd", mem2, mem_norm, wf["w_kv"])
    qb, ob, h3, xn4 = _attn_fwd("attn_fwd", xq, wf["w_q"], kb, vb, wf["w_o"], h2, ffn2_norm, tm_s)
    (g2, u2, a2), _ = _ffn_up("ffn2_up", xn4, wf["ffn2_w_gu"], tm_big, 1408)
    dh4, dh4b, loss_blk, d_final = _ffn_down_loss("ffn2_down_loss", a2, wf["ffn2_w_down"], h3,
                                                  final_norm.reshape(1, D), tgt, tm)

    gs = {}
    gs["final_norm"] = d_final

    core = ac.astype(jnp.int32).reshape(1)
    chip = my_chip.astype(jnp.int32).reshape(1)
    last_g, mixer_g, attn_g = _BIG[:2], _BIG[2:6], _BIG[6:]

    def layout(entries):
        offs, off = {}, 0
        for n, rows, _ in entries:
            offs[n] = (off, rows)
            off += rows
        return offs, off

    def dest(group, buf, name, row0=0):
        offs, total = group
        return _Dest(buf, total, offs[name][0], offs[name][1], row0)

    lay_last, lay_mixer, lay_attn = layout(last_g), layout(mixer_g), layout(attn_g)

    def ffn_bwd(tag, dhb, dh, g, u, a, xn, h_in, gain, wgu_t, wd, group, buf, act_hosted=None, dx_hosted=None):
        (dg, du), act_out = _ffn_bwd_act(tag + "_bwd_act", dhb, wd, g, u, tm_big, 1408, hosted=act_hosted)
        buf = _mm_tn(tag + "_dw_down", a, dhb, 0.5, 1408, TK, dest(group, buf, tag + "_w_down"))
        buf = _mm_tn(tag + "_dw_gate", dg, xn, 1.0, 1408, TK, dest(group, buf, tag + "_w_gu"))
        buf = _mm_tn(tag + "_dw_up", du, xn, 1.0, 1408, TK, dest(group, buf, tag + "_w_gu", DFF))
        (dh_o, dhb_o, dgain), dx_out = _dx_rms_bwd(
            tag + "_bwd_dx", [(dg, wgu_t, 0, DFF, False), (du, wgu_t, 1, DFF, False)], h_in, gain, dh, tm_s,
            hosted=dx_hosted(buf) if dx_hosted else None)
        return dh_o, dhb_o, dgain, buf, act_out, dx_out

    dh3, dh3b, gs["ffn2_norm"], g_attn, _, _ = ffn_bwd(
        "ffn2", dh4b, dh4, g2, u2, a2, xn4, h3, ffn2_norm, wf["ffn2_w_gu"], wf["ffn2_w_down"], lay_attn, None)

    g_attn = _mm_tn("dw_o", ob, dh3b, 1.0, 1024, 2 * TK, dest(lay_attn, g_attn, "w_o"))
    dq, dk, dv = _attn_bwd("attn_bwd", dh3b, wf["w_o"], qb, kb, vb, tm_s)
    g_attn, gs["mem_norm"] = _kv_bwd("kv_bwd", dk, dv, memn, wf["w_kv"], mem2, mem_norm,
                                     dest(lay_attn, g_attn, "w_kv"))
    g_attn = _mm_tn("dw_q", xq, dq, 1.0, 1024, 2 * TK, dest(lay_attn, g_attn, "w_q"))

    g4_attn = g_attn.reshape(4, 2, lay_attn[1], D)
    (dh2, dh2b, gs["xattn_norm"]), (sib_attn,) = _dx_rms_bwd(
        "attn_bwd_dx", [(dq, wf["w_q"], 0, D, True)], h2, xattn_norm, dh3, tm_s,
        hosted=_pair_exchange_comm(g4_attn))
    part_attn = _pair_sum("grads_pair_sum_attn", g4_attn, sib_attn, core, 784)

    g_mixer = _mm_tn("dw_out", merged, dh2b, 1.0, 1024, 2 * TK, dest(lay_mixer, None, "w_out"))
    dya, dyb, dp_g, cs_g = _merge_bwd("merge_bwd", dh2b, wf["w_out"], ya, yb, p, tm_s)
    g_mixer = _mm_tn("dw_a", act_a, dya, 1.0, 1024, 2 * TK, dest(lay_mixer, g_mixer, "w_a_out"))
    g_mixer = _mm_tn("dw_b", act_b, dyb, 1.0, 1024, 2 * TK, dest(lay_mixer, g_mixer, "w_b_out"))
    (dp_a, cs_a, d_convw, gs["conv_b"], gs["conv_ln_g"], gs["conv_ln_b"]), (chips_attn,) = _conv_bwd(
        "conv_bwd", dya, wf["w_a_out"], c_conv, p, conv_w_pad, conv_ln_g, conv_ln_b, tm_c,
        hosted=_chip_exchange_comm(part_attn))
    gsum_attn = _chip_sum("grads_chip_sum_attn", part_attn, chips_attn, chip, 784)
    dp_b, cs_b, d_sguw, d_sgub, gs["sgu_ln_g"], gs["sgu_ln_b"] = _sgu_bwd(
        "sgu_bwd", dyb, wf["w_b_out"], p, sgu_ln_g, sgu_ln_b, sgu_w[0], bias_full, tm_s)
    gs["conv_w"] = d_convw[:CW].reshape(1, CW, D)
    gs["sgu_w"] = d_sguw
    gs["sgu_b"] = d_sgub[:, :GROUPS].T
    gs["b_in"] = jnp.concatenate([cs_a, cs_b, cs_g], axis=1)
    for j, (tag, dpart) in enumerate((("a", dp_a), ("b", dp_b), ("g", dp_g))):
        g_mixer = _mm_tn("dw_in_" + tag, dpart, n_mix, 1.0, 1024, 2 * TK,
                         dest(lay_mixer, g_mixer, "w_in", 2 * D * j))
    g4_mixer = g_mixer.reshape(4, 2, lay_mixer[1], D)
    (dh1, dh1b, gs["mix_norm"]), (sib_mixer,) = _dx_rms_bwd(
        "mix_bwd_dx", [(dp_a, wf["w_in"], 0, 2 * D, False), (dp_b, wf["w_in"], 1, 2 * D, False),
                       (dp_g, wf["w_in"], 2, 2 * D, False)], h1, mix_norm, dh2, tm_s,
        hosted=_pair_exchange_comm(g4_mixer))
    part_mixer = _pair_sum("grads_pair_sum_mixer", g4_mixer, sib_mixer, core, 576)

    dx, _, gs["ffn1_norm"], _, (chips_mixer,), (last_slots,) = ffn_bwd(
        "ffn1", dh1b, dh1, g1, u1, a1, xn1, xs, ffn1_norm, wf["ffn1_w_gu"], wf["ffn1_w_down"], lay_last, None,
        act_hosted=_chip_exchange_comm(part_mixer), dx_hosted=_all_to_all_comm)
    gsum_mixer = _chip_sum("grads_chip_sum_mixer", part_mixer, chips_mixer, chip, 576)
    gsum_last = _slot_sum("grads_slot_sum_ffn1", last_slots, 528)

    grads, deltas, new_m, new_v = {}, {}, {}, {}
    for entries, gsum in ((last_g, gsum_last), (mixer_g, gsum_mixer), (attn_g, gsum_attn)):
        off = 0
        for name, rows, transposed in entries:
            gsh = gsum[off:off + rows]
            off += rows
            gsh = gsh.T if transposed else gsh
            d, mo, vo = _adamw("adamw_" + name, w[name][0], gsh, mom[name][0], vel[name][0], gsh.shape[0] // 2)
            grads[name], deltas[name], new_m[name], new_v[name] = gsh[None], d[None], mo[None], vo[None]

    shapes = {n: w[n].shape for n, _ in _SMALL}
    (g8,) = _run_comm("gather_small_grads", _gather_comm(_pack_small(gs, my_dev)))
    sg, sd, sm, sv = _adamw_small("adamw_small", _pack_small(w, my_dev), g8,
                                  _pack_small(mom, my_dev), _pack_small(vel, my_dev))
    for dst, src in ((grads, sg), (deltas, sd), (new_m, sm), (new_v, sv)):
        dst.update(_unpack_small(src, shapes, my_dev))

    loss = lax.psum(loss_blk[0, 0], AXES)
    order = ["ffn1_norm", "ffn1_w_gu", "ffn1_w_down", "mix_norm", "w_in", "b_in", "conv_w", "conv_b",
             "conv_ln_g", "conv_ln_b", "w_a_out", "sgu_ln_g", "sgu_ln_b", "sgu_w", "sgu_b", "w_b_out",
             "w_out", "xattn_norm", "mem_norm", "w_q", "w_kv", "w_o", "ffn2_norm", "ffn2_w_gu",
             "ffn2_w_down", "final_norm"]
    return (loss, dx.reshape(x.shape), *[grads[n] for n in order], *[deltas[n] for n in order],
            *[new_m[n] for n in order], *[new_v[n] for n in order])
```

```python
import functools
import math

import jax
import jax.numpy as jnp
from jax import lax
from jax.experimental import pallas as pl
from jax.experimental.pallas import tpu as pltpu

F32 = jnp.float32
BF16 = jnp.bfloat16
MESH = pl.DeviceIdType.MESH
AXES = ("x", "y", "c")

D = 1024
DFF = 2816
NMEM = 256
HEADS = 4
HD = D // HEADS
CW = 31
HALO = 32
CHUNK = 128
GROUPS = 4
GD = D // GROUPS
EPS_RMS = 1e-6
EPS_LN = 1e-5
LR, B1, B2, EPS_ADAM, WD, STEP = 0.001, 0.9, 0.999, 1e-08, 0.01, 10
NDEV = 8
VMEM_LIMIT = 56 * 1024 * 1024
TK = 2048


def _params(*sem):
    return pltpu.CompilerParams(dimension_semantics=sem, vmem_limit_bytes=VMEM_LIMIT)


def _dot(a, b):
    return jnp.dot(a, b, preferred_element_type=F32)


def _dot_nt(a, b):
    return lax.dot_general(a, b, (((1,), (1,)), ((), ())), preferred_element_type=F32)


def _dot_tn(a, b):
    return lax.dot_general(a, b, (((0,), (0,)), ((), ())), preferred_element_type=F32)


def _sigmoid(x):
    return 0.5 * jnp.tanh(0.5 * x) + 0.5


_GELU_C = math.sqrt(2.0 / math.pi)


def _gelu_with_grad(x):
    x2 = x * x
    t = jnp.tanh(_GELU_C * (x + 0.044715 * (x2 * x)))
    half = 0.5 * (1.0 + t)
    return x * half, half + 0.5 * x * (1.0 - t * t) * (_GELU_C * (1.0 + 3.0 * 0.044715 * x2))


def _gelu(x):
    return _gelu_with_grad(x)[0]


def _rms_stats(h):
    r = lax.rsqrt(jnp.mean(h * h, axis=-1, keepdims=True) + EPS_RMS)
    return r, h * r


def _rms_bwd(dxn, h, gain):
    r, xh = _rms_stats(h)
    dgain = jnp.sum(dxn * xh, axis=0, keepdims=True)
    dxh = dxn * gain
    dh = r * (dxh - xh * jnp.mean(dxh * xh, axis=-1, keepdims=True))
    return dh, dgain


def _ln_stats(c):
    mu = jnp.mean(c, axis=-1, keepdims=True)
    xc = c - mu
    rstd = lax.rsqrt(jnp.mean(xc * xc, axis=-1, keepdims=True) + EPS_LN)
    return rstd, xc * rstd


def _ln_bwd(dy, xhat, rstd, g):
    dxh = dy * g
    return rstd * (dxh - jnp.mean(dxh, axis=-1, keepdims=True)
                   - xhat * jnp.mean(dxh * xhat, axis=-1, keepdims=True))


def _row(tm, cols, cb=0):
    return pl.BlockSpec((tm, cols), lambda i, _cb=cb: (i, _cb))


def _const(shape):
    n = len(shape)
    return pl.BlockSpec(shape, lambda *_: (0,) * n)


def _sds(shape, dtype):
    return jax.ShapeDtypeStruct(shape, dtype)


def _acc(ref, first, val):
    @pl.when(first)
    def _():
        ref[...] = jnp.zeros_like(ref)
    ref[...] += val


class _Comm:
    def __init__(self, args, out_shapes, scratch, start, finish, forward=None):
        self.args, self.out_shapes, self.scratch = args, out_shapes, scratch
        self.start, self.finish, self.forward = start, finish, forward


_ANY = pl.BlockSpec(memory_space=pl.ANY)


def _run_comm(name, comm):
    ni, no = len(comm.args), len(comm.out_shapes)

    def body(*refs):
        ins, outs, sems = refs[:ni], refs[ni:ni + no], refs[ni + no:]
        comm.start(ins, outs, sems)
        if comm.forward:
            comm.forward(ins, outs, sems)
        comm.finish(ins, outs, sems)

    return pl.pallas_call(
        body, name=name, out_shape=list(comm.out_shapes), in_specs=[_ANY] * ni, out_specs=[_ANY] * no,
        scratch_shapes=list(comm.scratch),
    )(*comm.args)


def _call(name, body, grid, in_specs, out_specs, out_shape, args, scratch=(), sem=None, hosted=None):
    n_in, n_out, n_scr = len(in_specs), len(out_specs), len(scratch)
    if not hosted:
        outs = pl.pallas_call(
            body, name=name, grid=grid, in_specs=list(in_specs), out_specs=list(out_specs),
            out_shape=list(out_shape), scratch_shapes=list(scratch), compiler_params=_params(*sem),
        )(*args)
        return outs, []
    comms = list(hosted) if isinstance(hosted, (list, tuple)) else [hosted]
    hi = sum(len(cm.args) for cm in comms)
    ho = sum(len(cm.out_shapes) for cm in comms)

    def wrapped(*refs):
        ins, h_in = refs[:n_in], refs[n_in:n_in + hi]
        o0 = n_in + hi
        outs, h_out = refs[o0:o0 + n_out], refs[o0 + n_out:o0 + n_out + ho]
        s0 = o0 + n_out + ho
        scr, h_sems = refs[s0:s0 + n_scr], refs[s0 + n_scr:]
        ids = [pl.program_id(a) for a in range(len(grid))]
        first = functools.reduce(jnp.logical_and, [i == 0 for i in ids])
        last = functools.reduce(jnp.logical_and, [i == g - 1 for i, g in zip(ids, grid)])
        parts, a0, b0, c0 = [], 0, 0, 0
        for cm in comms:
            na, nb, nc = len(cm.args), len(cm.out_shapes), len(cm.scratch)
            parts.append((cm, h_in[a0:a0 + na], h_out[b0:b0 + nb], h_sems[c0:c0 + nc]))
            a0, b0, c0 = a0 + na, b0 + nb, c0 + nc

        @pl.when(first)
        def _():
            for cm, ci, co, cs in parts:
                cm.start(ci, co, cs)

        body(*ins, *outs, *scr)

        step, total = ids[0], grid[0]
        for i, g in zip(ids[1:], grid[1:]):
            step, total = step * g + i, total * g

        @pl.when(step == (3 * total) // 4)
        def _():
            for cm, ci, co, cs in parts:
                if cm.forward:
                    cm.forward(ci, co, cs)

        @pl.when(last)
        def _():
            for cm, ci, co, cs in parts:
                cm.finish(ci, co, cs)

    res = pl.pallas_call(
        wrapped, name=name, grid=grid, in_specs=list(in_specs) + [_ANY] * hi,
        out_specs=list(out_specs) + [_ANY] * ho,
        out_shape=list(out_shape) + [s for cm in comms for s in cm.out_shapes],
        scratch_shapes=list(scratch) + [s for cm in comms for s in cm.scratch],
        compiler_params=_params(*(["arbitrary"] * len(grid))),
    )(*args, *[a for cm in comms for a in cm.args])
    return res[:n_out], res[n_out:]


def _gather_comm(blk):
    r, cdim = blk.shape

    def copies(x_ref, out_ref, send_sems, recv_sems, local_sem):
        x, y, c = lax.axis_index("x"), lax.axis_index("y"), lax.axis_index("c")
        me, sibling = (x, y, c), (x, y, 1 - c)
        chips = [(1 - x, y), (x, 1 - y), (1 - x, 1 - y)]

        def slot(px, py, pc):
            return out_ref.at[4 * px + 2 * py + pc]

        def copy(k, block, to, src=None):
            return pltpu.make_async_remote_copy(
                src_ref=slot(*block) if src is None else src, dst_ref=slot(*block),
                send_sem=send_sems.at[k], recv_sem=recv_sems.at[k],
                device_id=to, device_id_type=MESH)

        mine = pltpu.make_async_copy(x_ref, slot(*me), local_sem)
        first = [copy(0, me, sibling, src=x_ref)]
        first += [copy(1 + j, me, (*chip, c), src=x_ref) for j, chip in enumerate(chips)]
        passed = [copy(4 + j, (*chip, c), sibling) for j, chip in enumerate(chips)]
        landed = [copy(1 + j, (*chip, c), me) for j, chip in enumerate(chips)]
        from_sibling = [copy(0, sibling, me)] + [copy(4 + j, (*chip, 1 - c), me) for j, chip in enumerate(chips)]
        return mine, first, passed, landed, from_sibling

    def start(ins, outs, sems):
        mine, first, _, _, _ = copies(ins[0], outs[0], *sems)
        mine.start()
        for cp in first:
            cp.start()

    def forward(ins, outs, sems):
        _, _, passed, landed, _ = copies(ins[0], outs[0], *sems)
        for arrived, onward in zip(landed, passed):
            arrived.wait_recv()
            onward.start()

    def finish(ins, outs, sems):
        mine, first, passed, _, from_sibling = copies(ins[0], outs[0], *sems)
        for cp in from_sibling:
            cp.wait_recv()
        for cp in first + passed:
            cp.wait_send()
        mine.wait()

    return _Comm([blk], [_sds((NDEV, r, cdim), blk.dtype)],
                 [pltpu.SemaphoreType.DMA((7,)), pltpu.SemaphoreType.DMA((7,)), pltpu.SemaphoreType.DMA],
                 start, finish, forward)


def _exchange_comm(src, n, out_rows, make):
    r, cdim = out_rows

    def copies(src_ref, out_ref, send_sems, recv_sems):
        out = []
        for k in range(n):
            s, d, to = make(k, src_ref, out_ref)
            out.append(pltpu.make_async_remote_copy(
                src_ref=s, dst_ref=d, send_sem=send_sems.at[k], recv_sem=recv_sems.at[k],
                device_id=to, device_id_type=MESH))
        return out

    def start(ins, outs, sems):
        for cp in copies(ins[0], outs[0], *sems):
            cp.start()

    def finish(ins, outs, sems):
        cps = copies(ins[0], outs[0], *sems)
        for cp in cps:
            cp.wait_recv()
        for cp in cps:
            cp.wait_send()

    return _Comm([src], [_sds((n, r, cdim), src.dtype)],
                 [pltpu.SemaphoreType.DMA((n,)), pltpu.SemaphoreType.DMA((n,))], start, finish)


def _pair_exchange_comm(g4):
    def make(k, g_ref, out_ref):
        x, y, c = lax.axis_index("x"), lax.axis_index("y"), lax.axis_index("c")
        return g_ref.at[k, 1 - c], out_ref.at[k], (x, y, 1 - c)

    return _exchange_comm(g4, 4, g4.shape[2:], make)


def _chip_exchange_comm(part):
    def make(k, p_ref, out_ref):
        x, y, c = lax.axis_index("x"), lax.axis_index("y"), lax.axis_index("c")
        px = x if ((k + 1) >> 1) == 0 else 1 - x
        py = y if ((k + 1) & 1) == 0 else 1 - y
        return p_ref.at[2 * px + py], out_ref.at[k], (px, py, c)

    return _exchange_comm(part, 3, part.shape[1:], make)


def _all_to_all_comm(g8):
    _, r, cdim = g8.shape

    def copies(g_ref, out_ref, send_sems, recv_sems, local_sem):
        x, y, c = lax.axis_index("x"), lax.axis_index("y"), lax.axis_index("c")
        me = 4 * x + 2 * y + c
        remote = []
        for k in range(1, NDEV):
            px = 1 - x if k & 4 else x
            py = 1 - y if k & 2 else y
            pc = 1 - c if k & 1 else c
            remote.append(pltpu.make_async_remote_copy(
                src_ref=g_ref.at[4 * px + 2 * py + pc], dst_ref=out_ref.at[me],
                send_sem=send_sems.at[k - 1], recv_sem=recv_sems.at[k - 1],
                device_id=(px, py, pc), device_id_type=MESH))
        return pltpu.make_async_copy(g_ref.at[me], out_ref.at[me], local_sem), remote

    def start(ins, outs, sems):
        mine, remote = copies(ins[0], outs[0], *sems)
        mine.start()
        for cp in remote:
            cp.start()

    def finish(ins, outs, sems):
        mine, remote = copies(ins[0], outs[0], *sems)
        for cp in remote:
            cp.wait_recv()
        for cp in remote:
            cp.wait_send()
        mine.wait()

    return _Comm([g8], [_sds((NDEV, r, cdim), g8.dtype)],
                 [pltpu.SemaphoreType.DMA((NDEV - 1,)), pltpu.SemaphoreType.DMA((NDEV - 1,)),
                  pltpu.SemaphoreType.DMA], start, finish)


def _slot_sum(name, slots, tr):
    _, r, cdim = slots.shape

    def body(s_ref, o_ref):
        s = s_ref[0].astype(F32)
        for k in range(1, NDEV):
            s = s + s_ref[k].astype(F32)
        o_ref[...] = s

    return pl.pallas_call(
        body, name=name, grid=(r // tr,),
        in_specs=[pl.BlockSpec((NDEV, tr, cdim), lambda i: (0, i, 0))],
        out_specs=pl.BlockSpec((tr, cdim), lambda i: (i, 0)),
        out_shape=_sds((r, cdim), F32), compiler_params=_params("parallel"),
    )(slots)


def _pair_sum(name, g4, recv, core, tr):
    _, _, r, cdim = g4.shape

    def body(core_ref, a_ref, b_ref, o_ref):
        o_ref[...] = (a_ref[...].astype(F32) + b_ref[...].astype(F32)).astype(o_ref.dtype)

    return pl.pallas_call(
        body, name=name,
        grid_spec=pltpu.PrefetchScalarGridSpec(
            num_scalar_prefetch=1, grid=(4, r // tr),
            in_specs=[pl.BlockSpec((None, None, tr, cdim), lambda k, i, cr: (k, cr[0], i, 0)),
                      pl.BlockSpec((None, tr, cdim), lambda k, i, cr: (k, i, 0))],
            out_specs=pl.BlockSpec((None, tr, cdim), lambda k, i, cr: (k, i, 0))),
        out_shape=_sds((4, r, cdim), BF16),
        compiler_params=_params("parallel", "parallel"),
    )(core, g4, recv)


def _chip_sum(name, part, recv, chip, tr):
    _, r, cdim = part.shape

    def body(chip_ref, a_ref, b_ref, o_ref):
        s = a_ref[...].astype(F32)
        for k in range(3):
            s = s + b_ref[k].astype(F32)
        o_ref[...] = s

    return pl.pallas_call(
        body, name=name,
        grid_spec=pltpu.PrefetchScalarGridSpec(
            num_scalar_prefetch=1, grid=(r // tr,),
            in_specs=[pl.BlockSpec((None, tr, cdim), lambda i, cr: (cr[0], i, 0)),
                      pl.BlockSpec((3, tr, cdim), lambda i, cr: (0, i, 0))],
            out_specs=pl.BlockSpec((tr, cdim), lambda i, cr: (i, 0))),
        out_shape=_sds((r, cdim), F32),
        compiler_params=_params("parallel"),
    )(chip, part, recv)


def _rms_cast(name, h, gain, tm, hosted=None):
    t = h.shape[0]

    def body(h_ref, g_ref, o_ref):
        _, xh = _rms_stats(h_ref[...])
        o_ref[...] = (xh * g_ref[...]).astype(BF16)

    return _call(name, body, (t // tm,), [_row(tm, D), _const((1, D))], [_row(tm, D)],
                 [_sds((t, D), BF16)], (h, gain), sem=("parallel",), hosted=hosted)


def _ffn_up(name, xn, wgu_t, tm, tn, hosted=None):
    t = xn.shape[0]
    nh = DFF // tn

    def body(x_ref, wg_ref, wu_ref, s_ref, f_ref, a_ref):
        x = x_ref[...]
        g = _dot_nt(x, wg_ref[...])
        u = _dot_nt(x, wu_ref[...])
        sg = _sigmoid(g)
        s = g * sg
        s_ref[...] = s.astype(BF16)
        f_ref[...] = (u * (sg + s * (1.0 - sg))).astype(BF16)
        a_ref[...] = (s * u).astype(BF16)

    o = pl.BlockSpec((tm, tn), lambda j, i: (i, j))
    return _call(name, body, (nh, t // tm),
                 [pl.BlockSpec((tm, D), lambda j, i: (i, 0)),
                  pl.BlockSpec((tn, D), lambda j, i: (j, 0)),
                  pl.BlockSpec((tn, D), lambda j, i: (j + nh, 0))],
                 [o, o, o], [_sds((t, DFF), BF16)] * 3, (xn, wgu_t, wgu_t),
                 sem=("parallel", "parallel"), hosted=hosted)


def _ffn_down(name, a, wd, h, gain, tm):
    t = a.shape[0]

    def body(a_ref, w_ref, h_ref, g_ref, o_ref, n_ref):
        hn = h_ref[...] + 0.5 * _dot(a_ref[...], w_ref[...])
        o_ref[...] = hn
        _, xh = _rms_stats(hn)
        n_ref[...] = (xh * g_ref[...]).astype(BF16)

    return pl.pallas_call(
        body, name=name, grid=(t // tm,),
        in_specs=[_row(tm, DFF), _const((DFF, D)), _row(tm, D), _const((1, D))],
        out_specs=[_row(tm, D), _row(tm, D)],
        out_shape=[_sds((t, D), F32), _sds((t, D), BF16)],
        compiler_params=_params("parallel"),
    )(a, wd, h, gain)


def _mix_in(name, n, win_t, b_in, tm, tn, hosted=None):
    t = n.shape[0]

    def body(n_ref, w_ref, b_ref, p_ref):
        p_ref[...] = (_dot_nt(n_ref[...], w_ref[...]) + b_ref[...]).astype(BF16)

    return _call(name, body, (6 * D // tn, t // tm),
                 [pl.BlockSpec((tm, D), lambda j, i: (i, 0)),
                  pl.BlockSpec((tn, D), lambda j, i: (j, 0)),
                  pl.BlockSpec((1, tn), lambda j, i: (0, j))],
                 [pl.BlockSpec((tm, tn), lambda j, i: (i, j))], [_sds((t, 6 * D), BF16)],
                 (n, win_t, b_in), sem=("parallel", "parallel"), hosted=hosted)


RC = 64
LANES = 128


def _shift_copies(ext, shifted, tm):
    n = tm + HALO - 8
    for m in range(1, 8):
        shifted[m - 1] = ext[pl.ds(m, n), :]


def _by_residue(offs):
    groups = {}
    for k, off in enumerate(offs):
        q, m = divmod(off, 8)
        groups.setdefault(m, []).append((k, q))
    return groups


def _residue_window(ext, shifted, m, taps, base, cs):
    src = ext if m == 0 else shifted.at[m - 1]
    return src[pl.ds(base, RC + 8 * max(q for _, q in taps)), cs]


def _tap_sum(out_ref, bias_ref, w_ref, ext, shifted, offs, tm):
    groups = _by_residue(offs)

    def chunk(j, carry):
        base = pl.multiple_of(j * RC, RC)
        for c in range(D // LANES):
            cs = pl.ds(c * LANES, LANES)
            acc = jnp.zeros((RC, LANES), F32)
            if bias_ref is not None:
                acc = acc + bias_ref[:, cs]
            for m, taps in groups.items():
                big = _residue_window(ext, shifted, m, taps, base, cs)
                for k, q in taps:
                    acc = acc + w_ref[pl.ds(k, 1), cs] * big[8 * q:8 * q + RC]
            out_ref[pl.ds(base, RC), cs] = acc
        return carry

    lax.fori_loop(0, tm // RC, chunk, 0)


def _tap_corr(dw_ref, dc_ext, ext, shifted, offs, tm):
    groups = _by_residue(offs)
    for c in range(D // LANES):
        cs = pl.ds(c * LANES, LANES)

        def chunk(j, accs, cs=cs):
            base = pl.multiple_of(j * RC, RC)
            dcv = dc_ext[pl.ds(base, RC), cs]
            out = list(accs)
            for m, taps in groups.items():
                big = _residue_window(ext, shifted, m, taps, base, cs)
                for k, q in taps:
                    prod = dcv * big[8 * q:8 * q + RC]
                    part = prod[0:8]
                    for s in range(1, RC // 8):
                        part = part + prod[8 * s:8 * s + 8]
                    out[k] = accs[k] + part
            return tuple(out)

        accs = lax.fori_loop(0, tm // RC, chunk, tuple(jnp.zeros((8, LANES), F32) for _ in offs))
        for k in range(len(offs)):
            dw_ref[pl.ds(k, 1), cs] += jnp.sum(accs[k], axis=0, keepdims=True)


def _conv_fwd(name, p, conv_w, conv_b, ln_g, ln_b, tm):
    t = p.shape[0]

    def body(av_ref, ag_ref, w_ref, cb_ref, lg_ref, lb_ref, c_ref, a_ref, ext, shifted):
        i = pl.program_id(0)

        @pl.when(i == 0)
        def _():
            ext[pl.ds(0, HALO), :] = jnp.zeros((HALO, D), F32)

        ext[pl.ds(HALO, tm), :] = av_ref[...].astype(F32) * _sigmoid(ag_ref[...].astype(F32))
        _shift_copies(ext, shifted, tm)
        _tap_sum(c_ref, cb_ref, w_ref, ext, shifted, [HALO - (CW - 1) + k for k in range(CW)], tm)
        rstd, chat = _ln_stats(c_ref[...])
        ca = chat * lg_ref[...] + lb_ref[...]
        a_ref[...] = (ca * _sigmoid(ca)).astype(BF16)
        ext[pl.ds(0, HALO), :] = ext[pl.ds(tm, HALO), :]

    return pl.pallas_call(
        body, name=name, grid=(t // tm,),
        in_specs=[_row(tm, D, 0), _row(tm, D, 1), _const((HALO, D)), _const((1, D)),
                  _const((1, D)), _const((1, D))],
        out_specs=[_row(tm, D), _row(tm, D)],
        out_shape=[_sds((t, D), F32), _sds((t, D), BF16)],
        scratch_shapes=[pltpu.VMEM((tm + HALO, D), F32), pltpu.VMEM((7, tm + HALO - 8, D), F32)],
        compiler_params=_params("arbitrary"),
    )(p, p, conv_w, conv_b, ln_g, ln_b)


def _sgu_mask():
    rows = lax.broadcasted_iota(jnp.int32, (CHUNK, CHUNK), 0)
    cols = lax.broadcasted_iota(jnp.int32, (CHUNK, CHUNK), 1)
    return cols <= rows


def _sgu_fwd(name, p, ln_g, ln_b, sgu_w, bias_full, tm):
    t = p.shape[0]

    def body(bu_ref, bv_ref, lg_ref, lb_ref, ws_ref, bias_ref, o_ref):
        mask = _sgu_mask()
        _, vhat = _ln_stats(_gelu(bv_ref[...].astype(F32)))
        vn = (vhat * lg_ref[...] + lb_ref[...]).astype(BF16)
        ub = _gelu(bu_ref[...].astype(F32))
        for g in range(GROUPS):
            wm = jnp.where(mask, ws_ref[g], 0.0).astype(BF16)
            cs = slice(g * GD, (g + 1) * GD)
            for cc in range(tm // CHUNK):
                rs = slice(cc * CHUNK, (cc + 1) * CHUNK)
                mixed = _dot(wm, vn[rs, cs]) + bias_ref[:, cs]
                o_ref[rs, cs] = (ub[rs, cs] * mixed).astype(BF16)

    return pl.pallas_call(
        body, name=name, grid=(t // tm,),
        in_specs=[_row(tm, D, 2), _row(tm, D, 3), _const((1, D)), _const((1, D)),
                  _const((GROUPS, CHUNK, CHUNK)), _const((CHUNK, D))],
        out_specs=_row(tm, D), out_shape=_sds((t, D), BF16),
        compiler_params=_params("parallel"),
    )(p, p, ln_g, ln_b, sgu_w, bias_full)


def _merge_fwd(name, act_a, act_b, p, w_a, w_b, w_out, h, gain, tm):
    t = h.shape[0]

    def body(a_ref, b_ref, ga_ref, gb_ref, wa_ref, wb_ref, wo_ref, h_ref, g_ref,
             ya_ref, yb_ref, mg_ref, ho_ref, xn_ref):
        ya = _dot(a_ref[...], wa_ref[...])
        yb = _dot(b_ref[...], wb_ref[...])
        ya_ref[...] = ya.astype(BF16)
        yb_ref[...] = yb.astype(BF16)
        merged = (_sigmoid(ga_ref[...].astype(F32)) * ya
                  + _sigmoid(gb_ref[...].astype(F32)) * yb).astype(BF16)
        mg_ref[...] = merged
        hn = h_ref[...] + _dot(merged, wo_ref[...])
        ho_ref[...] = hn
        _, xh = _rms_stats(hn)
        xn_ref[...] = (xh * g_ref[...]).astype(BF16)

    rb = _row(tm, D)
    return pl.pallas_call(
        body, name=name, grid=(t // tm,),
        in_specs=[rb, rb, _row(tm, D, 4), _row(tm, D, 5), _const((D, D)), _const((D, D)),
                  _const((D, D)), rb, _const((1, D))],
        out_specs=[rb] * 5,
        out_shape=[_sds((t, D), BF16)] * 3 + [_sds((t, D), F32), _sds((t, D), BF16)],
        compiler_params=_params("parallel"),
    )(act_a, act_b, p, p, w_a, w_b, w_out, h, gain)


def _kv_fwd(name, mem, gain, wkv_t):
    def body(m_ref, g_ref, w_ref, mn_ref, k_ref, v_ref):
        _, xh = _rms_stats(m_ref[...])
        mn = (xh * g_ref[...]).astype(BF16)
        mn_ref[...] = mn
        kv = _dot_nt(mn, w_ref[...])
        k_ref[...] = kv[:, :D].astype(BF16)
        v_ref[...] = kv[:, D:].astype(BF16)

    return pl.pallas_call(
        body, name=name,
        out_shape=[_sds((NMEM, D), BF16)] * 3,
        compiler_params=pltpu.CompilerParams(vmem_limit_bytes=VMEM_LIMIT),
    )(mem, gain, wkv_t)


def _softmax_rows(s):
    e = jnp.exp(s - jnp.max(s, axis=-1, keepdims=True))
    return e / jnp.sum(e, axis=-1, keepdims=True)


def _attn_fwd(name, xq, w_q, kb, vb, w_o, h, gain, tm):
    t = h.shape[0]
    scale = 1.0 / math.sqrt(HD)

    def body(x_ref, wq_ref, k_ref, v_ref, wo_ref, h_ref, g_ref, q_ref, p_ref, o_ref, ho_ref, xn_ref):
        q_ref[...] = _dot(x_ref[...], wq_ref[...]).astype(BF16)
        for hd in range(HEADS):
            cs = slice(hd * HD, (hd + 1) * HD)
            ps = slice(hd * NMEM, (hd + 1) * NMEM)
            p_ref[:, ps] = _softmax_rows(_dot_nt(q_ref[:, cs], k_ref[:, cs]) * scale).astype(BF16)
            o_ref[:, cs] = _dot(p_ref[:, ps], v_ref[:, cs]).astype(BF16)
        hn = h_ref[...] + _dot(o_ref[...], wo_ref[...])
        ho_ref[...] = hn
        _, xh = _rms_stats(hn)
        xn_ref[...] = (xh * g_ref[...]).astype(BF16)

    rb = _row(tm, D)
    return pl.pallas_call(
        body, name=name, grid=(t // tm,),
        in_specs=[rb, _const((D, D)), _const((NMEM, D)), _const((NMEM, D)), _const((D, D)), rb,
                  _const((1, D))],
        out_specs=[rb, _row(tm, HEADS * NMEM), rb, rb, rb],
        out_shape=[_sds((t, D), BF16), _sds((t, HEADS * NMEM), BF16), _sds((t, D), BF16),
                   _sds((t, D), F32), _sds((t, D), BF16)],
        compiler_params=_params("parallel"),
    )(xq, w_q, kb, vb, w_o, h, gain)


def _ffn_down_loss(name, a, wd, h, gain, target, tm):
    t = h.shape[0]
    steps = t // tm

    def body(a_ref, w_ref, h_ref, g_ref, t_ref, dh_ref, dhb_ref, loss_ref, dg_ref, lacc):
        i = pl.program_id(0)
        hv = h_ref[...] + 0.5 * _dot(a_ref[...], w_ref[...])
        r, xh = _rms_stats(hv)
        err = xh * g_ref[...] - t_ref[...]
        _acc(lacc, i == 0, jnp.sum(err * err, axis=0, keepdims=True))
        dy = err * (1.0 / D)
        _acc(dg_ref, i == 0, jnp.sum(dy * xh, axis=0, keepdims=True))
        dxh = dy * g_ref[...]
        dh = r * (dxh - xh * jnp.mean(dxh * xh, axis=-1, keepdims=True))
        dh_ref[...] = dh
        dhb_ref[...] = dh.astype(BF16)

        @pl.when(i == steps - 1)
        def _():
            loss_ref[...] = jnp.zeros((8, 128), F32) + (0.5 / D) * jnp.sum(lacc[...])

    rb = _row(tm, D)
    return pl.pallas_call(
        body, name=name, grid=(steps,),
        in_specs=[_row(tm, DFF), _const((DFF, D)), rb, _const((1, D)), rb],
        out_specs=[rb, rb, _const((8, 128)), _const((1, D))],
        out_shape=[_sds((t, D), F32), _sds((t, D), BF16), _sds((8, 128), F32), _sds((1, D), F32)],
        scratch_shapes=[pltpu.VMEM((1, D), F32)],
        compiler_params=_params("arbitrary"),
    )(a, wd, h, gain, target)


def _ffn_bwd_act(name, dhb, wd, s, f, tm, tn, hosted=None):
    t = dhb.shape[0]

    def body(d_ref, w_ref, s_ref, f_ref, dg_ref, du_ref):
        da = 0.5 * _dot_nt(d_ref[...], w_ref[...])
        dg_ref[...] = (da * f_ref[...].astype(F32)).astype(BF16)
        du_ref[...] = (da * s_ref[...].astype(F32)).astype(BF16)

    o = pl.BlockSpec((tm, tn), lambda j, i: (i, j))
    return _call(name, body, (DFF // tn, t // tm),
                 [pl.BlockSpec((tm, D), lambda j, i: (i, 0)),
                  pl.BlockSpec((tn, D), lambda j, i: (j, 0)), o, o],
                 [o, o], [_sds((t, DFF), BF16)] * 2, (dhb, wd, s, f),
                 sem=("parallel", "parallel"), hosted=hosted)


def _dx_rms_bwd(name, pairs, h, gain, dh_in, tm, hosted=None):
    t = h.shape[0]
    np_ = len(pairs)

    def body(*refs):
        a_refs = refs[:np_]
        b_refs = refs[np_:2 * np_]
        h_ref, g_ref, d_ref, o_ref, ob_ref, dg_ref = refs[2 * np_:]
        dxn = None
        for (a_ref, b_ref, pr) in zip(a_refs, b_refs, pairs):
            y = _dot_nt(a_ref[...], b_ref[...]) if pr[4] else _dot(a_ref[...], b_ref[...])
            dxn = y if dxn is None else dxn + y
        dh, dgain = _rms_bwd(dxn, h_ref[...], g_ref[...])
        _acc(dg_ref, pl.program_id(0) == 0, dgain)
        out = d_ref[...] + dh
        o_ref[...] = out
        ob_ref[...] = out.astype(BF16)

    ins, args = [], []
    for (a, b, blk, rows, tr) in pairs:
        ins.append(_row(tm, a.shape[1]))
        args.append(a)
    for (a, b, blk, rows, tr) in pairs:
        ins.append(pl.BlockSpec((rows, b.shape[1]), lambda i, _b=blk: (_b, 0)))
        args.append(b)
    rb = _row(tm, D)
    ins += [rb, _const((1, D)), rb]
    args += [h, gain, dh_in]
    return _call(name, body, (t // tm,), ins, [rb, rb, _const((1, D))],
                 [_sds((t, D), F32), _sds((t, D), BF16), _sds((1, D), F32)], args,
                 sem=("arbitrary",), hosted=hosted)


class _Dest:
    def __init__(self, buf, total_rows, off, rows, row0=0):
        self.buf, self.total_rows, self.off, self.rows, self.row0 = buf, total_rows, off, rows, row0

    def segments(self, lo, hi):
        out = []
        for d in range(NDEV):
            a, b = max(lo + self.row0, d * self.rows), min(hi + self.row0, (d + 1) * self.rows)
            if a < b:
                out.append((a - self.row0 - lo, d, self.off + a - d * self.rows, b - a))
        return out


def _store_segments(stage, buf_ref, sems, segs):
    cps = [pltpu.make_async_copy(stage.at[pl.ds(s0, n)], buf_ref.at[d, pl.ds(r0, n)], sems.at[j])
           for j, (s0, d, r0, n) in enumerate(segs)]
    for cp in cps:
        cp.start()
    for cp in cps:
        cp.wait()


def _mm_tn(name, a, b, scale, tmo, tk, dest):
    t, m = a.shape
    n = b.shape[1]
    tk = min(tk, t)
    steps = t // tk
    tiles = m // tmo
    seg_lists = [dest.segments(i * tmo, (i + 1) * tmo) for i in range(tiles)]
    fresh = dest.buf is None

    def body(*refs):
        a_ref, b_ref = refs[0], refs[1]
        buf_ref, acc, stage, sems = refs[-4:]
        i, k = pl.program_id(0), pl.program_id(1)
        _acc(acc, k == 0, _dot_tn(a_ref[...], b_ref[...]))

        @pl.when(k == steps - 1)
        def _():
            stage[...] = (acc[...] * scale).astype(BF16)
            for ti, segs in enumerate(seg_lists):
                @pl.when(i == ti)
                def _(segs=segs):
                    _store_segments(stage, buf_ref, sems, segs)

    ins = [pl.BlockSpec((tk, tmo), lambda i, k: (k, i)), pl.BlockSpec((tk, n), lambda i, k: (k, 0))]
    args = [a, b]
    if not fresh:
        ins.append(_ANY)
        args.append(dest.buf)
    return pl.pallas_call(
        body, name=name, grid=(tiles, steps), in_specs=ins, out_specs=_ANY,
        out_shape=_sds((NDEV, dest.total_rows, n), BF16),
        input_output_aliases={} if fresh else {2: 0},
        scratch_shapes=[pltpu.VMEM((tmo, n), F32), pltpu.VMEM((tmo, n), BF16),
                        pltpu.SemaphoreType.DMA((max(len(s) for s in seg_lists),))],
        compiler_params=_params("arbitrary", "arbitrary"),
    )(*args)


def _attn_bwd(name, dhb, w_o, qb, pb, kb, vb, tm):
    t = dhb.shape[0]
    scale = 1.0 / math.sqrt(HD)

    def body(d_ref, wo_ref, q_ref, p_ref, k_ref, v_ref, dq_ref, dk_ref, dv_ref, do_s):
        i = pl.program_id(0)

        @pl.when(i == 0)
        def _():
            dk_ref[...] = jnp.zeros_like(dk_ref)
            dv_ref[...] = jnp.zeros_like(dv_ref)

        do_s[...] = _dot_nt(d_ref[...], wo_ref[...]).astype(BF16)
        for hd in range(HEADS):
            cs = slice(hd * HD, (hd + 1) * HD)
            pb16 = p_ref[:, hd * NMEM:(hd + 1) * NMEM]
            p = pb16.astype(F32)
            do = do_s[:, cs]
            dp = _dot_nt(do, v_ref[:, cs])
            ds = (p * (dp - jnp.sum(dp * p, axis=-1, keepdims=True)) * scale).astype(BF16)
            dq_ref[:, cs] = _dot(ds, k_ref[:, cs]).astype(BF16)
            dk_ref[:, cs] += _dot_tn(ds, q_ref[:, cs])
            dv_ref[:, cs] += _dot_tn(pb16, do)

    rb = _row(tm, D)
    return pl.pallas_call(
        body, name=name, grid=(t // tm,),
        in_specs=[rb, _const((D, D)), rb, _row(tm, HEADS * NMEM), _const((NMEM, D)), _const((NMEM, D))],
        out_specs=[rb, _const((NMEM, D)), _const((NMEM, D))],
        out_shape=[_sds((t, D), BF16), _sds((NMEM, D), F32), _sds((NMEM, D), F32)],
        scratch_shapes=[pltpu.VMEM((tm, D), BF16)],
        compiler_params=_params("arbitrary"),
    )(dhb, w_o, qb, pb, kb, vb)


def _kv_bwd(name, dk, dv, memn, wkv_t, mem, gain, dest):
    segs = dest.segments(0, 2 * D)
    vmem = pl.BlockSpec(memory_space=pltpu.VMEM)

    def body(dk_ref, dv_ref, mn_ref, w_ref, m_ref, g_ref, buf_in, buf_ref, dg_ref, stage, sems):
        dkb = dk_ref[...].astype(BF16)
        dvb = dv_ref[...].astype(BF16)
        mn = mn_ref[...]
        stage[pl.ds(0, D), :] = _dot_tn(dkb, mn).astype(BF16)
        stage[pl.ds(D, D), :] = _dot_tn(dvb, mn).astype(BF16)
        _store_segments(stage, buf_ref, sems, segs)
        dmn = _dot(dkb, w_ref[pl.ds(0, D), :]) + _dot(dvb, w_ref[pl.ds(D, D), :])
        _, xh = _rms_stats(m_ref[...])
        dg_ref[...] = jnp.sum(dmn * xh, axis=0, keepdims=True)

    return pl.pallas_call(
        body, name=name,
        in_specs=[vmem] * 6 + [_ANY], out_specs=[_ANY, vmem],
        out_shape=[_sds(dest.buf.shape, BF16), _sds((1, D), F32)],
        input_output_aliases={6: 0},
        scratch_shapes=[pltpu.VMEM((2 * D, D), BF16), pltpu.SemaphoreType.DMA((len(segs),))],
        compiler_params=pltpu.CompilerParams(vmem_limit_bytes=VMEM_LIMIT),
    )(dk, dv, memn, wkv_t, mem, gain, dest.buf)


def _merge_bwd(name, dhb, w_out, ya, yb, p, tm):
    t = dhb.shape[0]

    def body(d_ref, w_ref, ya_ref, yb_ref, ga_ref, gb_ref, dya_ref, dyb_ref, dp_ref, cs_ref):
        dm = _dot_nt(d_ref[...], w_ref[...])
        sa = _sigmoid(ga_ref[...].astype(F32))
        sb = _sigmoid(gb_ref[...].astype(F32))
        dya_ref[...] = (dm * sa).astype(BF16)
        dyb_ref[...] = (dm * sb).astype(BF16)
        dga = dm * ya_ref[...].astype(F32) * (sa * (1.0 - sa))
        dgb = dm * yb_ref[...].astype(F32) * (sb * (1.0 - sb))
        dp_ref[:, pl.ds(0, D)] = dga.astype(BF16)
        dp_ref[:, pl.ds(D, D)] = dgb.astype(BF16)
        first = pl.program_id(0) == 0

        @pl.when(first)
        def _():
            cs_ref[...] = jnp.zeros_like(cs_ref)
        cs_ref[:, pl.ds(0, D)] += jnp.sum(dga, axis=0, keepdims=True)
        cs_ref[:, pl.ds(D, D)] += jnp.sum(dgb, axis=0, keepdims=True)

    rb = _row(tm, D)
    return pl.pallas_call(
        body, name=name, grid=(t // tm,),
        in_specs=[rb, _const((D, D)), rb, rb, _row(tm, D, 4), _row(tm, D, 5)],
        out_specs=[rb, rb, _row(tm, 2 * D), _const((1, 2 * D))],
        out_shape=[_sds((t, D), BF16), _sds((t, D), BF16), _sds((t, 2 * D), BF16),
                   _sds((1, 2 * D), F32)],
        compiler_params=_params("arbitrary"),
    )(dhb, w_out, ya, yb, p, p)


def _conv_bwd(name, dya, w_a, c, p, conv_w, ln_g, ln_b, tm, hosted=None):
    t = dya.shape[0]
    steps = t // tm
    hb = tm // HALO

    def rev(i):
        return steps - 1 - i

    def body(dy_ref, wa_ref, c_ref, av_ref, ag_ref, avh_ref, agh_ref, w_ref, lg_ref, lb_ref,
             dp_ref, cs_ref, dw_ref, dcb_ref, dlg_ref, dlb_ref, dc_ext, a_ext, dc_sh, a_sh, da0_s):
        i = pl.program_id(0)
        first = i == 0

        @pl.when(first)
        def _():
            dc_ext[pl.ds(tm, HALO), :] = jnp.zeros((HALO, D), F32)
            dw_ref[...] = jnp.zeros_like(dw_ref)
            cs_ref[...] = jnp.zeros_like(cs_ref)

        d_act = _dot_nt(dy_ref[...], wa_ref[...])
        rstd, chat = _ln_stats(c_ref[...])
        ca = chat * lg_ref[...] + lb_ref[...]
        sc = _sigmoid(ca)
        dca = d_act * (sc * (1.0 + ca * (1.0 - sc)))
        _acc(dlg_ref, first, jnp.sum(dca * chat, axis=0, keepdims=True))
        _acc(dlb_ref, first, jnp.sum(dca, axis=0, keepdims=True))
        dc = _ln_bwd(dca, chat, rstd, lg_ref[...])
        _acc(dcb_ref, first, jnp.sum(dc, axis=0, keepdims=True))
        dc_ext[pl.ds(0, tm), :] = dc

        av = av_ref[...].astype(F32)
        sg = _sigmoid(ag_ref[...].astype(F32))
        a_ext[pl.ds(HALO, tm), :] = av * sg
        halo = avh_ref[...].astype(F32) * _sigmoid(agh_ref[...].astype(F32))
        a_ext[pl.ds(0, HALO), :] = jnp.where(i == steps - 1, 0.0, halo)

        _shift_copies(dc_ext, dc_sh, tm)
        _shift_copies(a_ext, a_sh, tm)
        _tap_sum(da0_s, None, w_ref, dc_ext, dc_sh, [CW - 1 - k for k in range(CW)], tm)
        _tap_corr(dw_ref, dc_ext, a_ext, a_sh, [HALO - (CW - 1) + k for k in range(CW)], tm)
        da0 = da0_s[...]
        dav = da0 * sg
        dag = da0 * av * (sg * (1.0 - sg))
        dp_ref[:, pl.ds(0, D)] = dav.astype(BF16)
        dp_ref[:, pl.ds(D, D)] = dag.astype(BF16)
        cs_ref[:, pl.ds(0, D)] += jnp.sum(dav, axis=0, keepdims=True)
        cs_ref[:, pl.ds(D, D)] += jnp.sum(dag, axis=0, keepdims=True)
        dc_ext[pl.ds(tm, HALO), :] = dc_ext[pl.ds(0, HALO), :]

    def rrow(cols, cb=0):
        return pl.BlockSpec((tm, cols), lambda i, _cb=cb: (rev(i), _cb))

    def halo_spec(cb):
        return pl.BlockSpec((HALO, D), lambda i, _cb=cb: (jnp.maximum(rev(i) * hb - 1, 0), _cb))

    return _call(
        name, body, (steps,),
        [rrow(D), _const((D, D)), rrow(D), rrow(D, 0), rrow(D, 1), halo_spec(0),
         halo_spec(1), _const((HALO, D)), _const((1, D)), _const((1, D))],
        [rrow(2 * D), _const((1, 2 * D)), _const((HALO, D)), _const((1, D)),
         _const((1, D)), _const((1, D))],
        [_sds((t, 2 * D), BF16), _sds((1, 2 * D), F32), _sds((HALO, D), F32),
         _sds((1, D), F32), _sds((1, D), F32), _sds((1, D), F32)],
        (dya, w_a, c, p, p, p, p, conv_w, ln_g, ln_b),
        scratch=[pltpu.VMEM((tm + HALO, D), F32), pltpu.VMEM((tm + HALO, D), F32),
                 pltpu.VMEM((7, tm + HALO - 8, D), F32), pltpu.VMEM((7, tm + HALO - 8, D), F32),
                 pltpu.VMEM((tm, D), F32)],
        sem=("arbitrary",), hosted=hosted)


def _sgu_bwd(name, dyb, w_b, p, ln_g, ln_b, sgu_w, bias_full, tm):
    t = dyb.shape[0]
    steps = t // tm

    def body(dy_ref, wb_ref, bu_ref, bv_ref, lg_ref, lb_ref, ws_ref, bias_ref,
             dp_ref, cs_ref, dws_ref, dsb_ref, dlg_ref, dlb_ref, dub_s, dvn_s, dbias_s):
        i = pl.program_id(0)
        first = i == 0
        mask = _sgu_mask()

        @pl.when(first)
        def _():
            dws_ref[...] = jnp.zeros_like(dws_ref)
            dbias_s[...] = jnp.zeros_like(dbias_s)
            cs_ref[...] = jnp.zeros_like(cs_ref)

        dob = _dot_nt(dy_ref[...], wb_ref[...])
        bu = bu_ref[...].astype(F32)
        bv = bv_ref[...].astype(F32)
        ub, ub_grad = _gelu_with_grad(bu)
        vb, vb_grad = _gelu_with_grad(bv)
        rstd, vhat = _ln_stats(vb)
        vn = (vhat * lg_ref[...] + lb_ref[...]).astype(BF16)
        for g in range(GROUPS):
            wm = jnp.where(mask, ws_ref[g], 0.0).astype(BF16)
            cs = slice(g * GD, (g + 1) * GD)
            for cc in range(tm // CHUNK):
                rs = slice(cc * CHUNK, (cc + 1) * CHUNK)
                vblk = vn[rs, cs]
                mixed = _dot(wm, vblk) + bias_ref[:, cs]
                dob_blk = dob[rs, cs]
                dub_s[rs, cs] = dob_blk * mixed
                dmixed = dob_blk * ub[rs, cs]
                dbias_s[:, cs] += dmixed
                dmb = dmixed.astype(BF16)
                dws_ref[g] += _dot_nt(dmb, vblk)
                dvn_s[rs, cs] = _dot_tn(wm, dmb)
        dbu = dub_s[...] * ub_grad
        dvn = dvn_s[...]
        _acc(dlg_ref, first, jnp.sum(dvn * vhat, axis=0, keepdims=True))
        _acc(dlb_ref, first, jnp.sum(dvn, axis=0, keepdims=True))
        dbv = _ln_bwd(dvn, vhat, rstd, lg_ref[...]) * vb_grad
        dp_ref[:, pl.ds(0, D)] = dbu.astype(BF16)
        dp_ref[:, pl.ds(D, D)] = dbv.astype(BF16)
        cs_ref[:, pl.ds(0, D)] += jnp.sum(dbu, axis=0, keepdims=True)
        cs_ref[:, pl.ds(D, D)] += jnp.sum(dbv, axis=0, keepdims=True)

        @pl.when(i == steps - 1)
        def _():
            lane = lax.broadcasted_iota(jnp.int32, (CHUNK, CHUNK), 1)
            dsb = jnp.zeros((CHUNK, CHUNK), F32)
            for g in range(GROUPS):
                dws_ref[g] = jnp.where(mask, dws_ref[g], 0.0)
                dsb = jnp.where(lane == g, jnp.sum(dbias_s[:, g * GD:(g + 1) * GD], axis=1, keepdims=True), dsb)
            dsb_ref[...] = dsb

    rb = _row(tm, D)
    return pl.pallas_call(
        body, name=name, grid=(steps,),
        in_specs=[rb, _const((D, D)), _row(tm, D, 2), _row(tm, D, 3), _const((1, D)), _const((1, D)),
                  _const((GROUPS, CHUNK, CHUNK)), _const((CHUNK, D))],
        out_specs=[_row(tm, 2 * D), _const((1, 2 * D)), _const((GROUPS, CHUNK, CHUNK)),
                   _const((CHUNK, CHUNK)), _const((1, D)), _const((1, D))],
        out_shape=[_sds((t, 2 * D), BF16), _sds((1, 2 * D), F32), _sds((GROUPS, CHUNK, CHUNK), F32),
                   _sds((CHUNK, CHUNK), F32), _sds((1, D), F32), _sds((1, D), F32)],
        scratch_shapes=[pltpu.VMEM((tm, D), F32), pltpu.VMEM((tm, D), F32), pltpu.VMEM((CHUNK, D), F32)],
        compiler_params=_params("arbitrary"),
    )(dyb, w_b, p, p, ln_g, ln_b, sgu_w, bias_full)


def _adam_math(w, g, m, v):
    m = B1 * m + (1.0 - B1) * g
    v = B2 * v + (1.0 - B2) * (g * g)
    m_hat = m / (1.0 - B1 ** STEP)
    v_hat = v / (1.0 - B2 ** STEP)
    delta = -LR * (m_hat / (jnp.sqrt(v_hat) + EPS_ADAM) + WD * w)
    return delta, m, v


def _adamw(name, w, g, m, v, tr):
    r, cdim = w.shape

    def body(w_ref, g_ref, m_ref, v_ref, d_ref, mo_ref, vo_ref):
        d, mn, vn = _adam_math(w_ref[...], g_ref[...], m_ref[...], v_ref[...])
        d_ref[...] = d
        mo_ref[...] = mn
        vo_ref[...] = vn

    blk = pl.BlockSpec((tr, cdim), lambda i: (i, 0))
    return pl.pallas_call(
        body, name=name, grid=(r // tr,), in_specs=[blk] * 4, out_specs=[blk] * 3,
        out_shape=[_sds((r, cdim), F32)] * 3, compiler_params=_params("parallel"),
    )(w, g, m, v)


def _adamw_small(name, w, g8, m, v):
    r, cdim = w.shape

    def body(w_ref, g_ref, m_ref, v_ref, go_ref, d_ref, mo_ref, vo_ref):
        g = g_ref[0]
        for k in range(1, NDEV):
            g = g + g_ref[k]
        go_ref[...] = g
        d, mn, vn = _adam_math(w_ref[...], g, m_ref[...], v_ref[...])
        d_ref[...] = d
        mo_ref[...] = mn
        vo_ref[...] = vn

    return pl.pallas_call(
        body, name=name, out_shape=[_sds((r, cdim), F32)] * 4,
        compiler_params=pltpu.CompilerParams(vmem_limit_bytes=VMEM_LIMIT),
    )(w, g8, m, v)


_BIG = [("ffn1_w_gu", 704, True), ("ffn1_w_down", 352, False), ("w_in", 768, True),
        ("w_a_out", 128, False), ("w_b_out", 128, False), ("w_out", 128, False),
        ("w_q", 128, False), ("w_kv", 256, True), ("w_o", 128, False),
        ("ffn2_w_gu", 704, True), ("ffn2_w_down", 352, False)]
_BIG_ROWS = sum(r for _, r, _ in _BIG)

_SMALL = [("ffn1_norm", 1), ("mix_norm", 1), ("b_in", 6), ("conv_w", HALO), ("conv_b", 1),
          ("conv_ln_g", 1), ("conv_ln_b", 1), ("sgu_ln_g", 1), ("sgu_ln_b", 1), ("sgu_w", 64),
          ("sgu_b", 1), ("xattn_norm", 1), ("mem_norm", 1), ("ffn2_norm", 1), ("final_norm", 1)]
_SMALL_ROWS = 120


def _pack_small(vals, my_dev):
    rows = []
    for name, nrows in _SMALL:
        a = vals[name].astype(F32)
        if name == "conv_w":
            if a.shape[-1] != D:
                slab = jnp.zeros((HALO, D), F32)
                a = lax.dynamic_update_slice(slab, jnp.pad(a.reshape(CW, -1), ((0, HALO - CW), (0, 0))),
                                             (0, my_dev * (D // NDEV)))
            else:
                a = jnp.pad(a.reshape(CW, D), ((0, HALO - CW), (0, 0)))
        elif name == "sgu_b":
            a = jnp.pad(a.reshape(1, -1), ((0, 0), (0, D - GROUPS * CHUNK)))
        else:
            a = a.reshape(nrows, D)
        rows.append(a)
    packed = jnp.concatenate(rows, axis=0)
    return jnp.pad(packed, ((0, _SMALL_ROWS - packed.shape[0]), (0, 0)))


def _unpack_small(packed, shapes, my_dev):
    out, off = {}, 0
    for name, nrows in _SMALL:
        a = packed[off:off + nrows]
        off += nrows
        if name == "conv_w":
            a = lax.dynamic_slice(a, (0, my_dev * (D // NDEV)), (CW, D // NDEV))
        elif name == "sgu_b":
            a = a[:, :GROUPS * CHUNK]
        out[name] = a.reshape(shapes[name])
    return out


def kernel(x, mem, ffn1_norm, ffn1_w_gu, ffn1_w_down, mix_norm, w_in, b_in, conv_w, conv_b, conv_ln_g, conv_ln_b, w_a_out, sgu_ln_g, sgu_ln_b, sgu_w, sgu_b, w_b_out, w_out, xattn_norm, mem_norm, w_q, w_kv, w_o, ffn2_norm, ffn2_w_gu, ffn2_w_down, final_norm, loss_target, m_ffn1_norm, m_ffn1_w_gu, m_ffn1_w_down, m_mix_norm, m_w_in, m_b_in, m_conv_w, m_conv_b, m_conv_ln_g, m_conv_ln_b, m_w_a_out, m_sgu_ln_g, m_sgu_ln_b, m_sgu_w, m_sgu_b, m_w_b_out, m_w_out, m_xattn_norm, m_mem_norm, m_w_q, m_w_kv, m_w_o, m_ffn2_norm, m_ffn2_w_gu, m_ffn2_w_down, m_final_norm, v_ffn1_norm, v_ffn1_w_gu, v_ffn1_w_down, v_mix_norm, v_w_in, v_b_in, v_conv_w, v_conv_b, v_conv_ln_g, v_conv_ln_b, v_w_a_out, v_sgu_ln_g, v_sgu_ln_b, v_sgu_w, v_sgu_b, v_w_b_out, v_w_out, v_xattn_norm, v_mem_norm, v_w_q, v_w_kv, v_w_o, v_ffn2_norm, v_ffn2_w_gu, v_ffn2_w_down, v_final_norm):
    env = dict(locals())
    names = [n for n, _, _ in _BIG] + [n for n, _ in _SMALL]
    w = {n: env[n] for n in names}
    mom = {n: env["m_" + n] for n in names}
    vel = {n: env["v_" + n] for n in names}

    ax, ay, ac = lax.axis_index("x"), lax.axis_index("y"), lax.axis_index("c")
    my_chip = 2 * ax + ay
    my_dev = 2 * my_chip + ac

    t = x.shape[1]
    tm = min(512, t)
    tm_big = min(1024, t)
    tm_s = min(512, t)
    tm_c = min(256, t)
    xs = x.reshape(t, D)
    tgt = loss_target.reshape(t, D)
    mem2 = mem.reshape(NMEM, D)

    first, mid, late = _BIG[:1], _BIG[1:6], _BIG[6:]

    def gathers(entries):
        return [_gather_comm((w[n][0].T if tr else w[n][0]).astype(BF16)) for n, _, tr in entries]

    def whole(gathered, entries):
        return {n: g.reshape(NDEV * rows, D) for g, (n, rows, _) in zip(gathered, entries)}

    conv_slab = lax.dynamic_update_slice(
        jnp.zeros((HALO, D), F32), jnp.pad(conv_w[0], ((0, HALO - CW), (0, 0))), (0, my_dev * (D // NDEV)))
    bias_full = jnp.repeat(sgu_b[0].T, GD, axis=1)
    b_in2 = b_in.reshape(1, 6 * D)

    (xn1,), (*full_first, conv_w8) = _rms_cast(
        "norm_x", xs, ffn1_norm, tm, hosted=gathers(first) + [_gather_comm(conv_slab)])
    conv_w_pad = jnp.sum(conv_w8, axis=0)
    wf = whole(full_first, first)
    (g1, u1, a1), full_mid = _ffn_up("ffn1_up", xn1, wf["ffn1_w_gu"], tm_big, 1408, hosted=gathers(mid))
    wf.update(whole(full_mid, mid))
    h1, n_mix = _ffn_down("ffn1_down", a1, wf["ffn1_w_down"], xs, mix_norm, tm)
    (p,), full_late = _mix_in("mix_in", n_mix, wf["w_in"], b_in2, tm_big, 1536, hosted=gathers(late))
    wf.update(whole(full_late, late))
    c_conv, act_a = _conv_fwd("conv_fwd", p, conv_w_pad, conv_b, conv_ln_g, conv_ln_b, tm_c)
    act_b = _sgu_fwd("sgu_fwd", p, sgu_ln_g, sgu_ln_b, sgu_w[0], bias_full, tm_s)
    ya, yb, merged, h2, xq = _merge_fwd("merge_fwd", act_a, act_b, p, wf["w_a_out"], wf["w_b_out"],
                                        wf["w_out"], h1, xattn_norm, tm_s)
    memn, kb, vb = _kv_fwd("kv_fwd", mem2, mem_norm, wf["w_kv"])
    qb, probs, ob, h3, xn4 = _attn_fwd("attn_fwd", xq, wf["w_q"], kb, vb, wf["w_o"], h2, ffn2_norm, tm_s)
    (g2, u2, a2), _ = _ffn_up("ffn2_up", xn4, wf["ffn2_w_gu"], tm_big, 1408)
    dh4, dh4b, loss_blk, d_final = _ffn_down_loss("ffn2_down_loss", a2, wf["ffn2_w_down"], h3,
                                                  final_norm.reshape(1, D), tgt, tm)

    gs = {}
    gs["final_norm"] = d_final

    core = ac.astype(jnp.int32).reshape(1)
    chip = my_chip.astype(jnp.int32).reshape(1)
    last_g, mixer_g, attn_g = _BIG[:2], _BIG[2:6], _BIG[6:]

    def layout(entries):
        offs, off = {}, 0
        for n, rows, _ in entries:
            offs[n] = (off, rows)
            off += rows
        return offs, off

    def dest(group, buf, name, row0=0):
        offs, total = group
        return _Dest(buf, total, offs[name][0], offs[name][1], row0)

    lay_last, lay_mixer, lay_attn = layout(last_g), layout(mixer_g), layout(attn_g)

    def ffn_bwd(tag, dhb, dh, g, u, a, xn, h_in, gain, wgu_t, wd, group, buf, act_hosted=None, dx_hosted=None):
        (dg, du), act_out = _ffn_bwd_act(tag + "_bwd_act", dhb, wd, g, u, tm_big, 1408, hosted=act_hosted)
        buf = _mm_tn(tag + "_dw_down", a, dhb, 0.5, 1408, TK, dest(group, buf, tag + "_w_down"))
        buf = _mm_tn(tag + "_dw_gate", dg, xn, 1.0, 1408, TK, dest(group, buf, tag + "_w_gu"))
        buf = _mm_tn(tag + "_dw_up", du, xn, 1.0, 1408, TK, dest(group, buf, tag + "_w_gu", DFF))
        (dh_o, dhb_o, dgain), dx_out = _dx_rms_bwd(
            tag + "_bwd_dx", [(dg, wgu_t, 0, DFF, False), (du, wgu_t, 1, DFF, False)], h_in, gain, dh, tm_s,
            hosted=dx_hosted(buf) if dx_hosted else None)
        return dh_o, dhb_o, dgain, buf, act_out, dx_out

    dh3, dh3b, gs["ffn2_norm"], g_attn, _, _ = ffn_bwd(
        "ffn2", dh4b, dh4, g2, u2, a2, xn4, h3, ffn2_norm, wf["ffn2_w_gu"], wf["ffn2_w_down"], lay_attn, None)

    g_attn = _mm_tn("dw_o", ob, dh3b, 1.0, 1024, 2 * TK, dest(lay_attn, g_attn, "w_o"))
    dq, dk, dv = _attn_bwd("attn_bwd", dh3b, wf["w_o"], qb, probs, kb, vb, tm_s)
    g_attn, gs["mem_norm"] = _kv_bwd("kv_bwd", dk, dv, memn, wf["w_kv"], mem2, mem_norm,
                                     dest(lay_attn, g_attn, "w_kv"))
    g_attn = _mm_tn("dw_q", xq, dq, 1.0, 1024, 2 * TK, dest(lay_attn, g_attn, "w_q"))

    g4_attn = g_attn.reshape(4, 2, lay_attn[1], D)
    (dh2, dh2b, gs["xattn_norm"]), (sib_attn,) = _dx_rms_bwd(
        "attn_bwd_dx", [(dq, wf["w_q"], 0, D, True)], h2, xattn_norm, dh3, tm_s,
        hosted=_pair_exchange_comm(g4_attn))
    part_attn = _pair_sum("grads_pair_sum_attn", g4_attn, sib_attn, core, 784)

    g_mixer = _mm_tn("dw_out", merged, dh2b, 1.0, 1024, 2 * TK, dest(lay_mixer, None, "w_out"))
    dya, dyb, dp_g, cs_g = _merge_bwd("merge_bwd", dh2b, wf["w_out"], ya, yb, p, tm_s)
    g_mixer = _mm_tn("dw_a", act_a, dya, 1.0, 1024, 2 * TK, dest(lay_mixer, g_mixer, "w_a_out"))
    g_mixer = _mm_tn("dw_b", act_b, dyb, 1.0, 1024, 2 * TK, dest(lay_mixer, g_mixer, "w_b_out"))
    (dp_a, cs_a, d_convw, gs["conv_b"], gs["conv_ln_g"], gs["conv_ln_b"]), (chips_attn,) = _conv_bwd(
        "conv_bwd", dya, wf["w_a_out"], c_conv, p, conv_w_pad, conv_ln_g, conv_ln_b, tm_c,
        hosted=_chip_exchange_comm(part_attn))
    gsum_attn = _chip_sum("grads_chip_sum_attn", part_attn, chips_attn, chip, 784)
    dp_b, cs_b, d_sguw, d_sgub, gs["sgu_ln_g"], gs["sgu_ln_b"] = _sgu_bwd(
        "sgu_bwd", dyb, wf["w_b_out"], p, sgu_ln_g, sgu_ln_b, sgu_w[0], bias_full, tm_s)
    gs["conv_w"] = d_convw[:CW].reshape(1, CW, D)
    gs["sgu_w"] = d_sguw
    gs["sgu_b"] = d_sgub[:, :GROUPS].T
    gs["b_in"] = jnp.concatenate([cs_a, cs_b, cs_g], axis=1)
    for j, (tag, dpart) in enumerate((("a", dp_a), ("b", dp_b), ("g", dp_g))):
        g_mixer = _mm_tn("dw_in_" + tag, dpart, n_mix, 1.0, 1024, 2 * TK,
                         dest(lay_mixer, g_mixer, "w_in", 2 * D * j))
    g4_mixer = g_mixer.reshape(4, 2, lay_mixer[1], D)
    (dh1, dh1b, gs["mix_norm"]), (sib_mixer,) = _dx_rms_bwd(
        "mix_bwd_dx", [(dp_a, wf["w_in"], 0, 2 * D, False), (dp_b, wf["w_in"], 1, 2 * D, False),
                       (dp_g, wf["w_in"], 2, 2 * D, False)], h1, mix_norm, dh2, tm_s,
        hosted=_pair_exchange_comm(g4_mixer))
    part_mixer = _pair_sum("grads_pair_sum_mixer", g4_mixer, sib_mixer, core, 576)

    dx, _, gs["ffn1_norm"], _, (chips_mixer,), (last_slots,) = ffn_bwd(
        "ffn1", dh1b, dh1, g1, u1, a1, xn1, xs, ffn1_norm, wf["ffn1_w_gu"], wf["ffn1_w_down"], lay_last, None,
        act_hosted=_chip_exchange_comm(part_mixer), dx_hosted=_all_to_all_comm)
    gsum_mixer = _chip_sum("grads_chip_sum_mixer", part_mixer, chips_mixer, chip, 576)
    gsum_last = _slot_sum("grads_slot_sum_ffn1", last_slots, 528)

    grads, deltas, new_m, new_v = {}, {}, {}, {}
    for entries, gsum in ((last_g, gsum_last), (mixer_g, gsum_mixer), (attn_g, gsum_attn)):
        off = 0
        for name, rows, transposed in entries:
            gsh = gsum[off:off + rows]
            off += rows
            gsh = gsh.T if transposed else gsh
            d, mo, vo = _adamw("adamw_" + name, w[name][0], gsh, mom[name][0], vel[name][0], gsh.shape[0] // 2)
            grads[name], deltas[name], new_m[name], new_v[name] = gsh[None], d[None], mo[None], vo[None]

    shapes = {n: w[n].shape for n, _ in _SMALL}
    (g8,) = _run_comm("gather_small_grads", _gather_comm(_pack_small(gs, my_dev)))
    sg, sd, sm, sv = _adamw_small("adamw_small", _pack_small(w, my_dev), g8,
                                  _pack_small(mom, my_dev), _pack_small(vel, my_dev))
    for dst, src in ((grads, sg), (deltas, sd), (new_m, sm), (new_v, sv)):
        dst.update(_unpack_small(src, shapes, my_dev))

    loss = lax.psum(loss_blk[0, 0], AXES)
    order = ["ffn1_norm", "ffn1_w_gu", "ffn1_w_down", "mix_norm", "w_in", "b_in", "conv_w", "conv_b",
             "conv_ln_g", "conv_ln_b", "w_a_out", "sgu_ln_g", "sgu_ln_b", "sgu_w", "sgu_b", "w_b_out",
             "w_out", "xattn_norm", "mem_norm", "w_q", "w_kv", "w_o", "ffn2_norm", "ffn2_w_gu",
             "ffn2_w_down", "final_norm"]
    return (loss, dx.reshape(x.shape), *[grads[n] for n in order], *[deltas[n] for n in order],
            *[new_m[n] for n in order], *[new_v[n] for n in order])
```

```python
import functools
import math

import jax
import jax.numpy as jnp
from jax import lax
from jax.experimental import pallas as pl
from jax.experimental.pallas import tpu as pltpu

F32 = jnp.float32
BF16 = jnp.bfloat16
MESH = pl.DeviceIdType.MESH
AXES = ("x", "y", "c")

D = 1024
DFF = 2816
NMEM = 256
HEADS = 4
HD = D // HEADS
CW = 31
HALO = 32
CHUNK = 128
GROUPS = 4
GD = D // GROUPS
EPS_RMS = 1e-6
EPS_LN = 1e-5
LR, B1, B2, EPS_ADAM, WD, STEP = 0.001, 0.9, 0.999, 1e-08, 0.01, 10
NDEV = 8
VMEM_LIMIT = 56 * 1024 * 1024
TK = 2048


def _params(*sem):
    return pltpu.CompilerParams(dimension_semantics=sem, vmem_limit_bytes=VMEM_LIMIT)


def _dot(a, b):
    return jnp.dot(a, b, preferred_element_type=F32)


def _dot_nt(a, b):
    return lax.dot_general(a, b, (((1,), (1,)), ((), ())), preferred_element_type=F32)


def _dot_tn(a, b):
    return lax.dot_general(a, b, (((0,), (0,)), ((), ())), preferred_element_type=F32)


def _sigmoid(x):
    return 0.5 * jnp.tanh(0.5 * x) + 0.5


_GELU_C = math.sqrt(2.0 / math.pi)


def _gelu_with_grad(x):
    x2 = x * x
    t = jnp.tanh(_GELU_C * (x + 0.044715 * (x2 * x)))
    half = 0.5 * (1.0 + t)
    return x * half, half + 0.5 * x * (1.0 - t * t) * (_GELU_C * (1.0 + 3.0 * 0.044715 * x2))


def _gelu(x):
    return _gelu_with_grad(x)[0]


def _rms_stats(h):
    r = lax.rsqrt(jnp.mean(h * h, axis=-1, keepdims=True) + EPS_RMS)
    return r, h * r


def _rms_bwd(dxn, h, gain):
    r, xh = _rms_stats(h)
    dgain = jnp.sum(dxn * xh, axis=0, keepdims=True)
    dxh = dxn * gain
    dh = r * (dxh - xh * jnp.mean(dxh * xh, axis=-1, keepdims=True))
    return dh, dgain


def _ln_stats(c):
    mu = jnp.mean(c, axis=-1, keepdims=True)
    xc = c - mu
    rstd = lax.rsqrt(jnp.mean(xc * xc, axis=-1, keepdims=True) + EPS_LN)
    return rstd, xc * rstd


def _ln_bwd(dy, xhat, rstd, g):
    dxh = dy * g
    return rstd * (dxh - jnp.mean(dxh, axis=-1, keepdims=True)
                   - xhat * jnp.mean(dxh * xhat, axis=-1, keepdims=True))


def _row(tm, cols, cb=0):
    return pl.BlockSpec((tm, cols), lambda i, _cb=cb: (i, _cb))


def _const(shape):
    n = len(shape)
    return pl.BlockSpec(shape, lambda *_: (0,) * n)


def _sds(shape, dtype):
    return jax.ShapeDtypeStruct(shape, dtype)


def _acc(ref, first, val):
    @pl.when(first)
    def _():
        ref[...] = jnp.zeros_like(ref)
    ref[...] += val


class _Comm:
    def __init__(self, args, out_shapes, scratch, start, finish, forward=None):
        self.args, self.out_shapes, self.scratch = args, out_shapes, scratch
        self.start, self.finish, self.forward = start, finish, forward


_ANY = pl.BlockSpec(memory_space=pl.ANY)


def _run_comm(name, comm):
    ni, no = len(comm.args), len(comm.out_shapes)

    def body(*refs):
        ins, outs, sems = refs[:ni], refs[ni:ni + no], refs[ni + no:]
        comm.start(ins, outs, sems)
        if comm.forward:
            comm.forward(ins, outs, sems)
        comm.finish(ins, outs, sems)

    return pl.pallas_call(
        body, name=name, out_shape=list(comm.out_shapes), in_specs=[_ANY] * ni, out_specs=[_ANY] * no,
        scratch_shapes=list(comm.scratch),
    )(*comm.args)


def _call(name, body, grid, in_specs, out_specs, out_shape, args, scratch=(), sem=None, hosted=None):
    n_in, n_out, n_scr = len(in_specs), len(out_specs), len(scratch)
    if not hosted:
        outs = pl.pallas_call(
            body, name=name, grid=grid, in_specs=list(in_specs), out_specs=list(out_specs),
            out_shape=list(out_shape), scratch_shapes=list(scratch), compiler_params=_params(*sem),
        )(*args)
        return outs, []
    comms = list(hosted) if isinstance(hosted, (list, tuple)) else [hosted]
    hi = sum(len(cm.args) for cm in comms)
    ho = sum(len(cm.out_shapes) for cm in comms)

    def wrapped(*refs):
        ins, h_in = refs[:n_in], refs[n_in:n_in + hi]
        o0 = n_in + hi
        outs, h_out = refs[o0:o0 + n_out], refs[o0 + n_out:o0 + n_out + ho]
        s0 = o0 + n_out + ho
        scr, h_sems = refs[s0:s0 + n_scr], refs[s0 + n_scr:]
        ids = [pl.program_id(a) for a in range(len(grid))]
        first = functools.reduce(jnp.logical_and, [i == 0 for i in ids])
        last = functools.reduce(jnp.logical_and, [i == g - 1 for i, g in zip(ids, grid)])
        parts, a0, b0, c0 = [], 0, 0, 0
        for cm in comms:
            na, nb, nc = len(cm.args), len(cm.out_shapes), len(cm.scratch)
            parts.append((cm, h_in[a0:a0 + na], h_out[b0:b0 + nb], h_sems[c0:c0 + nc]))
            a0, b0, c0 = a0 + na, b0 + nb, c0 + nc

        @pl.when(first)
        def _():
            for cm, ci, co, cs in parts:
                cm.start(ci, co, cs)

        body(*ins, *outs, *scr)

        step, total = ids[0], grid[0]
        for i, g in zip(ids[1:], grid[1:]):
            step, total = step * g + i, total * g

        @pl.when(step == (3 * total) // 4)
        def _():
            for cm, ci, co, cs in parts:
                if cm.forward:
                    cm.forward(ci, co, cs)

        @pl.when(last)
        def _():
            for cm, ci, co, cs in parts:
                cm.finish(ci, co, cs)

    res = pl.pallas_call(
        wrapped, name=name, grid=grid, in_specs=list(in_specs) + [_ANY] * hi,
        out_specs=list(out_specs) + [_ANY] * ho,
        out_shape=list(out_shape) + [s for cm in comms for s in cm.out_shapes],
        scratch_shapes=list(scratch) + [s for cm in comms for s in cm.scratch],
        compiler_params=_params(*(["arbitrary"] * len(grid))),
    )(*args, *[a for cm in comms for a in cm.args])
    return res[:n_out], res[n_out:]


def _gather_comm(blk):
    r, cdim = blk.shape

    def copies(x_ref, out_ref, send_sems, recv_sems, local_sem):
        x, y, c = lax.axis_index("x"), lax.axis_index("y"), lax.axis_index("c")
        me, sibling = (x, y, c), (x, y, 1 - c)
        chips = [(1 - x, y), (x, 1 - y), (1 - x, 1 - y)]

        def slot(px, py, pc):
            return out_ref.at[4 * px + 2 * py + pc]

        def copy(k, block, to, src=None):
            return pltpu.make_async_remote_copy(
                src_ref=slot(*block) if src is None else src, dst_ref=slot(*block),
                send_sem=send_sems.at[k], recv_sem=recv_sems.at[k],
                device_id=to, device_id_type=MESH)

        mine = pltpu.make_async_copy(x_ref, slot(*me), local_sem)
        first = [copy(0, me, sibling, src=x_ref)]
        first += [copy(1 + j, me, (*chip, c), src=x_ref) for j, chip in enumerate(chips)]
        passed = [copy(4 + j, (*chip, c), sibling) for j, chip in enumerate(chips)]
        landed = [copy(1 + j, (*chip, c), me) for j, chip in enumerate(chips)]
        from_sibling = [copy(0, sibling, me)] + [copy(4 + j, (*chip, 1 - c), me) for j, chip in enumerate(chips)]
        return mine, first, passed, landed, from_sibling

    def start(ins, outs, sems):
        mine, first, _, _, _ = copies(ins[0], outs[0], *sems)
        mine.start()
        for cp in first:
            cp.start()

    def forward(ins, outs, sems):
        _, _, passed, landed, _ = copies(ins[0], outs[0], *sems)
        for arrived, onward in zip(landed, passed):
            arrived.wait_recv()
            onward.start()

    def finish(ins, outs, sems):
        mine, first, passed, _, from_sibling = copies(ins[0], outs[0], *sems)
        for cp in from_sibling:
            cp.wait_recv()
        for cp in first + passed:
            cp.wait_send()
        mine.wait()

    return _Comm([blk], [_sds((NDEV, r, cdim), blk.dtype)],
                 [pltpu.SemaphoreType.DMA((7,)), pltpu.SemaphoreType.DMA((7,)), pltpu.SemaphoreType.DMA],
                 start, finish, forward)


def _exchange_comm(src, n, out_rows, make):
    r, cdim = out_rows

    def copies(src_ref, out_ref, send_sems, recv_sems):
        out = []
        for k in range(n):
            s, d, to = make(k, src_ref, out_ref)
            out.append(pltpu.make_async_remote_copy(
                src_ref=s, dst_ref=d, send_sem=send_sems.at[k], recv_sem=recv_sems.at[k],
                device_id=to, device_id_type=MESH))
        return out

    def start(ins, outs, sems):
        for cp in copies(ins[0], outs[0], *sems):
            cp.start()

    def finish(ins, outs, sems):
        cps = copies(ins[0], outs[0], *sems)
        for cp in cps:
            cp.wait_recv()
        for cp in cps:
            cp.wait_send()

    return _Comm([src], [_sds((n, r, cdim), src.dtype)],
                 [pltpu.SemaphoreType.DMA((n,)), pltpu.SemaphoreType.DMA((n,))], start, finish)


def _pair_exchange_comm(g4):
    def make(k, g_ref, out_ref):
        x, y, c = lax.axis_index("x"), lax.axis_index("y"), lax.axis_index("c")
        return g_ref.at[k, 1 - c], out_ref.at[k], (x, y, 1 - c)

    return _exchange_comm(g4, 4, g4.shape[2:], make)


def _chip_exchange_comm(part):
    def make(k, p_ref, out_ref):
        x, y, c = lax.axis_index("x"), lax.axis_index("y"), lax.axis_index("c")
        px = x if ((k + 1) >> 1) == 0 else 1 - x
        py = y if ((k + 1) & 1) == 0 else 1 - y
        return p_ref.at[2 * px + py], out_ref.at[k], (px, py, c)

    return _exchange_comm(part, 3, part.shape[1:], make)


def _all_to_all_comm(g8):
    _, r, cdim = g8.shape

    def copies(g_ref, out_ref, send_sems, recv_sems, local_sem):
        x, y, c = lax.axis_index("x"), lax.axis_index("y"), lax.axis_index("c")
        me = 4 * x + 2 * y + c
        remote = []
        for k in range(1, NDEV):
            px = 1 - x if k & 4 else x
            py = 1 - y if k & 2 else y
            pc = 1 - c if k & 1 else c
            remote.append(pltpu.make_async_remote_copy(
                src_ref=g_ref.at[4 * px + 2 * py + pc], dst_ref=out_ref.at[me],
                send_sem=send_sems.at[k - 1], recv_sem=recv_sems.at[k - 1],
                device_id=(px, py, pc), device_id_type=MESH))
        return pltpu.make_async_copy(g_ref.at[me], out_ref.at[me], local_sem), remote

    def start(ins, outs, sems):
        mine, remote = copies(ins[0], outs[0], *sems)
        mine.start()
        for cp in remote:
            cp.start()

    def finish(ins, outs, sems):
        mine, remote = copies(ins[0], outs[0], *sems)
        for cp in remote:
            cp.wait_recv()
        for cp in remote:
            cp.wait_send()
        mine.wait()

    return _Comm([g8], [_sds((NDEV, r, cdim), g8.dtype)],
                 [pltpu.SemaphoreType.DMA((NDEV - 1,)), pltpu.SemaphoreType.DMA((NDEV - 1,)),
                  pltpu.SemaphoreType.DMA], start, finish)


def _slot_sum(name, slots, tr):
    _, r, cdim = slots.shape

    def body(s_ref, o_ref):
        s = s_ref[0].astype(F32)
        for k in range(1, NDEV):
            s = s + s_ref[k].astype(F32)
        o_ref[...] = s

    return pl.pallas_call(
        body, name=name, grid=(r // tr,),
        in_specs=[pl.BlockSpec((NDEV, tr, cdim), lambda i: (0, i, 0))],
        out_specs=pl.BlockSpec((tr, cdim), lambda i: (i, 0)),
        out_shape=_sds((r, cdim), F32), compiler_params=_params("parallel"),
    )(slots)


def _pair_sum(name, g4, recv, core, tr):
    _, _, r, cdim = g4.shape

    def body(core_ref, a_ref, b_ref, o_ref):
        o_ref[...] = (a_ref[...].astype(F32) + b_ref[...].astype(F32)).astype(o_ref.dtype)

    return pl.pallas_call(
        body, name=name,
        grid_spec=pltpu.PrefetchScalarGridSpec(
            num_scalar_prefetch=1, grid=(4, r // tr),
            in_specs=[pl.BlockSpec((None, None, tr, cdim), lambda k, i, cr: (k, cr[0], i, 0)),
                      pl.BlockSpec((None, tr, cdim), lambda k, i, cr: (k, i, 0))],
            out_specs=pl.BlockSpec((None, tr, cdim), lambda k, i, cr: (k, i, 0))),
        out_shape=_sds((4, r, cdim), BF16),
        compiler_params=_params("parallel", "parallel"),
    )(core, g4, recv)


def _chip_sum(name, part, recv, chip, tr):
    _, r, cdim = part.shape

    def body(chip_ref, a_ref, b_ref, o_ref):
        s = a_ref[...].astype(F32)
        for k in range(3):
            s = s + b_ref[k].astype(F32)
        o_ref[...] = s

    return pl.pallas_call(
        body, name=name,
        grid_spec=pltpu.PrefetchScalarGridSpec(
            num_scalar_prefetch=1, grid=(r // tr,),
            in_specs=[pl.BlockSpec((None, tr, cdim), lambda i, cr: (cr[0], i, 0)),
                      pl.BlockSpec((3, tr, cdim), lambda i, cr: (0, i, 0))],
            out_specs=pl.BlockSpec((tr, cdim), lambda i, cr: (i, 0))),
        out_shape=_sds((r, cdim), F32),
        compiler_params=_params("parallel"),
    )(chip, part, recv)


def _rms_cast(name, h, gain, tm, hosted=None):
    t = h.shape[0]

    def body(h_ref, g_ref, o_ref):
        _, xh = _rms_stats(h_ref[...])
        o_ref[...] = (xh * g_ref[...]).astype(BF16)

    return _call(name, body, (t // tm,), [_row(tm, D), _const((1, D))], [_row(tm, D)],
                 [_sds((t, D), BF16)], (h, gain), sem=("parallel",), hosted=hosted)


def _ffn_up(name, xn, wgu_t, tm, tn, hosted=None):
    t = xn.shape[0]
    nh = DFF // tn

    def body(x_ref, wg_ref, wu_ref, s_ref, f_ref, a_ref):
        x = x_ref[...]
        g = _dot_nt(x, wg_ref[...])
        u = _dot_nt(x, wu_ref[...])
        sg = _sigmoid(g)
        s = g * sg
        s_ref[...] = s.astype(BF16)
        f_ref[...] = (u * (sg + s * (1.0 - sg))).astype(BF16)
        a_ref[...] = (s * u).astype(BF16)

    o = pl.BlockSpec((tm, tn), lambda j, i: (i, j))
    return _call(name, body, (nh, t // tm),
                 [pl.BlockSpec((tm, D), lambda j, i: (i, 0)),
                  pl.BlockSpec((tn, D), lambda j, i: (j, 0)),
                  pl.BlockSpec((tn, D), lambda j, i: (j + nh, 0))],
                 [o, o, o], [_sds((t, DFF), BF16)] * 3, (xn, wgu_t, wgu_t),
                 sem=("parallel", "parallel"), hosted=hosted)


def _ffn_down(name, a, wd, h, gain, tm):
    t = a.shape[0]

    def body(a_ref, w_ref, h_ref, g_ref, o_ref, n_ref):
        hn = h_ref[...] + 0.5 * _dot(a_ref[...], w_ref[...])
        o_ref[...] = hn
        _, xh = _rms_stats(hn)
        n_ref[...] = (xh * g_ref[...]).astype(BF16)

    return pl.pallas_call(
        body, name=name, grid=(t // tm,),
        in_specs=[_row(tm, DFF), _const((DFF, D)), _row(tm, D), _const((1, D))],
        out_specs=[_row(tm, D), _row(tm, D)],
        out_shape=[_sds((t, D), F32), _sds((t, D), BF16)],
        compiler_params=_params("parallel"),
    )(a, wd, h, gain)


def _mix_in(name, n, win_t, b_in, tm, tn, hosted=None):
    t = n.shape[0]

    def body(n_ref, w_ref, b_ref, p_ref):
        p_ref[...] = (_dot_nt(n_ref[...], w_ref[...]) + b_ref[...]).astype(BF16)

    return _call(name, body, (6 * D // tn, t // tm),
                 [pl.BlockSpec((tm, D), lambda j, i: (i, 0)),
                  pl.BlockSpec((tn, D), lambda j, i: (j, 0)),
                  pl.BlockSpec((1, tn), lambda j, i: (0, j))],
                 [pl.BlockSpec((tm, tn), lambda j, i: (i, j))], [_sds((t, 6 * D), BF16)],
                 (n, win_t, b_in), sem=("parallel", "parallel"), hosted=hosted)


RC = 64
LANES = 128


def _shift_copies(ext, shifted, tm):
    n = tm + HALO - 8
    for m in range(1, 8):
        shifted[m - 1] = ext[pl.ds(m, n), :]


def _by_residue(offs):
    groups = {}
    for k, off in enumerate(offs):
        q, m = divmod(off, 8)
        groups.setdefault(m, []).append((k, q))
    return groups


def _residue_window(ext, shifted, m, taps, base, cs):
    src = ext if m == 0 else shifted.at[m - 1]
    return src[pl.ds(base, RC + 8 * max(q for _, q in taps)), cs]


def _tap_sum(out_ref, bias_ref, w_ref, ext, shifted, offs, tm):
    groups = _by_residue(offs)

    def chunk(j, carry):
        base = pl.multiple_of(j * RC, RC)
        for c in range(D // LANES):
            cs = pl.ds(c * LANES, LANES)
            acc = jnp.zeros((RC, LANES), F32)
            if bias_ref is not None:
                acc = acc + bias_ref[:, cs]
            for m, taps in groups.items():
                big = _residue_window(ext, shifted, m, taps, base, cs)
                for k, q in taps:
                    acc = acc + w_ref[pl.ds(k, 1), cs] * big[8 * q:8 * q + RC]
            out_ref[pl.ds(base, RC), cs] = acc
        return carry

    lax.fori_loop(0, tm // RC, chunk, 0)


def _tap_corr(dw_ref, dc_ext, ext, shifted, offs, tm):
    groups = _by_residue(offs)
    for c in range(D // LANES):
        cs = pl.ds(c * LANES, LANES)

        def chunk(j, accs, cs=cs):
            base = pl.multiple_of(j * RC, RC)
            dcv = dc_ext[pl.ds(base, RC), cs]
            out = list(accs)
            for m, taps in groups.items():
                big = _residue_window(ext, shifted, m, taps, base, cs)
                for k, q in taps:
                    prod = dcv * big[8 * q:8 * q + RC]
                    part = prod[0:8]
                    for s in range(1, RC // 8):
                        part = part + prod[8 * s:8 * s + 8]
                    out[k] = accs[k] + part
            return tuple(out)

        accs = lax.fori_loop(0, tm // RC, chunk, tuple(jnp.zeros((8, LANES), F32) for _ in offs))
        for k in range(len(offs)):
            dw_ref[pl.ds(k, 1), cs] += jnp.sum(accs[k], axis=0, keepdims=True)


def _conv_fwd(name, p, conv_w, conv_b, ln_g, ln_b, tm):
    t = p.shape[0]

    def body(av_ref, ag_ref, w_ref, cb_ref, lg_ref, lb_ref, c_ref, a_ref, ext, shifted):
        i = pl.program_id(0)

        @pl.when(i == 0)
        def _():
            ext[pl.ds(0, HALO), :] = jnp.zeros((HALO, D), F32)

        ext[pl.ds(HALO, tm), :] = av_ref[...].astype(F32) * _sigmoid(ag_ref[...].astype(F32))
        _shift_copies(ext, shifted, tm)
        _tap_sum(c_ref, cb_ref, w_ref, ext, shifted, [HALO - (CW - 1) + k for k in range(CW)], tm)
        rstd, chat = _ln_stats(c_ref[...])
        ca = chat * lg_ref[...] + lb_ref[...]
        a_ref[...] = (ca * _sigmoid(ca)).astype(BF16)
        ext[pl.ds(0, HALO), :] = ext[pl.ds(tm, HALO), :]

    return pl.pallas_call(
        body, name=name, grid=(t // tm,),
        in_specs=[_row(tm, D, 0), _row(tm, D, 1), _const((HALO, D)), _const((1, D)),
                  _const((1, D)), _const((1, D))],
        out_specs=[_row(tm, D), _row(tm, D)],
        out_shape=[_sds((t, D), F32), _sds((t, D), BF16)],
        scratch_shapes=[pltpu.VMEM((tm + HALO, D), F32), pltpu.VMEM((7, tm + HALO - 8, D), F32)],
        compiler_params=_params("arbitrary"),
    )(p, p, conv_w, conv_b, ln_g, ln_b)


def _sgu_mask():
    rows = lax.broadcasted_iota(jnp.int32, (CHUNK, CHUNK), 0)
    cols = lax.broadcasted_iota(jnp.int32, (CHUNK, CHUNK), 1)
    return cols <= rows


def _sgu_fwd(name, p, ln_g, ln_b, sgu_w, bias_full, tm):
    t = p.shape[0]

    def body(bu_ref, bv_ref, lg_ref, lb_ref, ws_ref, bias_ref, o_ref):
        mask = _sgu_mask()
        _, vhat = _ln_stats(_gelu(bv_ref[...].astype(F32)))
        vn = (vhat * lg_ref[...] + lb_ref[...]).astype(BF16)
        ub = _gelu(bu_ref[...].astype(F32))
        for g in range(GROUPS):
            wm = jnp.where(mask, ws_ref[g], 0.0).astype(BF16)
            cs = slice(g * GD, (g + 1) * GD)
            for cc in range(tm // CHUNK):
                rs = slice(cc * CHUNK, (cc + 1) * CHUNK)
                mixed = _dot(wm, vn[rs, cs]) + bias_ref[:, cs]
                o_ref[rs, cs] = (ub[rs, cs] * mixed).astype(BF16)

    return pl.pallas_call(
        body, name=name, grid=(t // tm,),
        in_specs=[_row(tm, D, 2), _row(tm, D, 3), _const((1, D)), _const((1, D)),
                  _const((GROUPS, CHUNK, CHUNK)), _const((CHUNK, D))],
        out_specs=_row(tm, D), out_shape=_sds((t, D), BF16),
        compiler_params=_params("parallel"),
    )(p, p, ln_g, ln_b, sgu_w, bias_full)


def _merge_fwd(name, act_a, act_b, p, w_a, w_b, w_out, h, gain, tm):
    t = h.shape[0]

    def body(a_ref, b_ref, ga_ref, gb_ref, wa_ref, wb_ref, wo_ref, h_ref, g_ref,
             sa_ref, sb_ref, fa_ref, fb_ref, mg_ref, ho_ref, xn_ref):
        ya = _dot(a_ref[...], wa_ref[...])
        yb = _dot(b_ref[...], wb_ref[...])
        sa = _sigmoid(ga_ref[...].astype(F32))
        sb = _sigmoid(gb_ref[...].astype(F32))
        ga = sa * ya
        gb = sb * yb
        sa_ref[...] = sa.astype(BF16)
        sb_ref[...] = sb.astype(BF16)
        fa_ref[...] = (ga * (1.0 - sa)).astype(BF16)
        fb_ref[...] = (gb * (1.0 - sb)).astype(BF16)
        merged = (ga + gb).astype(BF16)
        mg_ref[...] = merged
        hn = h_ref[...] + _dot(merged, wo_ref[...])
        ho_ref[...] = hn
        _, xh = _rms_stats(hn)
        xn_ref[...] = (xh * g_ref[...]).astype(BF16)

    rb = _row(tm, D)
    return pl.pallas_call(
        body, name=name, grid=(t // tm,),
        in_specs=[rb, rb, _row(tm, D, 4), _row(tm, D, 5), _const((D, D)), _const((D, D)),
                  _const((D, D)), rb, _const((1, D))],
        out_specs=[rb] * 7,
        out_shape=[_sds((t, D), BF16)] * 5 + [_sds((t, D), F32), _sds((t, D), BF16)],
        compiler_params=_params("parallel"),
    )(act_a, act_b, p, p, w_a, w_b, w_out, h, gain)


def _kv_fwd(name, mem, gain, wkv_t):
    def body(m_ref, g_ref, w_ref, mn_ref, k_ref, v_ref):
        _, xh = _rms_stats(m_ref[...])
        mn = (xh * g_ref[...]).astype(BF16)
        mn_ref[...] = mn
        kv = _dot_nt(mn, w_ref[...])
        k_ref[...] = kv[:, :D].astype(BF16)
        v_ref[...] = kv[:, D:].astype(BF16)

    return pl.pallas_call(
        body, name=name,
        out_shape=[_sds((NMEM, D), BF16)] * 3,
        compiler_params=pltpu.CompilerParams(vmem_limit_bytes=VMEM_LIMIT),
    )(mem, gain, wkv_t)


def _softmax_rows(s):
    e = jnp.exp(s - jnp.max(s, axis=-1, keepdims=True))
    return e / jnp.sum(e, axis=-1, keepdims=True)


def _attn_fwd(name, xq, w_q, kb, vb, w_o, h, gain, tm):
    t = h.shape[0]
    scale = 1.0 / math.sqrt(HD)

    def body(x_ref, wq_ref, k_ref, v_ref, wo_ref, h_ref, g_ref, q_ref, p_ref, o_ref, ho_ref, xn_ref):
        q_ref[...] = _dot(x_ref[...], wq_ref[...]).astype(BF16)
        for hd in range(HEADS):
            cs = slice(hd * HD, (hd + 1) * HD)
            ps = slice(hd * NMEM, (hd + 1) * NMEM)
            p_ref[:, ps] = _softmax_rows(_dot_nt(q_ref[:, cs], k_ref[:, cs]) * scale).astype(BF16)
            o_ref[:, cs] = _dot(p_ref[:, ps], v_ref[:, cs]).astype(BF16)
        hn = h_ref[...] + _dot(o_ref[...], wo_ref[...])
        ho_ref[...] = hn
        _, xh = _rms_stats(hn)
        xn_ref[...] = (xh * g_ref[...]).astype(BF16)

    rb = _row(tm, D)
    return pl.pallas_call(
        body, name=name, grid=(t // tm,),
        in_specs=[rb, _const((D, D)), _const((NMEM, D)), _const((NMEM, D)), _const((D, D)), rb,
                  _const((1, D))],
        out_specs=[rb, _row(tm, HEADS * NMEM), rb, rb, rb],
        out_shape=[_sds((t, D), BF16), _sds((t, HEADS * NMEM), BF16), _sds((t, D), BF16),
                   _sds((t, D), F32), _sds((t, D), BF16)],
        compiler_params=_params("parallel"),
    )(xq, w_q, kb, vb, w_o, h, gain)


def _ffn_down_loss(name, a, wd, h, gain, target, tm):
    t = h.shape[0]
    steps = t // tm

    def body(a_ref, w_ref, h_ref, g_ref, t_ref, dh_ref, dhb_ref, loss_ref, dg_ref, lacc):
        i = pl.program_id(0)
        hv = h_ref[...] + 0.5 * _dot(a_ref[...], w_ref[...])
        r, xh = _rms_stats(hv)
        err = xh * g_ref[...] - t_ref[...]
        _acc(lacc, i == 0, jnp.sum(err * err, axis=0, keepdims=True))
        dy = err * (1.0 / D)
        _acc(dg_ref, i == 0, jnp.sum(dy * xh, axis=0, keepdims=True))
        dxh = dy * g_ref[...]
        dh = r * (dxh - xh * jnp.mean(dxh * xh, axis=-1, keepdims=True))
        dh_ref[...] = dh
        dhb_ref[...] = dh.astype(BF16)

        @pl.when(i == steps - 1)
        def _():
            loss_ref[...] = jnp.zeros((8, 128), F32) + (0.5 / D) * jnp.sum(lacc[...])

    rb = _row(tm, D)
    return pl.pallas_call(
        body, name=name, grid=(steps,),
        in_specs=[_row(tm, DFF), _const((DFF, D)), rb, _const((1, D)), rb],
        out_specs=[rb, rb, _const((8, 128)), _const((1, D))],
        out_shape=[_sds((t, D), F32), _sds((t, D), BF16), _sds((8, 128), F32), _sds((1, D), F32)],
        scratch_shapes=[pltpu.VMEM((1, D), F32)],
        compiler_params=_params("arbitrary"),
    )(a, wd, h, gain, target)


def _ffn_bwd_act(name, dhb, wd, s, f, tm, tn, hosted=None):
    t = dhb.shape[0]

    def body(d_ref, w_ref, s_ref, f_ref, dg_ref, du_ref):
        da = 0.5 * _dot_nt(d_ref[...], w_ref[...])
        dg_ref[...] = (da * f_ref[...].astype(F32)).astype(BF16)
        du_ref[...] = (da * s_ref[...].astype(F32)).astype(BF16)

    o = pl.BlockSpec((tm, tn), lambda j, i: (i, j))
    return _call(name, body, (DFF // tn, t // tm),
                 [pl.BlockSpec((tm, D), lambda j, i: (i, 0)),
                  pl.BlockSpec((tn, D), lambda j, i: (j, 0)), o, o],
                 [o, o], [_sds((t, DFF), BF16)] * 2, (dhb, wd, s, f),
                 sem=("parallel", "parallel"), hosted=hosted)


def _dx_rms_bwd(name, pairs, h, gain, dh_in, tm, hosted=None):
    t = h.shape[0]
    np_ = len(pairs)

    def body(*refs):
        a_refs = refs[:np_]
        b_refs = refs[np_:2 * np_]
        h_ref, g_ref, d_ref, o_ref, ob_ref, dg_ref = refs[2 * np_:]
        dxn = None
        for (a_ref, b_ref, pr) in zip(a_refs, b_refs, pairs):
            y = _dot_nt(a_ref[...], b_ref[...]) if pr[4] else _dot(a_ref[...], b_ref[...])
            dxn = y if dxn is None else dxn + y
        dh, dgain = _rms_bwd(dxn, h_ref[...], g_ref[...])
        _acc(dg_ref, pl.program_id(0) == 0, dgain)
        out = d_ref[...] + dh
        o_ref[...] = out
        ob_ref[...] = out.astype(BF16)

    ins, args = [], []
    for (a, b, blk, rows, tr) in pairs:
        ins.append(_row(tm, a.shape[1]))
        args.append(a)
    for (a, b, blk, rows, tr) in pairs:
        ins.append(pl.BlockSpec((rows, b.shape[1]), lambda i, _b=blk: (_b, 0)))
        args.append(b)
    rb = _row(tm, D)
    ins += [rb, _const((1, D)), rb]
    args += [h, gain, dh_in]
    return _call(name, body, (t // tm,), ins, [rb, rb, _const((1, D))],
                 [_sds((t, D), F32), _sds((t, D), BF16), _sds((1, D), F32)], args,
                 sem=("arbitrary",), hosted=hosted)


class _Dest:
    def __init__(self, buf, total_rows, off, rows, row0=0):
        self.buf, self.total_rows, self.off, self.rows, self.row0 = buf, total_rows, off, rows, row0

    def segments(self, lo, hi):
        out = []
        for d in range(NDEV):
            a, b = max(lo + self.row0, d * self.rows), min(hi + self.row0, (d + 1) * self.rows)
            if a < b:
                out.append((a - self.row0 - lo, d, self.off + a - d * self.rows, b - a))
        return out


def _store_segments(stage, buf_ref, sems, segs):
    cps = [pltpu.make_async_copy(stage.at[pl.ds(s0, n)], buf_ref.at[d, pl.ds(r0, n)], sems.at[j])
           for j, (s0, d, r0, n) in enumerate(segs)]
    for cp in cps:
        cp.start()
    for cp in cps:
        cp.wait()


def _mm_tn(name, a, b, scale, tmo, tk, dest):
    t, m = a.shape
    n = b.shape[1]
    tk = min(tk, t)
    steps = t // tk
    tiles = m // tmo
    seg_lists = [dest.segments(i * tmo, (i + 1) * tmo) for i in range(tiles)]
    fresh = dest.buf is None

    def body(*refs):
        a_ref, b_ref = refs[0], refs[1]
        buf_ref, acc, stage, sems = refs[-4:]
        i, k = pl.program_id(0), pl.program_id(1)
        _acc(acc, k == 0, _dot_tn(a_ref[...], b_ref[...]))

        @pl.when(k == steps - 1)
        def _():
            stage[...] = (acc[...] * scale).astype(BF16)
            for ti, segs in enumerate(seg_lists):
                @pl.when(i == ti)
                def _(segs=segs):
                    _store_segments(stage, buf_ref, sems, segs)

    ins = [pl.BlockSpec((tk, tmo), lambda i, k: (k, i)), pl.BlockSpec((tk, n), lambda i, k: (k, 0))]
    args = [a, b]
    if not fresh:
        ins.append(_ANY)
        args.append(dest.buf)
    return pl.pallas_call(
        body, name=name, grid=(tiles, steps), in_specs=ins, out_specs=_ANY,
        out_shape=_sds((NDEV, dest.total_rows, n), BF16),
        input_output_aliases={} if fresh else {2: 0},
        scratch_shapes=[pltpu.VMEM((tmo, n), F32), pltpu.VMEM((tmo, n), BF16),
                        pltpu.SemaphoreType.DMA((max(len(s) for s in seg_lists),))],
        compiler_params=_params("arbitrary", "arbitrary"),
    )(*args)


def _attn_bwd(name, dhb, w_o, qb, pb, kb, vb, tm):
    t = dhb.shape[0]
    scale = 1.0 / math.sqrt(HD)

    def body(d_ref, wo_ref, q_ref, p_ref, k_ref, v_ref, dq_ref, dk_ref, dv_ref, do_s):
        i = pl.program_id(0)

        @pl.when(i == 0)
        def _():
            dk_ref[...] = jnp.zeros_like(dk_ref)
            dv_ref[...] = jnp.zeros_like(dv_ref)

        do_s[...] = _dot_nt(d_ref[...], wo_ref[...]).astype(BF16)
        for hd in range(HEADS):
            cs = slice(hd * HD, (hd + 1) * HD)
            pb16 = p_ref[:, hd * NMEM:(hd + 1) * NMEM]
            p = pb16.astype(F32)
            do = do_s[:, cs]
            dp = _dot_nt(do, v_ref[:, cs])
            ds = (p * (dp - jnp.sum(dp * p, axis=-1, keepdims=True)) * scale).astype(BF16)
            dq_ref[:, cs] = _dot(ds, k_ref[:, cs]).astype(BF16)
            dk_ref[:, cs] += _dot_tn(ds, q_ref[:, cs])
            dv_ref[:, cs] += _dot_tn(pb16, do)

    rb = _row(tm, D)
    return pl.pallas_call(
        body, name=name, grid=(t // tm,),
        in_specs=[rb, _const((D, D)), rb, _row(tm, HEADS * NMEM), _const((NMEM, D)), _const((NMEM, D))],
        out_specs=[rb, _const((NMEM, D)), _const((NMEM, D))],
        out_shape=[_sds((t, D), BF16), _sds((NMEM, D), F32), _sds((NMEM, D), F32)],
        scratch_shapes=[pltpu.VMEM((tm, D), BF16)],
        compiler_params=_params("arbitrary"),
    )(dhb, w_o, qb, pb, kb, vb)


def _kv_bwd(name, dk, dv, memn, wkv_t, mem, gain, dest):
    segs = dest.segments(0, 2 * D)
    vmem = pl.BlockSpec(memory_space=pltpu.VMEM)

    def body(dk_ref, dv_ref, mn_ref, w_ref, m_ref, g_ref, buf_in, buf_ref, dg_ref, stage, sems):
        dkb = dk_ref[...].astype(BF16)
        dvb = dv_ref[...].astype(BF16)
        mn = mn_ref[...]
        stage[pl.ds(0, D), :] = _dot_tn(dkb, mn).astype(BF16)
        stage[pl.ds(D, D), :] = _dot_tn(dvb, mn).astype(BF16)
        _store_segments(stage, buf_ref, sems, segs)
        dmn = _dot(dkb, w_ref[pl.ds(0, D), :]) + _dot(dvb, w_ref[pl.ds(D, D), :])
        _, xh = _rms_stats(m_ref[...])
        dg_ref[...] = jnp.sum(dmn * xh, axis=0, keepdims=True)

    return pl.pallas_call(
        body, name=name,
        in_specs=[vmem] * 6 + [_ANY], out_specs=[_ANY, vmem],
        out_shape=[_sds(dest.buf.shape, BF16), _sds((1, D), F32)],
        input_output_aliases={6: 0},
        scratch_shapes=[pltpu.VMEM((2 * D, D), BF16), pltpu.SemaphoreType.DMA((len(segs),))],
        compiler_params=pltpu.CompilerParams(vmem_limit_bytes=VMEM_LIMIT),
    )(dk, dv, memn, wkv_t, mem, gain, dest.buf)


def _merge_bwd(name, dhb, w_out, sa, sb, fa, fb, tm):
    t = dhb.shape[0]

    def body(d_ref, w_ref, sa_ref, sb_ref, fa_ref, fb_ref, dya_ref, dyb_ref, dp_ref, cs_ref):
        dm = _dot_nt(d_ref[...], w_ref[...])
        dya_ref[...] = (dm * sa_ref[...].astype(F32)).astype(BF16)
        dyb_ref[...] = (dm * sb_ref[...].astype(F32)).astype(BF16)
        dga = dm * fa_ref[...].astype(F32)
        dgb = dm * fb_ref[...].astype(F32)
        dp_ref[:, pl.ds(0, D)] = dga.astype(BF16)
        dp_ref[:, pl.ds(D, D)] = dgb.astype(BF16)
        first = pl.program_id(0) == 0

        @pl.when(first)
        def _():
            cs_ref[...] = jnp.zeros_like(cs_ref)
        cs_ref[:, pl.ds(0, D)] += jnp.sum(dga, axis=0, keepdims=True)
        cs_ref[:, pl.ds(D, D)] += jnp.sum(dgb, axis=0, keepdims=True)

    rb = _row(tm, D)
    return pl.pallas_call(
        body, name=name, grid=(t // tm,),
        in_specs=[rb, _const((D, D)), rb, rb, rb, rb],
        out_specs=[rb, rb, _row(tm, 2 * D), _const((1, 2 * D))],
        out_shape=[_sds((t, D), BF16), _sds((t, D), BF16), _sds((t, 2 * D), BF16),
                   _sds((1, 2 * D), F32)],
        compiler_params=_params("arbitrary"),
    )(dhb, w_out, sa, sb, fa, fb)


def _conv_bwd(name, dya, w_a, c, p, conv_w, ln_g, ln_b, tm, hosted=None):
    t = dya.shape[0]
    steps = t // tm
    hb = tm // HALO

    def rev(i):
        return steps - 1 - i

    def body(dy_ref, wa_ref, c_ref, av_ref, ag_ref, avh_ref, agh_ref, w_ref, lg_ref, lb_ref,
             dp_ref, cs_ref, dw_ref, dcb_ref, dlg_ref, dlb_ref, dc_ext, a_ext, dc_sh, a_sh, da0_s):
        i = pl.program_id(0)
        first = i == 0

        @pl.when(first)
        def _():
            dc_ext[pl.ds(tm, HALO), :] = jnp.zeros((HALO, D), F32)
            dw_ref[...] = jnp.zeros_like(dw_ref)
            cs_ref[...] = jnp.zeros_like(cs_ref)

        d_act = _dot_nt(dy_ref[...], wa_ref[...])
        rstd, chat = _ln_stats(c_ref[...])
        ca = chat * lg_ref[...] + lb_ref[...]
        sc = _sigmoid(ca)
        dca = d_act * (sc * (1.0 + ca * (1.0 - sc)))
        _acc(dlg_ref, first, jnp.sum(dca * chat, axis=0, keepdims=True))
        _acc(dlb_ref, first, jnp.sum(dca, axis=0, keepdims=True))
        dc = _ln_bwd(dca, chat, rstd, lg_ref[...])
        _acc(dcb_ref, first, jnp.sum(dc, axis=0, keepdims=True))
        dc_ext[pl.ds(0, tm), :] = dc

        av = av_ref[...].astype(F32)
        sg = _sigmoid(ag_ref[...].astype(F32))
        a_ext[pl.ds(HALO, tm), :] = av * sg
        halo = avh_ref[...].astype(F32) * _sigmoid(agh_ref[...].astype(F32))
        a_ext[pl.ds(0, HALO), :] = jnp.where(i == steps - 1, 0.0, halo)

        _shift_copies(dc_ext, dc_sh, tm)
        _shift_copies(a_ext, a_sh, tm)
        _tap_sum(da0_s, None, w_ref, dc_ext, dc_sh, [CW - 1 - k for k in range(CW)], tm)
        _tap_corr(dw_ref, dc_ext, a_ext, a_sh, [HALO - (CW - 1) + k for k in range(CW)], tm)
        da0 = da0_s[...]
        dav = da0 * sg
        dag = da0 * av * (sg * (1.0 - sg))
        dp_ref[:, pl.ds(0, D)] = dav.astype(BF16)
        dp_ref[:, pl.ds(D, D)] = dag.astype(BF16)
        cs_ref[:, pl.ds(0, D)] += jnp.sum(dav, axis=0, keepdims=True)
        cs_ref[:, pl.ds(D, D)] += jnp.sum(dag, axis=0, keepdims=True)
        dc_ext[pl.ds(tm, HALO), :] = dc_ext[pl.ds(0, HALO), :]

    def rrow(cols, cb=0):
        return pl.BlockSpec((tm, cols), lambda i, _cb=cb: (rev(i), _cb))

    def halo_spec(cb):
        return pl.BlockSpec((HALO, D), lambda i, _cb=cb: (jnp.maximum(rev(i) * hb - 1, 0), _cb))

    return _call(
        name, body, (steps,),
        [rrow(D), _const((D, D)), rrow(D), rrow(D, 0), rrow(D, 1), halo_spec(0),
         halo_spec(1), _const((HALO, D)), _const((1, D)), _const((1, D))],
        [rrow(2 * D), _const((1, 2 * D)), _const((HALO, D)), _const((1, D)),
         _const((1, D)), _const((1, D))],
        [_sds((t, 2 * D), BF16), _sds((1, 2 * D), F32), _sds((HALO, D), F32),
         _sds((1, D), F32), _sds((1, D), F32), _sds((1, D), F32)],
        (dya, w_a, c, p, p, p, p, conv_w, ln_g, ln_b),
        scratch=[pltpu.VMEM((tm + HALO, D), F32), pltpu.VMEM((tm + HALO, D), F32),
                 pltpu.VMEM((7, tm + HALO - 8, D), F32), pltpu.VMEM((7, tm + HALO - 8, D), F32),
                 pltpu.VMEM((tm, D), F32)],
        sem=("arbitrary",), hosted=hosted)


def _sgu_bwd(name, dyb, w_b, p, ln_g, ln_b, sgu_w, bias_full, tm):
    t = dyb.shape[0]
    steps = t // tm

    def body(dy_ref, wb_ref, bu_ref, bv_ref, lg_ref, lb_ref, ws_ref, bias_ref,
             dp_ref, cs_ref, dws_ref, dsb_ref, dlg_ref, dlb_ref, dub_s, dvn_s, dbias_s):
        i = pl.program_id(0)
        first = i == 0
        mask = _sgu_mask()

        @pl.when(first)
        def _():
            dws_ref[...] = jnp.zeros_like(dws_ref)
            dbias_s[...] = jnp.zeros_like(dbias_s)
            cs_ref[...] = jnp.zeros_like(cs_ref)

        dob = _dot_nt(dy_ref[...], wb_ref[...])
        bu = bu_ref[...].astype(F32)
        bv = bv_ref[...].astype(F32)
        ub, ub_grad = _gelu_with_grad(bu)
        vb, vb_grad = _gelu_with_grad(bv)
        rstd, vhat = _ln_stats(vb)
        vn = (vhat * lg_ref[...] + lb_ref[...]).astype(BF16)
        for g in range(GROUPS):
            wm = jnp.where(mask, ws_ref[g], 0.0).astype(BF16)
            cs = slice(g * GD, (g + 1) * GD)
            for cc in range(tm // CHUNK):
                rs = slice(cc * CHUNK, (cc + 1) * CHUNK)
                vblk = vn[rs, cs]
                mixed = _dot(wm, vblk) + bias_ref[:, cs]
                dob_blk = dob[rs, cs]
                dub_s[rs, cs] = dob_blk * mixed
                dmixed = dob_blk * ub[rs, cs]
                dbias_s[:, cs] += dmixed
                dmb = dmixed.astype(BF16)
                dws_ref[g] += _dot_nt(dmb, vblk)
                dvn_s[rs, cs] = _dot_tn(wm, dmb)
        dbu = dub_s[...] * ub_grad
        dvn = dvn_s[...]
        _acc(dlg_ref, first, jnp.sum(dvn * vhat, axis=0, keepdims=True))
        _acc(dlb_ref, first, jnp.sum(dvn, axis=0, keepdims=True))
        dbv = _ln_bwd(dvn, vhat, rstd, lg_ref[...]) * vb_grad
        dp_ref[:, pl.ds(0, D)] = dbu.astype(BF16)
        dp_ref[:, pl.ds(D, D)] = dbv.astype(BF16)
        cs_ref[:, pl.ds(0, D)] += jnp.sum(dbu, axis=0, keepdims=True)
        cs_ref[:, pl.ds(D, D)] += jnp.sum(dbv, axis=0, keepdims=True)

        @pl.when(i == steps - 1)
        def _():
            lane = lax.broadcasted_iota(jnp.int32, (CHUNK, CHUNK), 1)
            dsb = jnp.zeros((CHUNK, CHUNK), F32)
            for g in range(GROUPS):
                dws_ref[g] = jnp.where(mask, dws_ref[g], 0.0)
                dsb = jnp.where(lane == g, jnp.sum(dbias_s[:, g * GD:(g + 1) * GD], axis=1, keepdims=True), dsb)
            dsb_ref[...] = dsb

    rb = _row(tm, D)
    return pl.pallas_call(
        body, name=name, grid=(steps,),
        in_specs=[rb, _const((D, D)), _row(tm, D, 2), _row(tm, D, 3), _const((1, D)), _const((1, D)),
                  _const((GROUPS, CHUNK, CHUNK)), _const((CHUNK, D))],
        out_specs=[_row(tm, 2 * D), _const((1, 2 * D)), _const((GROUPS, CHUNK, CHUNK)),
                   _const((CHUNK, CHUNK)), _const((1, D)), _const((1, D))],
        out_shape=[_sds((t, 2 * D), BF16), _sds((1, 2 * D), F32), _sds((GROUPS, CHUNK, CHUNK), F32),
                   _sds((CHUNK, CHUNK), F32), _sds((1, D), F32), _sds((1, D), F32)],
        scratch_shapes=[pltpu.VMEM((tm, D), F32), pltpu.VMEM((tm, D), F32), pltpu.VMEM((CHUNK, D), F32)],
        compiler_params=_params("arbitrary"),
    )(dyb, w_b, p, p, ln_g, ln_b, sgu_w, bias_full)


def _adam_math(w, g, m, v):
    m = B1 * m + (1.0 - B1) * g
    v = B2 * v + (1.0 - B2) * (g * g)
    m_hat = m / (1.0 - B1 ** STEP)
    v_hat = v / (1.0 - B2 ** STEP)
    delta = -LR * (m_hat / (jnp.sqrt(v_hat) + EPS_ADAM) + WD * w)
    return delta, m, v


def _adamw(name, w, g, m, v, tr):
    r, cdim = w.shape

    def body(w_ref, g_ref, m_ref, v_ref, d_ref, mo_ref, vo_ref):
        d, mn, vn = _adam_math(w_ref[...], g_ref[...], m_ref[...], v_ref[...])
        d_ref[...] = d
        mo_ref[...] = mn
        vo_ref[...] = vn

    blk = pl.BlockSpec((tr, cdim), lambda i: (i, 0))
    return pl.pallas_call(
        body, name=name, grid=(r // tr,), in_specs=[blk] * 4, out_specs=[blk] * 3,
        out_shape=[_sds((r, cdim), F32)] * 3, compiler_params=_params("parallel"),
    )(w, g, m, v)


def _adamw_small(name, w, g8, m, v):
    r, cdim = w.shape

    def body(w_ref, g_ref, m_ref, v_ref, go_ref, d_ref, mo_ref, vo_ref):
        g = g_ref[0]
        for k in range(1, NDEV):
            g = g + g_ref[k]
        go_ref[...] = g
        d, mn, vn = _adam_math(w_ref[...], g, m_ref[...], v_ref[...])
        d_ref[...] = d
        mo_ref[...] = mn
        vo_ref[...] = vn

    return pl.pallas_call(
        body, name=name, out_shape=[_sds((r, cdim), F32)] * 4,
        compiler_params=pltpu.CompilerParams(vmem_limit_bytes=VMEM_LIMIT),
    )(w, g8, m, v)


_BIG = [("ffn1_w_gu", 704, True), ("ffn1_w_down", 352, False), ("w_in", 768, True),
        ("w_a_out", 128, False), ("w_b_out", 128, False), ("w_out", 128, False),
        ("w_q", 128, False), ("w_kv", 256, True), ("w_o", 128, False),
        ("ffn2_w_gu", 704, True), ("ffn2_w_down", 352, False)]
_BIG_ROWS = sum(r for _, r, _ in _BIG)

_SMALL = [("ffn1_norm", 1), ("mix_norm", 1), ("b_in", 6), ("conv_w", HALO), ("conv_b", 1),
          ("conv_ln_g", 1), ("conv_ln_b", 1), ("sgu_ln_g", 1), ("sgu_ln_b", 1), ("sgu_w", 64),
          ("sgu_b", 1), ("xattn_norm", 1), ("mem_norm", 1), ("ffn2_norm", 1), ("final_norm", 1)]
_SMALL_ROWS = 120


def _pack_small(vals, my_dev):
    rows = []
    for name, nrows in _SMALL:
        a = vals[name].astype(F32)
        if name == "conv_w":
            if a.shape[-1] != D:
                slab = jnp.zeros((HALO, D), F32)
                a = lax.dynamic_update_slice(slab, jnp.pad(a.reshape(CW, -1), ((0, HALO - CW), (0, 0))),
                                             (0, my_dev * (D // NDEV)))
            else:
                a = jnp.pad(a.reshape(CW, D), ((0, HALO - CW), (0, 0)))
        elif name == "sgu_b":
            a = jnp.pad(a.reshape(1, -1), ((0, 0), (0, D - GROUPS * CHUNK)))
        else:
            a = a.reshape(nrows, D)
        rows.append(a)
    packed = jnp.concatenate(rows, axis=0)
    return jnp.pad(packed, ((0, _SMALL_ROWS - packed.shape[0]), (0, 0)))


def _unpack_small(packed, shapes, my_dev):
    out, off = {}, 0
    for name, nrows in _SMALL:
        a = packed[off:off + nrows]
        off += nrows
        if name == "conv_w":
            a = lax.dynamic_slice(a, (0, my_dev * (D // NDEV)), (CW, D // NDEV))
        elif name == "sgu_b":
            a = a[:, :GROUPS * CHUNK]
        out[name] = a.reshape(shapes[name])
    return out


def kernel(x, mem, ffn1_norm, ffn1_w_gu, ffn1_w_down, mix_norm, w_in, b_in, conv_w, conv_b, conv_ln_g, conv_ln_b, w_a_out, sgu_ln_g, sgu_ln_b, sgu_w, sgu_b, w_b_out, w_out, xattn_norm, mem_norm, w_q, w_kv, w_o, ffn2_norm, ffn2_w_gu, ffn2_w_down, final_norm, loss_target, m_ffn1_norm, m_ffn1_w_gu, m_ffn1_w_down, m_mix_norm, m_w_in, m_b_in, m_conv_w, m_conv_b, m_conv_ln_g, m_conv_ln_b, m_w_a_out, m_sgu_ln_g, m_sgu_ln_b, m_sgu_w, m_sgu_b, m_w_b_out, m_w_out, m_xattn_norm, m_mem_norm, m_w_q, m_w_kv, m_w_o, m_ffn2_norm, m_ffn2_w_gu, m_ffn2_w_down, m_final_norm, v_ffn1_norm, v_ffn1_w_gu, v_ffn1_w_down, v_mix_norm, v_w_in, v_b_in, v_conv_w, v_conv_b, v_conv_ln_g, v_conv_ln_b, v_w_a_out, v_sgu_ln_g, v_sgu_ln_b, v_sgu_w, v_sgu_b, v_w_b_out, v_w_out, v_xattn_norm, v_mem_norm, v_w_q, v_w_kv, v_w_o, v_ffn2_norm, v_ffn2_w_gu, v_ffn2_w_down, v_final_norm):
    env = dict(locals())
    names = [n for n, _, _ in _BIG] + [n for n, _ in _SMALL]
    w = {n: env[n] for n in names}
    mom = {n: env["m_" + n] for n in names}
    vel = {n: env["v_" + n] for n in names}

    ax, ay, ac = lax.axis_index("x"), lax.axis_index("y"), lax.axis_index("c")
    my_chip = 2 * ax + ay
    my_dev = 2 * my_chip + ac

    t = x.shape[1]
    tm = min(512, t)
    tm_big = min(1024, t)
    tm_s = min(512, t)
    tm_c = min(256, t)
    xs = x.reshape(t, D)
    tgt = loss_target.reshape(t, D)
    mem2 = mem.reshape(NMEM, D)

    first, mid, late = _BIG[:1], _BIG[1:6], _BIG[6:]

    def gathers(entries):
        return [_gather_comm((w[n][0].T if tr else w[n][0]).astype(BF16)) for n, _, tr in entries]

    def whole(gathered, entries):
        return {n: g.reshape(NDEV * rows, D) for g, (n, rows, _) in zip(gathered, entries)}

    conv_slab = lax.dynamic_update_slice(
        jnp.zeros((HALO, D), F32), jnp.pad(conv_w[0], ((0, HALO - CW), (0, 0))), (0, my_dev * (D // NDEV)))
    bias_full = jnp.repeat(sgu_b[0].T, GD, axis=1)
    b_in2 = b_in.reshape(1, 6 * D)

    (xn1,), (*full_first, conv_w8) = _rms_cast(
        "norm_x", xs, ffn1_norm, tm, hosted=gathers(first) + [_gather_comm(conv_slab)])
    conv_w_pad = jnp.sum(conv_w8, axis=0)
    wf = whole(full_first, first)
    (g1, u1, a1), full_mid = _ffn_up("ffn1_up", xn1, wf["ffn1_w_gu"], tm_big, 1408, hosted=gathers(mid))
    wf.update(whole(full_mid, mid))
    h1, n_mix = _ffn_down("ffn1_down", a1, wf["ffn1_w_down"], xs, mix_norm, tm)
    (p,), full_late = _mix_in("mix_in", n_mix, wf["w_in"], b_in2, tm_big, 1536, hosted=gathers(late))
    wf.update(whole(full_late, late))
    c_conv, act_a = _conv_fwd("conv_fwd", p, conv_w_pad, conv_b, conv_ln_g, conv_ln_b, tm_c)
    act_b = _sgu_fwd("sgu_fwd", p, sgu_ln_g, sgu_ln_b, sgu_w[0], bias_full, tm_s)
    gate_a, gate_b, fac_a, fac_b, merged, h2, xq = _merge_fwd(
        "merge_fwd", act_a, act_b, p, wf["w_a_out"], wf["w_b_out"], wf["w_out"], h1, xattn_norm, tm_s)
    memn, kb, vb = _kv_fwd("kv_fwd", mem2, mem_norm, wf["w_kv"])
    qb, probs, ob, h3, xn4 = _attn_fwd("attn_fwd", xq, wf["w_q"], kb, vb, wf["w_o"], h2, ffn2_norm, tm_s)
    (g2, u2, a2), _ = _ffn_up("ffn2_up", xn4, wf["ffn2_w_gu"], tm_big, 1408)
    dh4, dh4b, loss_blk, d_final = _ffn_down_loss("ffn2_down_loss", a2, wf["ffn2_w_down"], h3,
                                                  final_norm.reshape(1, D), tgt, tm)

    gs = {}
    gs["final_norm"] = d_final

    core = ac.astype(jnp.int32).reshape(1)
    chip = my_chip.astype(jnp.int32).reshape(1)
    last_g, mixer_g, attn_g = _BIG[:2], _BIG[2:6], _BIG[6:]

    def layout(entries):
        offs, off = {}, 0
        for n, rows, _ in entries:
            offs[n] = (off, rows)
            off += rows
        return offs, off

    def dest(group, buf, name, row0=0):
        offs, total = group
        return _Dest(buf, total, offs[name][0], offs[name][1], row0)

    lay_last, lay_mixer, lay_attn = layout(last_g), layout(mixer_g), layout(attn_g)

    def ffn_bwd(tag, dhb, dh, g, u, a, xn, h_in, gain, wgu_t, wd, group, buf, act_hosted=None, dx_hosted=None):
        (dg, du), act_out = _ffn_bwd_act(tag + "_bwd_act", dhb, wd, g, u, tm_big, 1408, hosted=act_hosted)
        buf = _mm_tn(tag + "_dw_down", a, dhb, 0.5, 1408, TK, dest(group, buf, tag + "_w_down"))
        buf = _mm_tn(tag + "_dw_gate", dg, xn, 1.0, 1408, TK, dest(group, buf, tag + "_w_gu"))
        buf = _mm_tn(tag + "_dw_up", du, xn, 1.0, 1408, TK, dest(group, buf, tag + "_w_gu", DFF))
        (dh_o, dhb_o, dgain), dx_out = _dx_rms_bwd(
            tag + "_bwd_dx", [(dg, wgu_t, 0, DFF, False), (du, wgu_t, 1, DFF, False)], h_in, gain, dh, tm_s,
            hosted=dx_hosted(buf) if dx_hosted else None)
        return dh_o, dhb_o, dgain, buf, act_out, dx_out

    dh3, dh3b, gs["ffn2_norm"], g_attn, _, _ = ffn_bwd(
        "ffn2", dh4b, dh4, g2, u2, a2, xn4, h3, ffn2_norm, wf["ffn2_w_gu"], wf["ffn2_w_down"], lay_attn, None)

    g_attn = _mm_tn("dw_o", ob, dh3b, 1.0, 1024, 2 * TK, dest(lay_attn, g_attn, "w_o"))
    dq, dk, dv = _attn_bwd("attn_bwd", dh3b, wf["w_o"], qb, probs, kb, vb, tm_s)
    g_attn, gs["mem_norm"] = _kv_bwd("kv_bwd", dk, dv, memn, wf["w_kv"], mem2, mem_norm,
                                     dest(lay_attn, g_attn, "w_kv"))
    g_attn = _mm_tn("dw_q", xq, dq, 1.0, 1024, 2 * TK, dest(lay_attn, g_attn, "w_q"))

    g4_attn = g_attn.reshape(4, 2, lay_attn[1], D)
    (dh2, dh2b, gs["xattn_norm"]), (sib_attn,) = _dx_rms_bwd(
        "attn_bwd_dx", [(dq, wf["w_q"], 0, D, True)], h2, xattn_norm, dh3, tm_s,
        hosted=_pair_exchange_comm(g4_attn))
    part_attn = _pair_sum("grads_pair_sum_attn", g4_attn, sib_attn, core, 784)

    g_mixer = _mm_tn("dw_out", merged, dh2b, 1.0, 1024, 2 * TK, dest(lay_mixer, None, "w_out"))
    dya, dyb, dp_g, cs_g = _merge_bwd("merge_bwd", dh2b, wf["w_out"], gate_a, gate_b, fac_a, fac_b, tm_s)
    g_mixer = _mm_tn("dw_a", act_a, dya, 1.0, 1024, 2 * TK, dest(lay_mixer, g_mixer, "w_a_out"))
    g_mixer = _mm_tn("dw_b", act_b, dyb, 1.0, 1024, 2 * TK, dest(lay_mixer, g_mixer, "w_b_out"))
    (dp_a, cs_a, d_convw, gs["conv_b"], gs["conv_ln_g"], gs["conv_ln_b"]), (chips_attn,) = _conv_bwd(
        "conv_bwd", dya, wf["w_a_out"], c_conv, p, conv_w_pad, conv_ln_g, conv_ln_b, tm_c,
        hosted=_chip_exchange_comm(part_attn))
    gsum_attn = _chip_sum("grads_chip_sum_attn", part_attn, chips_attn, chip, 784)
    dp_b, cs_b, d_sguw, d_sgub, gs["sgu_ln_g"], gs["sgu_ln_b"] = _sgu_bwd(
        "sgu_bwd", dyb, wf["w_b_out"], p, sgu_ln_g, sgu_ln_b, sgu_w[0], bias_full, tm_s)
    gs["conv_w"] = d_convw[:CW].reshape(1, CW, D)
    gs["sgu_w"] = d_sguw
    gs["sgu_b"] = d_sgub[:, :GROUPS].T
    gs["b_in"] = jnp.concatenate([cs_a, cs_b, cs_g], axis=1)
    for j, (tag, dpart) in enumerate((("a", dp_a), ("b", dp_b), ("g", dp_g))):
        g_mixer = _mm_tn("dw_in_" + tag, dpart, n_mix, 1.0, 1024, 2 * TK,
                         dest(lay_mixer, g_mixer, "w_in", 2 * D * j))
    g4_mixer = g_mixer.reshape(4, 2, lay_mixer[1], D)
    (dh1, dh1b, gs["mix_norm"]), (sib_mixer,) = _dx_rms_bwd(
        "mix_bwd_dx", [(dp_a, wf["w_in"], 0, 2 * D, False), (dp_b, wf["w_in"], 1, 2 * D, False),
                       (dp_g, wf["w_in"], 2, 2 * D, False)], h1, mix_norm, dh2, tm_s,
        hosted=_pair_exchange_comm(g4_mixer))
    part_mixer = _pair_sum("grads_pair_sum_mixer", g4_mixer, sib_mixer, core, 576)

    dx, _, gs["ffn1_norm"], _, (chips_mixer,), (last_slots,) = ffn_bwd(
        "ffn1", dh1b, dh1, g1, u1, a1, xn1, xs, ffn1_norm, wf["ffn1_w_gu"], wf["ffn1_w_down"], lay_last, None,
        act_hosted=_chip_exchange_comm(part_mixer), dx_hosted=_all_to_all_comm)
    gsum_mixer = _chip_sum("grads_chip_sum_mixer", part_mixer, chips_mixer, chip, 576)
    gsum_last = _slot_sum("grads_slot_sum_ffn1", last_slots, 528)

    grads, deltas, new_m, new_v = {}, {}, {}, {}
    for entries, gsum in ((last_g, gsum_last), (mixer_g, gsum_mixer), (attn_g, gsum_attn)):
        off = 0
        for name, rows, transposed in entries:
            gsh = gsum[off:off + rows]
            off += rows
            gsh = gsh.T if transposed else gsh
            d, mo, vo = _adamw("adamw_" + name, w[name][0], gsh, mom[name][0], vel[name][0], gsh.shape[0] // 2)
            grads[name], deltas[name], new_m[name], new_v[name] = gsh[None], d[None], mo[None], vo[None]

    shapes = {n: w[n].shape for n, _ in _SMALL}
    (g8,) = _run_comm("gather_small_grads", _gather_comm(_pack_small(gs, my_dev)))
    sg, sd, sm, sv = _adamw_small("adamw_small", _pack_small(w, my_dev), g8,
                                  _pack_small(mom, my_dev), _pack_small(vel, my_dev))
    for dst, src in ((grads, sg), (deltas, sd), (new_m, sm), (new_v, sv)):
        dst.update(_unpack_small(src, shapes, my_dev))

    loss = lax.psum(loss_blk[0, 0], AXES)
    order = ["ffn1_norm", "ffn1_w_gu", "ffn1_w_down", "mix_norm", "w_in", "b_in", "conv_w", "conv_b",
             "conv_ln_g", "conv_ln_b", "w_a_out", "sgu_ln_g", "sgu_ln_b", "sgu_w", "sgu_b", "w_b_out",
             "w_out", "xattn_norm", "mem_norm", "w_q", "w_kv", "w_o", "ffn2_norm", "ffn2_w_gu",
             "ffn2_w_down", "final_norm"]
    return (loss, dx.reshape(x.shape), *[grads[n] for n in order], *[deltas[n] for n in order],
            *[new_m[n] for n in order], *[new_v[n] for n in order])
```

```python
import functools
import math

import jax
import jax.numpy as jnp
from jax import lax
from jax.experimental import pallas as pl
from jax.experimental.pallas import tpu as pltpu

F32 = jnp.float32
BF16 = jnp.bfloat16
MESH = pl.DeviceIdType.MESH
AXES = ("x", "y", "c")

D = 1024
DFF = 2816
NMEM = 256
HEADS = 4
HD = D // HEADS
CW = 31
HALO = 32
CHUNK = 128
GROUPS = 4
GD = D // GROUPS
EPS_RMS = 1e-6
EPS_LN = 1e-5
LR, B1, B2, EPS_ADAM, WD, STEP = 0.001, 0.9, 0.999, 1e-08, 0.01, 10
NDEV = 8
VMEM_LIMIT = 56 * 1024 * 1024
TK = 2048


def _params(*sem):
    return pltpu.CompilerParams(dimension_semantics=sem, vmem_limit_bytes=VMEM_LIMIT)


def _dot(a, b):
    return jnp.dot(a, b, preferred_element_type=F32)


def _dot_nt(a, b):
    return lax.dot_general(a, b, (((1,), (1,)), ((), ())), preferred_element_type=F32)


def _dot_tn(a, b):
    return lax.dot_general(a, b, (((0,), (0,)), ((), ())), preferred_element_type=F32)


def _sigmoid(x):
    return 0.5 * jnp.tanh(0.5 * x) + 0.5


_GELU_C = math.sqrt(2.0 / math.pi)


def _gelu_with_grad(x):
    x2 = x * x
    t = jnp.tanh(_GELU_C * (x + 0.044715 * (x2 * x)))
    half = 0.5 * (1.0 + t)
    return x * half, half + 0.5 * x * (1.0 - t * t) * (_GELU_C * (1.0 + 3.0 * 0.044715 * x2))


def _gelu(x):
    return _gelu_with_grad(x)[0]


def _rms_stats(h):
    r = lax.rsqrt(jnp.mean(h * h, axis=-1, keepdims=True) + EPS_RMS)
    return r, h * r


def _rms_bwd(dxn, h, gain):
    r, xh = _rms_stats(h)
    dgain = jnp.sum(dxn * xh, axis=0, keepdims=True)
    dxh = dxn * gain
    dh = r * (dxh - xh * jnp.mean(dxh * xh, axis=-1, keepdims=True))
    return dh, dgain


def _ln_stats(c):
    mu = jnp.mean(c, axis=-1, keepdims=True)
    xc = c - mu
    rstd = lax.rsqrt(jnp.mean(xc * xc, axis=-1, keepdims=True) + EPS_LN)
    return rstd, xc * rstd


def _ln_bwd(dy, xhat, rstd, g):
    dxh = dy * g
    return rstd * (dxh - jnp.mean(dxh, axis=-1, keepdims=True)
                   - xhat * jnp.mean(dxh * xhat, axis=-1, keepdims=True))


def _row(tm, cols, cb=0):
    return pl.BlockSpec((tm, cols), lambda i, _cb=cb: (i, _cb))


def _const(shape):
    n = len(shape)
    return pl.BlockSpec(shape, lambda *_: (0,) * n)


def _sds(shape, dtype):
    return jax.ShapeDtypeStruct(shape, dtype)


def _acc(ref, first, val):
    @pl.when(first)
    def _():
        ref[...] = jnp.zeros_like(ref)
    ref[...] += val


class _Comm:
    def __init__(self, args, out_shapes, scratch, start, finish, forward=None):
        self.args, self.out_shapes, self.scratch = args, out_shapes, scratch
        self.start, self.finish, self.forward = start, finish, forward


_ANY = pl.BlockSpec(memory_space=pl.ANY)


def _run_comm(name, comm):
    ni, no = len(comm.args), len(comm.out_shapes)

    def body(*refs):
        ins, outs, sems = refs[:ni], refs[ni:ni + no], refs[ni + no:]
        comm.start(ins, outs, sems)
        if comm.forward:
            comm.forward(ins, outs, sems)
        comm.finish(ins, outs, sems)

    return pl.pallas_call(
        body, name=name, out_shape=list(comm.out_shapes), in_specs=[_ANY] * ni, out_specs=[_ANY] * no,
        scratch_shapes=list(comm.scratch),
    )(*comm.args)


def _call(name, body, grid, in_specs, out_specs, out_shape, args, scratch=(), sem=None, hosted=None,
          forward_late=False):
    n_in, n_out, n_scr = len(in_specs), len(out_specs), len(scratch)
    if not hosted:
        outs = pl.pallas_call(
            body, name=name, grid=grid, in_specs=list(in_specs), out_specs=list(out_specs),
            out_shape=list(out_shape), scratch_shapes=list(scratch), compiler_params=_params(*sem),
        )(*args)
        return outs, []
    comms = list(hosted) if isinstance(hosted, (list, tuple)) else [hosted]
    hi = sum(len(cm.args) for cm in comms)
    ho = sum(len(cm.out_shapes) for cm in comms)

    def wrapped(*refs):
        ins, h_in = refs[:n_in], refs[n_in:n_in + hi]
        o0 = n_in + hi
        outs, h_out = refs[o0:o0 + n_out], refs[o0 + n_out:o0 + n_out + ho]
        s0 = o0 + n_out + ho
        scr, h_sems = refs[s0:s0 + n_scr], refs[s0 + n_scr:]
        ids = [pl.program_id(a) for a in range(len(grid))]
        first = functools.reduce(jnp.logical_and, [i == 0 for i in ids])
        last = functools.reduce(jnp.logical_and, [i == g - 1 for i, g in zip(ids, grid)])
        parts, a0, b0, c0 = [], 0, 0, 0
        for cm in comms:
            na, nb, nc = len(cm.args), len(cm.out_shapes), len(cm.scratch)
            parts.append((cm, h_in[a0:a0 + na], h_out[b0:b0 + nb], h_sems[c0:c0 + nc]))
            a0, b0, c0 = a0 + na, b0 + nb, c0 + nc

        @pl.when(first)
        def _():
            for cm, ci, co, cs in parts:
                cm.start(ci, co, cs)

        body(*ins, *outs, *scr)

        step, total = ids[0], grid[0]
        for i, g in zip(ids[1:], grid[1:]):
            step, total = step * g + i, total * g

        @pl.when(step == (total - 1 if forward_late else (3 * total) // 4))
        def _():
            for cm, ci, co, cs in parts:
                if cm.forward:
                    cm.forward(ci, co, cs)

        @pl.when(last)
        def _():
            for cm, ci, co, cs in parts:
                cm.finish(ci, co, cs)

    res = pl.pallas_call(
        wrapped, name=name, grid=grid, in_specs=list(in_specs) + [_ANY] * hi,
        out_specs=list(out_specs) + [_ANY] * ho,
        out_shape=list(out_shape) + [s for cm in comms for s in cm.out_shapes],
        scratch_shapes=list(scratch) + [s for cm in comms for s in cm.scratch],
        compiler_params=_params(*(["arbitrary"] * len(grid))),
    )(*args, *[a for cm in comms for a in cm.args])
    return res[:n_out], res[n_out:]


def _gather_comm(blk):
    r, cdim = blk.shape

    def copies(x_ref, out_ref, send_sems, recv_sems, local_sem):
        x, y, c = lax.axis_index("x"), lax.axis_index("y"), lax.axis_index("c")
        me, sibling = (x, y, c), (x, y, 1 - c)
        chips = [(1 - x, y), (x, 1 - y), (1 - x, 1 - y)]

        def slot(px, py, pc):
            return out_ref.at[4 * px + 2 * py + pc]

        def copy(k, block, to, src=None):
            return pltpu.make_async_remote_copy(
                src_ref=slot(*block) if src is None else src, dst_ref=slot(*block),
                send_sem=send_sems.at[k], recv_sem=recv_sems.at[k],
                device_id=to, device_id_type=MESH)

        mine = pltpu.make_async_copy(x_ref, slot(*me), local_sem)
        first = [copy(0, me, sibling, src=x_ref)]
        first += [copy(1 + j, me, (*chip, c), src=x_ref) for j, chip in enumerate(chips)]
        passed = [copy(4 + j, (*chip, c), sibling) for j, chip in enumerate(chips)]
        landed = [copy(1 + j, (*chip, c), me) for j, chip in enumerate(chips)]
        from_sibling = [copy(0, sibling, me)] + [copy(4 + j, (*chip, 1 - c), me) for j, chip in enumerate(chips)]
        return mine, first, passed, landed, from_sibling

    def start(ins, outs, sems):
        mine, first, _, _, _ = copies(ins[0], outs[0], *sems)
        mine.start()
        for cp in first:
            cp.start()

    def forward(ins, outs, sems):
        _, _, passed, landed, _ = copies(ins[0], outs[0], *sems)
        for arrived, onward in zip(landed, passed):
            arrived.wait_recv()
            onward.start()

    def finish(ins, outs, sems):
        mine, first, passed, _, from_sibling = copies(ins[0], outs[0], *sems)
        for cp in from_sibling:
            cp.wait_recv()
        for cp in first + passed:
            cp.wait_send()
        mine.wait()

    return _Comm([blk], [_sds((NDEV, r, cdim), blk.dtype)],
                 [pltpu.SemaphoreType.DMA((7,)), pltpu.SemaphoreType.DMA((7,)), pltpu.SemaphoreType.DMA],
                 start, finish, forward)


def _exchange_comm(src, n, out_rows, make):
    r, cdim = out_rows

    def copies(src_ref, out_ref, send_sems, recv_sems):
        out = []
        for k in range(n):
            s, d, to = make(k, src_ref, out_ref)
            out.append(pltpu.make_async_remote_copy(
                src_ref=s, dst_ref=d, send_sem=send_sems.at[k], recv_sem=recv_sems.at[k],
                device_id=to, device_id_type=MESH))
        return out

    def start(ins, outs, sems):
        for cp in copies(ins[0], outs[0], *sems):
            cp.start()

    def finish(ins, outs, sems):
        cps = copies(ins[0], outs[0], *sems)
        for cp in cps:
            cp.wait_recv()
        for cp in cps:
            cp.wait_send()

    return _Comm([src], [_sds((n, r, cdim), src.dtype)],
                 [pltpu.SemaphoreType.DMA((n,)), pltpu.SemaphoreType.DMA((n,))], start, finish)


def _pair_exchange_comm(g4):
    def make(k, g_ref, out_ref):
        x, y, c = lax.axis_index("x"), lax.axis_index("y"), lax.axis_index("c")
        return g_ref.at[k, 1 - c], out_ref.at[k], (x, y, 1 - c)

    return _exchange_comm(g4, 4, g4.shape[2:], make)


def _chip_exchange_comm(part):
    def make(k, p_ref, out_ref):
        x, y, c = lax.axis_index("x"), lax.axis_index("y"), lax.axis_index("c")
        px = x if ((k + 1) >> 1) == 0 else 1 - x
        py = y if ((k + 1) & 1) == 0 else 1 - y
        return p_ref.at[2 * px + py], out_ref.at[k], (px, py, c)

    return _exchange_comm(part, 3, part.shape[1:], make)


def _all_to_all_comm(g8):
    _, r, cdim = g8.shape

    def copies(g_ref, out_ref, send_sems, recv_sems, local_sem):
        x, y, c = lax.axis_index("x"), lax.axis_index("y"), lax.axis_index("c")
        me = 4 * x + 2 * y + c
        remote = []
        for k in range(1, NDEV):
            px = 1 - x if k & 4 else x
            py = 1 - y if k & 2 else y
            pc = 1 - c if k & 1 else c
            remote.append(pltpu.make_async_remote_copy(
                src_ref=g_ref.at[4 * px + 2 * py + pc], dst_ref=out_ref.at[me],
                send_sem=send_sems.at[k - 1], recv_sem=recv_sems.at[k - 1],
                device_id=(px, py, pc), device_id_type=MESH))
        return pltpu.make_async_copy(g_ref.at[me], out_ref.at[me], local_sem), remote

    def start(ins, outs, sems):
        mine, remote = copies(ins[0], outs[0], *sems)
        mine.start()
        for cp in remote:
            cp.start()

    def finish(ins, outs, sems):
        mine, remote = copies(ins[0], outs[0], *sems)
        for cp in remote:
            cp.wait_recv()
        for cp in remote:
            cp.wait_send()
        mine.wait()

    return _Comm([g8], [_sds((NDEV, r, cdim), g8.dtype)],
                 [pltpu.SemaphoreType.DMA((NDEV - 1,)), pltpu.SemaphoreType.DMA((NDEV - 1,)),
                  pltpu.SemaphoreType.DMA], start, finish)


def _slot_sum(name, slots, tr):
    _, r, cdim = slots.shape

    def body(s_ref, o_ref):
        s = s_ref[0].astype(F32)
        for k in range(1, NDEV):
            s = s + s_ref[k].astype(F32)
        o_ref[...] = s

    return pl.pallas_call(
        body, name=name, grid=(r // tr,),
        in_specs=[pl.BlockSpec((NDEV, tr, cdim), lambda i: (0, i, 0))],
        out_specs=pl.BlockSpec((tr, cdim), lambda i: (i, 0)),
        out_shape=_sds((r, cdim), F32), compiler_params=_params("parallel"),
    )(slots)


def _pair_sum(name, g4, recv, core, tr):
    _, _, r, cdim = g4.shape

    def body(core_ref, a_ref, b_ref, o_ref):
        o_ref[...] = (a_ref[...].astype(F32) + b_ref[...].astype(F32)).astype(o_ref.dtype)

    return pl.pallas_call(
        body, name=name,
        grid_spec=pltpu.PrefetchScalarGridSpec(
            num_scalar_prefetch=1, grid=(4, r // tr),
            in_specs=[pl.BlockSpec((None, None, tr, cdim), lambda k, i, cr: (k, cr[0], i, 0)),
                      pl.BlockSpec((None, tr, cdim), lambda k, i, cr: (k, i, 0))],
            out_specs=pl.BlockSpec((None, tr, cdim), lambda k, i, cr: (k, i, 0))),
        out_shape=_sds((4, r, cdim), BF16),
        compiler_params=_params("parallel", "parallel"),
    )(core, g4, recv)


def _chip_sum(name, part, recv, chip, tr):
    _, r, cdim = part.shape

    def body(chip_ref, a_ref, b_ref, o_ref):
        s = a_ref[...].astype(F32)
        for k in range(3):
            s = s + b_ref[k].astype(F32)
        o_ref[...] = s

    return pl.pallas_call(
        body, name=name,
        grid_spec=pltpu.PrefetchScalarGridSpec(
            num_scalar_prefetch=1, grid=(r // tr,),
            in_specs=[pl.BlockSpec((None, tr, cdim), lambda i, cr: (cr[0], i, 0)),
                      pl.BlockSpec((3, tr, cdim), lambda i, cr: (0, i, 0))],
            out_specs=pl.BlockSpec((tr, cdim), lambda i, cr: (i, 0))),
        out_shape=_sds((r, cdim), F32),
        compiler_params=_params("parallel"),
    )(chip, part, recv)


def _rms_cast(name, h, gain, tm, hosted=None):
    t = h.shape[0]

    def body(h_ref, g_ref, o_ref):
        _, xh = _rms_stats(h_ref[...])
        o_ref[...] = (xh * g_ref[...]).astype(BF16)

    return _call(name, body, (t // tm,), [_row(tm, D), _const((1, D))], [_row(tm, D)],
                 [_sds((t, D), BF16)], (h, gain), sem=("parallel",), hosted=hosted, forward_late=True)


def _ffn_up(name, xn, wgu_t, tm, tn, hosted=None):
    t = xn.shape[0]
    nh = DFF // tn

    def body(x_ref, wg_ref, wu_ref, s_ref, f_ref, a_ref):
        x = x_ref[...]
        g = _dot_nt(x, wg_ref[...])
        u = _dot_nt(x, wu_ref[...])
        sg = _sigmoid(g)
        s = g * sg
        s_ref[...] = s.astype(BF16)
        f_ref[...] = (u * (sg + s * (1.0 - sg))).astype(BF16)
        a_ref[...] = (s * u).astype(BF16)

    o = pl.BlockSpec((tm, tn), lambda j, i: (i, j))
    return _call(name, body, (nh, t // tm),
                 [pl.BlockSpec((tm, D), lambda j, i: (i, 0)),
                  pl.BlockSpec((tn, D), lambda j, i: (j, 0)),
                  pl.BlockSpec((tn, D), lambda j, i: (j + nh, 0))],
                 [o, o, o], [_sds((t, DFF), BF16)] * 3, (xn, wgu_t, wgu_t),
                 sem=("parallel", "parallel"), hosted=hosted)


def _ffn_down(name, a, wd, h, gain, tm):
    t = a.shape[0]

    def body(a_ref, w_ref, h_ref, g_ref, o_ref, n_ref):
        hn = h_ref[...] + 0.5 * _dot(a_ref[...], w_ref[...])
        o_ref[...] = hn
        _, xh = _rms_stats(hn)
        n_ref[...] = (xh * g_ref[...]).astype(BF16)

    return pl.pallas_call(
        body, name=name, grid=(t // tm,),
        in_specs=[_row(tm, DFF), _const((DFF, D)), _row(tm, D), _const((1, D))],
        out_specs=[_row(tm, D), _row(tm, D)],
        out_shape=[_sds((t, D), F32), _sds((t, D), BF16)],
        compiler_params=_params("parallel"),
    )(a, wd, h, gain)


def _mix_in(name, n, win_t, b_in, tm, tn, hosted=None):
    t = n.shape[0]

    def body(n_ref, w_ref, b_ref, p_ref):
        p_ref[...] = (_dot_nt(n_ref[...], w_ref[...]) + b_ref[...]).astype(BF16)

    return _call(name, body, (6 * D // tn, t // tm),
                 [pl.BlockSpec((tm, D), lambda j, i: (i, 0)),
                  pl.BlockSpec((tn, D), lambda j, i: (j, 0)),
                  pl.BlockSpec((1, tn), lambda j, i: (0, j))],
                 [pl.BlockSpec((tm, tn), lambda j, i: (i, j))], [_sds((t, 6 * D), BF16)],
                 (n, win_t, b_in), sem=("parallel", "parallel"), hosted=hosted)


RC = 64
LANES = 128


def _shift_copies(ext, shifted, tm):
    n = tm + HALO - 8
    for m in range(1, 8):
        shifted[m - 1] = ext[pl.ds(m, n), :]


def _by_residue(offs):
    groups = {}
    for k, off in enumerate(offs):
        q, m = divmod(off, 8)
        groups.setdefault(m, []).append((k, q))
    return groups


def _residue_window(ext, shifted, m, taps, base, cs):
    src = ext if m == 0 else shifted.at[m - 1]
    return src[pl.ds(base, RC + 8 * max(q for _, q in taps)), cs]


def _tap_sum(out_ref, bias_ref, w_ref, ext, shifted, offs, tm):
    groups = _by_residue(offs)

    def chunk(j, carry):
        base = pl.multiple_of(j * RC, RC)
        for c in range(D // LANES):
            cs = pl.ds(c * LANES, LANES)
            acc = jnp.zeros((RC, LANES), F32)
            if bias_ref is not None:
                acc = acc + bias_ref[:, cs]
            for m, taps in groups.items():
                big = _residue_window(ext, shifted, m, taps, base, cs)
                for k, q in taps:
                    acc = acc + w_ref[pl.ds(k, 1), cs] * big[8 * q:8 * q + RC]
            out_ref[pl.ds(base, RC), cs] = acc
        return carry

    lax.fori_loop(0, tm // RC, chunk, 0)


def _tap_corr(dw_ref, dc_ext, ext, shifted, offs, tm):
    groups = _by_residue(offs)
    for c in range(D // LANES):
        cs = pl.ds(c * LANES, LANES)

        def chunk(j, accs, cs=cs):
            base = pl.multiple_of(j * RC, RC)
            dcv = dc_ext[pl.ds(base, RC), cs]
            out = list(accs)
            for m, taps in groups.items():
                big = _residue_window(ext, shifted, m, taps, base, cs)
                for k, q in taps:
                    prod = dcv * big[8 * q:8 * q + RC]
                    part = prod[0:8]
                    for s in range(1, RC // 8):
                        part = part + prod[8 * s:8 * s + 8]
                    out[k] = accs[k] + part
            return tuple(out)

        accs = lax.fori_loop(0, tm // RC, chunk, tuple(jnp.zeros((8, LANES), F32) for _ in offs))
        for k in range(len(offs)):
            dw_ref[pl.ds(k, 1), cs] += jnp.sum(accs[k], axis=0, keepdims=True)


def _conv_fwd(name, p, conv_w, conv_b, ln_g, ln_b, tm):
    t = p.shape[0]

    def body(av_ref, ag_ref, w_ref, cb_ref, lg_ref, lb_ref, c_ref, a_ref, ext, shifted):
        i = pl.program_id(0)

        @pl.when(i == 0)
        def _():
            ext[pl.ds(0, HALO), :] = jnp.zeros((HALO, D), F32)

        ext[pl.ds(HALO, tm), :] = av_ref[...].astype(F32) * _sigmoid(ag_ref[...].astype(F32))
        _shift_copies(ext, shifted, tm)
        _tap_sum(c_ref, cb_ref, w_ref, ext, shifted, [HALO - (CW - 1) + k for k in range(CW)], tm)
        rstd, chat = _ln_stats(c_ref[...])
        ca = chat * lg_ref[...] + lb_ref[...]
        a_ref[...] = (ca * _sigmoid(ca)).astype(BF16)
        ext[pl.ds(0, HALO), :] = ext[pl.ds(tm, HALO), :]

    return pl.pallas_call(
        body, name=name, grid=(t // tm,),
        in_specs=[_row(tm, D, 0), _row(tm, D, 1), _const((HALO, D)), _const((1, D)),
                  _const((1, D)), _const((1, D))],
        out_specs=[_row(tm, D), _row(tm, D)],
        out_shape=[_sds((t, D), F32), _sds((t, D), BF16)],
        scratch_shapes=[pltpu.VMEM((tm + HALO, D), F32), pltpu.VMEM((7, tm + HALO - 8, D), F32)],
        compiler_params=_params("arbitrary"),
    )(p, p, conv_w, conv_b, ln_g, ln_b)


def _sgu_mask():
    rows = lax.broadcasted_iota(jnp.int32, (CHUNK, CHUNK), 0)
    cols = lax.broadcasted_iota(jnp.int32, (CHUNK, CHUNK), 1)
    return cols <= rows


def _sgu_fwd(name, p, ln_g, ln_b, sgu_w, bias_full, tm):
    t = p.shape[0]

    def body(bu_ref, bv_ref, lg_ref, lb_ref, ws_ref, bias_ref, o_ref):
        mask = _sgu_mask()
        _, vhat = _ln_stats(_gelu(bv_ref[...].astype(F32)))
        vn = (vhat * lg_ref[...] + lb_ref[...]).astype(BF16)
        ub = _gelu(bu_ref[...].astype(F32))
        for g in range(GROUPS):
            wm = jnp.where(mask, ws_ref[g], 0.0).astype(BF16)
            cs = slice(g * GD, (g + 1) * GD)
            for cc in range(tm // CHUNK):
                rs = slice(cc * CHUNK, (cc + 1) * CHUNK)
                mixed = _dot(wm, vn[rs, cs]) + bias_ref[:, cs]
                o_ref[rs, cs] = (ub[rs, cs] * mixed).astype(BF16)

    return pl.pallas_call(
        body, name=name, grid=(t // tm,),
        in_specs=[_row(tm, D, 2), _row(tm, D, 3), _const((1, D)), _const((1, D)),
                  _const((GROUPS, CHUNK, CHUNK)), _const((CHUNK, D))],
        out_specs=_row(tm, D), out_shape=_sds((t, D), BF16),
        compiler_params=_params("parallel"),
    )(p, p, ln_g, ln_b, sgu_w, bias_full)


def _merge_fwd(name, act_a, act_b, p, w_a, w_b, w_out, h, gain, tm):
    t = h.shape[0]

    def body(a_ref, b_ref, ga_ref, gb_ref, wa_ref, wb_ref, wo_ref, h_ref, g_ref,
             ya_ref, yb_ref, mg_ref, ho_ref, xn_ref):
        ya = _dot(a_ref[...], wa_ref[...])
        yb = _dot(b_ref[...], wb_ref[...])
        ya_ref[...] = ya.astype(BF16)
        yb_ref[...] = yb.astype(BF16)
        merged = (_sigmoid(ga_ref[...].astype(F32)) * ya
                  + _sigmoid(gb_ref[...].astype(F32)) * yb).astype(BF16)
        mg_ref[...] = merged
        hn = h_ref[...] + _dot(merged, wo_ref[...])
        ho_ref[...] = hn
        _, xh = _rms_stats(hn)
        xn_ref[...] = (xh * g_ref[...]).astype(BF16)

    rb = _row(tm, D)
    return pl.pallas_call(
        body, name=name, grid=(t // tm,),
        in_specs=[rb, rb, _row(tm, D, 4), _row(tm, D, 5), _const((D, D)), _const((D, D)),
                  _const((D, D)), rb, _const((1, D))],
        out_specs=[rb] * 5,
        out_shape=[_sds((t, D), BF16)] * 3 + [_sds((t, D), F32), _sds((t, D), BF16)],
        compiler_params=_params("parallel"),
    )(act_a, act_b, p, p, w_a, w_b, w_out, h, gain)


def _kv_fwd(name, mem, gain, wkv_t):
    def body(m_ref, g_ref, w_ref, mn_ref, k_ref, v_ref):
        _, xh = _rms_stats(m_ref[...])
        mn = (xh * g_ref[...]).astype(BF16)
        mn_ref[...] = mn
        kv = _dot_nt(mn, w_ref[...])
        k_ref[...] = kv[:, :D].astype(BF16)
        v_ref[...] = kv[:, D:].astype(BF16)

    return pl.pallas_call(
        body, name=name,
        out_shape=[_sds((NMEM, D), BF16)] * 3,
        compiler_params=pltpu.CompilerParams(vmem_limit_bytes=VMEM_LIMIT),
    )(mem, gain, wkv_t)


def _softmax_rows(s):
    e = jnp.exp(s - jnp.max(s, axis=-1, keepdims=True))
    return e / jnp.sum(e, axis=-1, keepdims=True)


def _attn_fwd(name, xq, w_q, kb, vb, w_o, h, gain, tm):
    t = h.shape[0]
    scale = 1.0 / math.sqrt(HD)

    def body(x_ref, wq_ref, k_ref, v_ref, wo_ref, h_ref, g_ref, q_ref, p_ref, o_ref, ho_ref, xn_ref):
        q_ref[...] = _dot(x_ref[...], wq_ref[...]).astype(BF16)
        for hd in range(HEADS):
            cs = slice(hd * HD, (hd + 1) * HD)
            ps = slice(hd * NMEM, (hd + 1) * NMEM)
            p_ref[:, ps] = _softmax_rows(_dot_nt(q_ref[:, cs], k_ref[:, cs]) * scale).astype(BF16)
            o_ref[:, cs] = _dot(p_ref[:, ps], v_ref[:, cs]).astype(BF16)
        hn = h_ref[...] + _dot(o_ref[...], wo_ref[...])
        ho_ref[...] = hn
        _, xh = _rms_stats(hn)
        xn_ref[...] = (xh * g_ref[...]).astype(BF16)

    rb = _row(tm, D)
    return pl.pallas_call(
        body, name=name, grid=(t // tm,),
        in_specs=[rb, _const((D, D)), _const((NMEM, D)), _const((NMEM, D)), _const((D, D)), rb,
                  _const((1, D))],
        out_specs=[rb, _row(tm, HEADS * NMEM), rb, rb, rb],
        out_shape=[_sds((t, D), BF16), _sds((t, HEADS * NMEM), BF16), _sds((t, D), BF16),
                   _sds((t, D), F32), _sds((t, D), BF16)],
        compiler_params=_params("parallel"),
    )(xq, w_q, kb, vb, w_o, h, gain)


def _ffn_down_loss(name, a, wd, h, gain, target, tm):
    t = h.shape[0]
    steps = t // tm

    def body(a_ref, w_ref, h_ref, g_ref, t_ref, dh_ref, dhb_ref, loss_ref, dg_ref, lacc):
        i = pl.program_id(0)
        hv = h_ref[...] + 0.5 * _dot(a_ref[...], w_ref[...])
        r, xh = _rms_stats(hv)
        err = xh * g_ref[...] - t_ref[...]
        _acc(lacc, i == 0, jnp.sum(err * err, axis=0, keepdims=True))
        dy = err * (1.0 / D)
        _acc(dg_ref, i == 0, jnp.sum(dy * xh, axis=0, keepdims=True))
        dxh = dy * g_ref[...]
        dh = r * (dxh - xh * jnp.mean(dxh * xh, axis=-1, keepdims=True))
        dh_ref[...] = dh
        dhb_ref[...] = dh.astype(BF16)

        @pl.when(i == steps - 1)
        def _():
            loss_ref[...] = jnp.zeros((8, 128), F32) + (0.5 / D) * jnp.sum(lacc[...])

    rb = _row(tm, D)
    return pl.pallas_call(
        body, name=name, grid=(steps,),
        in_specs=[_row(tm, DFF), _const((DFF, D)), rb, _const((1, D)), rb],
        out_specs=[rb, rb, _const((8, 128)), _const((1, D))],
        out_shape=[_sds((t, D), F32), _sds((t, D), BF16), _sds((8, 128), F32), _sds((1, D), F32)],
        scratch_shapes=[pltpu.VMEM((1, D), F32)],
        compiler_params=_params("arbitrary"),
    )(a, wd, h, gain, target)


def _ffn_bwd_act(name, dhb, wd, s, f, tm, tn, hosted=None):
    t = dhb.shape[0]

    def body(d_ref, w_ref, s_ref, f_ref, dg_ref, du_ref):
        da = 0.5 * _dot_nt(d_ref[...], w_ref[...])
        dg_ref[...] = (da * f_ref[...].astype(F32)).astype(BF16)
        du_ref[...] = (da * s_ref[...].astype(F32)).astype(BF16)

    o = pl.BlockSpec((tm, tn), lambda j, i: (i, j))
    return _call(name, body, (DFF // tn, t // tm),
                 [pl.BlockSpec((tm, D), lambda j, i: (i, 0)),
                  pl.BlockSpec((tn, D), lambda j, i: (j, 0)), o, o],
                 [o, o], [_sds((t, DFF), BF16)] * 2, (dhb, wd, s, f),
                 sem=("parallel", "parallel"), hosted=hosted)


def _dx_rms_bwd(name, pairs, h, gain, dh_in, tm, hosted=None):
    t = h.shape[0]
    np_ = len(pairs)

    def body(*refs):
        a_refs = refs[:np_]
        b_refs = refs[np_:2 * np_]
        h_ref, g_ref, d_ref, o_ref, ob_ref, dg_ref = refs[2 * np_:]
        dxn = None
        for (a_ref, b_ref, pr) in zip(a_refs, b_refs, pairs):
            y = _dot_nt(a_ref[...], b_ref[...]) if pr[4] else _dot(a_ref[...], b_ref[...])
            dxn = y if dxn is None else dxn + y
        dh, dgain = _rms_bwd(dxn, h_ref[...], g_ref[...])
        _acc(dg_ref, pl.program_id(0) == 0, dgain)
        out = d_ref[...] + dh
        o_ref[...] = out
        ob_ref[...] = out.astype(BF16)

    ins, args = [], []
    for (a, b, blk, rows, tr) in pairs:
        ins.append(_row(tm, a.shape[1]))
        args.append(a)
    for (a, b, blk, rows, tr) in pairs:
        ins.append(pl.BlockSpec((rows, b.shape[1]), lambda i, _b=blk: (_b, 0)))
        args.append(b)
    rb = _row(tm, D)
    ins += [rb, _const((1, D)), rb]
    args += [h, gain, dh_in]
    return _call(name, body, (t // tm,), ins, [rb, rb, _const((1, D))],
                 [_sds((t, D), F32), _sds((t, D), BF16), _sds((1, D), F32)], args,
                 sem=("arbitrary",), hosted=hosted)


class _Dest:
    def __init__(self, buf, total_rows, off, rows, row0=0):
        self.buf, self.total_rows, self.off, self.rows, self.row0 = buf, total_rows, off, rows, row0

    def segments(self, lo, hi):
        out = []
        for d in range(NDEV):
            a, b = max(lo + self.row0, d * self.rows), min(hi + self.row0, (d + 1) * self.rows)
            if a < b:
                out.append((a - self.row0 - lo, d, self.off + a - d * self.rows, b - a))
        return out


def _store_segments(stage, buf_ref, sems, segs):
    cps = [pltpu.make_async_copy(stage.at[pl.ds(s0, n)], buf_ref.at[d, pl.ds(r0, n)], sems.at[j])
           for j, (s0, d, r0, n) in enumerate(segs)]
    for cp in cps:
        cp.start()
    for cp in cps:
        cp.wait()


def _mm_tn(name, a, b, scale, tmo, tk, dest):
    t, m = a.shape
    n = b.shape[1]
    tk = min(tk, t)
    steps = t // tk
    tiles = m // tmo
    seg_lists = [dest.segments(i * tmo, (i + 1) * tmo) for i in range(tiles)]
    fresh = dest.buf is None

    def body(*refs):
        a_ref, b_ref = refs[0], refs[1]
        buf_ref, acc, stage, sems = refs[-4:]
        i, k = pl.program_id(0), pl.program_id(1)
        _acc(acc, k == 0, _dot_tn(a_ref[...], b_ref[...]))

        @pl.when(k == steps - 1)
        def _():
            stage[...] = (acc[...] * scale).astype(BF16)
            for ti, segs in enumerate(seg_lists):
                @pl.when(i == ti)
                def _(segs=segs):
                    _store_segments(stage, buf_ref, sems, segs)

    ins = [pl.BlockSpec((tk, tmo), lambda i, k: (k, i)), pl.BlockSpec((tk, n), lambda i, k: (k, 0))]
    args = [a, b]
    if not fresh:
        ins.append(_ANY)
        args.append(dest.buf)
    return pl.pallas_call(
        body, name=name, grid=(tiles, steps), in_specs=ins, out_specs=_ANY,
        out_shape=_sds((NDEV, dest.total_rows, n), BF16),
        input_output_aliases={} if fresh else {2: 0},
        scratch_shapes=[pltpu.VMEM((tmo, n), F32), pltpu.VMEM((tmo, n), BF16),
                        pltpu.SemaphoreType.DMA((max(len(s) for s in seg_lists),))],
        compiler_params=_params("arbitrary", "arbitrary"),
    )(*args)


def _attn_bwd(name, dhb, w_o, qb, pb, kb, vb, tm):
    t = dhb.shape[0]
    scale = 1.0 / math.sqrt(HD)

    def body(d_ref, wo_ref, q_ref, p_ref, k_ref, v_ref, dq_ref, dk_ref, dv_ref, do_s):
        i = pl.program_id(0)

        @pl.when(i == 0)
        def _():
            dk_ref[...] = jnp.zeros_like(dk_ref)
            dv_ref[...] = jnp.zeros_like(dv_ref)

        do_s[...] = _dot_nt(d_ref[...], wo_ref[...]).astype(BF16)
        for hd in range(HEADS):
            cs = slice(hd * HD, (hd + 1) * HD)
            pb16 = p_ref[:, hd * NMEM:(hd + 1) * NMEM]
            p = pb16.astype(F32)
            do = do_s[:, cs]
            dp = _dot_nt(do, v_ref[:, cs])
            ds = (p * (dp - jnp.sum(dp * p, axis=-1, keepdims=True)) * scale).astype(BF16)
            dq_ref[:, cs] = _dot(ds, k_ref[:, cs]).astype(BF16)
            dk_ref[:, cs] += _dot_tn(ds, q_ref[:, cs])
            dv_ref[:, cs] += _dot_tn(pb16, do)

    rb = _row(tm, D)
    return pl.pallas_call(
        body, name=name, grid=(t // tm,),
        in_specs=[rb, _const((D, D)), rb, _row(tm, HEADS * NMEM), _const((NMEM, D)), _const((NMEM, D))],
        out_specs=[rb, _const((NMEM, D)), _const((NMEM, D))],
        out_shape=[_sds((t, D), BF16), _sds((NMEM, D), F32), _sds((NMEM, D), F32)],
        scratch_shapes=[pltpu.VMEM((tm, D), BF16)],
        compiler_params=_params("arbitrary"),
    )(dhb, w_o, qb, pb, kb, vb)


def _kv_bwd(name, dk, dv, memn, wkv_t, mem, gain, dest):
    segs = dest.segments(0, 2 * D)
    vmem = pl.BlockSpec(memory_space=pltpu.VMEM)

    def body(dk_ref, dv_ref, mn_ref, w_ref, m_ref, g_ref, buf_in, buf_ref, dg_ref, stage, sems):
        dkb = dk_ref[...].astype(BF16)
        dvb = dv_ref[...].astype(BF16)
        mn = mn_ref[...]
        stage[pl.ds(0, D), :] = _dot_tn(dkb, mn).astype(BF16)
        stage[pl.ds(D, D), :] = _dot_tn(dvb, mn).astype(BF16)
        _store_segments(stage, buf_ref, sems, segs)
        dmn = _dot(dkb, w_ref[pl.ds(0, D), :]) + _dot(dvb, w_ref[pl.ds(D, D), :])
        _, xh = _rms_stats(m_ref[...])
        dg_ref[...] = jnp.sum(dmn * xh, axis=0, keepdims=True)

    return pl.pallas_call(
        body, name=name,
        in_specs=[vmem] * 6 + [_ANY], out_specs=[_ANY, vmem],
        out_shape=[_sds(dest.buf.shape, BF16), _sds((1, D), F32)],
        input_output_aliases={6: 0},
        scratch_shapes=[pltpu.VMEM((2 * D, D), BF16), pltpu.SemaphoreType.DMA((len(segs),))],
        compiler_params=pltpu.CompilerParams(vmem_limit_bytes=VMEM_LIMIT),
    )(dk, dv, memn, wkv_t, mem, gain, dest.buf)


def _merge_bwd(name, dhb, w_out, ya, yb, p, tm):
    t = dhb.shape[0]

    def body(d_ref, w_ref, ya_ref, yb_ref, ga_ref, gb_ref, dya_ref, dyb_ref, dp_ref, cs_ref):
        dm = _dot_nt(d_ref[...], w_ref[...])
        sa = _sigmoid(ga_ref[...].astype(F32))
        sb = _sigmoid(gb_ref[...].astype(F32))
        dya_ref[...] = (dm * sa).astype(BF16)
        dyb_ref[...] = (dm * sb).astype(BF16)
        dga = dm * ya_ref[...].astype(F32) * (sa * (1.0 - sa))
        dgb = dm * yb_ref[...].astype(F32) * (sb * (1.0 - sb))
        dp_ref[:, pl.ds(0, D)] = dga.astype(BF16)
        dp_ref[:, pl.ds(D, D)] = dgb.astype(BF16)
        first = pl.program_id(0) == 0

        @pl.when(first)
        def _():
            cs_ref[...] = jnp.zeros_like(cs_ref)
        cs_ref[:, pl.ds(0, D)] += jnp.sum(dga, axis=0, keepdims=True)
        cs_ref[:, pl.ds(D, D)] += jnp.sum(dgb, axis=0, keepdims=True)

    rb = _row(tm, D)
    return pl.pallas_call(
        body, name=name, grid=(t // tm,),
        in_specs=[rb, _const((D, D)), rb, rb, _row(tm, D, 4), _row(tm, D, 5)],
        out_specs=[rb, rb, _row(tm, 2 * D), _const((1, 2 * D))],
        out_shape=[_sds((t, D), BF16), _sds((t, D), BF16), _sds((t, 2 * D), BF16),
                   _sds((1, 2 * D), F32)],
        compiler_params=_params("arbitrary"),
    )(dhb, w_out, ya, yb, p, p)


def _conv_bwd(name, dya, w_a, c, p, conv_w, ln_g, ln_b, tm, hosted=None):
    t = dya.shape[0]
    steps = t // tm
    hb = tm // HALO

    def rev(i):
        return steps - 1 - i

    def body(dy_ref, wa_ref, c_ref, av_ref, ag_ref, avh_ref, agh_ref, w_ref, lg_ref, lb_ref,
             dp_ref, cs_ref, dw_ref, dcb_ref, dlg_ref, dlb_ref, dc_ext, a_ext, dc_sh, a_sh, da0_s):
        i = pl.program_id(0)
        first = i == 0

        @pl.when(first)
        def _():
            dc_ext[pl.ds(tm, HALO), :] = jnp.zeros((HALO, D), F32)
            dw_ref[...] = jnp.zeros_like(dw_ref)
            cs_ref[...] = jnp.zeros_like(cs_ref)

        d_act = _dot_nt(dy_ref[...], wa_ref[...])
        rstd, chat = _ln_stats(c_ref[...])
        ca = chat * lg_ref[...] + lb_ref[...]
        sc = _sigmoid(ca)
        dca = d_act * (sc * (1.0 + ca * (1.0 - sc)))
        _acc(dlg_ref, first, jnp.sum(dca * chat, axis=0, keepdims=True))
        _acc(dlb_ref, first, jnp.sum(dca, axis=0, keepdims=True))
        dc = _ln_bwd(dca, chat, rstd, lg_ref[...])
        _acc(dcb_ref, first, jnp.sum(dc, axis=0, keepdims=True))
        dc_ext[pl.ds(0, tm), :] = dc

        av = av_ref[...].astype(F32)
        sg = _sigmoid(ag_ref[...].astype(F32))
        a_ext[pl.ds(HALO, tm), :] = av * sg
        halo = avh_ref[...].astype(F32) * _sigmoid(agh_ref[...].astype(F32))
        a_ext[pl.ds(0, HALO), :] = jnp.where(i == steps - 1, 0.0, halo)

        _shift_copies(dc_ext, dc_sh, tm)
        _shift_copies(a_ext, a_sh, tm)
        _tap_sum(da0_s, None, w_ref, dc_ext, dc_sh, [CW - 1 - k for k in range(CW)], tm)
        _tap_corr(dw_ref, dc_ext, a_ext, a_sh, [HALO - (CW - 1) + k for k in range(CW)], tm)
        da0 = da0_s[...]
        dav = da0 * sg
        dag = da0 * av * (sg * (1.0 - sg))
        dp_ref[:, pl.ds(0, D)] = dav.astype(BF16)
        dp_ref[:, pl.ds(D, D)] = dag.astype(BF16)
        cs_ref[:, pl.ds(0, D)] += jnp.sum(dav, axis=0, keepdims=True)
        cs_ref[:, pl.ds(D, D)] += jnp.sum(dag, axis=0, keepdims=True)
        dc_ext[pl.ds(tm, HALO), :] = dc_ext[pl.ds(0, HALO), :]

    def rrow(cols, cb=0):
        return pl.BlockSpec((tm, cols), lambda i, _cb=cb: (rev(i), _cb))

    def halo_spec(cb):
        return pl.BlockSpec((HALO, D), lambda i, _cb=cb: (jnp.maximum(rev(i) * hb - 1, 0), _cb))

    return _call(
        name, body, (steps,),
        [rrow(D), _const((D, D)), rrow(D), rrow(D, 0), rrow(D, 1), halo_spec(0),
         halo_spec(1), _const((HALO, D)), _const((1, D)), _const((1, D))],
        [rrow(2 * D), _const((1, 2 * D)), _const((HALO, D)), _const((1, D)),
         _const((1, D)), _const((1, D))],
        [_sds((t, 2 * D), BF16), _sds((1, 2 * D), F32), _sds((HALO, D), F32),
         _sds((1, D), F32), _sds((1, D), F32), _sds((1, D), F32)],
        (dya, w_a, c, p, p, p, p, conv_w, ln_g, ln_b),
        scratch=[pltpu.VMEM((tm + HALO, D), F32), pltpu.VMEM((tm + HALO, D), F32),
                 pltpu.VMEM((7, tm + HALO - 8, D), F32), pltpu.VMEM((7, tm + HALO - 8, D), F32),
                 pltpu.VMEM((tm, D), F32)],
        sem=("arbitrary",), hosted=hosted)


def _sgu_bwd(name, dyb, w_b, p, ln_g, ln_b, sgu_w, bias_full, tm):
    t = dyb.shape[0]
    steps = t // tm

    def body(dy_ref, wb_ref, bu_ref, bv_ref, lg_ref, lb_ref, ws_ref, bias_ref,
             dp_ref, cs_ref, dws_ref, dsb_ref, dlg_ref, dlb_ref, dub_s, dvn_s, dbias_s):
        i = pl.program_id(0)
        first = i == 0
        mask = _sgu_mask()

        @pl.when(first)
        def _():
            dws_ref[...] = jnp.zeros_like(dws_ref)
            dbias_s[...] = jnp.zeros_like(dbias_s)
            cs_ref[...] = jnp.zeros_like(cs_ref)

        dob = _dot_nt(dy_ref[...], wb_ref[...])
        bu = bu_ref[...].astype(F32)
        bv = bv_ref[...].astype(F32)
        ub, ub_grad = _gelu_with_grad(bu)
        vb, vb_grad = _gelu_with_grad(bv)
        rstd, vhat = _ln_stats(vb)
        vn = (vhat * lg_ref[...] + lb_ref[...]).astype(BF16)
        for g in range(GROUPS):
            wm = jnp.where(mask, ws_ref[g], 0.0).astype(BF16)
            cs = slice(g * GD, (g + 1) * GD)
            for cc in range(tm // CHUNK):
                rs = slice(cc * CHUNK, (cc + 1) * CHUNK)
                vblk = vn[rs, cs]
                mixed = _dot(wm, vblk) + bias_ref[:, cs]
                dob_blk = dob[rs, cs]
                dub_s[rs, cs] = dob_blk * mixed
                dmixed = dob_blk * ub[rs, cs]
                dbias_s[:, cs] += dmixed
                dmb = dmixed.astype(BF16)
                dws_ref[g] += _dot_nt(dmb, vblk)
                dvn_s[rs, cs] = _dot_tn(wm, dmb)
        dbu = dub_s[...] * ub_grad
        dvn = dvn_s[...]
        _acc(dlg_ref, first, jnp.sum(dvn * vhat, axis=0, keepdims=True))
        _acc(dlb_ref, first, jnp.sum(dvn, axis=0, keepdims=True))
        dbv = _ln_bwd(dvn, vhat, rstd, lg_ref[...]) * vb_grad
        dp_ref[:, pl.ds(0, D)] = dbu.astype(BF16)
        dp_ref[:, pl.ds(D, D)] = dbv.astype(BF16)
        cs_ref[:, pl.ds(0, D)] += jnp.sum(dbu, axis=0, keepdims=True)
        cs_ref[:, pl.ds(D, D)] += jnp.sum(dbv, axis=0, keepdims=True)

        @pl.when(i == steps - 1)
        def _():
            lane = lax.broadcasted_iota(jnp.int32, (CHUNK, CHUNK), 1)
            dsb = jnp.zeros((CHUNK, CHUNK), F32)
            for g in range(GROUPS):
                dws_ref[g] = jnp.where(mask, dws_ref[g], 0.0)
                dsb = jnp.where(lane == g, jnp.sum(dbias_s[:, g * GD:(g + 1) * GD], axis=1, keepdims=True), dsb)
            dsb_ref[...] = dsb

    rb = _row(tm, D)
    return pl.pallas_call(
        body, name=name, grid=(steps,),
        in_specs=[rb, _const((D, D)), _row(tm, D, 2), _row(tm, D, 3), _const((1, D)), _const((1, D)),
                  _const((GROUPS, CHUNK, CHUNK)), _const((CHUNK, D))],
        out_specs=[_row(tm, 2 * D), _const((1, 2 * D)), _const((GROUPS, CHUNK, CHUNK)),
                   _const((CHUNK, CHUNK)), _const((1, D)), _const((1, D))],
        out_shape=[_sds((t, 2 * D), BF16), _sds((1, 2 * D), F32), _sds((GROUPS, CHUNK, CHUNK), F32),
                   _sds((CHUNK, CHUNK), F32), _sds((1, D), F32), _sds((1, D), F32)],
        scratch_shapes=[pltpu.VMEM((tm, D), F32), pltpu.VMEM((tm, D), F32), pltpu.VMEM((CHUNK, D), F32)],
        compiler_params=_params("arbitrary"),
    )(dyb, w_b, p, p, ln_g, ln_b, sgu_w, bias_full)


def _adam_math(w, g, m, v):
    m = B1 * m + (1.0 - B1) * g
    v = B2 * v + (1.0 - B2) * (g * g)
    m_hat = m / (1.0 - B1 ** STEP)
    v_hat = v / (1.0 - B2 ** STEP)
    delta = -LR * (m_hat / (jnp.sqrt(v_hat) + EPS_ADAM) + WD * w)
    return delta, m, v


def _adamw(name, w, g, m, v, tr):
    r, cdim = w.shape

    def body(w_ref, g_ref, m_ref, v_ref, d_ref, mo_ref, vo_ref):
        d, mn, vn = _adam_math(w_ref[...], g_ref[...], m_ref[...], v_ref[...])
        d_ref[...] = d
        mo_ref[...] = mn
        vo_ref[...] = vn

    blk = pl.BlockSpec((tr, cdim), lambda i: (i, 0))
    return pl.pallas_call(
        body, name=name, grid=(r // tr,), in_specs=[blk] * 4, out_specs=[blk] * 3,
        out_shape=[_sds((r, cdim), F32)] * 3, compiler_params=_params("parallel"),
    )(w, g, m, v)


def _adamw_small(name, w, g8, m, v):
    r, cdim = w.shape

    def body(w_ref, g_ref, m_ref, v_ref, go_ref, d_ref, mo_ref, vo_ref):
        g = g_ref[0]
        for k in range(1, NDEV):
            g = g + g_ref[k]
        go_ref[...] = g
        d, mn, vn = _adam_math(w_ref[...], g, m_ref[...], v_ref[...])
        d_ref[...] = d
        mo_ref[...] = mn
        vo_ref[...] = vn

    return pl.pallas_call(
        body, name=name, out_shape=[_sds((r, cdim), F32)] * 4,
        compiler_params=pltpu.CompilerParams(vmem_limit_bytes=VMEM_LIMIT),
    )(w, g8, m, v)


_BIG = [("ffn1_w_gu", 704, True), ("ffn1_w_down", 352, False), ("w_in", 768, True),
        ("w_a_out", 128, False), ("w_b_out", 128, False), ("w_out", 128, False),
        ("w_q", 128, False), ("w_kv", 256, True), ("w_o", 128, False),
        ("ffn2_w_gu", 704, True), ("ffn2_w_down", 352, False)]
_BIG_ROWS = sum(r for _, r, _ in _BIG)

_SMALL = [("ffn1_norm", 1), ("mix_norm", 1), ("b_in", 6), ("conv_w", HALO), ("conv_b", 1),
          ("conv_ln_g", 1), ("conv_ln_b", 1), ("sgu_ln_g", 1), ("sgu_ln_b", 1), ("sgu_w", 64),
          ("sgu_b", 1), ("xattn_norm", 1), ("mem_norm", 1), ("ffn2_norm", 1), ("final_norm", 1)]
_SMALL_ROWS = 120


def _pack_small(vals, my_dev):
    rows = []
    for name, nrows in _SMALL:
        a = vals[name].astype(F32)
        if name == "conv_w":
            if a.shape[-1] != D:
                slab = jnp.zeros((HALO, D), F32)
                a = lax.dynamic_update_slice(slab, jnp.pad(a.reshape(CW, -1), ((0, HALO - CW), (0, 0))),
                                             (0, my_dev * (D // NDEV)))
            else:
                a = jnp.pad(a.reshape(CW, D), ((0, HALO - CW), (0, 0)))
        elif name == "sgu_b":
            a = jnp.pad(a.reshape(1, -1), ((0, 0), (0, D - GROUPS * CHUNK)))
        else:
            a = a.reshape(nrows, D)
        rows.append(a)
    packed = jnp.concatenate(rows, axis=0)
    return jnp.pad(packed, ((0, _SMALL_ROWS - packed.shape[0]), (0, 0)))


def _unpack_small(packed, shapes, my_dev):
    out, off = {}, 0
    for name, nrows in _SMALL:
        a = packed[off:off + nrows]
        off += nrows
        if name == "conv_w":
            a = lax.dynamic_slice(a, (0, my_dev * (D // NDEV)), (CW, D // NDEV))
        elif name == "sgu_b":
            a = a[:, :GROUPS * CHUNK]
        out[name] = a.reshape(shapes[name])
    return out


def kernel(x, mem, ffn1_norm, ffn1_w_gu, ffn1_w_down, mix_norm, w_in, b_in, conv_w, conv_b, conv_ln_g, conv_ln_b, w_a_out, sgu_ln_g, sgu_ln_b, sgu_w, sgu_b, w_b_out, w_out, xattn_norm, mem_norm, w_q, w_kv, w_o, ffn2_norm, ffn2_w_gu, ffn2_w_down, final_norm, loss_target, m_ffn1_norm, m_ffn1_w_gu, m_ffn1_w_down, m_mix_norm, m_w_in, m_b_in, m_conv_w, m_conv_b, m_conv_ln_g, m_conv_ln_b, m_w_a_out, m_sgu_ln_g, m_sgu_ln_b, m_sgu_w, m_sgu_b, m_w_b_out, m_w_out, m_xattn_norm, m_mem_norm, m_w_q, m_w_kv, m_w_o, m_ffn2_norm, m_ffn2_w_gu, m_ffn2_w_down, m_final_norm, v_ffn1_norm, v_ffn1_w_gu, v_ffn1_w_down, v_mix_norm, v_w_in, v_b_in, v_conv_w, v_conv_b, v_conv_ln_g, v_conv_ln_b, v_w_a_out, v_sgu_ln_g, v_sgu_ln_b, v_sgu_w, v_sgu_b, v_w_b_out, v_w_out, v_xattn_norm, v_mem_norm, v_w_q, v_w_kv, v_w_o, v_ffn2_norm, v_ffn2_w_gu, v_ffn2_w_down, v_final_norm):
    env = dict(locals())
    names = [n for n, _, _ in _BIG] + [n for n, _ in _SMALL]
    w = {n: env[n] for n in names}
    mom = {n: env["m_" + n] for n in names}
    vel = {n: env["v_" + n] for n in names}

    ax, ay, ac = lax.axis_index("x"), lax.axis_index("y"), lax.axis_index("c")
    my_chip = 2 * ax + ay
    my_dev = 2 * my_chip + ac

    t = x.shape[1]
    tm = min(512, t)
    tm_big = min(1024, t)
    tm_s = min(512, t)
    tm_c = min(256, t)
    xs = x.reshape(t, D)
    tgt = loss_target.reshape(t, D)
    mem2 = mem.reshape(NMEM, D)

    first, mid, late = _BIG[:1], _BIG[1:6], _BIG[6:]

    def gathers(entries):
        return [_gather_comm((w[n][0].T if tr else w[n][0]).astype(BF16)) for n, _, tr in entries]

    def whole(gathered, entries):
        return {n: g.reshape(NDEV * rows, D) for g, (n, rows, _) in zip(gathered, entries)}

    conv_slab = lax.dynamic_update_slice(
        jnp.zeros((HALO, D), F32), jnp.pad(conv_w[0], ((0, HALO - CW), (0, 0))), (0, my_dev * (D // NDEV)))
    bias_full = jnp.repeat(sgu_b[0].T, GD, axis=1)
    b_in2 = b_in.reshape(1, 6 * D)

    (xn1,), (*full_first, conv_w8) = _rms_cast(
        "norm_x", xs, ffn1_norm, tm, hosted=gathers(first) + [_gather_comm(conv_slab)])
    conv_w_pad = jnp.sum(conv_w8, axis=0)
    wf = whole(full_first, first)
    (g1, u1, a1), full_mid = _ffn_up("ffn1_up", xn1, wf["ffn1_w_gu"], tm_big, 1408, hosted=gathers(mid))
    wf.update(whole(full_mid, mid))
    h1, n_mix = _ffn_down("ffn1_down", a1, wf["ffn1_w_down"], xs, mix_norm, tm)
    (p,), full_late = _mix_in("mix_in", n_mix, wf["w_in"], b_in2, tm_big, 1536, hosted=gathers(late))
    wf.update(whole(full_late, late))
    c_conv, act_a = _conv_fwd("conv_fwd", p, conv_w_pad, conv_b, conv_ln_g, conv_ln_b, tm_c)
    act_b = _sgu_fwd("sgu_fwd", p, sgu_ln_g, sgu_ln_b, sgu_w[0], bias_full, tm_s)
    ya, yb, merged, h2, xq = _merge_fwd("merge_fwd", act_a, act_b, p, wf["w_a_out"], wf["w_b_out"],
                                        wf["w_out"], h1, xattn_norm, tm_s)
    memn, kb, vb = _kv_fwd("kv_fwd", mem2, mem_norm, wf["w_kv"])
    qb, probs, ob, h3, xn4 = _attn_fwd("attn_fwd", xq, wf["w_q"], kb, vb, wf["w_o"], h2, ffn2_norm, tm_s)
    (g2, u2, a2), _ = _ffn_up("ffn2_up", xn4, wf["ffn2_w_gu"], tm_big, 1408)
    dh4, dh4b, loss_blk, d_final = _ffn_down_loss("ffn2_down_loss", a2, wf["ffn2_w_down"], h3,
                                                  final_norm.reshape(1, D), tgt, tm)

    gs = {}
    gs["final_norm"] = d_final

    core = ac.astype(jnp.int32).reshape(1)
    chip = my_chip.astype(jnp.int32).reshape(1)
    last_g, mixer_g, attn_g = _BIG[:2], _BIG[2:6], _BIG[6:]

    def layout(entries):
        offs, off = {}, 0
        for n, rows, _ in entries:
            offs[n] = (off, rows)
            off += rows
        return offs, off

    def dest(group, buf, name, row0=0):
        offs, total = group
        return _Dest(buf, total, offs[name][0], offs[name][1], row0)

    lay_last, lay_mixer, lay_attn = layout(last_g), layout(mixer_g), layout(attn_g)

    def ffn_bwd(tag, dhb, dh, g, u, a, xn, h_in, gain, wgu_t, wd, group, buf, act_hosted=None, dx_hosted=None):
        (dg, du), act_out = _ffn_bwd_act(tag + "_bwd_act", dhb, wd, g, u, tm_big, 1408, hosted=act_hosted)
        buf = _mm_tn(tag + "_dw_down", a, dhb, 0.5, 1408, TK, dest(group, buf, tag + "_w_down"))
        buf = _mm_tn(tag + "_dw_gate", dg, xn, 1.0, 1408, TK, dest(group, buf, tag + "_w_gu"))
        buf = _mm_tn(tag + "_dw_up", du, xn, 1.0, 1408, TK, dest(group, buf, tag + "_w_gu", DFF))
        (dh_o, dhb_o, dgain), dx_out = _dx_rms_bwd(
            tag + "_bwd_dx", [(dg, wgu_t, 0, DFF, False), (du, wgu_t, 1, DFF, False)], h_in, gain, dh, tm_s,
            hosted=dx_hosted(buf) if dx_hosted else None)
        return dh_o, dhb_o, dgain, buf, act_out, dx_out

    dh3, dh3b, gs["ffn2_norm"], g_attn, _, _ = ffn_bwd(
        "ffn2", dh4b, dh4, g2, u2, a2, xn4, h3, ffn2_norm, wf["ffn2_w_gu"], wf["ffn2_w_down"], lay_attn, None)

    g_attn = _mm_tn("dw_o", ob, dh3b, 1.0, 1024, 2 * TK, dest(lay_attn, g_attn, "w_o"))
    dq, dk, dv = _attn_bwd("attn_bwd", dh3b, wf["w_o"], qb, probs, kb, vb, tm_s)
    g_attn, gs["mem_norm"] = _kv_bwd("kv_bwd", dk, dv, memn, wf["w_kv"], mem2, mem_norm,
                                     dest(lay_attn, g_attn, "w_kv"))
    g_attn = _mm_tn("dw_q", xq, dq, 1.0, 1024, 2 * TK, dest(lay_attn, g_attn, "w_q"))

    g4_attn = g_attn.reshape(4, 2, lay_attn[1], D)
    (dh2, dh2b, gs["xattn_norm"]), (sib_attn,) = _dx_rms_bwd(
        "attn_bwd_dx", [(dq, wf["w_q"], 0, D, True)], h2, xattn_norm, dh3, tm_s,
        hosted=_pair_exchange_comm(g4_attn))
    part_attn = _pair_sum("grads_pair_sum_attn", g4_attn, sib_attn, core, 784)

    g_mixer = _mm_tn("dw_out", merged, dh2b, 1.0, 1024, 2 * TK, dest(lay_mixer, None, "w_out"))
    dya, dyb, dp_g, cs_g = _merge_bwd("merge_bwd", dh2b, wf["w_out"], ya, yb, p, tm_s)
    g_mixer = _mm_tn("dw_a", act_a, dya, 1.0, 1024, 2 * TK, dest(lay_mixer, g_mixer, "w_a_out"))
    g_mixer = _mm_tn("dw_b", act_b, dyb, 1.0, 1024, 2 * TK, dest(lay_mixer, g_mixer, "w_b_out"))
    (dp_a, cs_a, d_convw, gs["conv_b"], gs["conv_ln_g"], gs["conv_ln_b"]), (chips_attn,) = _conv_bwd(
        "conv_bwd", dya, wf["w_a_out"], c_conv, p, conv_w_pad, conv_ln_g, conv_ln_b, tm_c,
        hosted=_chip_exchange_comm(part_attn))
    gsum_attn = _chip_sum("grads_chip_sum_attn", part_attn, chips_attn, chip, 784)
    dp_b, cs_b, d_sguw, d_sgub, gs["sgu_ln_g"], gs["sgu_ln_b"] = _sgu_bwd(
        "sgu_bwd", dyb, wf["w_b_out"], p, sgu_ln_g, sgu_ln_b, sgu_w[0], bias_full, tm_s)
    gs["conv_w"] = d_convw[:CW].reshape(1, CW, D)
    gs["sgu_w"] = d_sguw
    gs["sgu_b"] = d_sgub[:, :GROUPS].T
    gs["b_in"] = jnp.concatenate([cs_a, cs_b, cs_g], axis=1)
    for j, (tag, dpart) in enumerate((("a", dp_a), ("b", dp_b), ("g", dp_g))):
        g_mixer = _mm_tn("dw_in_" + tag, dpart, n_mix, 1.0, 1024, 2 * TK,
                         dest(lay_mixer, g_mixer, "w_in", 2 * D * j))
    g4_mixer = g_mixer.reshape(4, 2, lay_mixer[1], D)
    (dh1, dh1b, gs["mix_norm"]), (sib_mixer,) = _dx_rms_bwd(
        "mix_bwd_dx", [(dp_a, wf["w_in"], 0, 2 * D, False), (dp_b, wf["w_in"], 1, 2 * D, False),
                       (dp_g, wf["w_in"], 2, 2 * D, False)], h1, mix_norm, dh2, tm_s,
        hosted=_pair_exchange_comm(g4_mixer))
    part_mixer = _pair_sum("grads_pair_sum_mixer", g4_mixer, sib_mixer, core, 576)

    dx, _, gs["ffn1_norm"], _, (chips_mixer,), (last_slots,) = ffn_bwd(
        "ffn1", dh1b, dh1, g1, u1, a1, xn1, xs, ffn1_norm, wf["ffn1_w_gu"], wf["ffn1_w_down"], lay_last, None,
        act_hosted=_chip_exchange_comm(part_mixer), dx_hosted=_all_to_all_comm)
    gsum_mixer = _chip_sum("grads_chip_sum_mixer", part_mixer, chips_mixer, chip, 576)
    gsum_last = _slot_sum("grads_slot_sum_ffn1", last_slots, 528)

    grads, deltas, new_m, new_v = {}, {}, {}, {}
    for entries, gsum in ((last_g, gsum_last), (mixer_g, gsum_mixer), (attn_g, gsum_attn)):
        off = 0
        for name, rows, transposed in entries:
            gsh = gsum[off:off + rows]
            off += rows
            gsh = gsh.T if transposed else gsh
            d, mo, vo = _adamw("adamw_" + name, w[name][0], gsh, mom[name][0], vel[name][0], gsh.shape[0] // 2)
            grads[name], deltas[name], new_m[name], new_v[name] = gsh[None], d[None], mo[None], vo[None]

    shapes = {n: w[n].shape for n, _ in _SMALL}
    (g8,) = _run_comm("gather_small_grads", _gather_comm(_pack_small(gs, my_dev)))
    sg, sd, sm, sv = _adamw_small("adamw_small", _pack_small(w, my_dev), g8,
                                  _pack_small(mom, my_dev), _pack_small(vel, my_dev))
    for dst, src in ((grads, sg), (deltas, sd), (new_m, sm), (new_v, sv)):
        dst.update(_unpack_small(src, shapes, my_dev))

    loss = lax.psum(loss_blk[0, 0], AXES)
    order = ["ffn1_norm", "ffn1_w_gu", "ffn1_w_down", "mix_norm", "w_in", "b_in", "conv_w", "conv_b",
             "conv_ln_g", "conv_ln_b", "w_a_out", "sgu_ln_g", "sgu_ln_b", "sgu_w", "sgu_b", "w_b_out",
             "w_out", "xattn_norm", "mem_norm", "w_q", "w_kv", "w_o", "ffn2_norm", "ffn2_w_gu",
             "ffn2_w_down", "final_norm"]
    return (loss, dx.reshape(x.shape), *[grads[n] for n in order], *[deltas[n] for n in order],
            *[new_m[n] for n in order], *[new_v[n] for n in order])
```

```python
import functools
import math

import jax
import jax.numpy as jnp
from jax import lax
from jax.experimental import pallas as pl
from jax.experimental.pallas import tpu as pltpu

F32 = jnp.float32
BF16 = jnp.bfloat16
MESH = pl.DeviceIdType.MESH
AXES = ("x", "y", "c")

D = 1024
DFF = 2816
NMEM = 256
HEADS = 4
HD = D // HEADS
CW = 31
HALO = 32
CHUNK = 128
GROUPS = 4
GD = D // GROUPS
EPS_RMS = 1e-6
EPS_LN = 1e-5
LR, B1, B2, EPS_ADAM, WD, STEP = 0.001, 0.9, 0.999, 1e-08, 0.01, 10
NDEV = 8
VMEM_LIMIT = 56 * 1024 * 1024
TK = 2048


def _params(*sem):
    return pltpu.CompilerParams(dimension_semantics=sem, vmem_limit_bytes=VMEM_LIMIT)


def _dot(a, b):
    return jnp.dot(a, b, preferred_element_type=F32)


def _dot_nt(a, b):
    return lax.dot_general(a, b, (((1,), (1,)), ((), ())), preferred_element_type=F32)


def _dot_tn(a, b):
    return lax.dot_general(a, b, (((0,), (0,)), ((), ())), preferred_element_type=F32)


def _sigmoid(x):
    return 0.5 * jnp.tanh(0.5 * x) + 0.5


_GELU_C = math.sqrt(2.0 / math.pi)


def _gelu_with_grad(x):
    x2 = x * x
    t = jnp.tanh(_GELU_C * (x + 0.044715 * (x2 * x)))
    half = 0.5 * (1.0 + t)
    return x * half, half + 0.5 * x * (1.0 - t * t) * (_GELU_C * (1.0 + 3.0 * 0.044715 * x2))


def _gelu(x):
    return _gelu_with_grad(x)[0]


def _rms_stats(h):
    r = lax.rsqrt(jnp.mean(h * h, axis=-1, keepdims=True) + EPS_RMS)
    return r, h * r


def _rms_bwd(dxn, h, gain):
    r, xh = _rms_stats(h)
    dgain = jnp.sum(dxn * xh, axis=0, keepdims=True)
    dxh = dxn * gain
    dh = r * (dxh - xh * jnp.mean(dxh * xh, axis=-1, keepdims=True))
    return dh, dgain


def _ln_stats(c):
    mu = jnp.mean(c, axis=-1, keepdims=True)
    xc = c - mu
    rstd = lax.rsqrt(jnp.mean(xc * xc, axis=-1, keepdims=True) + EPS_LN)
    return rstd, xc * rstd


def _ln_bwd(dy, xhat, rstd, g):
    dxh = dy * g
    return rstd * (dxh - jnp.mean(dxh, axis=-1, keepdims=True)
                   - xhat * jnp.mean(dxh * xhat, axis=-1, keepdims=True))


def _row(tm, cols, cb=0):
    return pl.BlockSpec((tm, cols), lambda i, _cb=cb: (i, _cb))


def _const(shape):
    n = len(shape)
    return pl.BlockSpec(shape, lambda *_: (0,) * n)


def _sds(shape, dtype):
    return jax.ShapeDtypeStruct(shape, dtype)


def _acc(ref, first, val):
    @pl.when(first)
    def _():
        ref[...] = jnp.zeros_like(ref)
    ref[...] += val


class _Comm:
    def __init__(self, args, out_shapes, scratch, start, finish, forward=None):
        self.args, self.out_shapes, self.scratch = args, out_shapes, scratch
        self.start, self.finish, self.forward = start, finish, forward


_ANY = pl.BlockSpec(memory_space=pl.ANY)


def _run_comm(name, comm):
    ni, no = len(comm.args), len(comm.out_shapes)

    def body(*refs):
        ins, outs, sems = refs[:ni], refs[ni:ni + no], refs[ni + no:]
        comm.start(ins, outs, sems)
        if comm.forward:
            comm.forward(ins, outs, sems)
        comm.finish(ins, outs, sems)

    return pl.pallas_call(
        body, name=name, out_shape=list(comm.out_shapes), in_specs=[_ANY] * ni, out_specs=[_ANY] * no,
        scratch_shapes=list(comm.scratch),
    )(*comm.args)


def _call(name, body, grid, in_specs, out_specs, out_shape, args, scratch=(), sem=None, hosted=None):
    n_in, n_out, n_scr = len(in_specs), len(out_specs), len(scratch)
    if not hosted:
        outs = pl.pallas_call(
            body, name=name, grid=grid, in_specs=list(in_specs), out_specs=list(out_specs),
            out_shape=list(out_shape), scratch_shapes=list(scratch), compiler_params=_params(*sem),
        )(*args)
        return outs, []
    comms = list(hosted) if isinstance(hosted, (list, tuple)) else [hosted]
    hi = sum(len(cm.args) for cm in comms)
    ho = sum(len(cm.out_shapes) for cm in comms)

    def wrapped(*refs):
        ins, h_in = refs[:n_in], refs[n_in:n_in + hi]
        o0 = n_in + hi
        outs, h_out = refs[o0:o0 + n_out], refs[o0 + n_out:o0 + n_out + ho]
        s0 = o0 + n_out + ho
        scr, h_sems = refs[s0:s0 + n_scr], refs[s0 + n_scr:]
        ids = [pl.program_id(a) for a in range(len(grid))]
        first = functools.reduce(jnp.logical_and, [i == 0 for i in ids])
        last = functools.reduce(jnp.logical_and, [i == g - 1 for i, g in zip(ids, grid)])
        parts, a0, b0, c0 = [], 0, 0, 0
        for cm in comms:
            na, nb, nc = len(cm.args), len(cm.out_shapes), len(cm.scratch)
            parts.append((cm, h_in[a0:a0 + na], h_out[b0:b0 + nb], h_sems[c0:c0 + nc]))
            a0, b0, c0 = a0 + na, b0 + nb, c0 + nc

        @pl.when(first)
        def _():
            for cm, ci, co, cs in parts:
                cm.start(ci, co, cs)

        body(*ins, *outs, *scr)

        step, total = ids[0], grid[0]
        for i, g in zip(ids[1:], grid[1:]):
            step, total = step * g + i, total * g

        @pl.when(step == (3 * total) // 4)
        def _():
            for cm, ci, co, cs in parts:
                if cm.forward:
                    cm.forward(ci, co, cs)

        @pl.when(last)
        def _():
            for cm, ci, co, cs in parts:
                cm.finish(ci, co, cs)

    res = pl.pallas_call(
        wrapped, name=name, grid=grid, in_specs=list(in_specs) + [_ANY] * hi,
        out_specs=list(out_specs) + [_ANY] * ho,
        out_shape=list(out_shape) + [s for cm in comms for s in cm.out_shapes],
        scratch_shapes=list(scratch) + [s for cm in comms for s in cm.scratch],
        compiler_params=_params(*(["arbitrary"] * len(grid))),
    )(*args, *[a for cm in comms for a in cm.args])
    return res[:n_out], res[n_out:]


def _gather_comm(blk):
    r, cdim = blk.shape

    def copies(x_ref, out_ref, send_sems, recv_sems, local_sem):
        x, y, c = lax.axis_index("x"), lax.axis_index("y"), lax.axis_index("c")
        me, sibling = (x, y, c), (x, y, 1 - c)
        chips = [(1 - x, y), (x, 1 - y), (1 - x, 1 - y)]

        def slot(px, py, pc):
            return out_ref.at[4 * px + 2 * py + pc]

        def copy(k, block, to, src=None):
            return pltpu.make_async_remote_copy(
                src_ref=slot(*block) if src is None else src, dst_ref=slot(*block),
                send_sem=send_sems.at[k], recv_sem=recv_sems.at[k],
                device_id=to, device_id_type=MESH)

        mine = pltpu.make_async_copy(x_ref, slot(*me), local_sem)
        first = [copy(0, me, sibling, src=x_ref)]
        first += [copy(1 + j, me, (*chip, c), src=x_ref) for j, chip in enumerate(chips)]
        passed = [copy(4 + j, (*chip, c), sibling) for j, chip in enumerate(chips)]
        landed = [copy(1 + j, (*chip, c), me) for j, chip in enumerate(chips)]
        from_sibling = [copy(0, sibling, me)] + [copy(4 + j, (*chip, 1 - c), me) for j, chip in enumerate(chips)]
        return mine, first, passed, landed, from_sibling

    def start(ins, outs, sems):
        mine, first, _, _, _ = copies(ins[0], outs[0], *sems)
        mine.start()
        for cp in first:
            cp.start()

    def forward(ins, outs, sems):
        _, _, passed, landed, _ = copies(ins[0], outs[0], *sems)
        for arrived, onward in zip(landed, passed):
            arrived.wait_recv()
            onward.start()

    def finish(ins, outs, sems):
        mine, first, passed, _, from_sibling = copies(ins[0], outs[0], *sems)
        for cp in from_sibling:
            cp.wait_recv()
        for cp in first + passed:
            cp.wait_send()
        mine.wait()

    return _Comm([blk], [_sds((NDEV, r, cdim), blk.dtype)],
                 [pltpu.SemaphoreType.DMA((7,)), pltpu.SemaphoreType.DMA((7,)), pltpu.SemaphoreType.DMA],
                 start, finish, forward)


def _exchange_comm(src, n, out_rows, make):
    r, cdim = out_rows

    def copies(src_ref, out_ref, send_sems, recv_sems):
        out = []
        for k in range(n):
            s, d, to = make(k, src_ref, out_ref)
            out.append(pltpu.make_async_remote_copy(
                src_ref=s, dst_ref=d, send_sem=send_sems.at[k], recv_sem=recv_sems.at[k],
                device_id=to, device_id_type=MESH))
        return out

    def start(ins, outs, sems):
        for cp in copies(ins[0], outs[0], *sems):
            cp.start()

    def finish(ins, outs, sems):
        cps = copies(ins[0], outs[0], *sems)
        for cp in cps:
            cp.wait_recv()
        for cp in cps:
            cp.wait_send()

    return _Comm([src], [_sds((n, r, cdim), src.dtype)],
                 [pltpu.SemaphoreType.DMA((n,)), pltpu.SemaphoreType.DMA((n,))], start, finish)


def _pair_exchange_comm(g4):
    def make(k, g_ref, out_ref):
        x, y, c = lax.axis_index("x"), lax.axis_index("y"), lax.axis_index("c")
        return g_ref.at[k, 1 - c], out_ref.at[k], (x, y, 1 - c)

    return _exchange_comm(g4, 4, g4.shape[2:], make)


def _chip_exchange_comm(part):
    def make(k, p_ref, out_ref):
        x, y, c = lax.axis_index("x"), lax.axis_index("y"), lax.axis_index("c")
        px = x if ((k + 1) >> 1) == 0 else 1 - x
        py = y if ((k + 1) & 1) == 0 else 1 - y
        return p_ref.at[2 * px + py], out_ref.at[k], (px, py, c)

    return _exchange_comm(part, 3, part.shape[1:], make)


def _all_to_all_comm(g8):
    _, r, cdim = g8.shape

    def copies(g_ref, out_ref, send_sems, recv_sems, local_sem):
        x, y, c = lax.axis_index("x"), lax.axis_index("y"), lax.axis_index("c")
        me = 4 * x + 2 * y + c
        remote = []
        for k in range(1, NDEV):
            px = 1 - x if k & 4 else x
            py = 1 - y if k & 2 else y
            pc = 1 - c if k & 1 else c
            remote.append(pltpu.make_async_remote_copy(
                src_ref=g_ref.at[4 * px + 2 * py + pc], dst_ref=out_ref.at[me],
                send_sem=send_sems.at[k - 1], recv_sem=recv_sems.at[k - 1],
                device_id=(px, py, pc), device_id_type=MESH))
        return pltpu.make_async_copy(g_ref.at[me], out_ref.at[me], local_sem), remote

    def start(ins, outs, sems):
        mine, remote = copies(ins[0], outs[0], *sems)
        mine.start()
        for cp in remote:
            cp.start()

    def finish(ins, outs, sems):
        mine, remote = copies(ins[0], outs[0], *sems)
        for cp in remote:
            cp.wait_recv()
        for cp in remote:
            cp.wait_send()
        mine.wait()

    return _Comm([g8], [_sds((NDEV, r, cdim), g8.dtype)],
                 [pltpu.SemaphoreType.DMA((NDEV - 1,)), pltpu.SemaphoreType.DMA((NDEV - 1,)),
                  pltpu.SemaphoreType.DMA], start, finish)


def _slot_sum(name, slots, tr):
    _, r, cdim = slots.shape

    def body(s_ref, o_ref):
        s = s_ref[0].astype(F32)
        for k in range(1, NDEV):
            s = s + s_ref[k].astype(F32)
        o_ref[...] = s

    return pl.pallas_call(
        body, name=name, grid=(r // tr,),
        in_specs=[pl.BlockSpec((NDEV, tr, cdim), lambda i: (0, i, 0))],
        out_specs=pl.BlockSpec((tr, cdim), lambda i: (i, 0)),
        out_shape=_sds((r, cdim), F32), compiler_params=_params("parallel"),
    )(slots)


def _pair_sum(name, g4, recv, core, tr):
    _, _, r, cdim = g4.shape

    def body(core_ref, a_ref, b_ref, o_ref):
        o_ref[...] = (a_ref[...].astype(F32) + b_ref[...].astype(F32)).astype(o_ref.dtype)

    return pl.pallas_call(
        body, name=name,
        grid_spec=pltpu.PrefetchScalarGridSpec(
            num_scalar_prefetch=1, grid=(4, r // tr),
            in_specs=[pl.BlockSpec((None, None, tr, cdim), lambda k, i, cr: (k, cr[0], i, 0)),
                      pl.BlockSpec((None, tr, cdim), lambda k, i, cr: (k, i, 0))],
            out_specs=pl.BlockSpec((None, tr, cdim), lambda k, i, cr: (k, i, 0))),
        out_shape=_sds((4, r, cdim), BF16),
        compiler_params=_params("parallel", "parallel"),
    )(core, g4, recv)


def _chip_sum(name, part, recv, chip, tr):
    _, r, cdim = part.shape

    def body(chip_ref, a_ref, b_ref, o_ref):
        s = a_ref[...].astype(F32)
        for k in range(3):
            s = s + b_ref[k].astype(F32)
        o_ref[...] = s

    return pl.pallas_call(
        body, name=name,
        grid_spec=pltpu.PrefetchScalarGridSpec(
            num_scalar_prefetch=1, grid=(r // tr,),
            in_specs=[pl.BlockSpec((None, tr, cdim), lambda i, cr: (cr[0], i, 0)),
                      pl.BlockSpec((3, tr, cdim), lambda i, cr: (0, i, 0))],
            out_specs=pl.BlockSpec((tr, cdim), lambda i, cr: (i, 0))),
        out_shape=_sds((r, cdim), F32),
        compiler_params=_params("parallel"),
    )(chip, part, recv)


def _rms_cast(name, h, gain, tm, hosted=None):
    t = h.shape[0]

    def body(h_ref, g_ref, o_ref):
        _, xh = _rms_stats(h_ref[...])
        o_ref[...] = (xh * g_ref[...]).astype(BF16)

    return _call(name, body, (t // tm,), [_row(tm, D), _const((1, D))], [_row(tm, D)],
                 [_sds((t, D), BF16)], (h, gain), sem=("parallel",), hosted=hosted)


def _ffn_up(name, xn, wgu_t, tm, tn, hosted=None):
    t = xn.shape[0]
    nh = DFF // tn

    def body(x_ref, wg_ref, wu_ref, s_ref, f_ref, a_ref):
        x = x_ref[...]
        g = _dot_nt(x, wg_ref[...])
        u = _dot_nt(x, wu_ref[...])
        sg = _sigmoid(g)
        s = g * sg
        s_ref[...] = s.astype(BF16)
        f_ref[...] = (u * (sg + s * (1.0 - sg))).astype(BF16)
        a_ref[...] = (s * u).astype(BF16)

    o = pl.BlockSpec((tm, tn), lambda j, i: (i, j))
    return _call(name, body, (nh, t // tm),
                 [pl.BlockSpec((tm, D), lambda j, i: (i, 0)),
                  pl.BlockSpec((tn, D), lambda j, i: (j, 0)),
                  pl.BlockSpec((tn, D), lambda j, i: (j + nh, 0))],
                 [o, o, o], [_sds((t, DFF), BF16)] * 3, (xn, wgu_t, wgu_t),
                 sem=("parallel", "parallel"), hosted=hosted)


def _ffn_down(name, a, wd, h, gain, tm):
    t = a.shape[0]

    def body(a_ref, w_ref, h_ref, g_ref, o_ref, n_ref):
        hn = h_ref[...] + 0.5 * _dot(a_ref[...], w_ref[...])
        o_ref[...] = hn
        _, xh = _rms_stats(hn)
        n_ref[...] = (xh * g_ref[...]).astype(BF16)

    return pl.pallas_call(
        body, name=name, grid=(t // tm,),
        in_specs=[_row(tm, DFF), _const((DFF, D)), _row(tm, D), _const((1, D))],
        out_specs=[_row(tm, D), _row(tm, D)],
        out_shape=[_sds((t, D), F32), _sds((t, D), BF16)],
        compiler_params=_params("parallel"),
    )(a, wd, h, gain)


def _mix_in(name, n, win_t, b_in, tm, tn, hosted=None):
    t = n.shape[0]

    def body(n_ref, w_ref, b_ref, p_ref):
        p_ref[...] = (_dot_nt(n_ref[...], w_ref[...]) + b_ref[...]).astype(BF16)

    return _call(name, body, (6 * D // tn, t // tm),
                 [pl.BlockSpec((tm, D), lambda j, i: (i, 0)),
                  pl.BlockSpec((tn, D), lambda j, i: (j, 0)),
                  pl.BlockSpec((1, tn), lambda j, i: (0, j))],
                 [pl.BlockSpec((tm, tn), lambda j, i: (i, j))], [_sds((t, 6 * D), BF16)],
                 (n, win_t, b_in), sem=("parallel", "parallel"), hosted=hosted)


RC = 64
LANES = 128


def _shift_copies(ext, shifted, tm):
    n = tm + HALO - 8
    for m in range(1, 8):
        shifted[m - 1] = ext[pl.ds(m, n), :]


def _by_residue(offs):
    groups = {}
    for k, off in enumerate(offs):
        q, m = divmod(off, 8)
        groups.setdefault(m, []).append((k, q))
    return groups


def _residue_window(ext, shifted, m, taps, base, cs):
    src = ext if m == 0 else shifted.at[m - 1]
    return src[pl.ds(base, RC + 8 * max(q for _, q in taps)), cs]


def _tap_sum(out_ref, bias_ref, w_ref, ext, shifted, offs, tm):
    groups = _by_residue(offs)

    def chunk(j, carry):
        base = pl.multiple_of(j * RC, RC)
        for c in range(D // LANES):
            cs = pl.ds(c * LANES, LANES)
            acc = jnp.zeros((RC, LANES), F32)
            if bias_ref is not None:
                acc = acc + bias_ref[:, cs]
            for m, taps in groups.items():
                big = _residue_window(ext, shifted, m, taps, base, cs)
                for k, q in taps:
                    acc = acc + w_ref[pl.ds(k, 1), cs] * big[8 * q:8 * q + RC]
            out_ref[pl.ds(base, RC), cs] = acc
        return carry

    lax.fori_loop(0, tm // RC, chunk, 0)


def _tap_corr(dw_ref, dc_ext, ext, shifted, offs, tm):
    groups = _by_residue(offs)
    for c in range(D // LANES):
        cs = pl.ds(c * LANES, LANES)

        def chunk(j, accs, cs=cs):
            base = pl.multiple_of(j * RC, RC)
            dcv = dc_ext[pl.ds(base, RC), cs]
            out = list(accs)
            for m, taps in groups.items():
                big = _residue_window(ext, shifted, m, taps, base, cs)
                for k, q in taps:
                    prod = dcv * big[8 * q:8 * q + RC]
                    part = prod[0:8]
                    for s in range(1, RC // 8):
                        part = part + prod[8 * s:8 * s + 8]
                    out[k] = accs[k] + part
            return tuple(out)

        accs = lax.fori_loop(0, tm // RC, chunk, tuple(jnp.zeros((8, LANES), F32) for _ in offs))
        for k in range(len(offs)):
            dw_ref[pl.ds(k, 1), cs] += jnp.sum(accs[k], axis=0, keepdims=True)


def _conv_fwd(name, p, conv_w, conv_b, ln_g, ln_b, tm):
    t = p.shape[0]

    def body(av_ref, ag_ref, w_ref, cb_ref, lg_ref, lb_ref, c_ref, a_ref, ext, shifted):
        i = pl.program_id(0)

        @pl.when(i == 0)
        def _():
            ext[pl.ds(0, HALO), :] = jnp.zeros((HALO, D), F32)

        ext[pl.ds(HALO, tm), :] = av_ref[...].astype(F32) * _sigmoid(ag_ref[...].astype(F32))
        _shift_copies(ext, shifted, tm)
        _tap_sum(c_ref, cb_ref, w_ref, ext, shifted, [HALO - (CW - 1) + k for k in range(CW)], tm)
        rstd, chat = _ln_stats(c_ref[...])
        ca = chat * lg_ref[...] + lb_ref[...]
        a_ref[...] = (ca * _sigmoid(ca)).astype(BF16)
        ext[pl.ds(0, HALO), :] = ext[pl.ds(tm, HALO), :]

    return pl.pallas_call(
        body, name=name, grid=(t // tm,),
        in_specs=[_row(tm, D, 0), _row(tm, D, 1), _const((HALO, D)), _const((1, D)),
                  _const((1, D)), _const((1, D))],
        out_specs=[_row(tm, D), _row(tm, D)],
        out_shape=[_sds((t, D), F32), _sds((t, D), BF16)],
        scratch_shapes=[pltpu.VMEM((tm + HALO, D), F32), pltpu.VMEM((7, tm + HALO - 8, D), F32)],
        compiler_params=_params("arbitrary"),
    )(p, p, conv_w, conv_b, ln_g, ln_b)


def _sgu_mask():
    rows = lax.broadcasted_iota(jnp.int32, (CHUNK, CHUNK), 0)
    cols = lax.broadcasted_iota(jnp.int32, (CHUNK, CHUNK), 1)
    return cols <= rows


def _sgu_fwd(name, p, ln_g, ln_b, sgu_w, bias_full, tm):
    t = p.shape[0]

    def body(bu_ref, bv_ref, lg_ref, lb_ref, ws_ref, bias_ref, o_ref):
        mask = _sgu_mask()
        _, vhat = _ln_stats(_gelu(bv_ref[...].astype(F32)))
        vn = (vhat * lg_ref[...] + lb_ref[...]).astype(BF16)
        ub = _gelu(bu_ref[...].astype(F32))
        for g in range(GROUPS):
            wm = jnp.where(mask, ws_ref[g], 0.0).astype(BF16)
            cs = slice(g * GD, (g + 1) * GD)
            for cc in range(tm // CHUNK):
                rs = slice(cc * CHUNK, (cc + 1) * CHUNK)
                mixed = _dot(wm, vn[rs, cs]) + bias_ref[:, cs]
                o_ref[rs, cs] = (ub[rs, cs] * mixed).astype(BF16)

    return pl.pallas_call(
        body, name=name, grid=(t // tm,),
        in_specs=[_row(tm, D, 2), _row(tm, D, 3), _const((1, D)), _const((1, D)),
                  _const((GROUPS, CHUNK, CHUNK)), _const((CHUNK, D))],
        out_specs=_row(tm, D), out_shape=_sds((t, D), BF16),
        compiler_params=_params("parallel"),
    )(p, p, ln_g, ln_b, sgu_w, bias_full)


def _merge_fwd(name, act_a, act_b, p, w_a, w_b, w_out, h, gain, tm):
    t = h.shape[0]

    def body(a_ref, b_ref, ga_ref, gb_ref, wa_ref, wb_ref, wo_ref, h_ref, g_ref,
             ya_ref, yb_ref, mg_ref, ho_ref, xn_ref):
        ya = _dot(a_ref[...], wa_ref[...])
        yb = _dot(b_ref[...], wb_ref[...])
        ya_ref[...] = ya.astype(BF16)
        yb_ref[...] = yb.astype(BF16)
        merged = (_sigmoid(ga_ref[...].astype(F32)) * ya
                  + _sigmoid(gb_ref[...].astype(F32)) * yb).astype(BF16)
        mg_ref[...] = merged
        hn = h_ref[...] + _dot(merged, wo_ref[...])
        ho_ref[...] = hn
        _, xh = _rms_stats(hn)
        xn_ref[...] = (xh * g_ref[...]).astype(BF16)

    rb = _row(tm, D)
    return pl.pallas_call(
        body, name=name, grid=(t // tm,),
        in_specs=[rb, rb, _row(tm, D, 4), _row(tm, D, 5), _const((D, D)), _const((D, D)),
                  _const((D, D)), rb, _const((1, D))],
        out_specs=[rb] * 5,
        out_shape=[_sds((t, D), BF16)] * 3 + [_sds((t, D), F32), _sds((t, D), BF16)],
        compiler_params=_params("parallel"),
    )(act_a, act_b, p, p, w_a, w_b, w_out, h, gain)


def _kv_fwd(name, mem, gain, wkv_t):
    def body(m_ref, g_ref, w_ref, mn_ref, k_ref, v_ref):
        _, xh = _rms_stats(m_ref[...])
        mn = (xh * g_ref[...]).astype(BF16)
        mn_ref[...] = mn
        kv = _dot_nt(mn, w_ref[...])
        k_ref[...] = kv[:, :D].astype(BF16)
        v_ref[...] = kv[:, D:].astype(BF16)

    return pl.pallas_call(
        body, name=name,
        out_shape=[_sds((NMEM, D), BF16)] * 3,
        compiler_params=pltpu.CompilerParams(vmem_limit_bytes=VMEM_LIMIT),
    )(mem, gain, wkv_t)


def _softmax_rows(s):
    e = jnp.exp(s - jnp.max(s, axis=-1, keepdims=True))
    return e / jnp.sum(e, axis=-1, keepdims=True)


def _attn_fwd(name, xq, w_q, kb, vb, w_o, h, gain, tm):
    t = h.shape[0]
    scale = 1.0 / math.sqrt(HD)

    def body(x_ref, wq_ref, k_ref, v_ref, wo_ref, h_ref, g_ref, q_ref, p_ref, o_ref, ho_ref, xn_ref):
        q_ref[...] = _dot(x_ref[...], wq_ref[...]).astype(BF16)
        for hd in range(HEADS):
            cs = slice(hd * HD, (hd + 1) * HD)
            ps = slice(hd * NMEM, (hd + 1) * NMEM)
            p_ref[:, ps] = _softmax_rows(_dot_nt(q_ref[:, cs], k_ref[:, cs]) * scale).astype(BF16)
            o_ref[:, cs] = _dot(p_ref[:, ps], v_ref[:, cs]).astype(BF16)
        hn = h_ref[...] + _dot(o_ref[...], wo_ref[...])
        ho_ref[...] = hn
        _, xh = _rms_stats(hn)
        xn_ref[...] = (xh * g_ref[...]).astype(BF16)

    rb = _row(tm, D)
    return pl.pallas_call(
        body, name=name, grid=(t // tm,),
        in_specs=[rb, _const((D, D)), _const((NMEM, D)), _const((NMEM, D)), _const((D, D)), rb,
                  _const((1, D))],
        out_specs=[rb, _row(tm, HEADS * NMEM), rb, rb, rb],
        out_shape=[_sds((t, D), BF16), _sds((t, HEADS * NMEM), BF16), _sds((t, D), BF16),
                   _sds((t, D), F32), _sds((t, D), BF16)],
        compiler_params=_params("parallel"),
    )(xq, w_q, kb, vb, w_o, h, gain)


def _ffn_down_loss(name, a, wd, h, gain, target, tm):
    t = h.shape[0]
    steps = t // tm

    def body(a_ref, w_ref, h_ref, g_ref, t_ref, dh_ref, dhb_ref, loss_ref, dg_ref, lacc):
        i = pl.program_id(0)
        hv = h_ref[...] + 0.5 * _dot(a_ref[...], w_ref[...])
        r, xh = _rms_stats(hv)
        err = xh * g_ref[...] - t_ref[...]
        _acc(lacc, i == 0, jnp.sum(err * err, axis=0, keepdims=True))
        dy = err * (1.0 / D)
        _acc(dg_ref, i == 0, jnp.sum(dy * xh, axis=0, keepdims=True))
        dxh = dy * g_ref[...]
        dh = r * (dxh - xh * jnp.mean(dxh * xh, axis=-1, keepdims=True))
        dh_ref[...] = dh
        dhb_ref[...] = dh.astype(BF16)

        @pl.when(i == steps - 1)
        def _():
            loss_ref[...] = jnp.zeros((8, 128), F32) + (0.5 / D) * jnp.sum(lacc[...])

    rb = _row(tm, D)
    return pl.pallas_call(
        body, name=name, grid=(steps,),
        in_specs=[_row(tm, DFF), _const((DFF, D)), rb, _const((1, D)), rb],
        out_specs=[rb, rb, _const((8, 128)), _const((1, D))],
        out_shape=[_sds((t, D), F32), _sds((t, D), BF16), _sds((8, 128), F32), _sds((1, D), F32)],
        scratch_shapes=[pltpu.VMEM((1, D), F32)],
        compiler_params=_params("arbitrary"),
    )(a, wd, h, gain, target)


def _ffn_bwd_act(name, dhb, wd, s, f, tm, tn, hosted=None):
    t = dhb.shape[0]

    def body(d_ref, w_ref, s_ref, f_ref, dg_ref, du_ref):
        da = 0.5 * _dot_nt(d_ref[...], w_ref[...])
        dg_ref[...] = (da * f_ref[...].astype(F32)).astype(BF16)
        du_ref[...] = (da * s_ref[...].astype(F32)).astype(BF16)

    o = pl.BlockSpec((tm, tn), lambda j, i: (i, j))
    return _call(name, body, (DFF // tn, t // tm),
                 [pl.BlockSpec((tm, D), lambda j, i: (i, 0)),
                  pl.BlockSpec((tn, D), lambda j, i: (j, 0)), o, o],
                 [o, o], [_sds((t, DFF), BF16)] * 2, (dhb, wd, s, f),
                 sem=("parallel", "parallel"), hosted=hosted)


def _dx_rms_bwd(name, pairs, h, gain, dh_in, tm, hosted=None):
    t = h.shape[0]
    np_ = len(pairs)

    def body(*refs):
        a_refs = refs[:np_]
        b_refs = refs[np_:2 * np_]
        h_ref, g_ref, d_ref, o_ref, ob_ref, dg_ref = refs[2 * np_:]
        dxn = None
        for (a_ref, b_ref, pr) in zip(a_refs, b_refs, pairs):
            y = _dot_nt(a_ref[...], b_ref[...]) if pr[4] else _dot(a_ref[...], b_ref[...])
            dxn = y if dxn is None else dxn + y
        dh, dgain = _rms_bwd(dxn, h_ref[...], g_ref[...])
        _acc(dg_ref, pl.program_id(0) == 0, dgain)
        out = d_ref[...] + dh
        o_ref[...] = out
        ob_ref[...] = out.astype(BF16)

    ins, args = [], []
    for (a, b, blk, rows, tr) in pairs:
        ins.append(_row(tm, a.shape[1]))
        args.append(a)
    for (a, b, blk, rows, tr) in pairs:
        ins.append(pl.BlockSpec((rows, b.shape[1]), lambda i, _b=blk: (_b, 0)))
        args.append(b)
    rb = _row(tm, D)
    ins += [rb, _const((1, D)), rb]
    args += [h, gain, dh_in]
    return _call(name, body, (t // tm,), ins, [rb, rb, _const((1, D))],
                 [_sds((t, D), F32), _sds((t, D), BF16), _sds((1, D), F32)], args,
                 sem=("arbitrary",), hosted=hosted)


class _Dest:
    def __init__(self, buf, total_rows, off, rows, row0=0):
        self.buf, self.total_rows, self.off, self.rows, self.row0 = buf, total_rows, off, rows, row0

    def segments(self, lo, hi):
        out = []
        for d in range(NDEV):
            a, b = max(lo + self.row0, d * self.rows), min(hi + self.row0, (d + 1) * self.rows)
            if a < b:
                out.append((a - self.row0 - lo, d, self.off + a - d * self.rows, b - a))
        return out


def _store_segments(stage, buf_ref, sems, segs):
    cps = [pltpu.make_async_copy(stage.at[pl.ds(s0, n)], buf_ref.at[d, pl.ds(r0, n)], sems.at[j])
           for j, (s0, d, r0, n) in enumerate(segs)]
    for cp in cps:
        cp.start()
    for cp in cps:
        cp.wait()


def _mm_tn(name, a, b, scale, tmo, tk, dest):
    t, m = a.shape
    n = b.shape[1]
    tk = min(tk, t)
    steps = t // tk
    tiles = m // tmo
    seg_lists = [dest.segments(i * tmo, (i + 1) * tmo) for i in range(tiles)]
    fresh = dest.buf is None

    def body(*refs):
        a_ref, b_ref = refs[0], refs[1]
        buf_ref, acc, stage, sems = refs[-4:]
        i, k = pl.program_id(0), pl.program_id(1)
        _acc(acc, k == 0, _dot_tn(a_ref[...], b_ref[...]))

        @pl.when(k == steps - 1)
        def _():
            stage[...] = (acc[...] * scale).astype(BF16)
            for ti, segs in enumerate(seg_lists):
                @pl.when(i == ti)
                def _(segs=segs):
                    _store_segments(stage, buf_ref, sems, segs)

    ins = [pl.BlockSpec((tk, tmo), lambda i, k: (k, i)), pl.BlockSpec((tk, n), lambda i, k: (k, 0))]
    args = [a, b]
    if not fresh:
        ins.append(_ANY)
        args.append(dest.buf)
    return pl.pallas_call(
        body, name=name, grid=(tiles, steps), in_specs=ins, out_specs=_ANY,
        out_shape=_sds((NDEV, dest.total_rows, n), BF16),
        input_output_aliases={} if fresh else {2: 0},
        scratch_shapes=[pltpu.VMEM((tmo, n), F32), pltpu.VMEM((tmo, n), BF16),
                        pltpu.SemaphoreType.DMA((max(len(s) for s in seg_lists),))],
        compiler_params=_params("arbitrary", "arbitrary"),
    )(*args)


def _attn_bwd(name, dhb, w_o, qb, pb, kb, vb, tm):
    t = dhb.shape[0]
    scale = 1.0 / math.sqrt(HD)

    def body(d_ref, wo_ref, q_ref, p_ref, k_ref, v_ref, dq_ref, dk_ref, dv_ref, do_s):
        i = pl.program_id(0)

        @pl.when(i == 0)
        def _():
            dk_ref[...] = jnp.zeros_like(dk_ref)
            dv_ref[...] = jnp.zeros_like(dv_ref)

        do_s[...] = _dot_nt(d_ref[...], wo_ref[...]).astype(BF16)
        for hd in range(HEADS):
            cs = slice(hd * HD, (hd + 1) * HD)
            pb16 = p_ref[:, hd * NMEM:(hd + 1) * NMEM]
            p = pb16.astype(F32)
            do = do_s[:, cs]
            dp = _dot_nt(do, v_ref[:, cs])
            ds = (p * (dp - jnp.sum(dp * p, axis=-1, keepdims=True)) * scale).astype(BF16)
            dq_ref[:, cs] = _dot(ds, k_ref[:, cs]).astype(BF16)
            dk_ref[:, cs] += _dot_tn(ds, q_ref[:, cs])
            dv_ref[:, cs] += _dot_tn(pb16, do)

    rb = _row(tm, D)
    return pl.pallas_call(
        body, name=name, grid=(t // tm,),
        in_specs=[rb, _const((D, D)), rb, _row(tm, HEADS * NMEM), _const((NMEM, D)), _const((NMEM, D))],
        out_specs=[rb, _const((NMEM, D)), _const((NMEM, D))],
        out_shape=[_sds((t, D), BF16), _sds((NMEM, D), F32), _sds((NMEM, D), F32)],
        scratch_shapes=[pltpu.VMEM((tm, D), BF16)],
        compiler_params=_params("arbitrary"),
    )(dhb, w_o, qb, pb, kb, vb)


def _kv_bwd(name, dk, dv, memn, wkv_t, mem, gain, dest):
    segs = dest.segments(0, 2 * D)
    vmem = pl.BlockSpec(memory_space=pltpu.VMEM)

    def body(dk_ref, dv_ref, mn_ref, w_ref, m_ref, g_ref, buf_in, buf_ref, dg_ref, stage, sems):
        dkb = dk_ref[...].astype(BF16)
        dvb = dv_ref[...].astype(BF16)
        mn = mn_ref[...]
        stage[pl.ds(0, D), :] = _dot_tn(dkb, mn).astype(BF16)
        stage[pl.ds(D, D), :] = _dot_tn(dvb, mn).astype(BF16)
        _store_segments(stage, buf_ref, sems, segs)
        dmn = _dot(dkb, w_ref[pl.ds(0, D), :]) + _dot(dvb, w_ref[pl.ds(D, D), :])
        _, xh = _rms_stats(m_ref[...])
        dg_ref[...] = jnp.sum(dmn * xh, axis=0, keepdims=True)

    return pl.pallas_call(
        body, name=name,
        in_specs=[vmem] * 6 + [_ANY], out_specs=[_ANY, vmem],
        out_shape=[_sds(dest.buf.shape, BF16), _sds((1, D), F32)],
        input_output_aliases={6: 0},
        scratch_shapes=[pltpu.VMEM((2 * D, D), BF16), pltpu.SemaphoreType.DMA((len(segs),))],
        compiler_params=pltpu.CompilerParams(vmem_limit_bytes=VMEM_LIMIT),
    )(dk, dv, memn, wkv_t, mem, gain, dest.buf)


def _merge_bwd(name, dhb, w_out, ya, yb, p, tm):
    t = dhb.shape[0]

    def body(d_ref, w_ref, ya_ref, yb_ref, ga_ref, gb_ref, dya_ref, dyb_ref, dp_ref, cs_ref):
        dm = _dot_nt(d_ref[...], w_ref[...])
        sa = _sigmoid(ga_ref[...].astype(F32))
        sb = _sigmoid(gb_ref[...].astype(F32))
        dya_ref[...] = (dm * sa).astype(BF16)
        dyb_ref[...] = (dm * sb).astype(BF16)
        dga = dm * ya_ref[...].astype(F32) * (sa * (1.0 - sa))
        dgb = dm * yb_ref[...].astype(F32) * (sb * (1.0 - sb))
        dp_ref[:, pl.ds(0, D)] = dga.astype(BF16)
        dp_ref[:, pl.ds(D, D)] = dgb.astype(BF16)
        first = pl.program_id(0) == 0

        @pl.when(first)
        def _():
            cs_ref[...] = jnp.zeros_like(cs_ref)
        cs_ref[:, pl.ds(0, D)] += jnp.sum(dga, axis=0, keepdims=True)
        cs_ref[:, pl.ds(D, D)] += jnp.sum(dgb, axis=0, keepdims=True)

    rb = _row(tm, D)
    return pl.pallas_call(
        body, name=name, grid=(t // tm,),
        in_specs=[rb, _const((D, D)), rb, rb, _row(tm, D, 4), _row(tm, D, 5)],
        out_specs=[rb, rb, _row(tm, 2 * D), _const((1, 2 * D))],
        out_shape=[_sds((t, D), BF16), _sds((t, D), BF16), _sds((t, 2 * D), BF16),
                   _sds((1, 2 * D), F32)],
        compiler_params=_params("arbitrary"),
    )(dhb, w_out, ya, yb, p, p)


def _conv_bwd(name, dya, w_a, c, p, conv_w, ln_g, ln_b, tm, hosted=None):
    t = dya.shape[0]
    steps = t // tm
    hb = tm // HALO

    def rev(i):
        return steps - 1 - i

    def body(dy_ref, wa_ref, c_ref, av_ref, ag_ref, avh_ref, agh_ref, w_ref, lg_ref, lb_ref,
             dp_ref, cs_ref, dw_ref, dcb_ref, dlg_ref, dlb_ref, dc_ext, a_ext, dc_sh, a_sh, da0_s):
        i = pl.program_id(0)
        first = i == 0

        @pl.when(first)
        def _():
            dc_ext[pl.ds(tm, HALO), :] = jnp.zeros((HALO, D), F32)
            dw_ref[...] = jnp.zeros_like(dw_ref)
            cs_ref[...] = jnp.zeros_like(cs_ref)

        d_act = _dot_nt(dy_ref[...], wa_ref[...])
        rstd, chat = _ln_stats(c_ref[...])
        ca = chat * lg_ref[...] + lb_ref[...]
        sc = _sigmoid(ca)
        dca = d_act * (sc * (1.0 + ca * (1.0 - sc)))
        _acc(dlg_ref, first, jnp.sum(dca * chat, axis=0, keepdims=True))
        _acc(dlb_ref, first, jnp.sum(dca, axis=0, keepdims=True))
        dc = _ln_bwd(dca, chat, rstd, lg_ref[...])
        _acc(dcb_ref, first, jnp.sum(dc, axis=0, keepdims=True))
        dc_ext[pl.ds(0, tm), :] = dc

        av = av_ref[...].astype(F32)
        sg = _sigmoid(ag_ref[...].astype(F32))
        a_ext[pl.ds(HALO, tm), :] = av * sg
        halo = avh_ref[...].astype(F32) * _sigmoid(agh_ref[...].astype(F32))
        a_ext[pl.ds(0, HALO), :] = jnp.where(i == steps - 1, 0.0, halo)

        _shift_copies(dc_ext, dc_sh, tm)
        _shift_copies(a_ext, a_sh, tm)
        _tap_sum(da0_s, None, w_ref, dc_ext, dc_sh, [CW - 1 - k for k in range(CW)], tm)
        _tap_corr(dw_ref, dc_ext, a_ext, a_sh, [HALO - (CW - 1) + k for k in range(CW)], tm)
        da0 = da0_s[...]
        dav = da0 * sg
        dag = da0 * av * (sg * (1.0 - sg))
        dp_ref[:, pl.ds(0, D)] = dav.astype(BF16)
        dp_ref[:, pl.ds(D, D)] = dag.astype(BF16)
        cs_ref[:, pl.ds(0, D)] += jnp.sum(dav, axis=0, keepdims=True)
        cs_ref[:, pl.ds(D, D)] += jnp.sum(dag, axis=0, keepdims=True)
        dc_ext[pl.ds(tm, HALO), :] = dc_ext[pl.ds(0, HALO), :]

    def rrow(cols, cb=0):
        return pl.BlockSpec((tm, cols), lambda i, _cb=cb: (rev(i), _cb))

    def halo_spec(cb):
        return pl.BlockSpec((HALO, D), lambda i, _cb=cb: (jnp.maximum(rev(i) * hb - 1, 0), _cb))

    return _call(
        name, body, (steps,),
        [rrow(D), _const((D, D)), rrow(D), rrow(D, 0), rrow(D, 1), halo_spec(0),
         halo_spec(1), _const((HALO, D)), _const((1, D)), _const((1, D))],
        [rrow(2 * D), _const((1, 2 * D)), _const((HALO, D)), _const((1, D)),
         _const((1, D)), _const((1, D))],
        [_sds((t, 2 * D), BF16), _sds((1, 2 * D), F32), _sds((HALO, D), F32),
         _sds((1, D), F32), _sds((1, D), F32), _sds((1, D), F32)],
        (dya, w_a, c, p, p, p, p, conv_w, ln_g, ln_b),
        scratch=[pltpu.VMEM((tm + HALO, D), F32), pltpu.VMEM((tm + HALO, D), F32),
                 pltpu.VMEM((7, tm + HALO - 8, D), F32), pltpu.VMEM((7, tm + HALO - 8, D), F32),
                 pltpu.VMEM((tm, D), F32)],
        sem=("arbitrary",), hosted=hosted)


def _sgu_bwd(name, dyb, w_b, p, ln_g, ln_b, sgu_w, bias_full, tm):
    t = dyb.shape[0]
    steps = t // tm

    def body(dy_ref, wb_ref, bu_ref, bv_ref, lg_ref, lb_ref, ws_ref, bias_ref,
             dp_ref, cs_ref, dws_ref, dsb_ref, dlg_ref, dlb_ref, dub_s, dvn_s, dbias_s):
        i = pl.program_id(0)
        first = i == 0
        mask = _sgu_mask()

        @pl.when(first)
        def _():
            dws_ref[...] = jnp.zeros_like(dws_ref)
            dbias_s[...] = jnp.zeros_like(dbias_s)
            cs_ref[...] = jnp.zeros_like(cs_ref)

        dob = _dot_nt(dy_ref[...], wb_ref[...])
        bu = bu_ref[...].astype(F32)
        bv = bv_ref[...].astype(F32)
        ub, ub_grad = _gelu_with_grad(bu)
        vb, vb_grad = _gelu_with_grad(bv)
        rstd, vhat = _ln_stats(vb)
        vn = (vhat * lg_ref[...] + lb_ref[...]).astype(BF16)
        for g in range(GROUPS):
            wm = jnp.where(mask, ws_ref[g], 0.0).astype(BF16)
            cs = slice(g * GD, (g + 1) * GD)
            for cc in range(tm // CHUNK):
                rs = slice(cc * CHUNK, (cc + 1) * CHUNK)
                vblk = vn[rs, cs]
                mixed = _dot(wm, vblk) + bias_ref[:, cs]
                dob_blk = dob[rs, cs]
                dub_s[rs, cs] = dob_blk * mixed
                dmixed = dob_blk * ub[rs, cs]
                dbias_s[:, cs] += dmixed
                dmb = dmixed.astype(BF16)
                dws_ref[g] += _dot_nt(dmb, vblk)
                dvn_s[rs, cs] = _dot_tn(wm, dmb)
        dbu = dub_s[...] * ub_grad
        dvn = dvn_s[...]
        _acc(dlg_ref, first, jnp.sum(dvn * vhat, axis=0, keepdims=True))
        _acc(dlb_ref, first, jnp.sum(dvn, axis=0, keepdims=True))
        dbv = _ln_bwd(dvn, vhat, rstd, lg_ref[...]) * vb_grad
        dp_ref[:, pl.ds(0, D)] = dbu.astype(BF16)
        dp_ref[:, pl.ds(D, D)] = dbv.astype(BF16)
        cs_ref[:, pl.ds(0, D)] += jnp.sum(dbu, axis=0, keepdims=True)
        cs_ref[:, pl.ds(D, D)] += jnp.sum(dbv, axis=0, keepdims=True)

        @pl.when(i == steps - 1)
        def _():
            lane = lax.broadcasted_iota(jnp.int32, (CHUNK, CHUNK), 1)
            dsb = jnp.zeros((CHUNK, CHUNK), F32)
            for g in range(GROUPS):
                dws_ref[g] = jnp.where(mask, dws_ref[g], 0.0)
                dsb = jnp.where(lane == g, jnp.sum(dbias_s[:, g * GD:(g + 1) * GD], axis=1, keepdims=True), dsb)
            dsb_ref[...] = dsb

    rb = _row(tm, D)
    return pl.pallas_call(
        body, name=name, grid=(steps,),
        in_specs=[rb, _const((D, D)), _row(tm, D, 2), _row(tm, D, 3), _const((1, D)), _const((1, D)),
                  _const((GROUPS, CHUNK, CHUNK)), _const((CHUNK, D))],
        out_specs=[_row(tm, 2 * D), _const((1, 2 * D)), _const((GROUPS, CHUNK, CHUNK)),
                   _const((CHUNK, CHUNK)), _const((1, D)), _const((1, D))],
        out_shape=[_sds((t, 2 * D), BF16), _sds((1, 2 * D), F32), _sds((GROUPS, CHUNK, CHUNK), F32),
                   _sds((CHUNK, CHUNK), F32), _sds((1, D), F32), _sds((1, D), F32)],
        scratch_shapes=[pltpu.VMEM((tm, D), F32), pltpu.VMEM((tm, D), F32), pltpu.VMEM((CHUNK, D), F32)],
        compiler_params=_params("arbitrary"),
    )(dyb, w_b, p, p, ln_g, ln_b, sgu_w, bias_full)


def _adam_math(w, g, m, v):
    m = B1 * m + (1.0 - B1) * g
    v = B2 * v + (1.0 - B2) * (g * g)
    m_hat = m / (1.0 - B1 ** STEP)
    v_hat = v / (1.0 - B2 ** STEP)
    delta = -LR * (m_hat / (jnp.sqrt(v_hat) + EPS_ADAM) + WD * w)
    return delta, m, v


def _adamw(name, w, g, m, v, tr):
    r, cdim = w.shape

    def body(w_ref, g_ref, m_ref, v_ref, d_ref, mo_ref, vo_ref):
        d, mn, vn = _adam_math(w_ref[...], g_ref[...], m_ref[...], v_ref[...])
        d_ref[...] = d
        mo_ref[...] = mn
        vo_ref[...] = vn

    blk = pl.BlockSpec((tr, cdim), lambda i: (i, 0))
    return pl.pallas_call(
        body, name=name, grid=(r // tr,), in_specs=[blk] * 4, out_specs=[blk] * 3,
        out_shape=[_sds((r, cdim), F32)] * 3, compiler_params=_params("parallel"),
    )(w, g, m, v)


def _adamw_small(name, w, g8, m, v):
    r, cdim = w.shape

    def body(w_ref, g_ref, m_ref, v_ref, go_ref, d_ref, mo_ref, vo_ref):
        g = g_ref[0]
        for k in range(1, NDEV):
            g = g + g_ref[k]
        go_ref[...] = g
        d, mn, vn = _adam_math(w_ref[...], g, m_ref[...], v_ref[...])
        d_ref[...] = d
        mo_ref[...] = mn
        vo_ref[...] = vn

    return pl.pallas_call(
        body, name=name, out_shape=[_sds((r, cdim), F32)] * 4,
        compiler_params=pltpu.CompilerParams(vmem_limit_bytes=VMEM_LIMIT),
    )(w, g8, m, v)


_BIG = [("ffn1_w_gu", 704, True), ("ffn1_w_down", 352, False), ("w_in", 768, True),
        ("w_a_out", 128, False), ("w_b_out", 128, False), ("w_out", 128, False),
        ("w_q", 128, False), ("w_kv", 256, True), ("w_o", 128, False),
        ("ffn2_w_gu", 704, True), ("ffn2_w_down", 352, False)]
_BIG_ROWS = sum(r for _, r, _ in _BIG)

_SMALL = [("ffn1_norm", 1), ("mix_norm", 1), ("b_in", 6), ("conv_w", HALO), ("conv_b", 1),
          ("conv_ln_g", 1), ("conv_ln_b", 1), ("sgu_ln_g", 1), ("sgu_ln_b", 1), ("sgu_w", 64),
          ("sgu_b", 1), ("xattn_norm", 1), ("mem_norm", 1), ("ffn2_norm", 1), ("final_norm", 1)]
_SMALL_ROWS = 120


def _pack_small(vals, my_dev):
    rows = []
    for name, nrows in _SMALL:
        a = vals[name].astype(F32)
        if name == "conv_w":
            if a.shape[-1] != D:
                slab = jnp.zeros((HALO, D), F32)
                a = lax.dynamic_update_slice(slab, jnp.pad(a.reshape(CW, -1), ((0, HALO - CW), (0, 0))),
                                             (0, my_dev * (D // NDEV)))
            else:
                a = jnp.pad(a.reshape(CW, D), ((0, HALO - CW), (0, 0)))
        elif name == "sgu_b":
            a = jnp.pad(a.reshape(1, -1), ((0, 0), (0, D - GROUPS * CHUNK)))
        else:
            a = a.reshape(nrows, D)
        rows.append(a)
    packed = jnp.concatenate(rows, axis=0)
    return jnp.pad(packed, ((0, _SMALL_ROWS - packed.shape[0]), (0, 0)))


def _unpack_small(packed, shapes, my_dev):
    out, off = {}, 0
    for name, nrows in _SMALL:
        a = packed[off:off + nrows]
        off += nrows
        if name == "conv_w":
            a = lax.dynamic_slice(a, (0, my_dev * (D // NDEV)), (CW, D // NDEV))
        elif name == "sgu_b":
            a = a[:, :GROUPS * CHUNK]
        out[name] = a.reshape(shapes[name])
    return out


def kernel(x, mem, ffn1_norm, ffn1_w_gu, ffn1_w_down, mix_norm, w_in, b_in, conv_w, conv_b, conv_ln_g, conv_ln_b, w_a_out, sgu_ln_g, sgu_ln_b, sgu_w, sgu_b, w_b_out, w_out, xattn_norm, mem_norm, w_q, w_kv, w_o, ffn2_norm, ffn2_w_gu, ffn2_w_down, final_norm, loss_target, m_ffn1_norm, m_ffn1_w_gu, m_ffn1_w_down, m_mix_norm, m_w_in, m_b_in, m_conv_w, m_conv_b, m_conv_ln_g, m_conv_ln_b, m_w_a_out, m_sgu_ln_g, m_sgu_ln_b, m_sgu_w, m_sgu_b, m_w_b_out, m_w_out, m_xattn_norm, m_mem_norm, m_w_q, m_w_kv, m_w_o, m_ffn2_norm, m_ffn2_w_gu, m_ffn2_w_down, m_final_norm, v_ffn1_norm, v_ffn1_w_gu, v_ffn1_w_down, v_mix_norm, v_w_in, v_b_in, v_conv_w, v_conv_b, v_conv_ln_g, v_conv_ln_b, v_w_a_out, v_sgu_ln_g, v_sgu_ln_b, v_sgu_w, v_sgu_b, v_w_b_out, v_w_out, v_xattn_norm, v_mem_norm, v_w_q, v_w_kv, v_w_o, v_ffn2_norm, v_ffn2_w_gu, v_ffn2_w_down, v_final_norm):
    env = dict(locals())
    names = [n for n, _, _ in _BIG] + [n for n, _ in _SMALL]
    w = {n: env[n] for n in names}
    mom = {n: env["m_" + n] for n in names}
    vel = {n: env["v_" + n] for n in names}

    ax, ay, ac = lax.axis_index("x"), lax.axis_index("y"), lax.axis_index("c")
    my_chip = 2 * ax + ay
    my_dev = 2 * my_chip + ac

    t = x.shape[1]
    tm = min(512, t)
    tm_big = min(1024, t)
    tm_s = min(512, t)
    tm_c = min(256, t)
    xs = x.reshape(t, D)
    tgt = loss_target.reshape(t, D)
    mem2 = mem.reshape(NMEM, D)

    first, mid, late = _BIG[:1], _BIG[1:6], _BIG[6:]

    def gathers(entries):
        return [_gather_comm((w[n][0].T if tr else w[n][0]).astype(BF16)) for n, _, tr in entries]

    def whole(gathered, entries):
        return {n: g.reshape(NDEV * rows, D) for g, (n, rows, _) in zip(gathered, entries)}

    conv_slab = lax.dynamic_update_slice(
        jnp.zeros((HALO, D), F32), jnp.pad(conv_w[0], ((0, HALO - CW), (0, 0))), (0, my_dev * (D // NDEV)))
    bias_full = jnp.repeat(sgu_b[0].T, GD, axis=1)
    b_in2 = b_in.reshape(1, 6 * D)

    (xn1,), (*full_first, conv_w8) = _rms_cast(
        "norm_x", xs, ffn1_norm, tm, hosted=gathers(first) + [_gather_comm(conv_slab)])
    conv_w_pad = jnp.sum(conv_w8, axis=0)
    wf = whole(full_first, first)
    (g1, u1, a1), full_mid = _ffn_up("ffn1_up", xn1, wf["ffn1_w_gu"], tm_big, 1408, hosted=gathers(mid))
    wf.update(whole(full_mid, mid))
    h1, n_mix = _ffn_down("ffn1_down", a1, wf["ffn1_w_down"], xs, mix_norm, tm_big)
    (p,), full_late = _mix_in("mix_in", n_mix, wf["w_in"], b_in2, tm_big, 3072, hosted=gathers(late))
    wf.update(whole(full_late, late))
    c_conv, act_a = _conv_fwd("conv_fwd", p, conv_w_pad, conv_b, conv_ln_g, conv_ln_b, tm_c)
    act_b = _sgu_fwd("sgu_fwd", p, sgu_ln_g, sgu_ln_b, sgu_w[0], bias_full, tm_s)
    ya, yb, merged, h2, xq = _merge_fwd("merge_fwd", act_a, act_b, p, wf["w_a_out"], wf["w_b_out"],
                                        wf["w_out"], h1, xattn_norm, tm_s)
    memn, kb, vb = _kv_fwd("kv_fwd", mem2, mem_norm, wf["w_kv"])
    qb, probs, ob, h3, xn4 = _attn_fwd("attn_fwd", xq, wf["w_q"], kb, vb, wf["w_o"], h2, ffn2_norm, tm_s)
    (g2, u2, a2), _ = _ffn_up("ffn2_up", xn4, wf["ffn2_w_gu"], tm_big, 1408)
    dh4, dh4b, loss_blk, d_final = _ffn_down_loss("ffn2_down_loss", a2, wf["ffn2_w_down"], h3,
                                                  final_norm.reshape(1, D), tgt, tm)

    gs = {}
    gs["final_norm"] = d_final

    core = ac.astype(jnp.int32).reshape(1)
    chip = my_chip.astype(jnp.int32).reshape(1)
    last_g, mixer_g, attn_g = _BIG[:2], _BIG[2:6], _BIG[6:]

    def layout(entries):
        offs, off = {}, 0
        for n, rows, _ in entries:
            offs[n] = (off, rows)
            off += rows
        return offs, off

    def dest(group, buf, name, row0=0):
        offs, total = group
        return _Dest(buf, total, offs[name][0], offs[name][1], row0)

    lay_last, lay_mixer, lay_attn = layout(last_g), layout(mixer_g), layout(attn_g)

    def ffn_bwd(tag, dhb, dh, g, u, a, xn, h_in, gain, wgu_t, wd, group, buf, act_hosted=None, dx_hosted=None):
        (dg, du), act_out = _ffn_bwd_act(tag + "_bwd_act", dhb, wd, g, u, tm_big, 1408, hosted=act_hosted)
        buf = _mm_tn(tag + "_dw_down", a, dhb, 0.5, 1408, TK, dest(group, buf, tag + "_w_down"))
        buf = _mm_tn(tag + "_dw_gate", dg, xn, 1.0, 1408, TK, dest(group, buf, tag + "_w_gu"))
        buf = _mm_tn(tag + "_dw_up", du, xn, 1.0, 1408, TK, dest(group, buf, tag + "_w_gu", DFF))
        (dh_o, dhb_o, dgain), dx_out = _dx_rms_bwd(
            tag + "_bwd_dx", [(dg, wgu_t, 0, DFF, False), (du, wgu_t, 1, DFF, False)], h_in, gain, dh, tm_s,
            hosted=dx_hosted(buf) if dx_hosted else None)
        return dh_o, dhb_o, dgain, buf, act_out, dx_out

    dh3, dh3b, gs["ffn2_norm"], g_attn, _, _ = ffn_bwd(
        "ffn2", dh4b, dh4, g2, u2, a2, xn4, h3, ffn2_norm, wf["ffn2_w_gu"], wf["ffn2_w_down"], lay_attn, None)

    g_attn = _mm_tn("dw_o", ob, dh3b, 1.0, 1024, 2 * TK, dest(lay_attn, g_attn, "w_o"))
    dq, dk, dv = _attn_bwd("attn_bwd", dh3b, wf["w_o"], qb, probs, kb, vb, tm_s)
    g_attn, gs["mem_norm"] = _kv_bwd("kv_bwd", dk, dv, memn, wf["w_kv"], mem2, mem_norm,
                                     dest(lay_attn, g_attn, "w_kv"))
    g_attn = _mm_tn("dw_q", xq, dq, 1.0, 1024, 2 * TK, dest(lay_attn, g_attn, "w_q"))

    g4_attn = g_attn.reshape(4, 2, lay_attn[1], D)
    (dh2, dh2b, gs["xattn_norm"]), (sib_attn,) = _dx_rms_bwd(
        "attn_bwd_dx", [(dq, wf["w_q"], 0, D, True)], h2, xattn_norm, dh3, tm_s,
        hosted=_pair_exchange_comm(g4_attn))
    part_attn = _pair_sum("grads_pair_sum_attn", g4_attn, sib_attn, core, 784)

    g_mixer = _mm_tn("dw_out", merged, dh2b, 1.0, 1024, 2 * TK, dest(lay_mixer, None, "w_out"))
    dya, dyb, dp_g, cs_g = _merge_bwd("merge_bwd", dh2b, wf["w_out"], ya, yb, p, tm_s)
    g_mixer = _mm_tn("dw_a", act_a, dya, 1.0, 1024, 2 * TK, dest(lay_mixer, g_mixer, "w_a_out"))
    g_mixer = _mm_tn("dw_b", act_b, dyb, 1.0, 1024, 2 * TK, dest(lay_mixer, g_mixer, "w_b_out"))
    (dp_a, cs_a, d_convw, gs["conv_b"], gs["conv_ln_g"], gs["conv_ln_b"]), (chips_attn,) = _conv_bwd(
        "conv_bwd", dya, wf["w_a_out"], c_conv, p, conv_w_pad, conv_ln_g, conv_ln_b, tm_c,
        hosted=_chip_exchange_comm(part_attn))
    gsum_attn = _chip_sum("grads_chip_sum_attn", part_attn, chips_attn, chip, 784)
    dp_b, cs_b, d_sguw, d_sgub, gs["sgu_ln_g"], gs["sgu_ln_b"] = _sgu_bwd(
        "sgu_bwd", dyb, wf["w_b_out"], p, sgu_ln_g, sgu_ln_b, sgu_w[0], bias_full, tm_s)
    gs["conv_w"] = d_convw[:CW].reshape(1, CW, D)
    gs["sgu_w"] = d_sguw
    gs["sgu_b"] = d_sgub[:, :GROUPS].T
    gs["b_in"] = jnp.concatenate([cs_a, cs_b, cs_g], axis=1)
    for j, (tag, dpart) in enumerate((("a", dp_a), ("b", dp_b), ("g", dp_g))):
        g_mixer = _mm_tn("dw_in_" + tag, dpart, n_mix, 1.0, 1024, 2 * TK,
                         dest(lay_mixer, g_mixer, "w_in", 2 * D * j))
    g4_mixer = g_mixer.reshape(4, 2, lay_mixer[1], D)
    (dh1, dh1b, gs["mix_norm"]), (sib_mixer,) = _dx_rms_bwd(
        "mix_bwd_dx", [(dp_a, wf["w_in"], 0, 2 * D, False), (dp_b, wf["w_in"], 1, 2 * D, False),
                       (dp_g, wf["w_in"], 2, 2 * D, False)], h1, mix_norm, dh2, tm_s,
        hosted=_pair_exchange_comm(g4_mixer))
    part_mixer = _pair_sum("grads_pair_sum_mixer", g4_mixer, sib_mixer, core, 576)

    dx, _, gs["ffn1_norm"], _, (chips_mixer,), (last_slots,) = ffn_bwd(
        "ffn1", dh1b, dh1, g1, u1, a1, xn1, xs, ffn1_norm, wf["ffn1_w_gu"], wf["ffn1_w_down"], lay_last, None,
        act_hosted=_chip_exchange_comm(part_mixer), dx_hosted=_all_to_all_comm)
    gsum_mixer = _chip_sum("grads_chip_sum_mixer", part_mixer, chips_mixer, chip, 576)
    gsum_last = _slot_sum("grads_slot_sum_ffn1", last_slots, 528)

    grads, deltas, new_m, new_v = {}, {}, {}, {}
    for entries, gsum in ((last_g, gsum_last), (mixer_g, gsum_mixer), (attn_g, gsum_attn)):
        off = 0
        for name, rows, transposed in entries:
            gsh = gsum[off:off + rows]
            off += rows
            gsh = gsh.T if transposed else gsh
            d, mo, vo = _adamw("adamw_" + name, w[name][0], gsh, mom[name][0], vel[name][0], gsh.shape[0] // 2)
            grads[name], deltas[name], new_m[name], new_v[name] = gsh[None], d[None], mo[None], vo[None]

    shapes = {n: w[n].shape for n, _ in _SMALL}
    (g8,) = _run_comm("gather_small_grads", _gather_comm(_pack_small(gs, my_dev)))
    sg, sd, sm, sv = _adamw_small("adamw_small", _pack_small(w, my_dev), g8,
                                  _pack_small(mom, my_dev), _pack_small(vel, my_dev))
    for dst, src in ((grads, sg), (deltas, sd), (new_m, sm), (new_v, sv)):
        dst.update(_unpack_small(src, shapes, my_dev))

    loss = lax.psum(loss_blk[0, 0], AXES)
    order = ["ffn1_norm", "ffn1_w_gu", "ffn1_w_down", "mix_norm", "w_in", "b_in", "conv_w", "conv_b",
             "conv_ln_g", "conv_ln_b", "w_a_out", "sgu_ln_g", "sgu_ln_b", "sgu_w", "sgu_b", "w_b_out",
             "w_out", "xattn_norm", "mem_norm", "w_q", "w_kv", "w_o", "ffn2_norm", "ffn2_w_gu",
             "ffn2_w_down", "final_norm"]
    return (loss, dx.reshape(x.shape), *[grads[n] for n in order], *[deltas[n] for n in order],
            *[new_m[n] for n in order], *[new_v[n] for n in order])
```

```python
import functools
import math

import jax
import jax.numpy as jnp
from jax import lax
from jax.experimental import pallas as pl
from jax.experimental.pallas import tpu as pltpu

F32 = jnp.float32
BF16 = jnp.bfloat16
MESH = pl.DeviceIdType.MESH
AXES = ("x", "y", "c")

D = 1024
DFF = 2816
NMEM = 256
HEADS = 4
HD = D // HEADS
CW = 31
HALO = 32
CHUNK = 128
GROUPS = 4
GD = D // GROUPS
EPS_RMS = 1e-6
EPS_LN = 1e-5
LR, B1, B2, EPS_ADAM, WD, STEP = 0.001, 0.9, 0.999, 1e-08, 0.01, 10
NDEV = 8
VMEM_LIMIT = 56 * 1024 * 1024
TK = 2048


def _params(*sem):
    return pltpu.CompilerParams(dimension_semantics=sem, vmem_limit_bytes=VMEM_LIMIT)


def _dot(a, b):
    return jnp.dot(a, b, preferred_element_type=F32)


def _dot_nt(a, b):
    return lax.dot_general(a, b, (((1,), (1,)), ((), ())), preferred_element_type=F32)


def _dot_tn(a, b):
    return lax.dot_general(a, b, (((0,), (0,)), ((), ())), preferred_element_type=F32)


def _sigmoid(x):
    return 0.5 * jnp.tanh(0.5 * x) + 0.5


_GELU_C = math.sqrt(2.0 / math.pi)


def _gelu_with_grad(x):
    x2 = x * x
    t = jnp.tanh(_GELU_C * (x + 0.044715 * (x2 * x)))
    half = 0.5 * (1.0 + t)
    return x * half, half + 0.5 * x * (1.0 - t * t) * (_GELU_C * (1.0 + 3.0 * 0.044715 * x2))


def _gelu(x):
    return _gelu_with_grad(x)[0]


def _rms_stats(h):
    r = lax.rsqrt(jnp.mean(h * h, axis=-1, keepdims=True) + EPS_RMS)
    return r, h * r


def _rms_bwd(dxn, h, gain):
    r, xh = _rms_stats(h)
    dgain = jnp.sum(dxn * xh, axis=0, keepdims=True)
    dxh = dxn * gain
    dh = r * (dxh - xh * jnp.mean(dxh * xh, axis=-1, keepdims=True))
    return dh, dgain


def _ln_stats(c):
    mu = jnp.mean(c, axis=-1, keepdims=True)
    xc = c - mu
    rstd = lax.rsqrt(jnp.mean(xc * xc, axis=-1, keepdims=True) + EPS_LN)
    return rstd, xc * rstd


def _ln_bwd(dy, xhat, rstd, g):
    dxh = dy * g
    return rstd * (dxh - jnp.mean(dxh, axis=-1, keepdims=True)
                   - xhat * jnp.mean(dxh * xhat, axis=-1, keepdims=True))


def _row(tm, cols, cb=0):
    return pl.BlockSpec((tm, cols), lambda i, _cb=cb: (i, _cb))


def _const(shape):
    n = len(shape)
    return pl.BlockSpec(shape, lambda *_: (0,) * n)


def _sds(shape, dtype):
    return jax.ShapeDtypeStruct(shape, dtype)


def _acc(ref, first, val):
    @pl.when(first)
    def _():
        ref[...] = jnp.zeros_like(ref)
    ref[...] += val


class _Comm:
    def __init__(self, args, out_shapes, scratch, start, finish, forward=None):
        self.args, self.out_shapes, self.scratch = args, out_shapes, scratch
        self.start, self.finish, self.forward = start, finish, forward


_ANY = pl.BlockSpec(memory_space=pl.ANY)


def _run_comm(name, comm):
    ni, no = len(comm.args), len(comm.out_shapes)

    def body(*refs):
        ins, outs, sems = refs[:ni], refs[ni:ni + no], refs[ni + no:]
        comm.start(ins, outs, sems)
        if comm.forward:
            comm.forward(ins, outs, sems)
        comm.finish(ins, outs, sems)

    return pl.pallas_call(
        body, name=name, out_shape=list(comm.out_shapes), in_specs=[_ANY] * ni, out_specs=[_ANY] * no,
        scratch_shapes=list(comm.scratch),
    )(*comm.args)


def _call(name, body, grid, in_specs, out_specs, out_shape, args, scratch=(), sem=None, hosted=None):
    n_in, n_out, n_scr = len(in_specs), len(out_specs), len(scratch)
    if not hosted:
        outs = pl.pallas_call(
            body, name=name, grid=grid, in_specs=list(in_specs), out_specs=list(out_specs),
            out_shape=list(out_shape), scratch_shapes=list(scratch), compiler_params=_params(*sem),
        )(*args)
        return outs, []
    comms = list(hosted) if isinstance(hosted, (list, tuple)) else [hosted]
    hi = sum(len(cm.args) for cm in comms)
    ho = sum(len(cm.out_shapes) for cm in comms)

    def wrapped(*refs):
        ins, h_in = refs[:n_in], refs[n_in:n_in + hi]
        o0 = n_in + hi
        outs, h_out = refs[o0:o0 + n_out], refs[o0 + n_out:o0 + n_out + ho]
        s0 = o0 + n_out + ho
        scr, h_sems = refs[s0:s0 + n_scr], refs[s0 + n_scr:]
        ids = [pl.program_id(a) for a in range(len(grid))]
        first = functools.reduce(jnp.logical_and, [i == 0 for i in ids])
        last = functools.reduce(jnp.logical_and, [i == g - 1 for i, g in zip(ids, grid)])
        parts, a0, b0, c0 = [], 0, 0, 0
        for cm in comms:
            na, nb, nc = len(cm.args), len(cm.out_shapes), len(cm.scratch)
            parts.append((cm, h_in[a0:a0 + na], h_out[b0:b0 + nb], h_sems[c0:c0 + nc]))
            a0, b0, c0 = a0 + na, b0 + nb, c0 + nc

        @pl.when(first)
        def _():
            for cm, ci, co, cs in parts:
                cm.start(ci, co, cs)

        body(*ins, *outs, *scr)

        step, total = ids[0], grid[0]
        for i, g in zip(ids[1:], grid[1:]):
            step, total = step * g + i, total * g

        @pl.when(step == (3 * total) // 4)
        def _():
            for cm, ci, co, cs in parts:
                if cm.forward:
                    cm.forward(ci, co, cs)

        @pl.when(last)
        def _():
            for cm, ci, co, cs in parts:
                cm.finish(ci, co, cs)

    res = pl.pallas_call(
        wrapped, name=name, grid=grid, in_specs=list(in_specs) + [_ANY] * hi,
        out_specs=list(out_specs) + [_ANY] * ho,
        out_shape=list(out_shape) + [s for cm in comms for s in cm.out_shapes],
        scratch_shapes=list(scratch) + [s for cm in comms for s in cm.scratch],
        compiler_params=_params(*(["arbitrary"] * len(grid))),
    )(*args, *[a for cm in comms for a in cm.args])
    return res[:n_out], res[n_out:]


def _gather_comm(blk):
    r, cdim = blk.shape

    def copies(x_ref, out_ref, send_sems, recv_sems, local_sem):
        x, y, c = lax.axis_index("x"), lax.axis_index("y"), lax.axis_index("c")
        me, sibling = (x, y, c), (x, y, 1 - c)
        chips = [(1 - x, y), (x, 1 - y), (1 - x, 1 - y)]

        def slot(px, py, pc):
            return out_ref.at[4 * px + 2 * py + pc]

        def copy(k, block, to, src=None):
            return pltpu.make_async_remote_copy(
                src_ref=slot(*block) if src is None else src, dst_ref=slot(*block),
                send_sem=send_sems.at[k], recv_sem=recv_sems.at[k],
                device_id=to, device_id_type=MESH)

        mine = pltpu.make_async_copy(x_ref, slot(*me), local_sem)
        first = [copy(0, me, sibling, src=x_ref)]
        first += [copy(1 + j, me, (*chip, c), src=x_ref) for j, chip in enumerate(chips)]
        passed = [copy(4 + j, (*chip, c), sibling) for j, chip in enumerate(chips)]
        landed = [copy(1 + j, (*chip, c), me) for j, chip in enumerate(chips)]
        from_sibling = [copy(0, sibling, me)] + [copy(4 + j, (*chip, 1 - c), me) for j, chip in enumerate(chips)]
        return mine, first, passed, landed, from_sibling

    def start(ins, outs, sems):
        mine, first, _, _, _ = copies(ins[0], outs[0], *sems)
        mine.start()
        for cp in first:
            cp.start()

    def forward(ins, outs, sems):
        _, _, passed, landed, _ = copies(ins[0], outs[0], *sems)
        for arrived, onward in zip(landed, passed):
            arrived.wait_recv()
            onward.start()

    def finish(ins, outs, sems):
        mine, first, passed, _, from_sibling = copies(ins[0], outs[0], *sems)
        for cp in from_sibling:
            cp.wait_recv()
        for cp in first + passed:
            cp.wait_send()
        mine.wait()

    return _Comm([blk], [_sds((NDEV, r, cdim), blk.dtype)],
                 [pltpu.SemaphoreType.DMA((7,)), pltpu.SemaphoreType.DMA((7,)), pltpu.SemaphoreType.DMA],
                 start, finish, forward)


def _exchange_comm(src, n, out_rows, make):
    r, cdim = out_rows

    def copies(src_ref, out_ref, send_sems, recv_sems):
        out = []
        for k in range(n):
            s, d, to = make(k, src_ref, out_ref)
            out.append(pltpu.make_async_remote_copy(
                src_ref=s, dst_ref=d, send_sem=send_sems.at[k], recv_sem=recv_sems.at[k],
                device_id=to, device_id_type=MESH))
        return out

    def start(ins, outs, sems):
        for cp in copies(ins[0], outs[0], *sems):
            cp.start()

    def finish(ins, outs, sems):
        cps = copies(ins[0], outs[0], *sems)
        for cp in cps:
            cp.wait_recv()
        for cp in cps:
            cp.wait_send()

    return _Comm([src], [_sds((n, r, cdim), src.dtype)],
                 [pltpu.SemaphoreType.DMA((n,)), pltpu.SemaphoreType.DMA((n,))], start, finish)


def _pair_exchange_comm(g4):
    def make(k, g_ref, out_ref):
        x, y, c = lax.axis_index("x"), lax.axis_index("y"), lax.axis_index("c")
        return g_ref.at[k, 1 - c], out_ref.at[k], (x, y, 1 - c)

    return _exchange_comm(g4, 4, g4.shape[2:], make)


def _chip_exchange_comm(part):
    def make(k, p_ref, out_ref):
        x, y, c = lax.axis_index("x"), lax.axis_index("y"), lax.axis_index("c")
        px = x if ((k + 1) >> 1) == 0 else 1 - x
        py = y if ((k + 1) & 1) == 0 else 1 - y
        return p_ref.at[2 * px + py], out_ref.at[k], (px, py, c)

    return _exchange_comm(part, 3, part.shape[1:], make)


def _all_to_all_comm(g8):
    _, r, cdim = g8.shape

    def copies(g_ref, out_ref, send_sems, recv_sems, local_sem):
        x, y, c = lax.axis_index("x"), lax.axis_index("y"), lax.axis_index("c")
        me = 4 * x + 2 * y + c
        remote = []
        for k in range(1, NDEV):
            px = 1 - x if k & 4 else x
            py = 1 - y if k & 2 else y
            pc = 1 - c if k & 1 else c
            remote.append(pltpu.make_async_remote_copy(
                src_ref=g_ref.at[4 * px + 2 * py + pc], dst_ref=out_ref.at[me],
                send_sem=send_sems.at[k - 1], recv_sem=recv_sems.at[k - 1],
                device_id=(px, py, pc), device_id_type=MESH))
        return pltpu.make_async_copy(g_ref.at[me], out_ref.at[me], local_sem), remote

    def start(ins, outs, sems):
        mine, remote = copies(ins[0], outs[0], *sems)
        mine.start()
        for cp in remote:
            cp.start()

    def finish(ins, outs, sems):
        mine, remote = copies(ins[0], outs[0], *sems)
        for cp in remote:
            cp.wait_recv()
        for cp in remote:
            cp.wait_send()
        mine.wait()

    return _Comm([g8], [_sds((NDEV, r, cdim), g8.dtype)],
                 [pltpu.SemaphoreType.DMA((NDEV - 1,)), pltpu.SemaphoreType.DMA((NDEV - 1,)),
                  pltpu.SemaphoreType.DMA], start, finish)


def _slot_sum(name, slots, tr):
    _, r, cdim = slots.shape

    def body(s_ref, o_ref):
        s = s_ref[0].astype(F32)
        for k in range(1, NDEV):
            s = s + s_ref[k].astype(F32)
        o_ref[...] = s

    return pl.pallas_call(
        body, name=name, grid=(r // tr,),
        in_specs=[pl.BlockSpec((NDEV, tr, cdim), lambda i: (0, i, 0))],
        out_specs=pl.BlockSpec((tr, cdim), lambda i: (i, 0)),
        out_shape=_sds((r, cdim), F32), compiler_params=_params("parallel"),
    )(slots)


def _pair_sum(name, g4, recv, core, tr):
    _, _, r, cdim = g4.shape

    def body(core_ref, a_ref, b_ref, o_ref):
        o_ref[...] = (a_ref[...].astype(F32) + b_ref[...].astype(F32)).astype(o_ref.dtype)

    return pl.pallas_call(
        body, name=name,
        grid_spec=pltpu.PrefetchScalarGridSpec(
            num_scalar_prefetch=1, grid=(4, r // tr),
            in_specs=[pl.BlockSpec((None, None, tr, cdim), lambda k, i, cr: (k, cr[0], i, 0)),
                      pl.BlockSpec((None, tr, cdim), lambda k, i, cr: (k, i, 0))],
            out_specs=pl.BlockSpec((None, tr, cdim), lambda k, i, cr: (k, i, 0))),
        out_shape=_sds((4, r, cdim), BF16),
        compiler_params=_params("parallel", "parallel"),
    )(core, g4, recv)


def _chip_sum(name, part, recv, chip, tr):
    _, r, cdim = part.shape

    def body(chip_ref, a_ref, b_ref, o_ref):
        s = a_ref[...].astype(F32)
        for k in range(3):
            s = s + b_ref[k].astype(F32)
        o_ref[...] = s

    return pl.pallas_call(
        body, name=name,
        grid_spec=pltpu.PrefetchScalarGridSpec(
            num_scalar_prefetch=1, grid=(r // tr,),
            in_specs=[pl.BlockSpec((None, tr, cdim), lambda i, cr: (cr[0], i, 0)),
                      pl.BlockSpec((3, tr, cdim), lambda i, cr: (0, i, 0))],
            out_specs=pl.BlockSpec((tr, cdim), lambda i, cr: (i, 0))),
        out_shape=_sds((r, cdim), F32),
        compiler_params=_params("parallel"),
    )(chip, part, recv)


def _rms_cast(name, h, gain, tm, hosted=None):
    t = h.shape[0]

    def body(h_ref, g_ref, o_ref):
        _, xh = _rms_stats(h_ref[...])
        o_ref[...] = (xh * g_ref[...]).astype(BF16)

    return _call(name, body, (t // tm,), [_row(tm, D), _const((1, D))], [_row(tm, D)],
                 [_sds((t, D), BF16)], (h, gain), sem=("parallel",), hosted=hosted)


def _ffn_up(name, xn, wgu_t, tm, tn, hosted=None):
    t = xn.shape[0]
    nh = DFF // tn

    def body(x_ref, wg_ref, wu_ref, s_ref, f_ref, a_ref):
        x = x_ref[...]
        g = _dot_nt(x, wg_ref[...])
        u = _dot_nt(x, wu_ref[...])
        sg = _sigmoid(g)
        s = g * sg
        s_ref[...] = s.astype(BF16)
        f_ref[...] = (u * (sg + s * (1.0 - sg))).astype(BF16)
        a_ref[...] = (s * u).astype(BF16)

    o = pl.BlockSpec((tm, tn), lambda j, i: (i, j))
    return _call(name, body, (nh, t // tm),
                 [pl.BlockSpec((tm, D), lambda j, i: (i, 0)),
                  pl.BlockSpec((tn, D), lambda j, i: (j, 0)),
                  pl.BlockSpec((tn, D), lambda j, i: (j + nh, 0))],
                 [o, o, o], [_sds((t, DFF), BF16)] * 3, (xn, wgu_t, wgu_t),
                 sem=("parallel", "parallel"), hosted=hosted)


def _ffn_down(name, a, wd, h, gain, tm):
    t = a.shape[0]

    def body(a_ref, w_ref, h_ref, g_ref, o_ref, n_ref):
        hn = h_ref[...] + 0.5 * _dot(a_ref[...], w_ref[...])
        o_ref[...] = hn
        _, xh = _rms_stats(hn)
        n_ref[...] = (xh * g_ref[...]).astype(BF16)

    return pl.pallas_call(
        body, name=name, grid=(t // tm,),
        in_specs=[_row(tm, DFF), _const((DFF, D)), _row(tm, D), _const((1, D))],
        out_specs=[_row(tm, D), _row(tm, D)],
        out_shape=[_sds((t, D), F32), _sds((t, D), BF16)],
        compiler_params=_params("parallel"),
    )(a, wd, h, gain)


def _mix_in(name, n, win_t, b_in, tm, tn, hosted=None):
    t = n.shape[0]

    def body(n_ref, w_ref, b_ref, p_ref):
        p_ref[...] = (_dot_nt(n_ref[...], w_ref[...]) + b_ref[...]).astype(BF16)

    return _call(name, body, (6 * D // tn, t // tm),
                 [pl.BlockSpec((tm, D), lambda j, i: (i, 0)),
                  pl.BlockSpec((tn, D), lambda j, i: (j, 0)),
                  pl.BlockSpec((1, tn), lambda j, i: (0, j))],
                 [pl.BlockSpec((tm, tn), lambda j, i: (i, j))], [_sds((t, 6 * D), BF16)],
                 (n, win_t, b_in), sem=("parallel", "parallel"), hosted=hosted)


RC = 64
LANES = 128


def _shift_copies(ext, shifted, tm):
    n = tm + HALO - 8
    for m in range(1, 8):
        shifted[m - 1] = ext[pl.ds(m, n), :]


def _by_residue(offs):
    groups = {}
    for k, off in enumerate(offs):
        q, m = divmod(off, 8)
        groups.setdefault(m, []).append((k, q))
    return groups


def _residue_window(ext, shifted, m, taps, base, cs):
    src = ext if m == 0 else shifted.at[m - 1]
    return src[pl.ds(base, RC + 8 * max(q for _, q in taps)), cs]


def _tap_sum(out_ref, bias_ref, w_ref, ext, shifted, offs, tm):
    groups = _by_residue(offs)

    def chunk(j, carry):
        base = pl.multiple_of(j * RC, RC)
        for c in range(D // LANES):
            cs = pl.ds(c * LANES, LANES)
            acc = jnp.zeros((RC, LANES), F32)
            if bias_ref is not None:
                acc = acc + bias_ref[:, cs]
            for m, taps in groups.items():
                big = _residue_window(ext, shifted, m, taps, base, cs)
                for k, q in taps:
                    acc = acc + w_ref[pl.ds(k, 1), cs] * big[8 * q:8 * q + RC]
            out_ref[pl.ds(base, RC), cs] = acc
        return carry

    lax.fori_loop(0, tm // RC, chunk, 0)


def _tap_corr(dw_ref, dc_ext, ext, shifted, offs, tm):
    groups = _by_residue(offs)
    for c in range(D // LANES):
        cs = pl.ds(c * LANES, LANES)

        def chunk(j, accs, cs=cs):
            base = pl.multiple_of(j * RC, RC)
            dcv = dc_ext[pl.ds(base, RC), cs]
            out = list(accs)
            for m, taps in groups.items():
                big = _residue_window(ext, shifted, m, taps, base, cs)
                for k, q in taps:
                    prod = dcv * big[8 * q:8 * q + RC]
                    part = prod[0:8]
                    for s in range(1, RC // 8):
                        part = part + prod[8 * s:8 * s + 8]
                    out[k] = accs[k] + part
            return tuple(out)

        accs = lax.fori_loop(0, tm // RC, chunk, tuple(jnp.zeros((8, LANES), F32) for _ in offs))
        for k in range(len(offs)):
            dw_ref[pl.ds(k, 1), cs] += jnp.sum(accs[k], axis=0, keepdims=True)


def _conv_fwd(name, p, conv_w, conv_b, ln_g, ln_b, tm):
    t = p.shape[0]

    def body(av_ref, ag_ref, w_ref, cb_ref, lg_ref, lb_ref, c_ref, a_ref, ext, shifted):
        i = pl.program_id(0)

        @pl.when(i == 0)
        def _():
            ext[pl.ds(0, HALO), :] = jnp.zeros((HALO, D), F32)

        ext[pl.ds(HALO, tm), :] = av_ref[...].astype(F32) * _sigmoid(ag_ref[...].astype(F32))
        _shift_copies(ext, shifted, tm)
        _tap_sum(c_ref, cb_ref, w_ref, ext, shifted, [HALO - (CW - 1) + k for k in range(CW)], tm)
        rstd, chat = _ln_stats(c_ref[...])
        ca = chat * lg_ref[...] + lb_ref[...]
        a_ref[...] = (ca * _sigmoid(ca)).astype(BF16)
        ext[pl.ds(0, HALO), :] = ext[pl.ds(tm, HALO), :]

    return pl.pallas_call(
        body, name=name, grid=(t // tm,),
        in_specs=[_row(tm, D, 0), _row(tm, D, 1), _const((HALO, D)), _const((1, D)),
                  _const((1, D)), _const((1, D))],
        out_specs=[_row(tm, D), _row(tm, D)],
        out_shape=[_sds((t, D), F32), _sds((t, D), BF16)],
        scratch_shapes=[pltpu.VMEM((tm + HALO, D), F32), pltpu.VMEM((7, tm + HALO - 8, D), F32)],
        compiler_params=_params("arbitrary"),
    )(p, p, conv_w, conv_b, ln_g, ln_b)


def _sgu_mask():
    rows = lax.broadcasted_iota(jnp.int32, (CHUNK, CHUNK), 0)
    cols = lax.broadcasted_iota(jnp.int32, (CHUNK, CHUNK), 1)
    return cols <= rows


def _sgu_fwd(name, p, ln_g, ln_b, sgu_w, bias_full, tm):
    t = p.shape[0]

    def body(bu_ref, bv_ref, lg_ref, lb_ref, ws_ref, bias_ref, o_ref):
        mask = _sgu_mask()
        _, vhat = _ln_stats(_gelu(bv_ref[...].astype(F32)))
        vn = (vhat * lg_ref[...] + lb_ref[...]).astype(BF16)
        ub = _gelu(bu_ref[...].astype(F32))
        for g in range(GROUPS):
            wm = jnp.where(mask, ws_ref[g], 0.0).astype(BF16)
            cs = slice(g * GD, (g + 1) * GD)
            for cc in range(tm // CHUNK):
                rs = slice(cc * CHUNK, (cc + 1) * CHUNK)
                mixed = _dot(wm, vn[rs, cs]) + bias_ref[:, cs]
                o_ref[rs, cs] = (ub[rs, cs] * mixed).astype(BF16)

    return pl.pallas_call(
        body, name=name, grid=(t // tm,),
        in_specs=[_row(tm, D, 2), _row(tm, D, 3), _const((1, D)), _const((1, D)),
                  _const((GROUPS, CHUNK, CHUNK)), _const((CHUNK, D))],
        out_specs=_row(tm, D), out_shape=_sds((t, D), BF16),
        compiler_params=_params("parallel"),
    )(p, p, ln_g, ln_b, sgu_w, bias_full)


def _merge_fwd(name, act_a, act_b, p, w_a, w_b, w_out, h, gain, tm):
    t = h.shape[0]

    def body(a_ref, b_ref, ga_ref, gb_ref, wa_ref, wb_ref, wo_ref, h_ref, g_ref,
             ya_ref, yb_ref, mg_ref, ho_ref, xn_ref):
        ya = _dot(a_ref[...], wa_ref[...])
        yb = _dot(b_ref[...], wb_ref[...])
        ya_ref[...] = ya.astype(BF16)
        yb_ref[...] = yb.astype(BF16)
        merged = (_sigmoid(ga_ref[...].astype(F32)) * ya
                  + _sigmoid(gb_ref[...].astype(F32)) * yb).astype(BF16)
        mg_ref[...] = merged
        hn = h_ref[...] + _dot(merged, wo_ref[...])
        ho_ref[...] = hn
        _, xh = _rms_stats(hn)
        xn_ref[...] = (xh * g_ref[...]).astype(BF16)

    rb = _row(tm, D)
    return pl.pallas_call(
        body, name=name, grid=(t // tm,),
        in_specs=[rb, rb, _row(tm, D, 4), _row(tm, D, 5), _const((D, D)), _const((D, D)),
                  _const((D, D)), rb, _const((1, D))],
        out_specs=[rb] * 5,
        out_shape=[_sds((t, D), BF16)] * 3 + [_sds((t, D), F32), _sds((t, D), BF16)],
        compiler_params=_params("parallel"),
    )(act_a, act_b, p, p, w_a, w_b, w_out, h, gain)


def _kv_fwd(name, mem, gain, wkv_t):
    def body(m_ref, g_ref, w_ref, mn_ref, k_ref, v_ref):
        _, xh = _rms_stats(m_ref[...])
        mn = (xh * g_ref[...]).astype(BF16)
        mn_ref[...] = mn
        kv = _dot_nt(mn, w_ref[...])
        k_ref[...] = kv[:, :D].astype(BF16)
        v_ref[...] = kv[:, D:].astype(BF16)

    return pl.pallas_call(
        body, name=name,
        out_shape=[_sds((NMEM, D), BF16)] * 3,
        compiler_params=pltpu.CompilerParams(vmem_limit_bytes=VMEM_LIMIT),
    )(mem, gain, wkv_t)


def _softmax_rows(s):
    e = jnp.exp(s - jnp.max(s, axis=-1, keepdims=True))
    return e / jnp.sum(e, axis=-1, keepdims=True)


def _attn_fwd(name, xq, w_q, kb, vb, w_o, h, gain, tm):
    t = h.shape[0]
    scale = 1.0 / math.sqrt(HD)

    def body(x_ref, wq_ref, k_ref, v_ref, wo_ref, h_ref, g_ref, q_ref, p_ref, o_ref, ho_ref, xn_ref):
        q_ref[...] = _dot(x_ref[...], wq_ref[...]).astype(BF16)
        for hd in range(HEADS):
            cs = slice(hd * HD, (hd + 1) * HD)
            ps = slice(hd * NMEM, (hd + 1) * NMEM)
            p_ref[:, ps] = _softmax_rows(_dot_nt(q_ref[:, cs], k_ref[:, cs]) * scale).astype(BF16)
            o_ref[:, cs] = _dot(p_ref[:, ps], v_ref[:, cs]).astype(BF16)
        hn = h_ref[...] + _dot(o_ref[...], wo_ref[...])
        ho_ref[...] = hn
        _, xh = _rms_stats(hn)
        xn_ref[...] = (xh * g_ref[...]).astype(BF16)

    rb = _row(tm, D)
    return pl.pallas_call(
        body, name=name, grid=(t // tm,),
        in_specs=[rb, _const((D, D)), _const((NMEM, D)), _const((NMEM, D)), _const((D, D)), rb,
                  _const((1, D))],
        out_specs=[rb, _row(tm, HEADS * NMEM), rb, rb, rb],
        out_shape=[_sds((t, D), BF16), _sds((t, HEADS * NMEM), BF16), _sds((t, D), BF16),
                   _sds((t, D), F32), _sds((t, D), BF16)],
        compiler_params=_params("parallel"),
    )(xq, w_q, kb, vb, w_o, h, gain)


def _ffn_down_loss(name, a, wd, h, gain, target, tm):
    t = h.shape[0]
    steps = t // tm

    def body(a_ref, w_ref, h_ref, g_ref, t_ref, dh_ref, dhb_ref, loss_ref, dg_ref, lacc):
        i = pl.program_id(0)
        hv = h_ref[...] + 0.5 * _dot(a_ref[...], w_ref[...])
        r, xh = _rms_stats(hv)
        err = xh * g_ref[...] - t_ref[...]
        _acc(lacc, i == 0, jnp.sum(err * err, axis=0, keepdims=True))
        dy = err * (1.0 / D)
        _acc(dg_ref, i == 0, jnp.sum(dy * xh, axis=0, keepdims=True))
        dxh = dy * g_ref[...]
        dh = r * (dxh - xh * jnp.mean(dxh * xh, axis=-1, keepdims=True))
        dh_ref[...] = dh
        dhb_ref[...] = dh.astype(BF16)

        @pl.when(i == steps - 1)
        def _():
            loss_ref[...] = jnp.zeros((8, 128), F32) + (0.5 / D) * jnp.sum(lacc[...])

    rb = _row(tm, D)
    return pl.pallas_call(
        body, name=name, grid=(steps,),
        in_specs=[_row(tm, DFF), _const((DFF, D)), rb, _const((1, D)), rb],
        out_specs=[rb, rb, _const((8, 128)), _const((1, D))],
        out_shape=[_sds((t, D), F32), _sds((t, D), BF16), _sds((8, 128), F32), _sds((1, D), F32)],
        scratch_shapes=[pltpu.VMEM((1, D), F32)],
        compiler_params=_params("arbitrary"),
    )(a, wd, h, gain, target)


def _ffn_bwd_act(name, dhb, wd, s, f, tm, tn, hosted=None):
    t = dhb.shape[0]

    def body(d_ref, w_ref, s_ref, f_ref, dg_ref, du_ref):
        da = 0.5 * _dot_nt(d_ref[...], w_ref[...])
        dg_ref[...] = (da * f_ref[...].astype(F32)).astype(BF16)
        du_ref[...] = (da * s_ref[...].astype(F32)).astype(BF16)

    o = pl.BlockSpec((tm, tn), lambda j, i: (i, j))
    return _call(name, body, (DFF // tn, t // tm),
                 [pl.BlockSpec((tm, D), lambda j, i: (i, 0)),
                  pl.BlockSpec((tn, D), lambda j, i: (j, 0)), o, o],
                 [o, o], [_sds((t, DFF), BF16)] * 2, (dhb, wd, s, f),
                 sem=("parallel", "parallel"), hosted=hosted)


def _dx_rms_bwd(name, pairs, h, gain, dh_in, tm, hosted=None):
    t = h.shape[0]
    np_ = len(pairs)

    def body(*refs):
        a_refs = refs[:np_]
        b_refs = refs[np_:2 * np_]
        h_ref, g_ref, d_ref, o_ref, ob_ref, dg_ref = refs[2 * np_:]
        dxn = None
        for (a_ref, b_ref, pr) in zip(a_refs, b_refs, pairs):
            y = _dot_nt(a_ref[...], b_ref[...]) if pr[4] else _dot(a_ref[...], b_ref[...])
            dxn = y if dxn is None else dxn + y
        dh, dgain = _rms_bwd(dxn, h_ref[...], g_ref[...])
        _acc(dg_ref, pl.program_id(0) == 0, dgain)
        out = d_ref[...] + dh
        o_ref[...] = out
        ob_ref[...] = out.astype(BF16)

    ins, args = [], []
    for (a, b, blk, rows, tr) in pairs:
        ins.append(_row(tm, a.shape[1]))
        args.append(a)
    for (a, b, blk, rows, tr) in pairs:
        ins.append(pl.BlockSpec((rows, b.shape[1]), lambda i, _b=blk: (_b, 0)))
        args.append(b)
    rb = _row(tm, D)
    ins += [rb, _const((1, D)), rb]
    args += [h, gain, dh_in]
    return _call(name, body, (t // tm,), ins, [rb, rb, _const((1, D))],
                 [_sds((t, D), F32), _sds((t, D), BF16), _sds((1, D), F32)], args,
                 sem=("arbitrary",), hosted=hosted)


class _Dest:
    def __init__(self, buf, total_rows, off, rows, row0=0):
        self.buf, self.total_rows, self.off, self.rows, self.row0 = buf, total_rows, off, rows, row0

    def segments(self, lo, hi):
        out = []
        for d in range(NDEV):
            a, b = max(lo + self.row0, d * self.rows), min(hi + self.row0, (d + 1) * self.rows)
            if a < b:
                out.append((a - self.row0 - lo, d, self.off + a - d * self.rows, b - a))
        return out


def _store_segments(stage, buf_ref, sems, segs):
    cps = [pltpu.make_async_copy(stage.at[pl.ds(s0, n)], buf_ref.at[d, pl.ds(r0, n)], sems.at[j])
           for j, (s0, d, r0, n) in enumerate(segs)]
    for cp in cps:
        cp.start()
    for cp in cps:
        cp.wait()


def _mm_tn(name, a, b, scale, tmo, tk, dest):
    t, m = a.shape
    n = b.shape[1]
    tk = min(tk, t)
    steps = t // tk
    tiles = m // tmo
    seg_lists = [dest.segments(i * tmo, (i + 1) * tmo) for i in range(tiles)]
    fresh = dest.buf is None

    def body(*refs):
        a_ref, b_ref = refs[0], refs[1]
        buf_ref, acc, stage, sems = refs[-4:]
        i, k = pl.program_id(0), pl.program_id(1)
        _acc(acc, k == 0, _dot_tn(a_ref[...], b_ref[...]))

        @pl.when(k == steps - 1)
        def _():
            stage[...] = (acc[...] * scale).astype(BF16)
            for ti, segs in enumerate(seg_lists):
                @pl.when(i == ti)
                def _(segs=segs):
                    _store_segments(stage, buf_ref, sems, segs)

    ins = [pl.BlockSpec((tk, tmo), lambda i, k: (k, i)), pl.BlockSpec((tk, n), lambda i, k: (k, 0))]
    args = [a, b]
    if not fresh:
        ins.append(_ANY)
        args.append(dest.buf)
    return pl.pallas_call(
        body, name=name, grid=(tiles, steps), in_specs=ins, out_specs=_ANY,
        out_shape=_sds((NDEV, dest.total_rows, n), BF16),
        input_output_aliases={} if fresh else {2: 0},
        scratch_shapes=[pltpu.VMEM((tmo, n), F32), pltpu.VMEM((tmo, n), BF16),
                        pltpu.SemaphoreType.DMA((max(len(s) for s in seg_lists),))],
        compiler_params=_params("arbitrary", "arbitrary"),
    )(*args)


def _attn_bwd(name, dhb, w_o, qb, pb, kb, vb, tm):
    t = dhb.shape[0]
    scale = 1.0 / math.sqrt(HD)

    def body(d_ref, wo_ref, q_ref, p_ref, k_ref, v_ref, dq_ref, dk_ref, dv_ref, do_s):
        i = pl.program_id(0)

        @pl.when(i == 0)
        def _():
            dk_ref[...] = jnp.zeros_like(dk_ref)
            dv_ref[...] = jnp.zeros_like(dv_ref)

        do_s[...] = _dot_nt(d_ref[...], wo_ref[...]).astype(BF16)
        for hd in range(HEADS):
            cs = slice(hd * HD, (hd + 1) * HD)
            pb16 = p_ref[:, hd * NMEM:(hd + 1) * NMEM]
            p = pb16.astype(F32)
            do = do_s[:, cs]
            dp = _dot_nt(do, v_ref[:, cs])
            ds = (p * (dp - jnp.sum(dp * p, axis=-1, keepdims=True)) * scale).astype(BF16)
            dq_ref[:, cs] = _dot(ds, k_ref[:, cs]).astype(BF16)
            dk_ref[:, cs] += _dot_tn(ds, q_ref[:, cs])
            dv_ref[:, cs] += _dot_tn(pb16, do)

    rb = _row(tm, D)
    return pl.pallas_call(
        body, name=name, grid=(t // tm,),
        in_specs=[rb, _const((D, D)), rb, _row(tm, HEADS * NMEM), _const((NMEM, D)), _const((NMEM, D))],
        out_specs=[rb, _const((NMEM, D)), _const((NMEM, D))],
        out_shape=[_sds((t, D), BF16), _sds((NMEM, D), F32), _sds((NMEM, D), F32)],
        scratch_shapes=[pltpu.VMEM((tm, D), BF16)],
        compiler_params=_params("arbitrary"),
    )(dhb, w_o, qb, pb, kb, vb)


def _kv_bwd(name, dk, dv, memn, wkv_t, mem, gain, dest):
    segs = dest.segments(0, 2 * D)
    vmem = pl.BlockSpec(memory_space=pltpu.VMEM)

    def body(dk_ref, dv_ref, mn_ref, w_ref, m_ref, g_ref, buf_in, buf_ref, dg_ref, stage, sems):
        dkb = dk_ref[...].astype(BF16)
        dvb = dv_ref[...].astype(BF16)
        mn = mn_ref[...]
        stage[pl.ds(0, D), :] = _dot_tn(dkb, mn).astype(BF16)
        stage[pl.ds(D, D), :] = _dot_tn(dvb, mn).astype(BF16)
        _store_segments(stage, buf_ref, sems, segs)
        dmn = _dot(dkb, w_ref[pl.ds(0, D), :]) + _dot(dvb, w_ref[pl.ds(D, D), :])
        _, xh = _rms_stats(m_ref[...])
        dg_ref[...] = jnp.sum(dmn * xh, axis=0, keepdims=True)

    return pl.pallas_call(
        body, name=name,
        in_specs=[vmem] * 6 + [_ANY], out_specs=[_ANY, vmem],
        out_shape=[_sds(dest.buf.shape, BF16), _sds((1, D), F32)],
        input_output_aliases={6: 0},
        scratch_shapes=[pltpu.VMEM((2 * D, D), BF16), pltpu.SemaphoreType.DMA((len(segs),))],
        compiler_params=pltpu.CompilerParams(vmem_limit_bytes=VMEM_LIMIT),
    )(dk, dv, memn, wkv_t, mem, gain, dest.buf)


def _merge_bwd(name, dhb, w_out, ya, yb, p, tm):
    t = dhb.shape[0]

    def body(d_ref, w_ref, ya_ref, yb_ref, ga_ref, gb_ref, dya_ref, dyb_ref, dp_ref, cs_ref):
        dm = _dot_nt(d_ref[...], w_ref[...])
        sa = _sigmoid(ga_ref[...].astype(F32))
        sb = _sigmoid(gb_ref[...].astype(F32))
        dya_ref[...] = (dm * sa).astype(BF16)
        dyb_ref[...] = (dm * sb).astype(BF16)
        dga = dm * ya_ref[...].astype(F32) * (sa * (1.0 - sa))
        dgb = dm * yb_ref[...].astype(F32) * (sb * (1.0 - sb))
        dp_ref[:, pl.ds(0, D)] = dga.astype(BF16)
        dp_ref[:, pl.ds(D, D)] = dgb.astype(BF16)
        first = pl.program_id(0) == 0

        @pl.when(first)
        def _():
            cs_ref[...] = jnp.zeros_like(cs_ref)
        cs_ref[:, pl.ds(0, D)] += jnp.sum(dga, axis=0, keepdims=True)
        cs_ref[:, pl.ds(D, D)] += jnp.sum(dgb, axis=0, keepdims=True)

    rb = _row(tm, D)
    return pl.pallas_call(
        body, name=name, grid=(t // tm,),
        in_specs=[rb, _const((D, D)), rb, rb, _row(tm, D, 4), _row(tm, D, 5)],
        out_specs=[rb, rb, _row(tm, 2 * D), _const((1, 2 * D))],
        out_shape=[_sds((t, D), BF16), _sds((t, D), BF16), _sds((t, 2 * D), BF16),
                   _sds((1, 2 * D), F32)],
        compiler_params=_params("arbitrary"),
    )(dhb, w_out, ya, yb, p, p)


def _conv_bwd(name, dya, w_a, c, p, conv_w, ln_g, ln_b, tm, hosted=None):
    t = dya.shape[0]
    steps = t // tm
    hb = tm // HALO

    def rev(i):
        return steps - 1 - i

    def body(dy_ref, wa_ref, c_ref, av_ref, ag_ref, avh_ref, agh_ref, w_ref, lg_ref, lb_ref,
             dp_ref, cs_ref, dw_ref, dcb_ref, dlg_ref, dlb_ref, dc_ext, a_ext, dc_sh, a_sh, da0_s):
        i = pl.program_id(0)
        first = i == 0

        @pl.when(first)
        def _():
            dc_ext[pl.ds(tm, HALO), :] = jnp.zeros((HALO, D), F32)
            dw_ref[...] = jnp.zeros_like(dw_ref)
            cs_ref[...] = jnp.zeros_like(cs_ref)

        d_act = _dot_nt(dy_ref[...], wa_ref[...])
        rstd, chat = _ln_stats(c_ref[...])
        ca = chat * lg_ref[...] + lb_ref[...]
        sc = _sigmoid(ca)
        dca = d_act * (sc * (1.0 + ca * (1.0 - sc)))
        _acc(dlg_ref, first, jnp.sum(dca * chat, axis=0, keepdims=True))
        _acc(dlb_ref, first, jnp.sum(dca, axis=0, keepdims=True))
        dc = _ln_bwd(dca, chat, rstd, lg_ref[...])
        _acc(dcb_ref, first, jnp.sum(dc, axis=0, keepdims=True))
        dc_ext[pl.ds(0, tm), :] = dc

        av = av_ref[...].astype(F32)
        sg = _sigmoid(ag_ref[...].astype(F32))
        a_ext[pl.ds(HALO, tm), :] = av * sg
        halo = avh_ref[...].astype(F32) * _sigmoid(agh_ref[...].astype(F32))
        a_ext[pl.ds(0, HALO), :] = jnp.where(i == steps - 1, 0.0, halo)

        _shift_copies(dc_ext, dc_sh, tm)
        _shift_copies(a_ext, a_sh, tm)
        _tap_sum(da0_s, None, w_ref, dc_ext, dc_sh, [CW - 1 - k for k in range(CW)], tm)
        _tap_corr(dw_ref, dc_ext, a_ext, a_sh, [HALO - (CW - 1) + k for k in range(CW)], tm)
        da0 = da0_s[...]
        dav = da0 * sg
        dag = da0 * av * (sg * (1.0 - sg))
        dp_ref[:, pl.ds(0, D)] = dav.astype(BF16)
        dp_ref[:, pl.ds(D, D)] = dag.astype(BF16)
        cs_ref[:, pl.ds(0, D)] += jnp.sum(dav, axis=0, keepdims=True)
        cs_ref[:, pl.ds(D, D)] += jnp.sum(dag, axis=0, keepdims=True)
        dc_ext[pl.ds(tm, HALO), :] = dc_ext[pl.ds(0, HALO), :]

    def rrow(cols, cb=0):
        return pl.BlockSpec((tm, cols), lambda i, _cb=cb: (rev(i), _cb))

    def halo_spec(cb):
        return pl.BlockSpec((HALO, D), lambda i, _cb=cb: (jnp.maximum(rev(i) * hb - 1, 0), _cb))

    return _call(
        name, body, (steps,),
        [rrow(D), _const((D, D)), rrow(D), rrow(D, 0), rrow(D, 1), halo_spec(0),
         halo_spec(1), _const((HALO, D)), _const((1, D)), _const((1, D))],
        [rrow(2 * D), _const((1, 2 * D)), _const((HALO, D)), _const((1, D)),
         _const((1, D)), _const((1, D))],
        [_sds((t, 2 * D), BF16), _sds((1, 2 * D), F32), _sds((HALO, D), F32),
         _sds((1, D), F32), _sds((1, D), F32), _sds((1, D), F32)],
        (dya, w_a, c, p, p, p, p, conv_w, ln_g, ln_b),
        scratch=[pltpu.VMEM((tm + HALO, D), F32), pltpu.VMEM((tm + HALO, D), F32),
                 pltpu.VMEM((7, tm + HALO - 8, D), F32), pltpu.VMEM((7, tm + HALO - 8, D), F32),
                 pltpu.VMEM((tm, D), F32)],
        sem=("arbitrary",), hosted=hosted)


def _sgu_bwd(name, dyb, w_b, p, ln_g, ln_b, sgu_w, bias_full, tm):
    t = dyb.shape[0]
    steps = t // tm

    def body(dy_ref, wb_ref, bu_ref, bv_ref, lg_ref, lb_ref, ws_ref, bias_ref,
             dp_ref, cs_ref, dws_ref, dsb_ref, dlg_ref, dlb_ref, dub_s, dvn_s, dbias_s):
        i = pl.program_id(0)
        first = i == 0
        mask = _sgu_mask()

        @pl.when(first)
        def _():
            dws_ref[...] = jnp.zeros_like(dws_ref)
            dbias_s[...] = jnp.zeros_like(dbias_s)
            cs_ref[...] = jnp.zeros_like(cs_ref)

        dob = _dot_nt(dy_ref[...], wb_ref[...])
        bu = bu_ref[...].astype(F32)
        bv = bv_ref[...].astype(F32)
        ub, ub_grad = _gelu_with_grad(bu)
        vb, vb_grad = _gelu_with_grad(bv)
        rstd, vhat = _ln_stats(vb)
        vn = (vhat * lg_ref[...] + lb_ref[...]).astype(BF16)
        for g in range(GROUPS):
            wm = jnp.where(mask, ws_ref[g], 0.0).astype(BF16)
            cs = slice(g * GD, (g + 1) * GD)
            for cc in range(tm // CHUNK):
                rs = slice(cc * CHUNK, (cc + 1) * CHUNK)
                vblk = vn[rs, cs]
                mixed = _dot(wm, vblk) + bias_ref[:, cs]
                dob_blk = dob[rs, cs]
                dub_s[rs, cs] = dob_blk * mixed
                dmixed = dob_blk * ub[rs, cs]
                dbias_s[:, cs] += dmixed
                dmb = dmixed.astype(BF16)
                dws_ref[g] += _dot_nt(dmb, vblk)
                dvn_s[rs, cs] = _dot_tn(wm, dmb)
        dbu = dub_s[...] * ub_grad
        dvn = dvn_s[...]
        _acc(dlg_ref, first, jnp.sum(dvn * vhat, axis=0, keepdims=True))
        _acc(dlb_ref, first, jnp.sum(dvn, axis=0, keepdims=True))
        dbv = _ln_bwd(dvn, vhat, rstd, lg_ref[...]) * vb_grad
        dp_ref[:, pl.ds(0, D)] = dbu.astype(BF16)
        dp_ref[:, pl.ds(D, D)] = dbv.astype(BF16)
        cs_ref[:, pl.ds(0, D)] += jnp.sum(dbu, axis=0, keepdims=True)
        cs_ref[:, pl.ds(D, D)] += jnp.sum(dbv, axis=0, keepdims=True)

        @pl.when(i == steps - 1)
        def _():
            lane = lax.broadcasted_iota(jnp.int32, (CHUNK, CHUNK), 1)
            dsb = jnp.zeros((CHUNK, CHUNK), F32)
            for g in range(GROUPS):
                dws_ref[g] = jnp.where(mask, dws_ref[g], 0.0)
                dsb = jnp.where(lane == g, jnp.sum(dbias_s[:, g * GD:(g + 1) * GD], axis=1, keepdims=True), dsb)
            dsb_ref[...] = dsb

    rb = _row(tm, D)
    return pl.pallas_call(
        body, name=name, grid=(steps,),
        in_specs=[rb, _const((D, D)), _row(tm, D, 2), _row(tm, D, 3), _const((1, D)), _const((1, D)),
                  _const((GROUPS, CHUNK, CHUNK)), _const((CHUNK, D))],
        out_specs=[_row(tm, 2 * D), _const((1, 2 * D)), _const((GROUPS, CHUNK, CHUNK)),
                   _const((CHUNK, CHUNK)), _const((1, D)), _const((1, D))],
        out_shape=[_sds((t, 2 * D), BF16), _sds((1, 2 * D), F32), _sds((GROUPS, CHUNK, CHUNK), F32),
                   _sds((CHUNK, CHUNK), F32), _sds((1, D), F32), _sds((1, D), F32)],
        scratch_shapes=[pltpu.VMEM((tm, D), F32), pltpu.VMEM((tm, D), F32), pltpu.VMEM((CHUNK, D), F32)],
        compiler_params=_params("arbitrary"),
    )(dyb, w_b, p, p, ln_g, ln_b, sgu_w, bias_full)


def _adam_math(w, g, m, v):
    m = B1 * m + (1.0 - B1) * g
    v = B2 * v + (1.0 - B2) * (g * g)
    m_hat = m / (1.0 - B1 ** STEP)
    v_hat = v / (1.0 - B2 ** STEP)
    delta = -LR * (m_hat / (jnp.sqrt(v_hat) + EPS_ADAM) + WD * w)
    return delta, m, v


def _adamw(name, w, g, m, v, tr):
    r, cdim = w.shape

    def body(w_ref, g_ref, m_ref, v_ref, d_ref, mo_ref, vo_ref):
        d, mn, vn = _adam_math(w_ref[...], g_ref[...], m_ref[...], v_ref[...])
        d_ref[...] = d
        mo_ref[...] = mn
        vo_ref[...] = vn

    blk = pl.BlockSpec((tr, cdim), lambda i: (i, 0))
    return pl.pallas_call(
        body, name=name, grid=(r // tr,), in_specs=[blk] * 4, out_specs=[blk] * 3,
        out_shape=[_sds((r, cdim), F32)] * 3, compiler_params=_params("parallel"),
    )(w, g, m, v)


def _adamw_small(name, w, g8, m, v):
    r, cdim = w.shape

    def body(w_ref, g_ref, m_ref, v_ref, go_ref, d_ref, mo_ref, vo_ref):
        g = g_ref[0]
        for k in range(1, NDEV):
            g = g + g_ref[k]
        go_ref[...] = g
        d, mn, vn = _adam_math(w_ref[...], g, m_ref[...], v_ref[...])
        d_ref[...] = d
        mo_ref[...] = mn
        vo_ref[...] = vn

    return pl.pallas_call(
        body, name=name, out_shape=[_sds((r, cdim), F32)] * 4,
        compiler_params=pltpu.CompilerParams(vmem_limit_bytes=VMEM_LIMIT),
    )(w, g8, m, v)


_BIG = [("ffn1_w_gu", 704, True), ("ffn1_w_down", 352, False), ("w_in", 768, True),
        ("w_a_out", 128, False), ("w_b_out", 128, False), ("w_out", 128, False),
        ("w_q", 128, False), ("w_kv", 256, True), ("w_o", 128, False),
        ("ffn2_w_gu", 704, True), ("ffn2_w_down", 352, False)]
_BIG_ROWS = sum(r for _, r, _ in _BIG)

_SMALL = [("ffn1_norm", 1), ("mix_norm", 1), ("b_in", 6), ("conv_w", HALO), ("conv_b", 1),
          ("conv_ln_g", 1), ("conv_ln_b", 1), ("sgu_ln_g", 1), ("sgu_ln_b", 1), ("sgu_w", 64),
          ("sgu_b", 1), ("xattn_norm", 1), ("mem_norm", 1), ("ffn2_norm", 1), ("final_norm", 1)]
_SMALL_ROWS = 120


def _pack_small(vals, my_dev):
    rows = []
    for name, nrows in _SMALL:
        a = vals[name].astype(F32)
        if name == "conv_w":
            if a.shape[-1] != D:
                slab = jnp.zeros((HALO, D), F32)
                a = lax.dynamic_update_slice(slab, jnp.pad(a.reshape(CW, -1), ((0, HALO - CW), (0, 0))),
                                             (0, my_dev * (D // NDEV)))
            else:
                a = jnp.pad(a.reshape(CW, D), ((0, HALO - CW), (0, 0)))
        elif name == "sgu_b":
            a = jnp.pad(a.reshape(1, -1), ((0, 0), (0, D - GROUPS * CHUNK)))
        else:
            a = a.reshape(nrows, D)
        rows.append(a)
    packed = jnp.concatenate(rows, axis=0)
    return jnp.pad(packed, ((0, _SMALL_ROWS - packed.shape[0]), (0, 0)))


def _unpack_small(packed, shapes, my_dev):
    out, off = {}, 0
    for name, nrows in _SMALL:
        a = packed[off:off + nrows]
        off += nrows
        if name == "conv_w":
            a = lax.dynamic_slice(a, (0, my_dev * (D // NDEV)), (CW, D // NDEV))
        elif name == "sgu_b":
            a = a[:, :GROUPS * CHUNK]
        out[name] = a.reshape(shapes[name])
    return out


def kernel(x, mem, ffn1_norm, ffn1_w_gu, ffn1_w_down, mix_norm, w_in, b_in, conv_w, conv_b, conv_ln_g, conv_ln_b, w_a_out, sgu_ln_g, sgu_ln_b, sgu_w, sgu_b, w_b_out, w_out, xattn_norm, mem_norm, w_q, w_kv, w_o, ffn2_norm, ffn2_w_gu, ffn2_w_down, final_norm, loss_target, m_ffn1_norm, m_ffn1_w_gu, m_ffn1_w_down, m_mix_norm, m_w_in, m_b_in, m_conv_w, m_conv_b, m_conv_ln_g, m_conv_ln_b, m_w_a_out, m_sgu_ln_g, m_sgu_ln_b, m_sgu_w, m_sgu_b, m_w_b_out, m_w_out, m_xattn_norm, m_mem_norm, m_w_q, m_w_kv, m_w_o, m_ffn2_norm, m_ffn2_w_gu, m_ffn2_w_down, m_final_norm, v_ffn1_norm, v_ffn1_w_gu, v_ffn1_w_down, v_mix_norm, v_w_in, v_b_in, v_conv_w, v_conv_b, v_conv_ln_g, v_conv_ln_b, v_w_a_out, v_sgu_ln_g, v_sgu_ln_b, v_sgu_w, v_sgu_b, v_w_b_out, v_w_out, v_xattn_norm, v_mem_norm, v_w_q, v_w_kv, v_w_o, v_ffn2_norm, v_ffn2_w_gu, v_ffn2_w_down, v_final_norm):
    env = dict(locals())
    names = [n for n, _, _ in _BIG] + [n for n, _ in _SMALL]
    w = {n: env[n] for n in names}
    mom = {n: env["m_" + n] for n in names}
    vel = {n: env["v_" + n] for n in names}

    ax, ay, ac = lax.axis_index("x"), lax.axis_index("y"), lax.axis_index("c")
    my_chip = 2 * ax + ay
    my_dev = 2 * my_chip + ac

    t = x.shape[1]
    tm = min(512, t)
    tm_big = min(1024, t)
    tm_s = min(512, t)
    tm_c = min(256, t)
    xs = x.reshape(t, D)
    tgt = loss_target.reshape(t, D)
    mem2 = mem.reshape(NMEM, D)

    first, mid, late = _BIG[:1], _BIG[1:6], _BIG[6:]

    def gathers(entries):
        return [_gather_comm((w[n][0].T if tr else w[n][0]).astype(BF16)) for n, _, tr in entries]

    def whole(gathered, entries):
        return {n: g.reshape(NDEV * rows, D) for g, (n, rows, _) in zip(gathered, entries)}

    conv_slab = lax.dynamic_update_slice(
        jnp.zeros((HALO, D), F32), jnp.pad(conv_w[0], ((0, HALO - CW), (0, 0))), (0, my_dev * (D // NDEV)))
    bias_full = jnp.repeat(sgu_b[0].T, GD, axis=1)
    b_in2 = b_in.reshape(1, 6 * D)

    (xn1,), (*full_first, conv_w8) = _rms_cast(
        "norm_x", xs, ffn1_norm, tm, hosted=gathers(first) + [_gather_comm(conv_slab)])
    conv_w_pad = jnp.sum(conv_w8, axis=0)
    wf = whole(full_first, first)
    (g1, u1, a1), full_mid = _ffn_up("ffn1_up", xn1, wf["ffn1_w_gu"], tm_big, 1408, hosted=gathers(mid))
    wf.update(whole(full_mid, mid))
    h1, n_mix = _ffn_down("ffn1_down", a1, wf["ffn1_w_down"], xs, mix_norm, tm_big)
    (p,), full_late = _mix_in("mix_in", n_mix, wf["w_in"], b_in2, tm_big, 3072, hosted=gathers(late))
    wf.update(whole(full_late, late))
    c_conv, act_a = _conv_fwd("conv_fwd", p, conv_w_pad, conv_b, conv_ln_g, conv_ln_b, tm_c)
    act_b = _sgu_fwd("sgu_fwd", p, sgu_ln_g, sgu_ln_b, sgu_w[0], bias_full, tm_s)
    ya, yb, merged, h2, xq = _merge_fwd("merge_fwd", act_a, act_b, p, wf["w_a_out"], wf["w_b_out"],
                                        wf["w_out"], h1, xattn_norm, tm_s)
    memn, kb, vb = _kv_fwd("kv_fwd", mem2, mem_norm, wf["w_kv"])
    qb, probs, ob, h3, xn4 = _attn_fwd("attn_fwd", xq, wf["w_q"], kb, vb, wf["w_o"], h2, ffn2_norm, tm_s)
    (g2, u2, a2), _ = _ffn_up("ffn2_up", xn4, wf["ffn2_w_gu"], tm_big, 1408)
    dh4, dh4b, loss_blk, d_final = _ffn_down_loss("ffn2_down_loss", a2, wf["ffn2_w_down"], h3,
                                                  final_norm.reshape(1, D), tgt, tm)

    gs = {}
    gs["final_norm"] = d_final

    core = ac.astype(jnp.int32).reshape(1)
    chip = my_chip.astype(jnp.int32).reshape(1)
    last_g, mixer_g, attn_g = _BIG[:2], _BIG[2:6], _BIG[6:]

    def layout(entries):
        offs, off = {}, 0
        for n, rows, _ in entries:
            offs[n] = (off, rows)
            off += rows
        return offs, off

    def dest(group, buf, name, row0=0):
        offs, total = group
        return _Dest(buf, total, offs[name][0], offs[name][1], row0)

    lay_last, lay_mixer, lay_attn = layout(last_g), layout(mixer_g), layout(attn_g)

    def ffn_bwd(tag, dhb, dh, g, u, a, xn, h_in, gain, wgu_t, wd, group, buf, act_hosted=None, dx_hosted=None):
        (dg, du), act_out = _ffn_bwd_act(tag + "_bwd_act", dhb, wd, g, u, tm_big, 1408, hosted=act_hosted)
        buf = _mm_tn(tag + "_dw_down", a, dhb, 0.5, 1408, TK, dest(group, buf, tag + "_w_down"))
        buf = _mm_tn(tag + "_dw_gate", dg, xn, 1.0, 1408, TK, dest(group, buf, tag + "_w_gu"))
        buf = _mm_tn(tag + "_dw_up", du, xn, 1.0, 1408, TK, dest(group, buf, tag + "_w_gu", DFF))
        (dh_o, dhb_o, dgain), dx_out = _dx_rms_bwd(
            tag + "_bwd_dx", [(dg, wgu_t, 0, DFF, False), (du, wgu_t, 1, DFF, False)], h_in, gain, dh, tm_s,
            hosted=dx_hosted(buf) if dx_hosted else None)
        return dh_o, dhb_o, dgain, buf, act_out, dx_out

    dh3, dh3b, gs["ffn2_norm"], g_attn, _, _ = ffn_bwd(
        "ffn2", dh4b, dh4, g2, u2, a2, xn4, h3, ffn2_norm, wf["ffn2_w_gu"], wf["ffn2_w_down"], lay_attn, None)

    g_attn = _mm_tn("dw_o", ob, dh3b, 1.0, 1024, 2 * TK, dest(lay_attn, g_attn, "w_o"))
    dq, dk, dv = _attn_bwd("attn_bwd", dh3b, wf["w_o"], qb, probs, kb, vb, tm_s)
    g_attn, gs["mem_norm"] = _kv_bwd("kv_bwd", dk, dv, memn, wf["w_kv"], mem2, mem_norm,
                                     dest(lay_attn, g_attn, "w_kv"))
    g_attn = _mm_tn("dw_q", xq, dq, 1.0, 1024, 2 * TK, dest(lay_attn, g_attn, "w_q"))

    (dh2, dh2b, gs["xattn_norm"]), _ = _dx_rms_bwd(
        "attn_bwd_dx", [(dq, wf["w_q"], 0, D, True)], h2, xattn_norm, dh3, tm_s)

    g_mixer = _mm_tn("dw_out", merged, dh2b, 1.0, 1024, 2 * TK, dest(lay_mixer, None, "w_out"))
    dya, dyb, dp_g, cs_g = _merge_bwd("merge_bwd", dh2b, wf["w_out"], ya, yb, p, tm_s)
    g_mixer = _mm_tn("dw_a", act_a, dya, 1.0, 1024, 2 * TK, dest(lay_mixer, g_mixer, "w_a_out"))
    g_mixer = _mm_tn("dw_b", act_b, dyb, 1.0, 1024, 2 * TK, dest(lay_mixer, g_mixer, "w_b_out"))
    (dp_a, cs_a, d_convw, gs["conv_b"], gs["conv_ln_g"], gs["conv_ln_b"]), (attn_slots,) = _conv_bwd(
        "conv_bwd", dya, wf["w_a_out"], c_conv, p, conv_w_pad, conv_ln_g, conv_ln_b, tm_c,
        hosted=_all_to_all_comm(g_attn))
    gsum_attn = _slot_sum("grads_slot_sum_attn", attn_slots, 784)
    dp_b, cs_b, d_sguw, d_sgub, gs["sgu_ln_g"], gs["sgu_ln_b"] = _sgu_bwd(
        "sgu_bwd", dyb, wf["w_b_out"], p, sgu_ln_g, sgu_ln_b, sgu_w[0], bias_full, tm_s)
    gs["conv_w"] = d_convw[:CW].reshape(1, CW, D)
    gs["sgu_w"] = d_sguw
    gs["sgu_b"] = d_sgub[:, :GROUPS].T
    gs["b_in"] = jnp.concatenate([cs_a, cs_b, cs_g], axis=1)
    for j, (tag, dpart) in enumerate((("a", dp_a), ("b", dp_b), ("g", dp_g))):
        g_mixer = _mm_tn("dw_in_" + tag, dpart, n_mix, 1.0, 1024, 2 * TK,
                         dest(lay_mixer, g_mixer, "w_in", 2 * D * j))
    g4_mixer = g_mixer.reshape(4, 2, lay_mixer[1], D)
    (dh1, dh1b, gs["mix_norm"]), (sib_mixer,) = _dx_rms_bwd(
        "mix_bwd_dx", [(dp_a, wf["w_in"], 0, 2 * D, False), (dp_b, wf["w_in"], 1, 2 * D, False),
                       (dp_g, wf["w_in"], 2, 2 * D, False)], h1, mix_norm, dh2, tm_s,
        hosted=_pair_exchange_comm(g4_mixer))
    part_mixer = _pair_sum("grads_pair_sum_mixer", g4_mixer, sib_mixer, core, 576)

    dx, _, gs["ffn1_norm"], _, (chips_mixer,), (last_slots,) = ffn_bwd(
        "ffn1", dh1b, dh1, g1, u1, a1, xn1, xs, ffn1_norm, wf["ffn1_w_gu"], wf["ffn1_w_down"], lay_last, None,
        act_hosted=_chip_exchange_comm(part_mixer), dx_hosted=_all_to_all_comm)
    gsum_mixer = _chip_sum("grads_chip_sum_mixer", part_mixer, chips_mixer, chip, 576)
    gsum_last = _slot_sum("grads_slot_sum_ffn1", last_slots, 528)

    grads, deltas, new_m, new_v = {}, {}, {}, {}
    for entries, gsum in ((last_g, gsum_last), (mixer_g, gsum_mixer), (attn_g, gsum_attn)):
        off = 0
        for name, rows, transposed in entries:
            gsh = gsum[off:off + rows]
            off += rows
            gsh = gsh.T if transposed else gsh
            d, mo, vo = _adamw("adamw_" + name, w[name][0], gsh, mom[name][0], vel[name][0], gsh.shape[0] // 2)
            grads[name], deltas[name], new_m[name], new_v[name] = gsh[None], d[None], mo[None], vo[None]

    shapes = {n: w[n].shape for n, _ in _SMALL}
    (g8,) = _run_comm("gather_small_grads", _gather_comm(_pack_small(gs, my_dev)))
    sg, sd, sm, sv = _adamw_small("adamw_small", _pack_small(w, my_dev), g8,
                                  _pack_small(mom, my_dev), _pack_small(vel, my_dev))
    for dst, src in ((grads, sg), (deltas, sd), (new_m, sm), (new_v, sv)):
        dst.update(_unpack_small(src, shapes, my_dev))

    loss = lax.psum(loss_blk[0, 0], AXES)
    order = ["ffn1_norm", "ffn1_w_gu", "ffn1_w_down", "mix_norm", "w_in", "b_in", "conv_w", "conv_b",
             "conv_ln_g", "conv_ln_b", "w_a_out", "sgu_ln_g", "sgu_ln_b", "sgu_w", "sgu_b", "w_b_out",
             "w_out", "xattn_norm", "mem_norm", "w_q", "w_kv", "w_o", "ffn2_norm", "ffn2_w_gu",
             "ffn2_w_down", "final_norm"]
    return (loss, dx.reshape(x.shape), *[grads[n] for n in order], *[deltas[n] for n in order],
            *[new_m[n] for n in order], *[new_v[n] for n in order])
```
